```python
import jax
import jax.numpy as jnp
from jax import lax
import numpy as np

D_MODEL = 1024
BATCH = 8
SEQ = 2048
DEPTH = 2
DEC_BATCH = 128
DEC_SEQ = 8
PAST_LEN = 2048
PAGE_SIZE = 128

N_EVEN = (DEPTH + 1) // 2
N_ODD = DEPTH // 2
MOBA_HEADS = 8
MOBA_HEAD_DIM = 64
MOBA_BLOCK = 256
MOBA_TOPK = 3
QUERY_BLOCK = 128
ROPE_THETA = 10000.0
HG_HEADS = 4
HG_DK = 128
HG_DV = 128
HG_CHUNK = 64
LRU_WIDTH = D_MODEL
LRU_HEADS = 4
LRU_BLOCK = LRU_WIDTH // LRU_HEADS
CONV_WIDTH = 4
LRU_C = 8.0
N_GROUPS = 4
EXPERTS_PER_GROUP = 4
N_EXPERTS = N_GROUPS * EXPERTS_PER_GROUP
EXPERT_TOPK = 2
EXPERT_FF = D_MODEL // 2
NORM_EPS = 1e-6

MOBA_WIDTH = MOBA_HEADS * MOBA_HEAD_DIM
HG_KEY_WIDTH = HG_HEADS * HG_DK
HG_VAL_WIDTH = HG_HEADS * HG_DV
EVEN_SPLITS = (MOBA_WIDTH, 2 * MOBA_WIDTH, 3 * MOBA_WIDTH,
               3 * MOBA_WIDTH + HG_KEY_WIDTH,
               3 * MOBA_WIDTH + 2 * HG_KEY_WIDTH,
               3 * MOBA_WIDTH + 2 * HG_KEY_WIDTH + HG_VAL_WIDTH)
EVEN_IN_WIDTH = 3 * MOBA_WIDTH + 2 * HG_KEY_WIDTH + 2 * HG_VAL_WIDTH
EVEN_OUT_WIDTH = MOBA_WIDTH + HG_VAL_WIDTH

kernel_name = 'hybrid_moba_hgrn2_rglru_hmoe_step'

F32 = jnp.float32


def _rmsnorm(x, w):
    xf = x.astype(F32)
    y = xf * lax.rsqrt(jnp.mean(xf * xf, axis=-1, keepdims=True) + NORM_EPS)
    return (y * w.astype(F32)).astype(x.dtype)


def _largest_divisor(n, cap):
    d = min(n, cap)
    while n % d:
        d -= 1
    return d


def _rope(x, pos):
    half = x.shape[-1] // 2
    inv_freq = jnp.power(ROPE_THETA, -jnp.arange(half, dtype=F32) / half)
    ang = pos.astype(F32)[:, None] * inv_freq[None, :]
    cos = jnp.cos(ang)[None, :, None, :]
    sin = jnp.sin(ang)[None, :, None, :]
    xf = x.astype(F32)
    x1, x2 = xf[..., :half], xf[..., half:]
    return jnp.concatenate([x1 * cos - x2 * sin, x2 * cos + x1 * sin], axis=-1).astype(x.dtype)


def _moba_one_seq(q, k, v, q_pos0):
    n_heads, lq, dh = q.shape
    n_blk = k.shape[1] // MOBA_BLOCK
    kb = k.reshape(n_heads, n_blk, MOBA_BLOCK, dh)
    vb = v.reshape(n_heads, n_blk, MOBA_BLOCK, dh)
    k_mean = jnp.mean(kb.astype(F32), axis=2)
    qb = _largest_divisor(lq, QUERY_BLOCK)
    n_sel = min(MOBA_TOPK, n_blk)
    scale = dh ** -0.5
    blk_ids = jnp.arange(n_blk)
    in_blk = jnp.arange(MOBA_BLOCK)
    head_ids = jnp.arange(n_heads)[:, None, None]

    def one_block(i):
        qf = lax.dynamic_slice_in_dim(q, i * qb, qb, axis=1).astype(F32)
        pos = q_pos0 + i * qb + jnp.arange(qb)
        own = pos // MOBA_BLOCK
        gate = jnp.einsum('hqd,hnd->hqn', qf, k_mean)
        gate = jnp.where(blk_ids[None, None, :] < own[None, :, None], gate, -jnp.inf)
        _, sel = lax.top_k(gate, n_sel)
        sel_ok = sel < own[None, :, None]
        blocks = jnp.concatenate(
            [sel, jnp.broadcast_to(own[None, :, None], (n_heads, qb, 1))], axis=-1)
        blk_ok = jnp.concatenate([sel_ok, jnp.ones((n_heads, qb, 1), dtype=bool)], axis=-1)
        key_pos = blocks[..., None] * MOBA_BLOCK + in_blk
        mask = blk_ok[..., None] & (key_pos <= pos[None, :, None, None])
        k_g = kb[head_ids, blocks].astype(F32)
        v_g = vb[head_ids, blocks].astype(F32)
        s = jnp.einsum('hqd,hqnkd->hqnk', qf, k_g) * scale
        s = jnp.where(mask, s, -jnp.inf).reshape(n_heads, qb, -1)
        p = jax.nn.softmax(s, axis=-1).reshape(n_heads, qb, n_sel + 1, MOBA_BLOCK)
        return jnp.einsum('hqnk,hqnkd->hqd', p, v_g).astype(q.dtype)

    out = lax.map(one_block, jnp.arange(lq // qb))
    return out.transpose(1, 0, 2, 3).reshape(n_heads, lq, dh)


def _moba(q, k, v, q_pos0):
    b_sz, seq_len, n_heads, dh = q.shape
    pad = (-k.shape[1]) % MOBA_BLOCK
    k = jnp.pad(k, ((0, 0), (0, pad), (0, 0), (0, 0)))
    v = jnp.pad(v, ((0, 0), (0, pad), (0, 0), (0, 0)))
    qt, kt, vt = q.transpose(0, 2, 1, 3), k.transpose(0, 2, 1, 3), v.transpose(0, 2, 1, 3)
    out = lax.map(lambda a: _moba_one_seq(a[0], a[1], a[2], q_pos0), (qt, kt, vt))
    return out.transpose(0, 2, 1, 3).reshape(b_sz, seq_len, n_heads * dh)


def _gated_linear_recurrence(q, k, v, log_f, s0):
    b_sz, seq_len, n_heads, _ = q.shape
    dv = v.shape[-1]
    cl = _largest_divisor(seq_len, HG_CHUNK)
    n_chunks = seq_len // cl

    def to_chunks(a):
        return a.reshape(b_sz, n_chunks, cl, n_heads, a.shape[-1]).transpose(1, 0, 3, 2, 4)

    causal = jnp.tril(jnp.ones((cl, cl), dtype=bool))[None, None, :, :, None]

    def step(s, inp):
        qc, kc, vc, lc = inp
        b = jnp.cumsum(lc, axis=2)
        b_last = b[:, :, -1:, :]
        o_inter = jnp.einsum('bhtk,bhkv->bhtv', qc * jnp.exp(b), s)
        decay = jnp.exp(jnp.where(causal, b[:, :, :, None, :] - b[:, :, None, :, :], -jnp.inf))
        scores = jnp.einsum('bhtsk,bhsk->bhts', qc[:, :, :, None, :] * decay, kc)
        o = o_inter + jnp.einsum('bhts,bhsv->bhtv', scores, vc)
        s_new = (jnp.exp(b_last[:, :, 0, :])[..., None] * s
                 + jnp.einsum('bhsk,bhsv->bhkv', kc * jnp.exp(b_last - b), vc))
        return s_new, o

    s_last, o = lax.scan(step, s0, (to_chunks(q), to_chunks(k), to_chunks(v), to_chunks(log_f)))
    o = o.transpose(1, 0, 3, 2, 4).reshape(b_sz, seq_len, n_heads, dv)
    return o, s_last


def _hgrn2(q, f_pre, i, g, lb, norm_w, s0):
    b_sz, seq_len, _ = q.shape
    qf = jax.nn.silu(q.astype(F32)).reshape(b_sz, seq_len, HG_HEADS, HG_DK)
    f = lb + (1.0 - lb) * jax.nn.sigmoid(f_pre.astype(F32)).reshape(b_sz, seq_len, HG_HEADS, HG_DK)
    kf = 1.0 - f
    log_f = jnp.log(f)
    vf = i.astype(F32).reshape(b_sz, seq_len, HG_HEADS, HG_DV)
    o, s_last = _gated_linear_recurrence(qf, kf, vf, log_f, s0.astype(F32))
    o = o * lax.rsqrt(jnp.mean(o * o, axis=-1, keepdims=True) + NORM_EPS) * norm_w.astype(F32)
    o = o * jax.nn.silu(g.astype(F32).reshape(b_sz, seq_len, HG_HEADS, HG_DV))
    return o.reshape(b_sz, seq_len, HG_VAL_WIDTH).astype(q.dtype), s_last


def _even_mixer(u, pos0, past_k, past_v, w_in, w_out, lb, norm_w, s0):
    b_sz, seq_len, _ = u.shape
    q_a, k_a, v_a, q_b, f_b, i_b, g_b = jnp.split(u @ w_in, EVEN_SPLITS, axis=-1)
    pos = pos0 + jnp.arange(seq_len, dtype=jnp.int32)
    q_a = _rope(q_a.reshape(b_sz, seq_len, MOBA_HEADS, MOBA_HEAD_DIM), pos)
    k_a = _rope(k_a.reshape(b_sz, seq_len, MOBA_HEADS, MOBA_HEAD_DIM), pos)
    v_a = v_a.reshape(b_sz, seq_len, MOBA_HEADS, MOBA_HEAD_DIM)
    if past_k is None:
        k_all, v_all = k_a, v_a
    else:
        k_all = jnp.concatenate([past_k.astype(k_a.dtype), k_a], axis=1)
        v_all = jnp.concatenate([past_v.astype(v_a.dtype), v_a], axis=1)
    o_a = _moba(q_a, k_all, v_all, pos0)
    o_b, s_last = _hgrn2(q_b, f_b, i_b, g_b, lb, norm_w, s0)
    out = jnp.concatenate([o_a, o_b], axis=-1) @ w_out
    return out, k_a, v_a, s_last


def _rg_lru(x, wa, ba, wx, bx, lam, h0):
    b_sz, seq_len, width = x.shape
    xf = x.astype(F32)
    xb = xf.reshape(b_sz, seq_len, LRU_HEADS, LRU_BLOCK)
    r = jax.nn.sigmoid(jnp.einsum('blhi,hij->blhj', xb, wa.astype(F32)).reshape(b_sz, seq_len, width) + ba.astype(F32))
    ig = jax.nn.sigmoid(jnp.einsum('blhi,hij->blhj', xb, wx.astype(F32)).reshape(b_sz, seq_len, width) + bx.astype(F32))
    log_a = -LRU_C * r * jax.nn.softplus(-lam.astype(F32))
    a = jnp.exp(log_a)
    gx = jnp.sqrt(-jnp.expm1(2.0 * log_a)) * (ig * xf)

    def step(h, inp):
        a_t, u_t = inp
        h = a_t * h + u_t
        return h, h

    h_last, hs = lax.scan(step, h0.astype(F32), (a.transpose(1, 0, 2), gx.transpose(1, 0, 2)))
    return hs.transpose(1, 0, 2).astype(x.dtype), h_last


def _griffin_block(u, w_in, conv_w, conv_b, wa, ba, wx, bx, lam, w_out, conv0, h0):
    seq_len = u.shape[1]
    proj = u @ w_in
    gate_br = jax.nn.gelu(proj[..., :LRU_WIDTH], approximate=True)
    rec = proj[..., LRU_WIDTH:]
    hist = jnp.concatenate([conv0.astype(rec.dtype), rec], axis=1)
    conv = conv_b
    for tap in range(CONV_WIDTH):
        conv = conv + hist[:, tap:tap + seq_len] * conv_w[tap]
    conv_state = hist[:, seq_len:]
    y, h_last = _rg_lru(conv, wa, ba, wx, bx, lam, h0)
    out = (y * gate_br) @ w_out
    return out, conv_state, h_last


def _hier_moe(u, rg_w, rg_b, re_w, re_b, w1, w3, w2):
    b_sz, seq_len, d = u.shape
    t = u.reshape(b_sz * seq_len, d)
    g_prob = jax.nn.softmax((t @ rg_w + rg_b).astype(F32), axis=-1)
    g_val, g_idx = lax.top_k(g_prob, 1)
    e_logits = (t @ re_w + re_b).astype(F32).reshape(-1, N_GROUPS, EXPERTS_PER_GROUP)
    e_logits = jnp.take_along_axis(e_logits, g_idx[:, :, None], axis=1)[:, 0]
    e_prob = jax.nn.softmax(e_logits, axis=-1)
    e_val, e_idx = lax.top_k(e_prob, EXPERT_TOPK)
    weights = g_val * e_val / jnp.sum(e_val, axis=-1, keepdims=True)
    expert = g_idx * EXPERTS_PER_GROUP + e_idx
    combine = jnp.sum(jax.nn.one_hot(expert, N_EXPERTS, dtype=F32) * weights[..., None], axis=1)
    h1 = jnp.einsum('td,edf->tef', t, w1)
    h3 = jnp.einsum('td,edf->tef', t, w3)
    act = jax.nn.silu(h1) * h3 * combine[:, :, None].astype(t.dtype)
    y = jnp.einsum('tef,efd->td', act, w2)
    return y.reshape(b_sz, seq_len, d)


def _run_group(x, c, past_k, past_v, hg0, conv0, lru0, ada_w, ada_b, norm_mix, norm_ffn, norm_final,
               w_in_even, w_out_even, hg_lower_bounds, hg_norm_w, w_in_odd, conv_w, conv_b,
               lru_wa, lru_ba, lru_wx, lru_bx, lru_lambda, w_out_odd, router_group_w, router_group_b,
               router_expert_w, router_expert_b, moe_w1, moe_w3, moe_w2):
    pos0 = 0 if past_k is None else past_k.shape[2]
    lbs = jnp.cumsum(jax.nn.softmax(hg_lower_bounds.astype(F32), axis=0), axis=0)
    b_sz = x.shape[0]
    h = x
    k_rows, v_rows, hg_states, conv_states, lru_states = [], [], [], [], []
    for l in range(DEPTH):
        j = l // 2
        mods = (c @ ada_w[l] + ada_b[l]).reshape(b_sz, 1, 6, D_MODEL)
        shift1, scale1, gate1 = mods[:, :, 0], mods[:, :, 1], mods[:, :, 2]
        shift2, scale2, gate2 = mods[:, :, 3], mods[:, :, 4], mods[:, :, 5]
        u = _rmsnorm(h, norm_mix[l]) * (1.0 + scale1) + shift1
        if l % 2 == 0:
            out, k_new, v_new, s_new = _even_mixer(
                u, pos0,
                None if past_k is None else past_k[j],
                None if past_v is None else past_v[j],
                w_in_even[j], w_out_even[j], lbs[j].reshape(HG_HEADS, HG_DK), hg_norm_w[j], hg0[j])
            k_rows.append(k_new)
            v_rows.append(v_new)
            hg_states.append(s_new.astype(x.dtype))
        else:
            out, c_state, h_last = _griffin_block(
                u, w_in_odd[j], conv_w[j], conv_b[j], lru_wa[j], lru_ba[j], lru_wx[j], lru_bx[j],
                lru_lambda[j], w_out_odd[j], conv0[j], lru0[j])
            conv_states.append(c_state)
            lru_states.append(h_last.astype(x.dtype))
        h = h + gate1 * out
        u = _rmsnorm(h, norm_ffn[l]) * (1.0 + scale2) + shift2
        h = h + gate2 * _hier_moe(u, router_group_w[l], router_group_b[l], router_expert_w[l],
                                  router_expert_b[l], moe_w1[l], moe_w3[l], moe_w2[l])
    y = _rmsnorm(h, norm_final)
    return (y, jnp.stack(k_rows), jnp.stack(v_rows), jnp.stack(hg_states),
            jnp.stack(conv_states), jnp.stack(lru_states))


def setup_inputs(seed: int = 0) -> dict:
    key = jax.random.key(seed)
    ks = jax.random.split(key, 36)
    n_pages = PAST_LEN // PAGE_SIZE
    n_used = DEC_BATCH * n_pages
    n_phys = n_used + n_used // 4

    def nrm(k, shape, scale):
        return jax.random.normal(k, shape, F32) * scale

    a_c = jax.random.uniform(ks[26], (N_ODD, LRU_WIDTH), F32, 0.9, 0.999)
    sig = a_c ** (1.0 / LRU_C)
    return {
        'x_prompt': nrm(ks[0], (BATCH, SEQ, D_MODEL), 1.0),
        'x_sample': nrm(ks[1], (DEC_BATCH, DEC_SEQ, D_MODEL), 1.0),
        'c_prompt': nrm(ks[2], (BATCH, D_MODEL), 1.0),
        'c_sample': nrm(ks[3], (DEC_BATCH, D_MODEL), 1.0),
        'cache_k': nrm(ks[4], (N_EVEN, n_phys, PAGE_SIZE, MOBA_HEADS, MOBA_HEAD_DIM), 1.0),
        'cache_v': nrm(ks[5], (N_EVEN, n_phys, PAGE_SIZE, MOBA_HEADS, MOBA_HEAD_DIM), 1.0),
        'state_hgrn': nrm(ks[6], (N_EVEN, DEC_BATCH, HG_HEADS, HG_DK, HG_DV), 0.5),
        'state_conv': nrm(ks[7], (N_ODD, DEC_BATCH, CONV_WIDTH - 1, LRU_WIDTH), 1.0),
        'state_lru': nrm(ks[8], (N_ODD, DEC_BATCH, LRU_WIDTH), 0.5),
        'page_table': jax.random.permutation(ks[9], n_phys)[:n_used].reshape(DEC_BATCH, n_pages).astype(jnp.int32),
        'ada_w': nrm(ks[10], (DEPTH, D_MODEL, 6 * D_MODEL), 0.2 * D_MODEL ** -0.5),
        'ada_b': nrm(ks[11], (DEPTH, 6 * D_MODEL), 0.02),
        'norm_mix': 1.0 + nrm(ks[12], (DEPTH, D_MODEL), 0.02),
        'norm_ffn': 1.0 + nrm(ks[13], (DEPTH, D_MODEL), 0.02),
        'norm_final': 1.0 + nrm(ks[14], (D_MODEL,), 0.02),
        'w_in_even': nrm(ks[15], (N_EVEN, D_MODEL, EVEN_IN_WIDTH), D_MODEL ** -0.5),
        'w_out_even': nrm(ks[16], (N_EVEN, EVEN_OUT_WIDTH, D_MODEL), EVEN_OUT_WIDTH ** -0.5),
        'hg_lower_bounds': nrm(ks[17], (N_EVEN + 1, HG_KEY_WIDTH), 1.0),
        'hg_norm_w': 1.0 + nrm(ks[18], (N_EVEN, HG_DV), 0.02),
        'w_in_odd': nrm(ks[19], (N_ODD, D_MODEL, 2 * LRU_WIDTH), D_MODEL ** -0.5),
        'conv_w': nrm(ks[20], (N_ODD, CONV_WIDTH, LRU_WIDTH), CONV_WIDTH ** -0.5),
        'conv_b': nrm(ks[21], (N_ODD, LRU_WIDTH), 0.02),
        'lru_wa': nrm(ks[22], (N_ODD, LRU_HEADS, LRU_BLOCK, LRU_BLOCK), LRU_BLOCK ** -0.5),
        'lru_ba': nrm(ks[23], (N_ODD, LRU_WIDTH), 0.02),
        'lru_wx': nrm(ks[24], (N_ODD, LRU_HEADS, LRU_BLOCK, LRU_BLOCK), LRU_BLOCK ** -0.5),
        'lru_bx': nrm(ks[25], (N_ODD, LRU_WIDTH), 0.02),
        'lru_lambda': jnp.log(sig) - jnp.log1p(-sig),
        'w_out_odd': nrm(ks[27], (N_ODD, LRU_WIDTH, D_MODEL), LRU_WIDTH ** -0.5),
        'router_group_w': nrm(ks[28], (DEPTH, D_MODEL, N_GROUPS), D_MODEL ** -0.5),
        'router_group_b': nrm(ks[29], (DEPTH, N_GROUPS), 0.01),
        'router_expert_w': nrm(ks[30], (DEPTH, D_MODEL, N_EXPERTS), D_MODEL ** -0.5),
        'router_expert_b': nrm(ks[31], (DEPTH, N_EXPERTS), 0.01),
        'moe_w1': nrm(ks[32], (DEPTH, N_EXPERTS, D_MODEL, EXPERT_FF), D_MODEL ** -0.5),
        'moe_w3': nrm(ks[33], (DEPTH, N_EXPERTS, D_MODEL, EXPERT_FF), D_MODEL ** -0.5),
        'moe_w2': nrm(ks[34], (DEPTH, N_EXPERTS, EXPERT_FF, D_MODEL), EXPERT_FF ** -0.5),
    }


def reference(x_prompt, x_sample, c_prompt, c_sample, cache_k, cache_v, state_hgrn, state_conv, state_lru,
              page_table, ada_w, ada_b, norm_mix, norm_ffn, norm_final, w_in_even, w_out_even,
              hg_lower_bounds, hg_norm_w, w_in_odd, conv_w, conv_b, lru_wa, lru_ba, lru_wx, lru_bx,
              lru_lambda, w_out_odd, router_group_w, router_group_b, router_expert_w, router_expert_b,
              moe_w1, moe_w3, moe_w2):
    dec_b, n_pages = page_table.shape
    past_len = n_pages * cache_k.shape[2]
    past_k = cache_k[:, page_table].reshape(N_EVEN, dec_b, past_len, MOBA_HEADS, MOBA_HEAD_DIM)
    past_v = cache_v[:, page_table].reshape(N_EVEN, dec_b, past_len, MOBA_HEADS, MOBA_HEAD_DIM)
    b_sz = x_prompt.shape[0]
    hg0 = jnp.zeros((N_EVEN, b_sz, HG_HEADS, HG_DK, HG_DV), x_prompt.dtype)
    conv0 = jnp.zeros((N_ODD, b_sz, CONV_WIDTH - 1, LRU_WIDTH), x_prompt.dtype)
    lru0 = jnp.zeros((N_ODD, b_sz, LRU_WIDTH), x_prompt.dtype)
    y_prompt, k_prompt, v_prompt, hgrn_prompt, conv_prompt, lru_prompt = _run_group(
        x_prompt, c_prompt, None, None, hg0, conv0, lru0, ada_w, ada_b, norm_mix, norm_ffn, norm_final,
        w_in_even, w_out_even, hg_lower_bounds, hg_norm_w, w_in_odd, conv_w, conv_b, lru_wa, lru_ba,
        lru_wx, lru_bx, lru_lambda, w_out_odd, router_group_w, router_group_b, router_expert_w,
        router_expert_b, moe_w1, moe_w3, moe_w2)
    y_sample, k_sample, v_sample, hgrn_sample, conv_sample, lru_sample = _run_group(
        x_sample, c_sample, past_k, past_v, state_hgrn, state_conv, state_lru, ada_w, ada_b, norm_mix,
        norm_ffn, norm_final, w_in_even, w_out_even, hg_lower_bounds, hg_norm_w, w_in_odd, conv_w, conv_b,
        lru_wa, lru_ba, lru_wx, lru_bx, lru_lambda, w_out_odd, router_group_w, router_group_b,
        router_expert_w, router_expert_b, moe_w1, moe_w3, moe_w2)
    return (y_prompt, y_sample, k_prompt, v_prompt, k_sample, v_sample, hgrn_prompt, hgrn_sample,
            conv_prompt, conv_sample, lru_prompt, lru_sample)
```

```python
import functools

import numpy as np
import jax
import jax.numpy as jnp
from jax import lax
from jax.experimental import pallas as pl
from jax.experimental.pallas import tpu as pltpu

F32 = jnp.float32
BF16 = jnp.bfloat16
I32 = jnp.int32
HIGHEST = lax.Precision.HIGHEST

MOBA_HEADS = 8
MOBA_HEAD_DIM = 64
MOBA_BLOCK = 256
MOBA_TOPK = 3
ROPE_THETA = 10000.0
HG_HEADS = 4
HG_DK = 128
LRU_HEADS = 4
CONV_WIDTH = 4
LRU_C = 8.0
N_GROUPS = 4
EXPERTS_PER_GROUP = 4
NORM_EPS = 1e-6

LANES = 128
SUBLANES = 8
VMEM_LIMIT = 56 * 1024 * 1024

ROW_TILE = 256
MOE_TILE = 256
HG_CHUNK = 128
N_BUCKETS = N_GROUPS * 6
PAIR_LO = (0, 0, 0, 1, 1, 2)
PAIR_HI = (1, 2, 3, 2, 3, 3)
NEG = -1e30


def _cparams(sem, vmem=VMEM_LIMIT):
    return pltpu.CompilerParams(dimension_semantics=sem, vmem_limit_bytes=vmem)


def _dot(a, b, **kw):
    return jnp.dot(a, b, preferred_element_type=F32, **kw)


def _dot_nt(a, b, **kw):
    return lax.dot_general(a, b, (((1,), (1,)), ((), ())), preferred_element_type=F32, **kw)


def _sigmoid(x):
    return 1.0 / (1.0 + jnp.exp(-x))


def _by_group(i, n_prompt_tiles, fn):
    @pl.when(i < n_prompt_tiles)
    def _():
        fn(True)

    @pl.when(i >= n_prompt_tiles)
    def _():
        fn(False)


def _pair_specs(cols, npt, rows=ROW_TILE, off_p=0, off_s=0):
    return (pl.BlockSpec((rows, cols), lambda i: (off_p + jnp.minimum(i, npt - 1), 0)),
            pl.BlockSpec((rows, cols), lambda i: (off_s + jnp.maximum(i - npt, 0), 0)))


def _mod_specs(d, npt, tiles_per_seq, rows=ROW_TILE):
    return (pl.BlockSpec((1, 6, d), lambda i: (jnp.minimum(i, npt - 1) // tiles_per_seq, 0, 0)),
            pl.BlockSpec((rows // SUBLANES, 6, d), lambda i: (jnp.maximum(i - npt, 0), 0, 0)))


def _norm_mod(x, nw, m3, scale_idx, shift_idx):
    r, d = x.shape
    nb = m3.shape[0]
    var = jnp.mean(x * x, axis=-1, keepdims=True)
    y = x * lax.rsqrt(var + NORM_EPS) * nw
    y3 = y.reshape(nb, r // nb, d)
    u3 = y3 * (1.0 + m3[:, scale_idx:scale_idx + 1, :]) + m3[:, shift_idx:shift_idx + 1, :]
    return u3.reshape(r, d)


def _gate_residual(h, m3, gate_idx, out):
    r, d = h.shape
    nb = m3.shape[0]
    return (h.reshape(nb, r // nb, d) + m3[:, gate_idx:gate_idx + 1, :] * out.reshape(nb, r // nb, d)).reshape(r, d)


def _ada_kernel(c_ref, w_ref, b_ref, o_ref):
    o_ref[0] = _dot(c_ref[...], w_ref[0], precision=HIGHEST) + b_ref[0]


def _ada_mods(c_all, ada_w, ada_b):
    depth, d, n6 = ada_w.shape
    nb = c_all.shape[0]
    tn = 1024
    return pl.pallas_call(
        _ada_kernel,
        out_shape=jax.ShapeDtypeStruct((depth, nb, n6), F32),
        grid=(depth, n6 // tn),
        in_specs=[pl.BlockSpec((nb, d), lambda l, n: (0, 0)),
                  pl.BlockSpec((1, d, tn), lambda l, n: (l, 0, n)),
                  pl.BlockSpec((1, 1, tn), lambda l, n: (l, 0, n))],
        out_specs=pl.BlockSpec((1, nb, tn), lambda l, n: (l, 0, n)),
        compiler_params=_cparams(("arbitrary", "arbitrary")),
        name="ada_mods",
    )(c_all, ada_w, ada_b.reshape(depth, 1, n6))


def _even_pre_kernel(xp_ref, xs_ref, mp_ref, ms_ref, nw_ref, w_ref, hb_ref, invf_ref,
                     q_ref, kp_ref, ks_ref, vp_ref, vs_ref, hg_ref, *, npt, tiles_per_seq, past_len, layer_slot):
    i = pl.program_id(0)
    aw = MOBA_HEADS * MOBA_HEAD_DIM
    hw = HG_HEADS * HG_DK

    def run(is_prompt):
        x = (xp_ref if is_prompt else xs_ref)[...]
        m3 = (mp_ref if is_prompt else ms_ref)[...]
        r = x.shape[0]
        nb = m3.shape[0]
        tl = r // nb
        ub = _norm_mod(x, nw_ref[...], m3, 1, 0).astype(BF16)

        pos0 = (i % tiles_per_seq) * r if is_prompt else past_len
        pos = (pos0 + lax.broadcasted_iota(I32, (nb, tl, LANES), 1)).astype(F32).reshape(r, LANES)
        ang = pos * invf_ref[...]
        reps = aw // LANES
        cos = jnp.concatenate([jnp.cos(ang)] * reps, axis=1)
        sin = jnp.concatenate([jnp.sin(ang)] * reps, axis=1)
        lane = lax.broadcasted_iota(I32, (r, aw), 1)
        half = MOBA_HEAD_DIM // 2
        first = (lane % MOBA_HEAD_DIM) < half
        sin = jnp.where(first, -sin, sin)

        def rope(t):
            rot = jnp.where(first, pltpu.roll(t, aw - half, 1), pltpu.roll(t, half, 1))
            return t * cos + rot * sin

        def seg(a, b):
            return _dot(ub, w_ref[:, a:b])

        q_ref[...] = rope(seg(0, aw))
        k = rope(seg(aw, 2 * aw))
        v = seg(2 * aw, 3 * aw)
        if is_prompt:
            kp_ref[...] = k
            vp_ref[...] = v
        else:
            ks_ref[...] = k
            vs_ref[...] = v
        o = 3 * aw
        qb = seg(o, o + hw)
        fb = seg(o + hw, o + 2 * hw)
        hb = hb_ref[...]
        e = jnp.exp(hb - jnp.max(hb, axis=0, keepdims=True))
        sm = e / jnp.sum(e, axis=0, keepdims=True)
        lb = jnp.sum(sm[0:layer_slot + 1, :], axis=0, keepdims=True)
        hg_ref[:, 0:hw] = qb * _sigmoid(qb)
        hg_ref[:, hw:2 * hw] = lb + (1.0 - lb) * _sigmoid(fb)
        hg_ref[:, 2 * hw:3 * hw] = seg(o + 2 * hw, o + 3 * hw)
        hg_ref[:, 3 * hw:4 * hw] = seg(o + 3 * hw, o + 4 * hw)

    _by_group(i, npt, run)


def _even_pre(xp, xs, mp, ms, nw, w_in, hb, invf, *, seq, past_len, layer_slot):
    tp, d = xp.shape
    ts = xs.shape[0]
    npt, nst = tp // ROW_TILE, ts // ROW_TILE
    tiles_per_seq = seq // ROW_TILE
    aw = MOBA_HEADS * MOBA_HEAD_DIM
    hw = HG_HEADS * HG_DK
    t = tp + ts
    const = lambda shape: pl.BlockSpec(shape, lambda i: (0,) * len(shape))
    kern = functools.partial(_even_pre_kernel, npt=npt, tiles_per_seq=tiles_per_seq, past_len=past_len,
                             layer_slot=layer_slot)
    kp_spec, ks_spec = _pair_specs(aw, npt)
    return pl.pallas_call(
        kern,
        out_shape=(jax.ShapeDtypeStruct((t, aw), F32),
                   jax.ShapeDtypeStruct((tp, aw), F32), jax.ShapeDtypeStruct((ts, aw), F32),
                   jax.ShapeDtypeStruct((tp, aw), F32), jax.ShapeDtypeStruct((ts, aw), F32),
                   jax.ShapeDtypeStruct((t, 4 * hw), F32)),
        grid=(npt + nst,),
        in_specs=[*_pair_specs(d, npt), *_mod_specs(d, npt, tiles_per_seq),
                  const((1, d)), const(w_in.shape), const(hb.shape), const((1, LANES))],
        out_specs=(pl.BlockSpec((ROW_TILE, aw), lambda i: (i, 0)), kp_spec, ks_spec, kp_spec, ks_spec,
                   pl.BlockSpec((ROW_TILE, 4 * hw), lambda i: (i, 0))),
        compiler_params=_cparams(("arbitrary",)),
        name="even_pre",
    )(xp, xs, mp, ms, nw, w_in, hb, invf)


def _top_blocks(g, valid):
    lane = lax.broadcasted_iota(I32, g.shape, 1)
    jl = lane % SUBLANES
    g = jnp.where(valid, g, -jnp.inf)
    cnt = jnp.zeros(g.shape, I32)
    for d in range(1, SUBLANES):
        wrapped = (jl + d) >= SUBLANES
        pg = jnp.where(wrapped, pltpu.roll(g, SUBLANES - d, 1), pltpu.roll(g, LANES - d, 1))
        beats = (pg > g) | ((pg == g) & wrapped)
        cnt = cnt + beats.astype(I32)
    return valid & (cnt < MOBA_TOPK)


def _head_expand(rows8, n_heads, width):
    x = jnp.concatenate([rows8] * n_heads, axis=0)
    r = lax.broadcasted_iota(I32, x.shape, 0) // SUBLANES
    l = lax.broadcasted_iota(I32, x.shape, 1) // (width // n_heads)
    return jnp.where(r == l, x, 0.0)


def _attn_prompt_kernel(q_ref, k_ref, v_ref, o_ref, km_ref, *, n_blk):
    i = pl.program_id(1)
    blk = MOBA_BLOCK
    aw = MOBA_HEADS * MOBA_HEAD_DIM
    scale = MOBA_HEAD_DIM ** -0.5

    @pl.when(i == 0)
    def _():
        for j in range(n_blk):
            km_ref[j:j + 1, :] = jnp.mean(k_ref[j * blk:(j + 1) * blk, :], axis=0, keepdims=True)

    q = q_ref[...]
    kmexp = _head_expand(km_ref[...], MOBA_HEADS, aw)
    kmexp = jnp.concatenate([kmexp, jnp.zeros_like(kmexp)], axis=0)
    gate = _dot_nt(q, kmexp, precision=HIGHEST)
    lane = lax.broadcasted_iota(I32, gate.shape, 1)
    valid = ((lane % SUBLANES) < i) & (lane < MOBA_HEADS * SUBLANES)
    sel = _top_blocks(gate, valid).astype(F32)

    row = lax.broadcasted_iota(I32, (blk, blk), 0)
    col = lax.broadcasted_iota(I32, (blk, blk), 1)
    tril = row >= col
    lane_p = lax.broadcasted_iota(I32, (blk, LANES), 1)
    own0 = pl.multiple_of(i * blk, blk)

    for hp in range(aw // LANES):
        cols = slice(hp * LANES, (hp + 1) * LANES)
        qp = q[:, cols] * scale
        qm = [jnp.where((lane_p // MOBA_HEAD_DIM) == s, qp, 0.0).astype(BF16) for s in (0, 1)]
        kk = k_ref[pl.ds(own0, blk), cols].astype(BF16)
        vv = v_ref[pl.ds(own0, blk), cols].astype(BF16)
        state = []
        for s in (0, 1):
            sc = jnp.where(tril, _dot_nt(qm[s], kk), NEG)
            m = jnp.max(sc, axis=1, keepdims=True)
            p = jnp.exp(sc - m)
            state += [m, jnp.sum(p, axis=1, keepdims=True), _dot(p.astype(BF16), vv)]

        def body(j, st, hp=hp, cols=cols, qm=qm):
            r0 = pl.multiple_of(j * blk, blk)
            kk = k_ref[pl.ds(r0, blk), cols].astype(BF16)
            vv = v_ref[pl.ds(r0, blk), cols].astype(BF16)
            out = []
            for s in (0, 1):
                m, l, acc = st[3 * s:3 * s + 3]
                c = (2 * hp + s) * SUBLANES + j
                colsel = jnp.sum(jnp.where(lane == c, sel, 0.0), axis=1, keepdims=True) > 0.5
                sc = jnp.where(colsel, _dot_nt(qm[s], kk), NEG)
                m_new = jnp.maximum(m, jnp.max(sc, axis=1, keepdims=True))
                alpha = jnp.exp(m - m_new)
                p = jnp.exp(sc - m_new)
                out += [m_new, alpha * l + jnp.sum(p, axis=1, keepdims=True), alpha * acc + _dot(p.astype(BF16), vv)]
            return tuple(out)

        st = lax.fori_loop(0, i, body, tuple(state))
        o_pair = jnp.where((lane_p // MOBA_HEAD_DIM) == 0, st[2] / st[1], st[5] / st[4])
        o_ref[:, cols] = o_pair.astype(BF16)


def _attn_prompt(q_all, k_p, v_p, *, batch, seq):
    aw = MOBA_HEADS * MOBA_HEAD_DIM
    n_blk = seq // MOBA_BLOCK
    kern = functools.partial(_attn_prompt_kernel, n_blk=n_blk)
    return pl.pallas_call(
        kern,
        out_shape=jax.ShapeDtypeStruct((batch * seq, aw), BF16),
        grid=(batch, n_blk),
        in_specs=[pl.BlockSpec((MOBA_BLOCK, aw), lambda b, i: (b * n_blk + i, 0)),
                  pl.BlockSpec((seq, aw), lambda b, i: (b, 0)),
                  pl.BlockSpec((seq, aw), lambda b, i: (b, 0))],
        out_specs=pl.BlockSpec((MOBA_BLOCK, aw), lambda b, i: (b * n_blk + i, 0)),
        scratch_shapes=[pltpu.VMEM((SUBLANES, aw), F32)],
        compiler_params=_cparams(("arbitrary", "arbitrary")),
        name="moba_prompt",
    )(q_all, k_p, v_p)


def _attn_sample_kernel(pt_ref, q_ref, kn_ref, vn_ref, ck_ref, cv_ref, o_ref, kbuf, vbuf, sem,
                        *, n_pages, page, n_blk):
    b = pl.program_id(0)
    nb = pl.num_programs(0)
    aw = MOBA_HEADS * MOBA_HEAD_DIM
    blk = MOBA_BLOCK
    dec = q_ref.shape[0]
    scale = MOBA_HEAD_DIM ** -0.5

    def page_copies(seq_idx, slot):
        cps = []
        for p in range(n_pages):
            pg = pt_ref[seq_idx * n_pages + p]
            cps.append(pltpu.make_async_copy(ck_ref.at[pg], kbuf.at[slot, pl.ds(p * page, page), :], sem.at[0, slot]))
            cps.append(pltpu.make_async_copy(cv_ref.at[pg], vbuf.at[slot, pl.ds(p * page, page), :], sem.at[1, slot]))
        return cps

    @pl.when(b == 0)
    def _():
        for c in page_copies(0, 0):
            c.start()

    @pl.when(b + 1 < nb)
    def _():
        for c in page_copies(b + 1, (b + 1) % 2):
            c.start()

    slot = b % 2
    for c in page_copies(b, slot):
        c.wait()

    kpast = kbuf[slot]
    vpast = vbuf[slot]
    q = q_ref[...]
    nh = MOBA_HEADS
    rows = nh * dec
    qexp = _head_expand(q, nh, aw)
    km = jnp.concatenate([jnp.mean(kpast[j * blk:(j + 1) * blk, :], axis=0, keepdims=True) for j in range(n_blk)],
                         axis=0)
    km = jnp.concatenate([km, jnp.zeros((LANES - n_blk, aw), F32)], axis=0)
    gate = _dot_nt(qexp, km, precision=HIGHEST)
    lane = lax.broadcasted_iota(I32, gate.shape, 1)
    sel = _top_blocks(gate, lane < n_blk).astype(F32)

    qs = (qexp * scale).astype(BF16)
    sc = _dot_nt(qs, kpast.astype(BF16))
    key_blk = lax.broadcasted_iota(I32, sc.shape, 1) // blk
    mask = jnp.zeros(sc.shape, F32)
    for j in range(n_blk):
        mask = jnp.where(key_blk == j, sel[:, j:j + 1], mask)
    sc = jnp.where(mask > 0.5, sc, NEG)

    pad = jnp.zeros((LANES - dec, aw), F32)
    kn = jnp.concatenate([kn_ref[...], pad], axis=0).astype(BF16)
    vn = jnp.concatenate([vn_ref[...], pad], axis=0).astype(BF16)
    sn = _dot_nt(qs, kn)
    qi = lax.broadcasted_iota(I32, sn.shape, 0) % dec
    sn = jnp.where(lane <= qi, sn, NEG)

    m = jnp.maximum(jnp.max(sc, axis=1, keepdims=True), jnp.max(sn, axis=1, keepdims=True))
    p = jnp.exp(sc - m)
    pn = jnp.exp(sn - m)
    l = jnp.sum(p, axis=1, keepdims=True) + jnp.sum(pn, axis=1, keepdims=True)
    o = (_dot(p.astype(BF16), vpast.astype(BF16)) + _dot(pn.astype(BF16), vn)) / l
    r = lax.broadcasted_iota(I32, o.shape, 0) // dec
    c = lax.broadcasted_iota(I32, o.shape, 1) // MOBA_HEAD_DIM
    o = jnp.where(r == c, o, 0.0)
    acc = o[0:dec]
    for h in range(1, nh):
        acc = acc + o[h * dec:(h + 1) * dec]
    o_ref[...] = acc


def _attn_sample(page_table, q_all, k_s, v_s, cache_k, cache_v, *, dec_batch, dec_seq, q_row0):
    n_phys, page = cache_k.shape[0], cache_k.shape[1]
    aw = MOBA_HEADS * MOBA_HEAD_DIM
    n_pages = page_table.shape[1]
    past_len = n_pages * page
    assert past_len % MOBA_BLOCK == 0 and dec_seq == SUBLANES and past_len // MOBA_BLOCK <= SUBLANES
    kern = functools.partial(_attn_sample_kernel, n_pages=n_pages, page=page, n_blk=past_len // MOBA_BLOCK)
    q_blk0 = q_row0 // dec_seq
    grid_spec = pltpu.PrefetchScalarGridSpec(
        num_scalar_prefetch=1,
        grid=(dec_batch,),
        in_specs=[pl.BlockSpec((dec_seq, aw), lambda b, pt: (q_blk0 + b, 0)),
                  pl.BlockSpec((dec_seq, aw), lambda b, pt: (b, 0)),
                  pl.BlockSpec((dec_seq, aw), lambda b, pt: (b, 0)),
                  pl.BlockSpec(memory_space=pl.ANY),
                  pl.BlockSpec(memory_space=pl.ANY)],
        out_specs=pl.BlockSpec((dec_seq, aw), lambda b, pt: (b, 0)),
        scratch_shapes=[pltpu.VMEM((2, past_len, aw), F32), pltpu.VMEM((2, past_len, aw), F32),
                        pltpu.SemaphoreType.DMA((2, 2))],
    )
    return pl.pallas_call(
        kern,
        out_shape=jax.ShapeDtypeStruct((dec_batch * dec_seq, aw), F32),
        grid_spec=grid_spec,
        compiler_params=_cparams(("arbitrary",)),
        name="moba_sample",
    )(page_table.reshape(-1), q_all, k_s, v_s, cache_k, cache_v)


def _hgrn_levels():
    n = HG_CHUNK
    return int(np.log2(n))


def _hgrn_cumsum_matrix():
    c = HG_CHUNK
    t = np.arange(c)[:, None]
    s = np.arange(c)[None, :]
    mats = [(s <= t)]
    for l in range(_hgrn_levels()):
        m = 2 ** l
        sep = (t // (2 * m)) * (2 * m) + m - 1
        mats.append(s <= sep)
    return jnp.asarray(np.concatenate(mats, axis=0).astype(np.float32), dtype=BF16)


def _hgrn_post(o, g, nw):
    o = o * lax.rsqrt(jnp.mean(o * o, axis=-1, keepdims=True) + NORM_EPS) * nw
    return o * (g * _sigmoid(g))


def _hgrn_prompt_kernel(q_ref, f_ref, i_ref, g_ref, ms_ref, nw_ref, o_ref, s_ref):
    c = HG_CHUNK
    dk = HG_DK
    n_chunks = q_ref.shape[0] // c
    levels = _hgrn_levels()
    row = lax.broadcasted_iota(I32, (c, c), 0)
    col = lax.broadcasted_iota(I32, (c, c), 1)
    rowk = lax.broadcasted_iota(I32, (c, dk), 0)
    upper = [((rowk // (2 ** l)) % 2) == 1 for l in range(levels)]
    same = [(row // (2 ** (l + 1))) == (col // (2 ** (l + 1))) for l in range(levels)]
    diag = row == col
    nw = nw_ref[...]

    def chunk(ci, st):
        r0 = pl.multiple_of(ci * c, c)
        q = q_ref[pl.ds(r0, c), :]
        f = f_ref[pl.ds(r0, c), :]
        v = i_ref[pl.ds(r0, c), :]
        g = g_ref[pl.ds(r0, c), :]
        lf = jnp.log(f)
        k = 1.0 - f
        hi = lf.astype(BF16)
        r1 = lf - hi.astype(F32)
        mid = r1.astype(BF16)
        lo = (r1 - mid.astype(F32)).astype(BF16)
        bb = _dot(ms_ref[...], jnp.concatenate([hi, mid, lo], axis=1))
        bb = bb[:, 0:dk] + bb[:, dk:2 * dk] + bb[:, 2 * dk:3 * dk]
        b = bb[0:c]
        a = jnp.where(diag, jnp.sum(q * k, axis=1, keepdims=True), 0.0)
        for l in range(levels):
            bs = bb[(l + 1) * c:(l + 2) * c]
            e = jnp.exp(jnp.where(upper[l], b - bs, bs - b))
            qp = jnp.where(upper[l], q * e, 0.0).astype(BF16)
            kp = jnp.where(upper[l], 0.0, k * e).astype(BF16)
            a = a + jnp.where(same[l], _dot_nt(qp, kp), 0.0)
        vb = v.astype(BF16)
        o = _dot_nt((q * jnp.exp(b)).astype(BF16), st.astype(BF16)) + _dot(a.astype(BF16), vb)
        blast = b[c - 1:c, :]
        kl = (k * jnp.exp(blast - b)).astype(BF16)
        st_new = st * jnp.exp(blast) + _dot(v.T.astype(BF16), kl)
        o_ref[pl.ds(r0, c), :] = _hgrn_post(o, g, nw).astype(BF16)
        return st_new

    st = lax.fori_loop(0, n_chunks, chunk, jnp.zeros((dk, dk), F32))
    s_ref[0] = st.T


def _hgrn_prompt(hg, nw, *, batch, seq):
    hw = HG_HEADS * HG_DK
    mstack = _hgrn_cumsum_matrix()
    spec = lambda off: pl.BlockSpec((seq, HG_DK), lambda b, h: (b, off + h))
    return pl.pallas_call(
        _hgrn_prompt_kernel,
        out_shape=(jax.ShapeDtypeStruct((batch * seq, hw), BF16),
                   jax.ShapeDtypeStruct((batch * HG_HEADS, HG_DK, HG_DK), F32)),
        grid=(batch, HG_HEADS),
        in_specs=[spec(0), spec(HG_HEADS), spec(2 * HG_HEADS), spec(3 * HG_HEADS),
                  pl.BlockSpec(mstack.shape, lambda b, h: (0, 0)),
                  pl.BlockSpec((1, HG_DK), lambda b, h: (0, 0))],
        out_specs=(pl.BlockSpec((seq, HG_DK), lambda b, h: (b, h)),
                   pl.BlockSpec((1, HG_DK, HG_DK), lambda b, h: (b * HG_HEADS + h, 0, 0))),
        compiler_params=_cparams(("arbitrary", "arbitrary")),
        name="hgrn_prompt",
    )(hg, hg, hg, hg, mstack, nw)


def _hgrn_sample_kernel(hg_ref, s0_ref, nw_ref, o_ref, s_ref, *, dec, seqs):
    dk = HG_DK
    hw = HG_HEADS * HG_DK
    nw = nw_ref[...]
    zpad = jnp.zeros((dk - 3 * dec, dk), F32)

    def one_seq(si, carry):
        r0 = pl.multiple_of(si * dec, dec)
        for h in range(HG_HEADS):
            cols = slice(h * dk, (h + 1) * dk)
            q = hg_ref[pl.ds(r0, dec), h * dk:(h + 1) * dk]
            f = hg_ref[pl.ds(r0, dec), hw + h * dk:hw + (h + 1) * dk]
            v = hg_ref[pl.ds(r0, dec), 2 * hw + h * dk:2 * hw + (h + 1) * dk]
            g = hg_ref[pl.ds(r0, dec), 3 * hw + h * dk:3 * hw + (h + 1) * dk]
            colsT = jnp.concatenate([f, 1.0 - f, q, zpad], axis=0).T
            st = s0_ref[si * HG_HEADS + h]
            outs = []
            for t in range(dec):
                st = colsT[:, t:t + 1] * st + colsT[:, dec + t:dec + t + 1] * v[t:t + 1, :]
                outs.append(jnp.sum(colsT[:, 2 * dec + t:2 * dec + t + 1] * st, axis=0, keepdims=True))
            o = jnp.concatenate(outs, axis=0)
            o_ref[pl.ds(r0, dec), cols] = _hgrn_post(o, g, nw)
            s_ref[si * HG_HEADS + h] = st
        return carry

    lax.fori_loop(0, seqs, one_seq, 0)


def _hgrn_sample(hg, s0, nw, *, dec_batch, dec_seq, row0):
    hw = HG_HEADS * HG_DK
    seqs = 8
    rows = seqs * dec_seq
    kern = functools.partial(_hgrn_sample_kernel, dec=dec_seq, seqs=seqs)
    st_spec = pl.BlockSpec((seqs * HG_HEADS, HG_DK, HG_DK), lambda i: (i, 0, 0))
    return pl.pallas_call(
        kern,
        out_shape=(jax.ShapeDtypeStruct((dec_batch * dec_seq, hw), F32),
                   jax.ShapeDtypeStruct((dec_batch * HG_HEADS, HG_DK, HG_DK), F32)),
        grid=(dec_batch // seqs,),
        in_specs=[pl.BlockSpec((rows, 4 * hw), lambda i: (row0 // rows + i, 0)), st_spec,
                  pl.BlockSpec((1, HG_DK), lambda i: (0, 0))],
        out_specs=(pl.BlockSpec((rows, hw), lambda i: (i, 0)), st_spec),
        compiler_params=_cparams(("arbitrary",)),
        name="hgrn_sample",
    )(hg, s0, nw)


def _post_mixer_kernel(*refs, n_a, npt):
    hp_ref, hs_ref, mp_ref, ms_ref = refs[0:4]
    a_refs = refs[4:4 + 2 * n_a]
    w_refs = refs[4 + 2 * n_a:4 + 3 * n_a]
    nffn_ref, rw_ref, rb_ref, stril_ref = refs[4 + 3 * n_a:8 + 3 * n_a]
    hmid_ref, u_ref, rinfo_ref, counts_ref, carry = refs[8 + 3 * n_a:]
    i = pl.program_id(0)

    @pl.when(i == 0)
    def _():
        carry[...] = jnp.zeros_like(carry)

    def run(is_prompt):
        h = (hp_ref if is_prompt else hs_ref)[...]
        m3 = (mp_ref if is_prompt else ms_ref)[...]
        out = None
        for a in range(n_a):
            av = a_refs[2 * a + (0 if is_prompt else 1)][...].astype(BF16)
            t = _dot(av, w_refs[a][...])
            out = t if out is None else out + t
        hm = _gate_residual(h, m3, 2, out)
        hmid_ref[...] = hm
        u = _norm_mod(hm, nffn_ref[...], m3, 4, 3)
        u_ref[...] = u

        logits = _dot(u, rw_ref[...], precision=HIGHEST) + rb_ref[...]
        col = lambda j: logits[:, j:j + 1]
        gl = [col(j) for j in range(N_GROUPS)]
        gmax = functools.reduce(jnp.maximum, gl)
        gsum = functools.reduce(lambda x, y: x + y, [jnp.exp(x - gmax) for x in gl])
        gval = 1.0 / gsum
        gidx = jnp.where(gl[0] == gmax, 0, jnp.where(gl[1] == gmax, 1, jnp.where(gl[2] == gmax, 2, 3)))
        el = []
        for k in range(EXPERTS_PER_GROUP):
            c = [col(N_GROUPS + g * EXPERTS_PER_GROUP + k) for g in range(N_GROUPS)]
            el.append(jnp.where(gidx == 0, c[0], jnp.where(gidx == 1, c[1], jnp.where(gidx == 2, c[2], c[3]))))
        emax = functools.reduce(jnp.maximum, el)
        pe = [jnp.exp(x - emax) for x in el]
        esum = functools.reduce(lambda x, y: x + y, pe)
        pk = [x / esum for x in pe]
        v1 = functools.reduce(jnp.maximum, pk)
        i1 = jnp.where(pk[0] == v1, 0, jnp.where(pk[1] == v1, 1, jnp.where(pk[2] == v1, 2, 3)))
        pk2 = [jnp.where(i1 == k, -1.0, pk[k]) for k in range(EXPERTS_PER_GROUP)]
        v2 = functools.reduce(jnp.maximum, pk2)
        i2 = jnp.where(pk2[0] == v2, 0, jnp.where(pk2[1] == v2, 1, jnp.where(pk2[2] == v2, 2, 3)))
        den = v1 + v2
        w1 = gval * v1 / den
        w2 = gval * v2 / den
        lo = jnp.minimum(i1, i2)
        hi = jnp.maximum(i1, i2)
        pair = jnp.where(lo == 0, hi - 1, jnp.where(lo == 1, hi + 1, 5))
        bucket = gidx * 6 + pair
        w_lo = jnp.where(i1 < i2, w1, w2)
        w_hi = jnp.where(i1 < i2, w2, w1)

        r = h.shape[0]
        lane = lax.broadcasted_iota(I32, (r, LANES), 1)
        onehot = lane == bucket
        before = _dot(stril_ref[...], onehot.astype(BF16)) + carry[...]
        rank = jnp.sum(jnp.where(onehot, before, 0.0), axis=1, keepdims=True)
        carry[...] = carry[...] + jnp.sum(onehot.astype(F32), axis=0, keepdims=True)
        rinfo_ref[...] = jnp.where(lane == 0, bucket.astype(F32),
                                   jnp.where(lane == 1, w_lo, jnp.where(lane == 2, w_hi,
                                                                        jnp.where(lane == 3, rank, 0.0))))
        counts_ref[...] = carry[...]

    _by_group(i, npt, run)


def _post_mixer(h_pair, h_off_s, mp, ms, a_pairs, ws, nffn, rw, rb, *, seq):
    hp, hs = h_pair
    d = hp.shape[1]
    tp = a_pairs[0][0].shape[0]
    ts = a_pairs[0][1].shape[0]
    npt, nst = tp // ROW_TILE, ts // ROW_TILE
    t = tp + ts
    n_a = len(a_pairs)
    tiles_per_seq = seq // ROW_TILE
    const = lambda shape: pl.BlockSpec(shape, lambda i: (0,) * len(shape))
    stril = jnp.asarray(np.tril(np.ones((ROW_TILE, ROW_TILE), np.float32), -1), dtype=BF16)
    in_specs = [*_pair_specs(d, npt, off_s=h_off_s), *_mod_specs(d, npt, tiles_per_seq)]
    args = [hp, hs, mp, ms]
    for ap, as_ in a_pairs:
        in_specs += list(_pair_specs(ap.shape[1], npt))
        args += [ap, as_]
    for w in ws:
        in_specs.append(const(w.shape))
        args.append(w)
    in_specs += [const((1, d)), const(rw.shape), const((1, LANES)), const(stril.shape)]
    args += [nffn, rw, rb, stril]
    tile = lambda c: pl.BlockSpec((ROW_TILE, c), lambda i: (i, 0))
    kern = functools.partial(_post_mixer_kernel, n_a=n_a, npt=npt)
    return pl.pallas_call(
        kern,
        out_shape=(jax.ShapeDtypeStruct((t, d), F32), jax.ShapeDtypeStruct((t, d), F32),
                   jax.ShapeDtypeStruct((t, LANES), F32), jax.ShapeDtypeStruct((1, LANES), F32)),
        grid=(npt + nst,),
        in_specs=in_specs,
        out_specs=(tile(d), tile(d), tile(LANES), const((1, LANES))),
        scratch_shapes=[pltpu.VMEM((1, LANES), F32)],
        compiler_params=_cparams(("arbitrary",)),
        name="post_mixer",
    )(*args)


def _invert_kernel(dest_ref, wlo_ref, whi_ref, tok_ref, slo_ref, shi_ref):
    n_slots = tok_ref.shape[0]
    n_tok = dest_ref.shape[0]

    def init(s, c):
        tok_ref[s] = 0
        slo_ref[s] = 0.0
        shi_ref[s] = 0.0
        return c

    lax.fori_loop(0, n_slots, init, 0, unroll=8)

    def body(t, c):
        s = dest_ref[t]
        tok_ref[s] = t
        slo_ref[s] = wlo_ref[t]
        shi_ref[s] = whi_ref[t]
        return c

    lax.fori_loop(0, n_tok, body, 0, unroll=8)


def _invert_perm(dest, w_lo, w_hi, n_slots):
    smem = pl.BlockSpec(memory_space=pltpu.SMEM)
    return pl.pallas_call(
        _invert_kernel,
        out_shape=(jax.ShapeDtypeStruct((n_slots,), I32), jax.ShapeDtypeStruct((n_slots,), F32),
                   jax.ShapeDtypeStruct((n_slots,), F32)),
        in_specs=[smem, smem, smem],
        out_specs=(smem, smem, smem),
        name="invert_perm",
    )(dest, w_lo, w_hi)


def _row_copy(idx_ref, src_ref, dst_ref, sem, base, r):
    return pltpu.make_async_copy(src_ref.at[pl.ds(idx_ref[base + r], 1), :], dst_ref.at[pl.ds(r, 1), :], sem)


def _gather_rows(idx_ref, src_ref, dst_ref, sem, base, rows):
    def start(r, c):
        _row_copy(idx_ref, src_ref, dst_ref, sem, base, r).start()
        return c

    def wait(r, c):
        _row_copy(idx_ref, src_ref, dst_ref, sem, base, r).wait()
        return c

    lax.fori_loop(0, rows, start, 0, unroll=8)
    lax.fori_loop(0, rows, wait, 0, unroll=8)


def _gather_kernel(idx_ref, src_ref, o_ref, sem):
    rows = o_ref.shape[0]
    _gather_rows(idx_ref, src_ref, o_ref, sem, pl.program_id(0) * rows, rows)


def _gather_tokens(tok, src, n_slots):
    d = src.shape[1]
    grid_spec = pltpu.PrefetchScalarGridSpec(
        num_scalar_prefetch=1,
        grid=(n_slots // MOE_TILE,),
        in_specs=[pl.BlockSpec(memory_space=pl.ANY)],
        out_specs=pl.BlockSpec((MOE_TILE, d), lambda i, idx: (i, 0)),
        scratch_shapes=[pltpu.SemaphoreType.DMA(())],
    )
    return pl.pallas_call(
        _gather_kernel,
        out_shape=jax.ShapeDtypeStruct((n_slots, d), F32),
        grid_spec=grid_spec,
        compiler_params=_cparams(("arbitrary",)),
        name="moe_gather",
    )(tok, src)


def _moe_kernel(ea_ref, eb_ref, valid_ref, fresh_ref, x_ref, wlo_ref, whi_ref,
                w1a_ref, w3a_ref, w2a_ref, w1b_ref, w3b_ref, w2b_ref, o_ref, wbuf1, wbuf3, wbuf2):
    i = pl.program_id(0)

    @pl.when(fresh_ref[i] == 1)
    def _():
        wbuf1[0] = w1a_ref[0, 0].astype(BF16)
        wbuf3[0] = w3a_ref[0, 0].astype(BF16)
        wbuf2[0] = w2a_ref[0, 0].astype(BF16)
        wbuf1[1] = w1b_ref[0, 0].astype(BF16)
        wbuf3[1] = w3b_ref[0, 0].astype(BF16)
        wbuf2[1] = w2b_ref[0, 0].astype(BF16)

    @pl.when(valid_ref[i] == 1)
    def _():
        xb = x_ref[...].astype(BF16)

        def expert(s):
            h1 = _dot(xb, wbuf1[s])
            h3 = _dot(xb, wbuf3[s])
            return _dot(((h1 * _sigmoid(h1)) * h3).astype(BF16), wbuf2[s])

        o_ref[...] = wlo_ref[...] * expert(0) + whi_ref[...] * expert(1)

    @pl.when(valid_ref[i] == 0)
    def _():
        o_ref[...] = jnp.zeros_like(o_ref)


def _moe_experts(tile_ea, tile_eb, tile_valid, tile_fresh, xs, s_lo, s_hi, w1, w3, w2, layer):
    n_slots, d = xs.shape
    ff = w1.shape[-1]
    wa = lambda shape: pl.BlockSpec((1, 1) + shape, lambda i, ea, eb, va, fr: (layer, ea[i], 0, 0))
    wb = lambda shape: pl.BlockSpec((1, 1) + shape, lambda i, ea, eb, va, fr: (layer, eb[i], 0, 0))
    rows = lambda c: pl.BlockSpec((MOE_TILE, c), lambda i, ea, eb, va, fr: (i, 0))
    grid_spec = pltpu.PrefetchScalarGridSpec(
        num_scalar_prefetch=4,
        grid=(n_slots // MOE_TILE,),
        in_specs=[rows(d), rows(1), rows(1),
                  wa((d, ff)), wa((d, ff)), wa((ff, d)), wb((d, ff)), wb((d, ff)), wb((ff, d))],
        out_specs=rows(d),
        scratch_shapes=[pltpu.VMEM((2, d, ff), BF16), pltpu.VMEM((2, d, ff), BF16), pltpu.VMEM((2, ff, d), BF16)],
    )
    return pl.pallas_call(
        _moe_kernel,
        out_shape=jax.ShapeDtypeStruct((n_slots, d), F32),
        grid_spec=grid_spec,
        compiler_params=_cparams(("arbitrary",)),
        name="moe_experts",
    )(tile_ea, tile_eb, tile_valid, tile_fresh, xs, s_lo.reshape(n_slots, 1), s_hi.reshape(n_slots, 1),
      w1, w3, w2, w1, w3, w2)


def _moe_plan(rinfo, counts, n_slots):
    n_tiles = n_slots // MOE_TILE
    cnt = counts[0, :N_BUCKETS].astype(I32)
    padded = ((cnt + MOE_TILE - 1) // MOE_TILE) * MOE_TILE
    ends = jnp.cumsum(padded)
    starts = ends - padded
    bucket = rinfo[:, 0].astype(I32)
    dest = starts[bucket] + rinfo[:, 3].astype(I32)
    n_valid = ends[-1] // MOE_TILE
    tiles = jnp.arange(n_tiles, dtype=I32)
    tb = jnp.minimum(jnp.searchsorted(ends, tiles * MOE_TILE, side="right").astype(I32), N_BUCKETS - 1)
    valid = tiles < n_valid
    tb = jnp.where(valid, tb, tb[jnp.maximum(n_valid - 1, 0)])
    lo = jnp.asarray(PAIR_LO, I32)
    hi = jnp.asarray(PAIR_HI, I32)
    ea = (tb // 6) * EXPERTS_PER_GROUP + lo[tb % 6]
    eb = (tb // 6) * EXPERTS_PER_GROUP + hi[tb % 6]
    fresh = jnp.concatenate([jnp.ones((1,), I32), (tb[1:] != tb[:-1]).astype(I32)])
    return dest, ea, eb, valid.astype(I32), fresh


def _moe(u, rinfo, counts, w1, w3, w2, layer):
    t = u.shape[0]
    n_slots = t + N_BUCKETS * MOE_TILE
    dest, ea, eb, valid, fresh = _moe_plan(rinfo, counts, n_slots)
    tok, s_lo, s_hi = _invert_perm(dest, rinfo[:, 1], rinfo[:, 2], n_slots)
    xs = _gather_tokens(tok, u, n_slots)
    ys = _moe_experts(ea, eb, valid, fresh, xs, s_lo, s_hi, w1, w3, w2, layer)
    return dest, ys


def _combine_kernel(dest_ref, ys_ref, hm_ref, mp_ref, ms_ref, nf_ref, *rest, npt, final):
    if final:
        yp_ref, ysm_ref, gbuf, sem = rest
    else:
        h_ref, gbuf, sem = rest
    i = pl.program_id(0)
    rows = gbuf.shape[0]
    _gather_rows(dest_ref, ys_ref, gbuf, sem, i * rows, rows)

    def run(is_prompt):
        m3 = (mp_ref if is_prompt else ms_ref)[...]
        h = _gate_residual(hm_ref[...], m3, 5, gbuf[...])
        if final:
            y = h * lax.rsqrt(jnp.mean(h * h, axis=-1, keepdims=True) + NORM_EPS) * nf_ref[...]
            (yp_ref if is_prompt else ysm_ref)[...] = y
        else:
            h_ref[...] = h

    _by_group(i, npt, run)


def _combine(dest, ys, hmid, mp, ms, nf, *, tp, seq, final):
    t, d = hmid.shape
    npt = tp // ROW_TILE
    tiles_per_seq = seq // ROW_TILE
    mpm = lambda f: (lambda i, dref: f(i))
    mp_spec, ms_spec = _mod_specs(d, npt, tiles_per_seq)
    yp_spec, ysm_spec = _pair_specs(d, npt)
    wrap = lambda s: pl.BlockSpec(s.block_shape, mpm(s.index_map))
    if final:
        out_shape = (jax.ShapeDtypeStruct((tp, d), F32), jax.ShapeDtypeStruct((t - tp, d), F32))
        out_specs = (wrap(yp_spec), wrap(ysm_spec))
    else:
        out_shape = jax.ShapeDtypeStruct((t, d), F32)
        out_specs = pl.BlockSpec((ROW_TILE, d), lambda i, dref: (i, 0))
    grid_spec = pltpu.PrefetchScalarGridSpec(
        num_scalar_prefetch=1,
        grid=(t // ROW_TILE,),
        in_specs=[pl.BlockSpec(memory_space=pl.ANY),
                  pl.BlockSpec((ROW_TILE, d), lambda i, dref: (i, 0)),
                  wrap(mp_spec), wrap(ms_spec),
                  pl.BlockSpec((1, d), lambda i, dref: (0, 0))],
        out_specs=out_specs,
        scratch_shapes=[pltpu.VMEM((ROW_TILE, d), F32), pltpu.SemaphoreType.DMA(())],
    )
    kern = functools.partial(_combine_kernel, npt=npt, final=final)
    return pl.pallas_call(
        kern, out_shape=out_shape, grid_spec=grid_spec,
        compiler_params=_cparams(("arbitrary",)),
        name="moe_combine_final" if final else "moe_combine",
    )(dest, ys, hmid, mp, ms, nf)


def _odd_pre_kernel(hp_ref, hs_ref, mp_ref, ms_ref, nw_ref, w_ref, gate_ref, rec_ref, *, npt):
    i = pl.program_id(0)
    width = gate_ref.shape[1]

    def run(is_prompt):
        h = (hp_ref if is_prompt else hs_ref)[...]
        m3 = (mp_ref if is_prompt else ms_ref)[...]
        ub = _norm_mod(h, nw_ref[...], m3, 1, 0).astype(BF16)
        x = _dot(ub, w_ref[:, 0:width])
        gate_ref[...] = 0.5 * x * (1.0 + jnp.tanh(np.sqrt(2.0 / np.pi) * (x + 0.044715 * (x * x * x))))
        rec_ref[...] = _dot(ub, w_ref[:, width:2 * width])

    _by_group(i, npt, run)


def _odd_pre(h, mp, ms, nw, w_in, *, tp, seq):
    t, d = h.shape
    npt = tp // ROW_TILE
    width = w_in.shape[1] // 2
    const = lambda shape: pl.BlockSpec(shape, lambda i: (0,) * len(shape))
    tile = lambda c: pl.BlockSpec((ROW_TILE, c), lambda i: (i, 0))
    kern = functools.partial(_odd_pre_kernel, npt=npt)
    return pl.pallas_call(
        kern,
        out_shape=(jax.ShapeDtypeStruct((t, width), F32), jax.ShapeDtypeStruct((t, width), F32)),
        grid=(t // ROW_TILE,),
        in_specs=[*_pair_specs(d, npt, off_s=npt), *_mod_specs(d, npt, seq // ROW_TILE),
                  const((1, d)), const(w_in.shape)],
        out_specs=(tile(width), tile(width)),
        compiler_params=_cparams(("arbitrary",)),
        name="odd_pre",
    )(h, h, mp, ms, nw, w_in)


def _lru_kernel(rec_ref, gate_ref, c0_ref, h0_ref, cw_ref, cb_ref, wa_ref, ba_ref, wx_ref, bx_ref, lam_ref,
                yg_ref, cs_ref, hl_ref, hist, hc, a_s, h_s, *, nb, tl):
    i = pl.program_id(1)
    r, width = rec_ref.shape
    taps = CONV_WIDTH
    blk = width // LRU_HEADS

    @pl.when(i == 0)
    def _():
        hist[:, SUBLANES - (taps - 1):SUBLANES, :] = c0_ref[...]
        hc[...] = h0_ref[...]

    rec3 = rec_ref[...].reshape(nb, tl, width)
    hist[:, SUBLANES:SUBLANES + tl, :] = rec3
    cw = cw_ref[...]
    conv = cb_ref[...] + rec3 * cw[taps - 1:taps, :]
    for back in range(1, taps):
        conv = conv + hist[:, SUBLANES - back:SUBLANES - back + tl, :] * cw[taps - 1 - back:taps - back, :]
    tail = hist[:, SUBLANES + tl - (taps - 1):SUBLANES + tl, :]
    cs_ref[...] = tail
    hist[:, SUBLANES - (taps - 1):SUBLANES, :] = tail

    cf = conv.reshape(r, width)
    cb16 = cf.astype(BF16)
    gr = jnp.concatenate([_dot(cb16[:, h * blk:(h + 1) * blk], wa_ref[h]) for h in range(LRU_HEADS)], axis=1)
    gi = jnp.concatenate([_dot(cb16[:, h * blk:(h + 1) * blk], wx_ref[h]) for h in range(LRU_HEADS)], axis=1)
    rg = _sigmoid(gr + ba_ref[...])
    ig = _sigmoid(gi + bx_ref[...])
    z = -lam_ref[...]
    softplus = jnp.maximum(z, 0.0) + jnp.log1p(jnp.exp(-jnp.abs(z)))
    log_a = -LRU_C * rg * softplus
    a = jnp.exp(log_a)
    gx = jnp.sqrt(1.0 - a * a) * (ig * cf)

    sub = lax.broadcasted_iota(I32, (r, width), 0) % SUBLANES
    aa, hh = a, gx
    for dlt in (1, 2, 4):
        ok = sub >= dlt
        a_sh = pltpu.roll(aa, dlt, 0)
        h_sh = pltpu.roll(hh, dlt, 0)
        hh = jnp.where(ok, hh + aa * h_sh, hh)
        aa = jnp.where(ok, aa * a_sh, aa)

    if tl == SUBLANES:
        y3 = hh.reshape(nb, tl, width) + aa.reshape(nb, tl, width) * hc[...]
        hl_ref[...] = y3[:, tl - 1:tl, :]
        y = y3.reshape(r, width)
    else:
        a_s[...] = aa
        h_s[...] = hh

        def group(j, carry):
            r0 = pl.multiple_of(j * SUBLANES, SUBLANES)
            yj = h_s[pl.ds(r0, SUBLANES), :] + a_s[pl.ds(r0, SUBLANES), :] * carry
            h_s[pl.ds(r0, SUBLANES), :] = yj
            return yj[SUBLANES - 1:SUBLANES, :]

        last = lax.fori_loop(0, r // SUBLANES, group, hc[0])
        hc[0] = last
        hl_ref[0] = last
        y = h_s[...]
    yg_ref[...] = (y * gate_ref[...]).astype(BF16)


def _lru(rec, gate, conv0, h0, cw, cb, wa, ba, wx, bx, lam, *, batch, seq, row0):
    width = rec.shape[1]
    taps = CONV_WIDTH
    if seq == SUBLANES:
        nb, tl = ROW_TILE // SUBLANES, SUBLANES
    else:
        nb, tl = 1, ROW_TILE
    r = nb * tl
    n_l = seq // tl
    blk0 = row0 // r
    const = lambda shape: pl.BlockSpec(shape, lambda b, i: (0,) * len(shape))
    tile_in = pl.BlockSpec((r, width), lambda b, i: (blk0 + b * n_l + i, 0))
    kern = functools.partial(_lru_kernel, nb=nb, tl=tl)
    return pl.pallas_call(
        kern,
        out_shape=(jax.ShapeDtypeStruct((batch * seq, width), BF16),
                   jax.ShapeDtypeStruct((batch, taps - 1, width), F32),
                   jax.ShapeDtypeStruct((batch, 1, width), F32)),
        grid=(batch // nb, n_l),
        in_specs=[tile_in, tile_in,
                  pl.BlockSpec((nb, taps - 1, width), lambda b, i: (b, 0, 0)),
                  pl.BlockSpec((nb, 1, width), lambda b, i: (b, 0, 0)),
                  const((taps, width)), const((1, width)), const(wa.shape), const((1, width)),
                  const(wx.shape), const((1, width)), const((1, width))],
        out_specs=(pl.BlockSpec((r, width), lambda b, i: (b * n_l + i, 0)),
                   pl.BlockSpec((nb, taps - 1, width), lambda b, i: (b, 0, 0)),
                   pl.BlockSpec((nb, 1, width), lambda b, i: (b, 0, 0))),
        scratch_shapes=[pltpu.VMEM((nb, tl + SUBLANES, width), F32), pltpu.VMEM((nb, 1, width), F32),
                        pltpu.VMEM((r, width), F32), pltpu.VMEM((r, width), F32)],
        compiler_params=_cparams(("arbitrary", "arbitrary")),
        name="rg_lru_sample" if seq == SUBLANES else "rg_lru_prompt",
    )(rec, gate, conv0, h0, cw, cb, wa, ba, wx, bx, lam)


def kernel(x_prompt, x_sample, c_prompt, c_sample, cache_k, cache_v, state_hgrn, state_conv, state_lru, page_table,
           ada_w, ada_b, norm_mix, norm_ffn, norm_final, w_in_even, w_out_even, hg_lower_bounds, hg_norm_w, w_in_odd,
           conv_w, conv_b, lru_wa, lru_ba, lru_wx, lru_bx, lru_lambda, w_out_odd, router_group_w, router_group_b,
           router_expert_w, router_expert_b, moe_w1, moe_w3, moe_w2):
    batch, seq, d = x_prompt.shape
    dec_batch, dec_seq, _ = x_sample.shape
    depth = ada_w.shape[0]
    assert depth == 2 and seq % ROW_TILE == 0 and seq % MOBA_BLOCK == 0
    tp, ts = batch * seq, dec_batch * dec_seq
    n_pages, page = page_table.shape[1], cache_k.shape[2]
    past_len = n_pages * page
    aw = MOBA_HEADS * MOBA_HEAD_DIM
    hw = HG_HEADS * HG_DK

    xp = x_prompt.reshape(tp, d)
    xs = x_sample.reshape(ts, d)
    mods = _ada_mods(jnp.concatenate([c_prompt, c_sample], axis=0), ada_w, ada_b)
    mods = mods.reshape(depth, batch + dec_batch, 6, d)
    mods_p, mods_s = mods[:, :batch], mods[:, batch:]

    half = MOBA_HEAD_DIM // 2
    inv_freq = jnp.power(ROPE_THETA, -jnp.arange(half, dtype=F32) / half)
    invf = jnp.tile(inv_freq, LANES // half).reshape(1, LANES)

    def router_mats(l):
        rw = jnp.zeros((d, LANES), F32)
        rw = rw.at[:, 0:N_GROUPS].set(router_group_w[l])
        rw = rw.at[:, N_GROUPS:N_GROUPS + N_GROUPS * EXPERTS_PER_GROUP].set(router_expert_w[l])
        rb = jnp.zeros((1, LANES), F32)
        rb = rb.at[0, 0:N_GROUPS].set(router_group_b[l])
        rb = rb.at[0, N_GROUPS:N_GROUPS + N_GROUPS * EXPERTS_PER_GROUP].set(router_expert_b[l])
        return rw, rb

    q_all, k_p, k_s, v_p, v_s, hg = _even_pre(
        xp, xs, mods_p[0], mods_s[0], norm_mix[0].reshape(1, d), w_in_even[0].astype(BF16), hg_lower_bounds, invf,
        seq=seq, past_len=past_len, layer_slot=0)
    oa_p = _attn_prompt(q_all, k_p, v_p, batch=batch, seq=seq)
    n_phys = cache_k.shape[1]
    oa_s = _attn_sample(page_table, q_all, k_s, v_s, cache_k[0].reshape(n_phys, page, aw),
                        cache_v[0].reshape(n_phys, page, aw), dec_batch=dec_batch, dec_seq=dec_seq, q_row0=tp)
    hnw = hg_norm_w[0].reshape(1, HG_DK)
    ob_p, hgrn_p = _hgrn_prompt(hg, hnw, batch=batch, seq=seq)
    ob_s, hgrn_s = _hgrn_sample(hg, state_hgrn[0].reshape(dec_batch * HG_HEADS, HG_DK, HG_DK), hnw,
                                dec_batch=dec_batch, dec_seq=dec_seq, row0=tp)
    w_out = w_out_even[0].astype(BF16)
    rw, rb = router_mats(0)
    hmid, u, rinfo, counts = _post_mixer((xp, xs), 0, mods_p[0], mods_s[0], [(oa_p, oa_s), (ob_p, ob_s)],
                                         [w_out[:aw], w_out[aw:]], norm_ffn[0].reshape(1, d), rw, rb, seq=seq)
    dest, ys = _moe(u, rinfo, counts, moe_w1, moe_w3, moe_w2, 0)
    h1 = _combine(dest, ys, hmid, mods_p[0], mods_s[0], norm_final.reshape(1, d), tp=tp, seq=seq, final=False)

    gate, rec = _odd_pre(h1, mods_p[1], mods_s[1], norm_mix[1].reshape(1, d), w_in_odd[0].astype(BF16), tp=tp, seq=seq)
    width = rec.shape[1]
    lru_args = (conv_w[0], conv_b[0].reshape(1, width), lru_wa[0].astype(BF16), lru_ba[0].reshape(1, width),
                lru_wx[0].astype(BF16), lru_bx[0].reshape(1, width), lru_lambda[0].reshape(1, width))
    yg_p, conv_p, lru_p = _lru(rec, gate, jnp.zeros((batch, CONV_WIDTH - 1, width), F32),
                               jnp.zeros((batch, 1, width), F32), *lru_args, batch=batch, seq=seq, row0=0)
    yg_s, conv_s, lru_s = _lru(rec, gate, state_conv[0], state_lru[0].reshape(dec_batch, 1, width), *lru_args,
                               batch=dec_batch, seq=dec_seq, row0=tp)
    rw, rb = router_mats(1)
    hmid, u, rinfo, counts = _post_mixer((h1, h1), tp // ROW_TILE, mods_p[1], mods_s[1], [(yg_p, yg_s)],
                                         [w_out_odd[0].astype(BF16)], norm_ffn[1].reshape(1, d), rw, rb, seq=seq)
    dest, ys = _moe(u, rinfo, counts, moe_w1, moe_w3, moe_w2, 1)
    y_p, y_s = _combine(dest, ys, hmid, mods_p[1], mods_s[1], norm_final.reshape(1, d), tp=tp, seq=seq, final=True)

    return (y_p.reshape(batch, seq, d), y_s.reshape(dec_batch, dec_seq, d),
            k_p.reshape(1, batch, seq, MOBA_HEADS, MOBA_HEAD_DIM), v_p.reshape(1, batch, seq, MOBA_HEADS, MOBA_HEAD_DIM),
            k_s.reshape(1, dec_batch, dec_seq, MOBA_HEADS, MOBA_HEAD_DIM),
            v_s.reshape(1, dec_batch, dec_seq, MOBA_HEADS, MOBA_HEAD_DIM),
            hgrn_p.reshape(1, batch, HG_HEADS, HG_DK, HG_DK), hgrn_s.reshape(1, dec_batch, HG_HEADS, HG_DK, HG_DK),
            conv_p.reshape(1, batch, CONV_WIDTH - 1, width), conv_s.reshape(1, dec_batch, CONV_WIDTH - 1, width),
            lru_p.reshape(1, batch, width), lru_s.reshape(1, dec_batch, width))
```

```python
import functools

import numpy as np
import jax
import jax.numpy as jnp
from jax import lax
from jax.experimental import pallas as pl
from jax.experimental.pallas import tpu as pltpu

F32 = jnp.float32
BF16 = jnp.bfloat16
I32 = jnp.int32
HIGHEST = lax.Precision.HIGHEST

MOBA_HEADS = 8
MOBA_HEAD_DIM = 64
MOBA_BLOCK = 256
MOBA_TOPK = 3
ROPE_THETA = 10000.0
HG_HEADS = 4
HG_DK = 128
LRU_HEADS = 4
CONV_WIDTH = 4
LRU_C = 8.0
N_GROUPS = 4
EXPERTS_PER_GROUP = 4
NORM_EPS = 1e-6

LANES = 128
SUBLANES = 8
VMEM_LIMIT = 56 * 1024 * 1024

ROW_TILE = 256
MOE_TILE = 256
HG_CHUNK = 128
N_BUCKETS = N_GROUPS * 6
PAIR_LO = (0, 0, 0, 1, 1, 2)
PAIR_HI = (1, 2, 3, 2, 3, 3)
NEG = -1e30


def _cparams(sem, vmem=VMEM_LIMIT):
    return pltpu.CompilerParams(dimension_semantics=sem, vmem_limit_bytes=vmem)


def _dot(a, b, **kw):
    return jnp.dot(a, b, preferred_element_type=F32, **kw)


def _dot_nt(a, b, **kw):
    return lax.dot_general(a, b, (((1,), (1,)), ((), ())), preferred_element_type=F32, **kw)


def _sigmoid(x):
    return 1.0 / (1.0 + jnp.exp(-x))


def _by_group(i, n_prompt_tiles, fn):
    @pl.when(i < n_prompt_tiles)
    def _():
        fn(True)

    @pl.when(i >= n_prompt_tiles)
    def _():
        fn(False)


def _pair_specs(cols, npt, rows=ROW_TILE, off_p=0, off_s=0):
    return (pl.BlockSpec((rows, cols), lambda i: (off_p + jnp.minimum(i, npt - 1), 0)),
            pl.BlockSpec((rows, cols), lambda i: (off_s + jnp.maximum(i - npt, 0), 0)))


def _mod_specs(d, npt, tiles_per_seq, rows=ROW_TILE):
    return (pl.BlockSpec((1, 6, d), lambda i: (jnp.minimum(i, npt - 1) // tiles_per_seq, 0, 0)),
            pl.BlockSpec((rows // SUBLANES, 6, d), lambda i: (jnp.maximum(i - npt, 0), 0, 0)))


def _norm_mod(x, nw, m3, scale_idx, shift_idx):
    r, d = x.shape
    nb = m3.shape[0]
    var = jnp.mean(x * x, axis=-1, keepdims=True)
    y = x * lax.rsqrt(var + NORM_EPS) * nw
    y3 = y.reshape(nb, r // nb, d)
    u3 = y3 * (1.0 + m3[:, scale_idx:scale_idx + 1, :]) + m3[:, shift_idx:shift_idx + 1, :]
    return u3.reshape(r, d)


def _gate_residual(h, m3, gate_idx, out):
    r, d = h.shape
    nb = m3.shape[0]
    return (h.reshape(nb, r // nb, d) + m3[:, gate_idx:gate_idx + 1, :] * out.reshape(nb, r // nb, d)).reshape(r, d)


def _ada_kernel(c_ref, w_ref, b_ref, o_ref):
    o_ref[0] = _dot(c_ref[...], w_ref[0], precision=HIGHEST) + b_ref[0]


def _ada_mods(c_all, ada_w, ada_b):
    depth, d, n6 = ada_w.shape
    nb = c_all.shape[0]
    tn = 1024
    return pl.pallas_call(
        _ada_kernel,
        out_shape=jax.ShapeDtypeStruct((depth, nb, n6), F32),
        grid=(depth, n6 // tn),
        in_specs=[pl.BlockSpec((nb, d), lambda l, n: (0, 0)),
                  pl.BlockSpec((1, d, tn), lambda l, n: (l, 0, n)),
                  pl.BlockSpec((1, 1, tn), lambda l, n: (l, 0, n))],
        out_specs=pl.BlockSpec((1, nb, tn), lambda l, n: (l, 0, n)),
        compiler_params=_cparams(("arbitrary", "arbitrary")),
        name="ada_mods",
    )(c_all, ada_w, ada_b.reshape(depth, 1, n6))


def _even_pre_kernel(xp_ref, xs_ref, mp_ref, ms_ref, nw_ref, w_ref, hb_ref, invf_ref,
                     q_ref, kp_ref, ks_ref, vp_ref, vs_ref, hg_ref, *, npt, tiles_per_seq, past_len, layer_slot):
    i = pl.program_id(0)
    aw = MOBA_HEADS * MOBA_HEAD_DIM
    hw = HG_HEADS * HG_DK

    def run(is_prompt):
        x = (xp_ref if is_prompt else xs_ref)[...]
        m3 = (mp_ref if is_prompt else ms_ref)[...]
        r = x.shape[0]
        nb = m3.shape[0]
        tl = r // nb
        ub = _norm_mod(x, nw_ref[...], m3, 1, 0).astype(BF16)

        pos0 = (i % tiles_per_seq) * r if is_prompt else past_len
        pos = (pos0 + lax.broadcasted_iota(I32, (nb, tl, LANES), 1)).astype(F32).reshape(r, LANES)
        ang = pos * invf_ref[...]
        reps = aw // LANES
        cos = jnp.concatenate([jnp.cos(ang)] * reps, axis=1)
        sin = jnp.concatenate([jnp.sin(ang)] * reps, axis=1)
        lane = lax.broadcasted_iota(I32, (r, aw), 1)
        half = MOBA_HEAD_DIM // 2
        first = (lane % MOBA_HEAD_DIM) < half
        sin = jnp.where(first, -sin, sin)

        def rope(t):
            rot = jnp.where(first, pltpu.roll(t, aw - half, 1), pltpu.roll(t, half, 1))
            return t * cos + rot * sin

        def seg(a, b):
            return _dot(ub, w_ref[:, a:b])

        q_ref[...] = rope(seg(0, aw))
        k = rope(seg(aw, 2 * aw))
        v = seg(2 * aw, 3 * aw)
        if is_prompt:
            kp_ref[...] = k
            vp_ref[...] = v
        else:
            ks_ref[...] = k
            vs_ref[...] = v
        o = 3 * aw
        qb = seg(o, o + hw)
        fb = seg(o + hw, o + 2 * hw)
        hb = hb_ref[...]
        e = jnp.exp(hb - jnp.max(hb, axis=0, keepdims=True))
        sm = e / jnp.sum(e, axis=0, keepdims=True)
        lb = jnp.sum(sm[0:layer_slot + 1, :], axis=0, keepdims=True)
        hg_ref[:, 0:hw] = qb * _sigmoid(qb)
        hg_ref[:, hw:2 * hw] = lb + (1.0 - lb) * _sigmoid(fb)
        hg_ref[:, 2 * hw:3 * hw] = seg(o + 2 * hw, o + 3 * hw)
        hg_ref[:, 3 * hw:4 * hw] = seg(o + 3 * hw, o + 4 * hw)

    _by_group(i, npt, run)


def _even_pre(xp, xs, mp, ms, nw, w_in, hb, invf, *, seq, past_len, layer_slot):
    tp, d = xp.shape
    ts = xs.shape[0]
    npt, nst = tp // ROW_TILE, ts // ROW_TILE
    tiles_per_seq = seq // ROW_TILE
    aw = MOBA_HEADS * MOBA_HEAD_DIM
    hw = HG_HEADS * HG_DK
    t = tp + ts
    const = lambda shape: pl.BlockSpec(shape, lambda i: (0,) * len(shape))
    kern = functools.partial(_even_pre_kernel, npt=npt, tiles_per_seq=tiles_per_seq, past_len=past_len,
                             layer_slot=layer_slot)
    kp_spec, ks_spec = _pair_specs(aw, npt)
    return pl.pallas_call(
        kern,
        out_shape=(jax.ShapeDtypeStruct((t, aw), F32),
                   jax.ShapeDtypeStruct((tp, aw), F32), jax.ShapeDtypeStruct((ts, aw), F32),
                   jax.ShapeDtypeStruct((tp, aw), F32), jax.ShapeDtypeStruct((ts, aw), F32),
                   jax.ShapeDtypeStruct((t, 4 * hw), F32)),
        grid=(npt + nst,),
        in_specs=[*_pair_specs(d, npt), *_mod_specs(d, npt, tiles_per_seq),
                  const((1, d)), const(w_in.shape), const(hb.shape), const((1, LANES))],
        out_specs=(pl.BlockSpec((ROW_TILE, aw), lambda i: (i, 0)), kp_spec, ks_spec, kp_spec, ks_spec,
                   pl.BlockSpec((ROW_TILE, 4 * hw), lambda i: (i, 0))),
        compiler_params=_cparams(("arbitrary",)),
        name="even_pre",
    )(xp, xs, mp, ms, nw, w_in, hb, invf)


def _top_blocks(g, valid):
    lane = lax.broadcasted_iota(I32, g.shape, 1)
    jl = lane % SUBLANES
    g = jnp.where(valid, g, -jnp.inf)
    cnt = jnp.zeros(g.shape, I32)
    for d in range(1, SUBLANES):
        wrapped = (jl + d) >= SUBLANES
        pg = jnp.where(wrapped, pltpu.roll(g, SUBLANES - d, 1), pltpu.roll(g, LANES - d, 1))
        beats = (pg > g) | ((pg == g) & wrapped)
        cnt = cnt + beats.astype(I32)
    return valid & (cnt < MOBA_TOPK)


def _head_expand(rows8, n_heads, width):
    x = jnp.concatenate([rows8] * n_heads, axis=0)
    r = lax.broadcasted_iota(I32, x.shape, 0) // SUBLANES
    l = lax.broadcasted_iota(I32, x.shape, 1) // (width // n_heads)
    return jnp.where(r == l, x, 0.0)


def _block_indicator(seq):
    key_blk = np.arange(seq)[:, None] // MOBA_BLOCK
    lane = np.arange(LANES)[None, :]
    ind = (lane < MOBA_HEADS * SUBLANES) & ((lane % SUBLANES) == key_blk)
    return jnp.asarray(ind.astype(np.float32), dtype=BF16)


def _attn_prompt_kernel(q_ref, k_ref, v_ref, ind_ref, o_ref, km_ref, *, n_blk):
    i = pl.program_id(1)
    blk = MOBA_BLOCK
    aw = MOBA_HEADS * MOBA_HEAD_DIM
    scale = MOBA_HEAD_DIM ** -0.5

    @pl.when(i == 0)
    def _():
        for j in range(n_blk):
            km_ref[j:j + 1, :] = jnp.mean(k_ref[j * blk:(j + 1) * blk, :], axis=0, keepdims=True)

    q = q_ref[...]
    kmexp = _head_expand(km_ref[...], MOBA_HEADS, aw)
    kmexp = jnp.concatenate([kmexp, jnp.zeros_like(kmexp)], axis=0)
    gate = _dot_nt(q, kmexp, precision=HIGHEST)
    lane = lax.broadcasted_iota(I32, gate.shape, 1)
    jl = lane % SUBLANES
    valid = (jl < i) & (lane < MOBA_HEADS * SUBLANES)
    keep = _top_blocks(gate, valid) | (jl == i)
    bias = jnp.where(keep, 0.0, NEG)

    row = lax.broadcasted_iota(I32, (blk, blk), 0)
    col = lax.broadcasted_iota(I32, (blk, blk), 1)
    tril = row >= col

    def tile(c):
        n_keys = (c + 1) * blk
        for hp in range(aw // LANES):
            cols = slice(hp * LANES, (hp + 1) * LANES)
            rhs = jnp.concatenate([k_ref[0:n_keys, cols].astype(BF16), ind_ref[0:n_keys, :]], axis=1)
            vv = v_ref[0:n_keys, cols].astype(BF16)
            qp = q[:, cols] * scale
            outs = []
            for s in (0, 1):
                h = 2 * hp + s
                qm = jnp.where((lane // MOBA_HEAD_DIM) == s, qp, 0.0).astype(BF16)
                hb = jnp.where((lane // SUBLANES) == h, bias, 0.0).astype(BF16)
                sc = _dot_nt(jnp.concatenate([qm, hb], axis=1), rhs)
                own = jnp.where(tril, sc[:, c * blk:], NEG)
                sc = own if c == 0 else jnp.concatenate([sc[:, :c * blk], own], axis=1)
                p = jnp.exp(sc - jnp.max(sc, axis=1, keepdims=True))
                outs.append(_dot(p.astype(BF16), vv) / jnp.sum(p, axis=1, keepdims=True))
            o_ref[:, cols] = jnp.where((lane // MOBA_HEAD_DIM) == 0, outs[0], outs[1]).astype(BF16)

    for c in range(n_blk):
        pl.when(i == c)(functools.partial(tile, c))


def _attn_prompt(q_all, k_p, v_p, *, batch, seq):
    aw = MOBA_HEADS * MOBA_HEAD_DIM
    n_blk = seq // MOBA_BLOCK
    assert n_blk <= SUBLANES
    ind = _block_indicator(seq)
    kern = functools.partial(_attn_prompt_kernel, n_blk=n_blk)
    return pl.pallas_call(
        kern,
        out_shape=jax.ShapeDtypeStruct((batch * seq, aw), BF16),
        grid=(batch, n_blk),
        in_specs=[pl.BlockSpec((MOBA_BLOCK, aw), lambda b, i: (b * n_blk + i, 0)),
                  pl.BlockSpec((seq, aw), lambda b, i: (b, 0)),
                  pl.BlockSpec((seq, aw), lambda b, i: (b, 0)),
                  pl.BlockSpec((seq, LANES), lambda b, i: (0, 0))],
        out_specs=pl.BlockSpec((MOBA_BLOCK, aw), lambda b, i: (b * n_blk + i, 0)),
        scratch_shapes=[pltpu.VMEM((SUBLANES, aw), F32)],
        compiler_params=_cparams(("arbitrary", "arbitrary")),
        name="moba_prompt",
    )(q_all, k_p, v_p, ind)


def _attn_sample_kernel(pt_ref, q_ref, kn_ref, vn_ref, ck_ref, cv_ref, o_ref, kbuf, vbuf, sem,
                        *, n_pages, page, n_blk):
    b = pl.program_id(0)
    nb = pl.num_programs(0)
    blk = MOBA_BLOCK
    hd = MOBA_HEAD_DIM
    dec = q_ref.shape[0]
    scale = hd ** -0.5

    def page_copies(seq_idx, slot):
        cps = []
        for p in range(n_pages):
            pg = pt_ref[seq_idx * n_pages + p]
            cps.append(pltpu.make_async_copy(ck_ref.at[pg], kbuf.at[slot, pl.ds(p * page, page)], sem.at[0, slot]))
            cps.append(pltpu.make_async_copy(cv_ref.at[pg], vbuf.at[slot, pl.ds(p * page, page)], sem.at[1, slot]))
        return cps

    @pl.when(b == 0)
    def _():
        for c in page_copies(0, 0):
            c.start()

    @pl.when(b + 1 < nb)
    def _():
        for c in page_copies(b + 1, (b + 1) % 2):
            c.start()

    slot = b % 2
    for c in page_copies(b, slot):
        c.wait()

    q = q_ref[...]
    kn = kn_ref[...]
    vn = vn_ref[...]
    rows = 2 * dec
    lane = lax.broadcasted_iota(I32, (rows, LANES), 1)
    qi = lax.broadcasted_iota(I32, (rows, LANES), 0)
    past = n_blk * blk
    key_blk = lax.broadcasted_iota(I32, (rows, past), 1) // blk
    zq = jnp.zeros((rows - dec, hd), F32)
    zk = jnp.zeros((LANES - dec, hd), F32)
    for h in range(MOBA_HEADS):
        cols = slice(h * hd, (h + 1) * hd)
        kh = kbuf[slot, :, h, :]
        vh = vbuf[slot, :, h, :]
        qh = jnp.concatenate([q[:, cols], zq], axis=0)
        km = jnp.mean(kh.reshape(n_blk, blk, hd), axis=1)
        km = jnp.concatenate([km, jnp.zeros((LANES - n_blk, hd), F32)], axis=0)
        gate = _dot_nt(qh, km, precision=HIGHEST)
        sel = _top_blocks(gate, lane < n_blk).astype(F32)
        qs = (qh * scale).astype(BF16)
        sc = _dot_nt(qs, kh.astype(BF16))
        mask = jnp.zeros(sc.shape, F32)
        for j in range(n_blk):
            mask = jnp.where(key_blk == j, sel[:, j:j + 1], mask)
        sc = jnp.where(mask > 0.5, sc, NEG)
        knh = jnp.concatenate([kn[:, cols], zk], axis=0).astype(BF16)
        vnh = jnp.concatenate([vn[:, cols], zk], axis=0).astype(BF16)
        sn = jnp.where(lane <= qi, _dot_nt(qs, knh), NEG)
        m = jnp.maximum(jnp.max(sc, axis=1, keepdims=True), jnp.max(sn, axis=1, keepdims=True))
        p = jnp.exp(sc - m)
        pn = jnp.exp(sn - m)
        l = jnp.sum(p, axis=1, keepdims=True) + jnp.sum(pn, axis=1, keepdims=True)
        o = (_dot(p.astype(BF16), vh.astype(BF16)) + _dot(pn.astype(BF16), vnh)) / l
        o_ref[:, cols] = o[0:dec]


def _attn_sample(page_table, q_all, k_s, v_s, cache_k, cache_v, *, dec_batch, dec_seq, q_row0):
    n_phys, page, n_heads, hd = cache_k.shape
    aw = MOBA_HEADS * MOBA_HEAD_DIM
    n_pages = page_table.shape[1]
    past_len = n_pages * page
    assert past_len % MOBA_BLOCK == 0 and dec_seq == SUBLANES and past_len // MOBA_BLOCK <= SUBLANES
    assert n_heads == MOBA_HEADS and hd == MOBA_HEAD_DIM
    kern = functools.partial(_attn_sample_kernel, n_pages=n_pages, page=page, n_blk=past_len // MOBA_BLOCK)
    q_blk0 = q_row0 // dec_seq
    grid_spec = pltpu.PrefetchScalarGridSpec(
        num_scalar_prefetch=1,
        grid=(dec_batch,),
        in_specs=[pl.BlockSpec((dec_seq, aw), lambda b, pt: (q_blk0 + b, 0)),
                  pl.BlockSpec((dec_seq, aw), lambda b, pt: (b, 0)),
                  pl.BlockSpec((dec_seq, aw), lambda b, pt: (b, 0)),
                  pl.BlockSpec(memory_space=pl.ANY),
                  pl.BlockSpec(memory_space=pl.ANY)],
        out_specs=pl.BlockSpec((dec_seq, aw), lambda b, pt: (b, 0)),
        scratch_shapes=[pltpu.VMEM((2, past_len, n_heads, hd), F32), pltpu.VMEM((2, past_len, n_heads, hd), F32),
                        pltpu.SemaphoreType.DMA((2, 2))],
    )
    return pl.pallas_call(
        kern,
        out_shape=jax.ShapeDtypeStruct((dec_batch * dec_seq, aw), F32),
        grid_spec=grid_spec,
        compiler_params=_cparams(("arbitrary",)),
        name="moba_sample",
    )(page_table.reshape(-1), q_all, k_s, v_s, cache_k, cache_v)


def _hgrn_levels():
    n = HG_CHUNK
    return int(np.log2(n))


def _hgrn_cumsum_matrix():
    c = HG_CHUNK
    t = np.arange(c)[:, None]
    s = np.arange(c)[None, :]
    mats = [(s <= t)]
    for l in range(_hgrn_levels()):
        m = 2 ** l
        sep = (t // (2 * m)) * (2 * m) + m - 1
        mats.append(s <= sep)
    return jnp.asarray(np.concatenate(mats, axis=0).astype(np.float32), dtype=BF16)


def _hgrn_post(o, g, nw):
    o = o * lax.rsqrt(jnp.mean(o * o, axis=-1, keepdims=True) + NORM_EPS) * nw
    return o * (g * _sigmoid(g))


def _hgrn_prompt_kernel(q_ref, f_ref, i_ref, g_ref, ms_ref, nw_ref, o_ref, s_ref):
    c = HG_CHUNK
    dk = HG_DK
    n_chunks = q_ref.shape[0] // c
    levels = _hgrn_levels()
    row = lax.broadcasted_iota(I32, (c, c), 0)
    col = lax.broadcasted_iota(I32, (c, c), 1)
    rowk = lax.broadcasted_iota(I32, (c, dk), 0)
    upper = [((rowk // (2 ** l)) % 2) == 1 for l in range(levels)]
    same = [(row // (2 ** (l + 1))) == (col // (2 ** (l + 1))) for l in range(levels)]
    diag = row == col
    nw = nw_ref[...]

    def chunk(ci, st):
        r0 = pl.multiple_of(ci * c, c)
        q = q_ref[pl.ds(r0, c), :]
        f = f_ref[pl.ds(r0, c), :]
        v = i_ref[pl.ds(r0, c), :]
        g = g_ref[pl.ds(r0, c), :]
        lf = jnp.log(f)
        k = 1.0 - f
        hi = lf.astype(BF16)
        r1 = lf - hi.astype(F32)
        mid = r1.astype(BF16)
        lo = (r1 - mid.astype(F32)).astype(BF16)
        bb = _dot(ms_ref[...], jnp.concatenate([hi, mid, lo], axis=1))
        bb = bb[:, 0:dk] + bb[:, dk:2 * dk] + bb[:, 2 * dk:3 * dk]
        b = bb[0:c]
        a = jnp.where(diag, jnp.sum(q * k, axis=1, keepdims=True), 0.0)
        for l in range(levels):
            bs = bb[(l + 1) * c:(l + 2) * c]
            e = jnp.exp(jnp.where(upper[l], b - bs, bs - b))
            qp = jnp.where(upper[l], q * e, 0.0).astype(BF16)
            kp = jnp.where(upper[l], 0.0, k * e).astype(BF16)
            a = a + jnp.where(same[l], _dot_nt(qp, kp), 0.0)
        vb = v.astype(BF16)
        o = _dot_nt((q * jnp.exp(b)).astype(BF16), st.astype(BF16)) + _dot(a.astype(BF16), vb)
        blast = b[c - 1:c, :]
        kl = (k * jnp.exp(blast - b)).astype(BF16)
        st_new = st * jnp.exp(blast) + _dot(v.T.astype(BF16), kl)
        o_ref[pl.ds(r0, c), :] = _hgrn_post(o, g, nw).astype(BF16)
        return st_new

    st = lax.fori_loop(0, n_chunks, chunk, jnp.zeros((dk, dk), F32))
    s_ref[0] = st.T


def _hgrn_prompt(hg, nw, *, batch, seq):
    hw = HG_HEADS * HG_DK
    mstack = _hgrn_cumsum_matrix()
    spec = lambda off: pl.BlockSpec((seq, HG_DK), lambda b, h: (b, off + h))
    return pl.pallas_call(
        _hgrn_prompt_kernel,
        out_shape=(jax.ShapeDtypeStruct((batch * seq, hw), BF16),
                   jax.ShapeDtypeStruct((batch * HG_HEADS, HG_DK, HG_DK), F32)),
        grid=(batch, HG_HEADS),
        in_specs=[spec(0), spec(HG_HEADS), spec(2 * HG_HEADS), spec(3 * HG_HEADS),
                  pl.BlockSpec(mstack.shape, lambda b, h: (0, 0)),
                  pl.BlockSpec((1, HG_DK), lambda b, h: (0, 0))],
        out_specs=(pl.BlockSpec((seq, HG_DK), lambda b, h: (b, h)),
                   pl.BlockSpec((1, HG_DK, HG_DK), lambda b, h: (b * HG_HEADS + h, 0, 0))),
        compiler_params=_cparams(("arbitrary", "arbitrary")),
        name="hgrn_prompt",
    )(hg, hg, hg, hg, mstack, nw)


def _hgrn_sample_kernel(hg_ref, s0_ref, nw_ref, o_ref, s_ref, *, dec, seqs):
    dk = HG_DK
    hw = HG_HEADS * HG_DK
    nw = nw_ref[...]
    zpad = jnp.zeros((dk - 3 * dec, dk), F32)

    def one_seq(si, carry):
        r0 = pl.multiple_of(si * dec, dec)
        for h in range(HG_HEADS):
            cols = slice(h * dk, (h + 1) * dk)
            q = hg_ref[pl.ds(r0, dec), h * dk:(h + 1) * dk]
            f = hg_ref[pl.ds(r0, dec), hw + h * dk:hw + (h + 1) * dk]
            v = hg_ref[pl.ds(r0, dec), 2 * hw + h * dk:2 * hw + (h + 1) * dk]
            g = hg_ref[pl.ds(r0, dec), 3 * hw + h * dk:3 * hw + (h + 1) * dk]
            colsT = jnp.concatenate([f, 1.0 - f, q, zpad], axis=0).T
            st = s0_ref[si * HG_HEADS + h]
            outs = []
            for t in range(dec):
                st = colsT[:, t:t + 1] * st + colsT[:, dec + t:dec + t + 1] * v[t:t + 1, :]
                outs.append(jnp.sum(colsT[:, 2 * dec + t:2 * dec + t + 1] * st, axis=0, keepdims=True))
            o = jnp.concatenate(outs, axis=0)
            o_ref[pl.ds(r0, dec), cols] = _hgrn_post(o, g, nw)
            s_ref[si * HG_HEADS + h] = st
        return carry

    lax.fori_loop(0, seqs, one_seq, 0)


def _hgrn_sample(hg, s0, nw, *, dec_batch, dec_seq, row0):
    hw = HG_HEADS * HG_DK
    seqs = 8
    rows = seqs * dec_seq
    kern = functools.partial(_hgrn_sample_kernel, dec=dec_seq, seqs=seqs)
    st_spec = pl.BlockSpec((seqs * HG_HEADS, HG_DK, HG_DK), lambda i: (i, 0, 0))
    return pl.pallas_call(
        kern,
        out_shape=(jax.ShapeDtypeStruct((dec_batch * dec_seq, hw), F32),
                   jax.ShapeDtypeStruct((dec_batch * HG_HEADS, HG_DK, HG_DK), F32)),
        grid=(dec_batch // seqs,),
        in_specs=[pl.BlockSpec((rows, 4 * hw), lambda i: (row0 // rows + i, 0)), st_spec,
                  pl.BlockSpec((1, HG_DK), lambda i: (0, 0))],
        out_specs=(pl.BlockSpec((rows, hw), lambda i: (i, 0)), st_spec),
        compiler_params=_cparams(("arbitrary",)),
        name="hgrn_sample",
    )(hg, s0, nw)


def _post_mixer_kernel(*refs, n_a, npt):
    hp_ref, hs_ref, mp_ref, ms_ref = refs[0:4]
    a_refs = refs[4:4 + 2 * n_a]
    w_refs = refs[4 + 2 * n_a:4 + 3 * n_a]
    nffn_ref, rw_ref, rb_ref, stril_ref = refs[4 + 3 * n_a:8 + 3 * n_a]
    hmid_ref, u_ref, rinfo_ref, counts_ref, carry = refs[8 + 3 * n_a:]
    i = pl.program_id(0)

    @pl.when(i == 0)
    def _():
        carry[...] = jnp.zeros_like(carry)

    def run(is_prompt):
        h = (hp_ref if is_prompt else hs_ref)[...]
        m3 = (mp_ref if is_prompt else ms_ref)[...]
        out = None
        for a in range(n_a):
            av = a_refs[2 * a + (0 if is_prompt else 1)][...].astype(BF16)
            t = _dot(av, w_refs[a][...])
            out = t if out is None else out + t
        hm = _gate_residual(h, m3, 2, out)
        hmid_ref[...] = hm
        u = _norm_mod(hm, nffn_ref[...], m3, 4, 3)
        for k in range(u.shape[1] // LANES):
            u_ref[:, k, :] = u[:, k * LANES:(k + 1) * LANES]

        u_hi = u.astype(BF16)
        u_lo = (u - u_hi.astype(F32)).astype(BF16)
        hh = _dot(u_hi, rw_ref[...])
        logits = hh[:, 0:LANES] + hh[:, LANES:2 * LANES] + _dot(u_lo, rw_ref[:, 0:LANES]) + rb_ref[...]
        col = lambda j: logits[:, j:j + 1]
        gl = [col(j) for j in range(N_GROUPS)]
        gmax = functools.reduce(jnp.maximum, gl)
        gsum = functools.reduce(lambda x, y: x + y, [jnp.exp(x - gmax) for x in gl])
        gval = 1.0 / gsum
        gidx = jnp.where(gl[0] == gmax, 0, jnp.where(gl[1] == gmax, 1, jnp.where(gl[2] == gmax, 2, 3)))
        el = []
        for k in range(EXPERTS_PER_GROUP):
            c = [col(N_GROUPS + g * EXPERTS_PER_GROUP + k) for g in range(N_GROUPS)]
            el.append(jnp.where(gidx == 0, c[0], jnp.where(gidx == 1, c[1], jnp.where(gidx == 2, c[2], c[3]))))
        emax = functools.reduce(jnp.maximum, el)
        pe = [jnp.exp(x - emax) for x in el]
        esum = functools.reduce(lambda x, y: x + y, pe)
        pk = [x / esum for x in pe]
        v1 = functools.reduce(jnp.maximum, pk)
        i1 = jnp.where(pk[0] == v1, 0, jnp.where(pk[1] == v1, 1, jnp.where(pk[2] == v1, 2, 3)))
        pk2 = [jnp.where(i1 == k, -1.0, pk[k]) for k in range(EXPERTS_PER_GROUP)]
        v2 = functools.reduce(jnp.maximum, pk2)
        i2 = jnp.where(pk2[0] == v2, 0, jnp.where(pk2[1] == v2, 1, jnp.where(pk2[2] == v2, 2, 3)))
        den = v1 + v2
        w1 = gval * v1 / den
        w2 = gval * v2 / den
        lo = jnp.minimum(i1, i2)
        hi = jnp.maximum(i1, i2)
        pair = jnp.where(lo == 0, hi - 1, jnp.where(lo == 1, hi + 1, 5))
        bucket = gidx * 6 + pair
        w_lo = jnp.where(i1 < i2, w1, w2)
        w_hi = jnp.where(i1 < i2, w2, w1)

        r = h.shape[0]
        lane = lax.broadcasted_iota(I32, (r, LANES), 1)
        onehot = lane == bucket
        before = _dot(stril_ref[...], onehot.astype(BF16)) + carry[...]
        rank = jnp.sum(jnp.where(onehot, before, 0.0), axis=1, keepdims=True)
        carry[...] = carry[...] + jnp.sum(onehot.astype(F32), axis=0, keepdims=True)
        rinfo_ref[...] = jnp.where(lane == 0, bucket.astype(F32),
                                   jnp.where(lane == 1, w_lo, jnp.where(lane == 2, w_hi,
                                                                        jnp.where(lane == 3, rank, 0.0))))
        counts_ref[...] = carry[...]

    _by_group(i, npt, run)


def _post_mixer(h_pair, h_off_s, mp, ms, a_pairs, ws, nffn, rw, rb, *, seq):
    hp, hs = h_pair
    d = hp.shape[1]
    tp = a_pairs[0][0].shape[0]
    ts = a_pairs[0][1].shape[0]
    npt, nst = tp // ROW_TILE, ts // ROW_TILE
    t = tp + ts
    n_a = len(a_pairs)
    tiles_per_seq = seq // ROW_TILE
    const = lambda shape: pl.BlockSpec(shape, lambda i: (0,) * len(shape))
    stril = jnp.asarray(np.tril(np.ones((ROW_TILE, ROW_TILE), np.float32), -1), dtype=BF16)
    in_specs = [*_pair_specs(d, npt, off_s=h_off_s), *_mod_specs(d, npt, tiles_per_seq)]
    args = [hp, hs, mp, ms]
    for ap, as_ in a_pairs:
        in_specs += list(_pair_specs(ap.shape[1], npt))
        args += [ap, as_]
    for w in ws:
        in_specs.append(const(w.shape))
        args.append(w)
    in_specs += [const((1, d)), const(rw.shape), const((1, LANES)), const(stril.shape)]
    args += [nffn, rw, rb, stril]
    tile = lambda c: pl.BlockSpec((ROW_TILE, c), lambda i: (i, 0))
    kern = functools.partial(_post_mixer_kernel, n_a=n_a, npt=npt)
    return pl.pallas_call(
        kern,
        out_shape=(jax.ShapeDtypeStruct((t, d), F32), jax.ShapeDtypeStruct((t, d // LANES, LANES), F32),
                   jax.ShapeDtypeStruct((t, LANES), F32), jax.ShapeDtypeStruct((1, LANES), F32)),
        grid=(npt + nst,),
        in_specs=in_specs,
        out_specs=(tile(d), pl.BlockSpec((ROW_TILE, d // LANES, LANES), lambda i: (i, 0, 0)), tile(LANES),
                   const((1, LANES))),
        scratch_shapes=[pltpu.VMEM((1, LANES), F32)],
        compiler_params=_cparams(("arbitrary",)),
        name="post_mixer",
    )(*args)


def _for_rows(rows, fn):
    def body(r, c):
        fn(r)
        return c

    lax.fori_loop(0, rows, body, 0, unroll=8)


def _scatter_kernel(dest_ref, src_ref, init_ref, o_ref, sem, *, rows):
    del init_ref
    c = pl.program_id(0)
    n = pl.num_programs(0)

    def copy(chunk, r):
        t = chunk * rows + r
        return pltpu.make_async_copy(src_ref.at[t], o_ref.at[dest_ref[t]], sem.at[chunk % 2])

    _for_rows(rows, lambda r: copy(c, r).start())

    @pl.when(c > 0)
    def _():
        _for_rows(rows, lambda r: copy(c - 1, r).wait())

    @pl.when(c == n - 1)
    def _():
        _for_rows(rows, lambda r: copy(c, r).wait())


def _scatter_tokens(dest, src, n_slots):
    t = src.shape[0]
    rows = ROW_TILE
    grid_spec = pltpu.PrefetchScalarGridSpec(
        num_scalar_prefetch=1,
        grid=(t // rows,),
        in_specs=[pl.BlockSpec(memory_space=pl.ANY), pl.BlockSpec(memory_space=pl.ANY)],
        out_specs=pl.BlockSpec(memory_space=pl.ANY),
        scratch_shapes=[pltpu.SemaphoreType.DMA((2,))],
    )
    return pl.pallas_call(
        functools.partial(_scatter_kernel, rows=rows),
        out_shape=jax.ShapeDtypeStruct((n_slots,) + src.shape[1:], src.dtype),
        grid_spec=grid_spec,
        input_output_aliases={2: 0},
        compiler_params=_cparams(("arbitrary",)),
        name="moe_scatter",
    )(dest, src, jnp.zeros((n_slots,) + src.shape[1:], src.dtype))


def _moe_kernel(ea_ref, eb_ref, valid_ref, fresh_ref, x_ref,
                w1a_ref, w3a_ref, w2a_ref, w1b_ref, w3b_ref, w2b_ref, o_ref, wbuf1, wbuf3, wbuf2):
    i = pl.program_id(0)
    nk = x_ref.shape[1]

    @pl.when(fresh_ref[i] == 1)
    def _():
        wbuf1[0] = w1a_ref[0, 0].astype(BF16)
        wbuf3[0] = w3a_ref[0, 0].astype(BF16)
        wbuf2[0] = w2a_ref[0, 0].astype(BF16)
        wbuf1[1] = w1b_ref[0, 0].astype(BF16)
        wbuf3[1] = w3b_ref[0, 0].astype(BF16)
        wbuf2[1] = w2b_ref[0, 0].astype(BF16)

    @pl.when(valid_ref[i] == 1)
    def _():
        xb = jnp.concatenate([x_ref[:, k, :] for k in range(nk)], axis=1).astype(BF16)
        for s in (0, 1):
            h1 = _dot(xb, wbuf1[s])
            h3 = _dot(xb, wbuf3[s])
            y = _dot(((h1 * _sigmoid(h1)) * h3).astype(BF16), wbuf2[s])
            for k in range(nk):
                o_ref[:, s * nk + k, :] = y[:, k * LANES:(k + 1) * LANES]

    @pl.when(valid_ref[i] == 0)
    def _():
        o_ref[...] = jnp.zeros_like(o_ref)


def _moe_experts(tile_ea, tile_eb, tile_valid, tile_fresh, xs, w1, w3, w2, layer):
    n_slots, nk, _ = xs.shape
    d = nk * LANES
    ff = w1.shape[-1]
    wa = lambda shape: pl.BlockSpec((1, 1) + shape, lambda i, ea, eb, va, fr: (layer, ea[i], 0, 0))
    wb = lambda shape: pl.BlockSpec((1, 1) + shape, lambda i, ea, eb, va, fr: (layer, eb[i], 0, 0))
    rows = lambda k: pl.BlockSpec((MOE_TILE, k, LANES), lambda i, ea, eb, va, fr: (i, 0, 0))
    grid_spec = pltpu.PrefetchScalarGridSpec(
        num_scalar_prefetch=4,
        grid=(n_slots // MOE_TILE,),
        in_specs=[rows(nk), wa((d, ff)), wa((d, ff)), wa((ff, d)), wb((d, ff)), wb((d, ff)), wb((ff, d))],
        out_specs=rows(2 * nk),
        scratch_shapes=[pltpu.VMEM((2, d, ff), BF16), pltpu.VMEM((2, d, ff), BF16), pltpu.VMEM((2, ff, d), BF16)],
    )
    return pl.pallas_call(
        _moe_kernel,
        out_shape=jax.ShapeDtypeStruct((n_slots, 2 * nk, LANES), F32),
        grid_spec=grid_spec,
        compiler_params=_cparams(("arbitrary",)),
        name="moe_experts",
    )(tile_ea, tile_eb, tile_valid, tile_fresh, xs, w1, w3, w2, w1, w3, w2)


def _moe_plan(rinfo, counts, n_slots):
    n_tiles = n_slots // MOE_TILE
    cnt = counts[0, :N_BUCKETS].astype(I32)
    padded = ((cnt + MOE_TILE - 1) // MOE_TILE) * MOE_TILE
    ends = jnp.cumsum(padded)
    starts = ends - padded
    bucket = rinfo[:, 0].astype(I32)
    dest = starts[bucket] + rinfo[:, 3].astype(I32)
    n_valid = ends[-1] // MOE_TILE
    tiles = jnp.arange(n_tiles, dtype=I32)
    tb = jnp.sum((ends[None, :] <= (tiles * MOE_TILE)[:, None]).astype(I32), axis=1)
    tb = jnp.minimum(tb, N_BUCKETS - 1)
    valid = tiles < n_valid
    tb = jnp.where(valid, tb, tb[jnp.maximum(n_valid - 1, 0)])
    lo = jnp.asarray(PAIR_LO, I32)
    hi = jnp.asarray(PAIR_HI, I32)
    ea = (tb // 6) * EXPERTS_PER_GROUP + lo[tb % 6]
    eb = (tb // 6) * EXPERTS_PER_GROUP + hi[tb % 6]
    fresh = jnp.concatenate([jnp.ones((1,), I32), (tb[1:] != tb[:-1]).astype(I32)])
    return dest, ea, eb, valid.astype(I32), fresh


def _moe(u_tiles, rinfo, counts, w1, w3, w2, layer):
    t = u_tiles.shape[0]
    n_slots = t + N_BUCKETS * MOE_TILE
    dest, ea, eb, valid, fresh = _moe_plan(rinfo, counts, n_slots)
    xs = _scatter_tokens(dest, u_tiles, n_slots)
    ys = _moe_experts(ea, eb, valid, fresh, xs, w1, w3, w2, layer)
    return dest, ys


def _combine_kernel(dest_ref, ys_ref, hm_ref, rinfo_ref, mp_ref, ms_ref, nf_ref, *rest, npt, final):
    if final:
        yp_ref, ysm_ref, gbuf, sem = rest
    else:
        h_ref, gbuf, sem = rest
    i = pl.program_id(0)
    n = pl.num_programs(0)
    rows = gbuf.shape[1]
    nk = gbuf.shape[2] // 2

    def copy(tile, r):
        slot = tile % 2
        return pltpu.make_async_copy(ys_ref.at[dest_ref[tile * rows + r]], gbuf.at[slot, r], sem.at[slot])

    @pl.when(i == 0)
    def _():
        _for_rows(rows, lambda r: copy(0, r).start())

    @pl.when(i + 1 < n)
    def _():
        _for_rows(rows, lambda r: copy(i + 1, r).start())

    _for_rows(rows, lambda r: copy(i, r).wait())
    slot = i % 2
    ya = jnp.concatenate([gbuf[slot, :, k, :] for k in range(nk)], axis=1)
    yb = jnp.concatenate([gbuf[slot, :, nk + k, :] for k in range(nk)], axis=1)
    rinfo = rinfo_ref[...]
    moe = rinfo[:, 1:2] * ya + rinfo[:, 2:3] * yb

    def run(is_prompt):
        m3 = (mp_ref if is_prompt else ms_ref)[...]
        h = _gate_residual(hm_ref[...], m3, 5, moe)
        if final:
            y = h * lax.rsqrt(jnp.mean(h * h, axis=-1, keepdims=True) + NORM_EPS) * nf_ref[...]
            (yp_ref if is_prompt else ysm_ref)[...] = y
        else:
            h_ref[...] = h

    _by_group(i, npt, run)


def _combine(dest, ys, hmid, rinfo, mp, ms, nf, *, tp, seq, final):
    t, d = hmid.shape
    npt = tp // ROW_TILE
    tiles_per_seq = seq // ROW_TILE
    mpm = lambda f: (lambda i, dref: f(i))
    mp_spec, ms_spec = _mod_specs(d, npt, tiles_per_seq)
    yp_spec, ysm_spec = _pair_specs(d, npt)
    wrap = lambda s: pl.BlockSpec(s.block_shape, mpm(s.index_map))
    if final:
        out_shape = (jax.ShapeDtypeStruct((tp, d), F32), jax.ShapeDtypeStruct((t - tp, d), F32))
        out_specs = (wrap(yp_spec), wrap(ysm_spec))
    else:
        out_shape = jax.ShapeDtypeStruct((t, d), F32)
        out_specs = pl.BlockSpec((ROW_TILE, d), lambda i, dref: (i, 0))
    grid_spec = pltpu.PrefetchScalarGridSpec(
        num_scalar_prefetch=1,
        grid=(t // ROW_TILE,),
        in_specs=[pl.BlockSpec(memory_space=pl.ANY),
                  pl.BlockSpec((ROW_TILE, d), lambda i, dref: (i, 0)),
                  pl.BlockSpec((ROW_TILE, LANES), lambda i, dref: (i, 0)),
                  wrap(mp_spec), wrap(ms_spec),
                  pl.BlockSpec((1, d), lambda i, dref: (0, 0))],
        out_specs=out_specs,
        scratch_shapes=[pltpu.VMEM((2, ROW_TILE) + ys.shape[1:], F32), pltpu.SemaphoreType.DMA((2,))],
    )
    kern = functools.partial(_combine_kernel, npt=npt, final=final)
    return pl.pallas_call(
        kern, out_shape=out_shape, grid_spec=grid_spec,
        compiler_params=_cparams(("arbitrary",)),
        name="moe_combine_final" if final else "moe_combine",
    )(dest, ys, hmid, rinfo, mp, ms, nf)


def _odd_pre_kernel(hp_ref, hs_ref, mp_ref, ms_ref, nw_ref, w_ref, gate_ref, rec_ref, *, npt):
    i = pl.program_id(0)
    width = gate_ref.shape[1]

    def run(is_prompt):
        h = (hp_ref if is_prompt else hs_ref)[...]
        m3 = (mp_ref if is_prompt else ms_ref)[...]
        ub = _norm_mod(h, nw_ref[...], m3, 1, 0).astype(BF16)
        x = _dot(ub, w_ref[:, 0:width])
        gate_ref[...] = 0.5 * x * (1.0 + jnp.tanh(np.sqrt(2.0 / np.pi) * (x + 0.044715 * (x * x * x))))
        rec_ref[...] = _dot(ub, w_ref[:, width:2 * width])

    _by_group(i, npt, run)


def _odd_pre(h, mp, ms, nw, w_in, *, tp, seq):
    t, d = h.shape
    npt = tp // ROW_TILE
    width = w_in.shape[1] // 2
    const = lambda shape: pl.BlockSpec(shape, lambda i: (0,) * len(shape))
    tile = lambda c: pl.BlockSpec((ROW_TILE, c), lambda i: (i, 0))
    kern = functools.partial(_odd_pre_kernel, npt=npt)
    return pl.pallas_call(
        kern,
        out_shape=(jax.ShapeDtypeStruct((t, width), F32), jax.ShapeDtypeStruct((t, width), F32)),
        grid=(t // ROW_TILE,),
        in_specs=[*_pair_specs(d, npt, off_s=npt), *_mod_specs(d, npt, seq // ROW_TILE),
                  const((1, d)), const(w_in.shape)],
        out_specs=(tile(width), tile(width)),
        compiler_params=_cparams(("arbitrary",)),
        name="odd_pre",
    )(h, h, mp, ms, nw, w_in)


def _lru_kernel(rec_ref, gate_ref, c0_ref, h0_ref, cw_ref, cb_ref, wa_ref, ba_ref, wx_ref, bx_ref, lam_ref,
                yg_ref, cs_ref, hl_ref, hist, hc, a_s, h_s, *, nb, tl):
    i = pl.program_id(1)
    r, width = rec_ref.shape
    taps = CONV_WIDTH
    blk = width // LRU_HEADS

    @pl.when(i == 0)
    def _():
        hist[:, SUBLANES - (taps - 1):SUBLANES, :] = c0_ref[...]
        hc[...] = h0_ref[...]

    rec3 = rec_ref[...].reshape(nb, tl, width)
    hist[:, SUBLANES:SUBLANES + tl, :] = rec3
    cw = cw_ref[...]
    conv = cb_ref[...] + rec3 * cw[taps - 1:taps, :]
    for back in range(1, taps):
        conv = conv + hist[:, SUBLANES - back:SUBLANES - back + tl, :] * cw[taps - 1 - back:taps - back, :]
    tail = hist[:, SUBLANES + tl - (taps - 1):SUBLANES + tl, :]
    cs_ref[...] = tail
    hist[:, SUBLANES - (taps - 1):SUBLANES, :] = tail

    cf = conv.reshape(r, width)
    cb16 = cf.astype(BF16)
    gr = jnp.concatenate([_dot(cb16[:, h * blk:(h + 1) * blk], wa_ref[h]) for h in range(LRU_HEADS)], axis=1)
    gi = jnp.concatenate([_dot(cb16[:, h * blk:(h + 1) * blk], wx_ref[h]) for h in range(LRU_HEADS)], axis=1)
    rg = _sigmoid(gr + ba_ref[...])
    ig = _sigmoid(gi + bx_ref[...])
    z = -lam_ref[...]
    softplus = jnp.maximum(z, 0.0) + jnp.log1p(jnp.exp(-jnp.abs(z)))
    log_a = -LRU_C * rg * softplus
    a = jnp.exp(log_a)
    gx = jnp.sqrt(1.0 - a * a) * (ig * cf)

    sub = lax.broadcasted_iota(I32, (r, width), 0) % SUBLANES
    aa, hh = a, gx
    for dlt in (1, 2, 4):
        ok = sub >= dlt
        a_sh = pltpu.roll(aa, dlt, 0)
        h_sh = pltpu.roll(hh, dlt, 0)
        hh = jnp.where(ok, hh + aa * h_sh, hh)
        aa = jnp.where(ok, aa * a_sh, aa)

    if tl == SUBLANES:
        y3 = hh.reshape(nb, tl, width) + aa.reshape(nb, tl, width) * hc[...]
        hl_ref[...] = y3[:, tl - 1:tl, :]
        y = y3.reshape(r, width)
    else:
        a_s[...] = aa
        h_s[...] = hh

        def group(j, carry):
            r0 = pl.multiple_of(j * SUBLANES, SUBLANES)
            yj = h_s[pl.ds(r0, SUBLANES), :] + a_s[pl.ds(r0, SUBLANES), :] * carry
            h_s[pl.ds(r0, SUBLANES), :] = yj
            return yj[SUBLANES - 1:SUBLANES, :]

        last = lax.fori_loop(0, r // SUBLANES, group, hc[0])
        hc[0] = last
        hl_ref[0] = last
        y = h_s[...]
    yg_ref[...] = (y * gate_ref[...]).astype(BF16)


def _lru(rec, gate, conv0, h0, cw, cb, wa, ba, wx, bx, lam, *, batch, seq, row0):
    width = rec.shape[1]
    taps = CONV_WIDTH
    if seq == SUBLANES:
        nb, tl = ROW_TILE // SUBLANES, SUBLANES
    else:
        nb, tl = 1, ROW_TILE
    r = nb * tl
    n_l = seq // tl
    blk0 = row0 // r
    const = lambda shape: pl.BlockSpec(shape, lambda b, i: (0,) * len(shape))
    tile_in = pl.BlockSpec((r, width), lambda b, i: (blk0 + b * n_l + i, 0))
    kern = functools.partial(_lru_kernel, nb=nb, tl=tl)
    return pl.pallas_call(
        kern,
        out_shape=(jax.ShapeDtypeStruct((batch * seq, width), BF16),
                   jax.ShapeDtypeStruct((batch, taps - 1, width), F32),
                   jax.ShapeDtypeStruct((batch, 1, width), F32)),
        grid=(batch // nb, n_l),
        in_specs=[tile_in, tile_in,
                  pl.BlockSpec((nb, taps - 1, width), lambda b, i: (b, 0, 0)),
                  pl.BlockSpec((nb, 1, width), lambda b, i: (b, 0, 0)),
                  const((taps, width)), const((1, width)), const(wa.shape), const((1, width)),
                  const(wx.shape), const((1, width)), const((1, width))],
        out_specs=(pl.BlockSpec((r, width), lambda b, i: (b * n_l + i, 0)),
                   pl.BlockSpec((nb, taps - 1, width), lambda b, i: (b, 0, 0)),
                   pl.BlockSpec((nb, 1, width), lambda b, i: (b, 0, 0))),
        scratch_shapes=[pltpu.VMEM((nb, tl + SUBLANES, width), F32), pltpu.VMEM((nb, 1, width), F32),
                        pltpu.VMEM((r, width), F32), pltpu.VMEM((r, width), F32)],
        compiler_params=_cparams(("arbitrary", "arbitrary")),
        name="rg_lru_sample" if seq == SUBLANES else "rg_lru_prompt",
    )(rec, gate, conv0, h0, cw, cb, wa, ba, wx, bx, lam)


def kernel(x_prompt, x_sample, c_prompt, c_sample, cache_k, cache_v, state_hgrn, state_conv, state_lru, page_table,
           ada_w, ada_b, norm_mix, norm_ffn, norm_final, w_in_even, w_out_even, hg_lower_bounds, hg_norm_w, w_in_odd,
           conv_w, conv_b, lru_wa, lru_ba, lru_wx, lru_bx, lru_lambda, w_out_odd, router_group_w, router_group_b,
           router_expert_w, router_expert_b, moe_w1, moe_w3, moe_w2):
    batch, seq, d = x_prompt.shape
    dec_batch, dec_seq, _ = x_sample.shape
    depth = ada_w.shape[0]
    assert depth == 2 and seq % ROW_TILE == 0 and seq % MOBA_BLOCK == 0
    tp, ts = batch * seq, dec_batch * dec_seq
    n_pages, page = page_table.shape[1], cache_k.shape[2]
    past_len = n_pages * page
    aw = MOBA_HEADS * MOBA_HEAD_DIM
    hw = HG_HEADS * HG_DK

    xp = x_prompt.reshape(tp, d)
    xs = x_sample.reshape(ts, d)
    mods = _ada_mods(jnp.concatenate([c_prompt, c_sample], axis=0), ada_w, ada_b)
    mods = mods.reshape(depth, batch + dec_batch, 6, d)
    mods_p, mods_s = mods[:, :batch], mods[:, batch:]

    half = MOBA_HEAD_DIM // 2
    inv_freq = jnp.power(ROPE_THETA, -jnp.arange(half, dtype=F32) / half)
    invf = jnp.tile(inv_freq, LANES // half).reshape(1, LANES)

    def router_mats(l):
        rw = jnp.zeros((d, LANES), F32)
        rw = rw.at[:, 0:N_GROUPS].set(router_group_w[l])
        rw = rw.at[:, N_GROUPS:N_GROUPS + N_GROUPS * EXPERTS_PER_GROUP].set(router_expert_w[l])
        rb = jnp.zeros((1, LANES), F32)
        rb = rb.at[0, 0:N_GROUPS].set(router_group_b[l])
        rb = rb.at[0, N_GROUPS:N_GROUPS + N_GROUPS * EXPERTS_PER_GROUP].set(router_expert_b[l])
        rw_hi = rw.astype(BF16)
        rw_lo = (rw - rw_hi.astype(F32)).astype(BF16)
        return jnp.concatenate([rw_hi, rw_lo], axis=1), rb

    q_all, k_p, k_s, v_p, v_s, hg = _even_pre(
        xp, xs, mods_p[0], mods_s[0], norm_mix[0].reshape(1, d), w_in_even[0].astype(BF16), hg_lower_bounds, invf,
        seq=seq, past_len=past_len, layer_slot=0)
    oa_p = _attn_prompt(q_all, k_p, v_p, batch=batch, seq=seq)
    oa_s = _attn_sample(page_table, q_all, k_s, v_s, cache_k[0], cache_v[0],
                        dec_batch=dec_batch, dec_seq=dec_seq, q_row0=tp)
    hnw = hg_norm_w[0].reshape(1, HG_DK)
    ob_p, hgrn_p = _hgrn_prompt(hg, hnw, batch=batch, seq=seq)
    ob_s, hgrn_s = _hgrn_sample(hg, state_hgrn[0].reshape(dec_batch * HG_HEADS, HG_DK, HG_DK), hnw,
                                dec_batch=dec_batch, dec_seq=dec_seq, row0=tp)
    w_out = w_out_even[0].astype(BF16)
    rw, rb = router_mats(0)
    hmid, u, rinfo, counts = _post_mixer((xp, xs), 0, mods_p[0], mods_s[0], [(oa_p, oa_s), (ob_p, ob_s)],
                                         [w_out[:aw], w_out[aw:]], norm_ffn[0].reshape(1, d), rw, rb, seq=seq)
    dest, ys = _moe(u, rinfo, counts, moe_w1, moe_w3, moe_w2, 0)
    h1 = _combine(dest, ys, hmid, rinfo, mods_p[0], mods_s[0], norm_final.reshape(1, d), tp=tp, seq=seq,
                  final=False)

    gate, rec = _odd_pre(h1, mods_p[1], mods_s[1], norm_mix[1].reshape(1, d), w_in_odd[0].astype(BF16), tp=tp, seq=seq)
    width = rec.shape[1]
    lru_args = (conv_w[0], conv_b[0].reshape(1, width), lru_wa[0].astype(BF16), lru_ba[0].reshape(1, width),
                lru_wx[0].astype(BF16), lru_bx[0].reshape(1, width), lru_lambda[0].reshape(1, width))
    yg_p, conv_p, lru_p = _lru(rec, gate, jnp.zeros((batch, CONV_WIDTH - 1, width), F32),
                               jnp.zeros((batch, 1, width), F32), *lru_args, batch=batch, seq=seq, row0=0)
    yg_s, conv_s, lru_s = _lru(rec, gate, state_conv[0], state_lru[0].reshape(dec_batch, 1, width), *lru_args,
                               batch=dec_batch, seq=dec_seq, row0=tp)
    rw, rb = router_mats(1)
    hmid, u, rinfo, counts = _post_mixer((h1, h1), tp // ROW_TILE, mods_p[1], mods_s[1], [(yg_p, yg_s)],
                                         [w_out_odd[0].astype(BF16)], norm_ffn[1].reshape(1, d), rw, rb, seq=seq)
    dest, ys = _moe(u, rinfo, counts, moe_w1, moe_w3, moe_w2, 1)
    y_p, y_s = _combine(dest, ys, hmid, rinfo, mods_p[1], mods_s[1], norm_final.reshape(1, d), tp=tp, seq=seq,
                        final=True)

    return (y_p.reshape(batch, seq, d), y_s.reshape(dec_batch, dec_seq, d),
            k_p.reshape(1, batch, seq, MOBA_HEADS, MOBA_HEAD_DIM), v_p.reshape(1, batch, seq, MOBA_HEADS, MOBA_HEAD_DIM),
            k_s.reshape(1, dec_batch, dec_seq, MOBA_HEADS, MOBA_HEAD_DIM),
            v_s.reshape(1, dec_batch, dec_seq, MOBA_HEADS, MOBA_HEAD_DIM),
            hgrn_p.reshape(1, batch, HG_HEADS, HG_DK, HG_DK), hgrn_s.reshape(1, dec_batch, HG_HEADS, HG_DK, HG_DK),
            conv_p.reshape(1, batch, CONV_WIDTH - 1, width), conv_s.reshape(1, dec_batch, CONV_WIDTH - 1, width),
            lru_p.reshape(1, batch, width), lru_s.reshape(1, dec_batch, width))
```

```python
import functools

import numpy as np
import jax
import jax.numpy as jnp
from jax import lax
from jax.experimental import pallas as pl
from jax.experimental.pallas import tpu as pltpu

F32 = jnp.float32
BF16 = jnp.bfloat16
I32 = jnp.int32
HIGHEST = lax.Precision.HIGHEST

MOBA_HEADS = 8
MOBA_HEAD_DIM = 64
MOBA_BLOCK = 256
MOBA_TOPK = 3
ROPE_THETA = 10000.0
HG_HEADS = 4
HG_DK = 128
LRU_HEADS = 4
CONV_WIDTH = 4
LRU_C = 8.0
N_GROUPS = 4
EXPERTS_PER_GROUP = 4
NORM_EPS = 1e-6

LANES = 128
SUBLANES = 8
VMEM_LIMIT = 56 * 1024 * 1024

ROW_TILE = 256
MOE_TILE = 256
HG_CHUNK = 128
N_BUCKETS = N_GROUPS * 6
PAIR_LO = (0, 0, 0, 1, 1, 2)
PAIR_HI = (1, 2, 3, 2, 3, 3)
NEG = -1e30


def _cparams(sem, vmem=VMEM_LIMIT):
    return pltpu.CompilerParams(dimension_semantics=sem, vmem_limit_bytes=vmem)


def _dot(a, b, **kw):
    return jnp.dot(a, b, preferred_element_type=F32, **kw)


def _dot_nt(a, b, **kw):
    return lax.dot_general(a, b, (((1,), (1,)), ((), ())), preferred_element_type=F32, **kw)


def _sigmoid(x):
    return jax.nn.sigmoid(x)


def _by_group(i, n_prompt_tiles, fn):
    @pl.when(i < n_prompt_tiles)
    def _():
        fn(True)

    @pl.when(i >= n_prompt_tiles)
    def _():
        fn(False)


def _pair_specs(cols, npt, rows=ROW_TILE, off_p=0, off_s=0):
    return (pl.BlockSpec((rows, cols), lambda i: (off_p + jnp.minimum(i, npt - 1), 0)),
            pl.BlockSpec((rows, cols), lambda i: (off_s + jnp.maximum(i - npt, 0), 0)))


def _mod_specs(d, npt, tiles_per_seq, rows=ROW_TILE):
    return (pl.BlockSpec((1, 6, d), lambda i: (jnp.minimum(i, npt - 1) // tiles_per_seq, 0, 0)),
            pl.BlockSpec((rows // SUBLANES, 6, d), lambda i: (jnp.maximum(i - npt, 0), 0, 0)))


def _norm_mod(x, nw, m3, scale_idx, shift_idx):
    r, d = x.shape
    nb = m3.shape[0]
    var = jnp.mean(x * x, axis=-1, keepdims=True)
    y = x * lax.rsqrt(var + NORM_EPS) * nw
    y3 = y.reshape(nb, r // nb, d)
    u3 = y3 * (1.0 + m3[:, scale_idx:scale_idx + 1, :]) + m3[:, shift_idx:shift_idx + 1, :]
    return u3.reshape(r, d)


def _gate_residual(h, m3, gate_idx, out):
    r, d = h.shape
    nb = m3.shape[0]
    return (h.reshape(nb, r // nb, d) + m3[:, gate_idx:gate_idx + 1, :] * out.reshape(nb, r // nb, d)).reshape(r, d)


def _ada_kernel(c_ref, w_ref, b_ref, o_ref):
    o_ref[0] = _dot(c_ref[...], w_ref[0], precision=HIGHEST) + b_ref[0]


def _ada_mods(c_all, ada_w, ada_b):
    depth, d, n6 = ada_w.shape
    nb = c_all.shape[0]
    tn = 1024
    return pl.pallas_call(
        _ada_kernel,
        out_shape=jax.ShapeDtypeStruct((depth, nb, n6), F32),
        grid=(depth, n6 // tn),
        in_specs=[pl.BlockSpec((nb, d), lambda l, n: (0, 0)),
                  pl.BlockSpec((1, d, tn), lambda l, n: (l, 0, n)),
                  pl.BlockSpec((1, 1, tn), lambda l, n: (l, 0, n))],
        out_specs=pl.BlockSpec((1, nb, tn), lambda l, n: (l, 0, n)),
        compiler_params=_cparams(("arbitrary", "arbitrary")),
        name="ada_mods",
    )(c_all, ada_w, ada_b.reshape(depth, 1, n6))


def _even_pre_kernel(xp_ref, xs_ref, mp_ref, ms_ref, nw_ref, w_ref, hb_ref, invf_ref,
                     q_ref, kp_ref, ks_ref, vp_ref, vs_ref, hg_ref, *, npt, tiles_per_seq, past_len, layer_slot):
    i = pl.program_id(0)
    aw = MOBA_HEADS * MOBA_HEAD_DIM
    hw = HG_HEADS * HG_DK

    def run(is_prompt):
        x = (xp_ref if is_prompt else xs_ref)[...]
        m3 = (mp_ref if is_prompt else ms_ref)[...]
        r = x.shape[0]
        nb = m3.shape[0]
        tl = r // nb
        ub = _norm_mod(x, nw_ref[...], m3, 1, 0).astype(BF16)

        pos0 = (i % tiles_per_seq) * r if is_prompt else past_len
        pos = (pos0 + lax.broadcasted_iota(I32, (nb, tl, LANES), 1)).astype(F32).reshape(r, LANES)
        ang = pos * invf_ref[...]
        reps = aw // LANES
        cos = jnp.concatenate([jnp.cos(ang)] * reps, axis=1)
        sin = jnp.concatenate([jnp.sin(ang)] * reps, axis=1)
        lane = lax.broadcasted_iota(I32, (r, aw), 1)
        half = MOBA_HEAD_DIM // 2
        first = (lane % MOBA_HEAD_DIM) < half
        sin = jnp.where(first, -sin, sin)

        def rope(t):
            rot = jnp.where(first, pltpu.roll(t, aw - half, 1), pltpu.roll(t, half, 1))
            return t * cos + rot * sin

        def seg(a, b):
            return _dot(ub, w_ref[:, a:b])

        q_ref[...] = rope(seg(0, aw))
        k = rope(seg(aw, 2 * aw))
        v = seg(2 * aw, 3 * aw)
        if is_prompt:
            kp_ref[...] = k
            vp_ref[...] = v
        else:
            ks_ref[...] = k
            vs_ref[...] = v
        o = 3 * aw
        qb = seg(o, o + hw)
        fb = seg(o + hw, o + 2 * hw)
        hb = hb_ref[...]
        e = jnp.exp(hb - jnp.max(hb, axis=0, keepdims=True))
        sm = e / jnp.sum(e, axis=0, keepdims=True)
        lb = jnp.sum(sm[0:layer_slot + 1, :], axis=0, keepdims=True)
        hg_ref[:, 0:hw] = qb * _sigmoid(qb)
        hg_ref[:, hw:2 * hw] = lb + (1.0 - lb) * _sigmoid(fb)
        hg_ref[:, 2 * hw:3 * hw] = seg(o + 2 * hw, o + 3 * hw)
        hg_ref[:, 3 * hw:4 * hw] = seg(o + 3 * hw, o + 4 * hw)

    _by_group(i, npt, run)


def _even_pre(xp, xs, mp, ms, nw, w_in, hb, invf, *, seq, past_len, layer_slot):
    tp, d = xp.shape
    ts = xs.shape[0]
    npt, nst = tp // ROW_TILE, ts // ROW_TILE
    tiles_per_seq = seq // ROW_TILE
    aw = MOBA_HEADS * MOBA_HEAD_DIM
    hw = HG_HEADS * HG_DK
    t = tp + ts
    const = lambda shape: pl.BlockSpec(shape, lambda i: (0,) * len(shape))
    kern = functools.partial(_even_pre_kernel, npt=npt, tiles_per_seq=tiles_per_seq, past_len=past_len,
                             layer_slot=layer_slot)
    kp_spec, ks_spec = _pair_specs(aw, npt)
    return pl.pallas_call(
        kern,
        out_shape=(jax.ShapeDtypeStruct((t, aw), F32),
                   jax.ShapeDtypeStruct((tp, aw), F32), jax.ShapeDtypeStruct((ts, aw), F32),
                   jax.ShapeDtypeStruct((tp, aw), F32), jax.ShapeDtypeStruct((ts, aw), F32),
                   jax.ShapeDtypeStruct((t, 4 * hw), F32)),
        grid=(npt + nst,),
        in_specs=[*_pair_specs(d, npt), *_mod_specs(d, npt, tiles_per_seq),
                  const((1, d)), const(w_in.shape), const(hb.shape), const((1, LANES))],
        out_specs=(pl.BlockSpec((ROW_TILE, aw), lambda i: (i, 0)), kp_spec, ks_spec, kp_spec, ks_spec,
                   pl.BlockSpec((ROW_TILE, 4 * hw), lambda i: (i, 0))),
        compiler_params=_cparams(("arbitrary",)),
        name="even_pre",
    )(xp, xs, mp, ms, nw, w_in, hb, invf)


def _top_blocks(g, valid):
    lane = lax.broadcasted_iota(I32, g.shape, 1)
    jl = lane % SUBLANES
    g = jnp.where(valid, g, -jnp.inf)
    cnt = jnp.zeros(g.shape, I32)
    for d in range(1, SUBLANES):
        wrapped = (jl + d) >= SUBLANES
        pg = jnp.where(wrapped, pltpu.roll(g, SUBLANES - d, 1), pltpu.roll(g, LANES - d, 1))
        beats = (pg > g) | ((pg == g) & wrapped)
        cnt = cnt + beats.astype(I32)
    return valid & (cnt < MOBA_TOPK)


def _head_expand(rows8, n_heads, width):
    x = jnp.concatenate([rows8] * n_heads, axis=0)
    r = lax.broadcasted_iota(I32, x.shape, 0) // SUBLANES
    l = lax.broadcasted_iota(I32, x.shape, 1) // (width // n_heads)
    return jnp.where(r == l, x, 0.0)


def _block_indicator(seq):
    key_blk = np.arange(seq)[:, None] // MOBA_BLOCK
    lane = np.arange(LANES)[None, :]
    ind = (lane < MOBA_HEADS * SUBLANES) & ((lane % SUBLANES) == key_blk)
    return jnp.asarray(ind.astype(np.float32), dtype=BF16)


def _attn_prompt_kernel(q_ref, k_ref, v_ref, ind_ref, o_ref, km_ref, *, n_blk):
    i = pl.program_id(1)
    blk = MOBA_BLOCK
    aw = MOBA_HEADS * MOBA_HEAD_DIM
    scale = MOBA_HEAD_DIM ** -0.5

    @pl.when(i == 0)
    def _():
        for j in range(n_blk):
            km_ref[j:j + 1, :] = jnp.mean(k_ref[j * blk:(j + 1) * blk, :], axis=0, keepdims=True)

    q = q_ref[...]
    kmexp = _head_expand(km_ref[...], MOBA_HEADS, aw)
    kmexp = jnp.concatenate([kmexp, jnp.zeros_like(kmexp)], axis=0)
    gate = _dot_nt(q, kmexp, precision=HIGHEST)
    lane = lax.broadcasted_iota(I32, gate.shape, 1)
    jl = lane % SUBLANES
    valid = (jl < i) & (lane < MOBA_HEADS * SUBLANES)
    keep = _top_blocks(gate, valid) | (jl == i)
    bias = jnp.where(keep, 0.0, NEG)

    row = lax.broadcasted_iota(I32, (blk, blk), 0)
    col = lax.broadcasted_iota(I32, (blk, blk), 1)
    tril = row >= col

    def tile(c):
        n_keys = (c + 1) * blk
        for hp in range(aw // LANES):
            cols = slice(hp * LANES, (hp + 1) * LANES)
            rhs = jnp.concatenate([k_ref[0:n_keys, cols].astype(BF16), ind_ref[0:n_keys, :]], axis=1)
            vv = v_ref[0:n_keys, cols].astype(BF16)
            qp = q[:, cols] * scale
            outs = []
            for s in (0, 1):
                h = 2 * hp + s
                qm = jnp.where((lane // MOBA_HEAD_DIM) == s, qp, 0.0).astype(BF16)
                hb = jnp.where((lane // SUBLANES) == h, bias, 0.0).astype(BF16)
                sc = _dot_nt(jnp.concatenate([qm, hb], axis=1), rhs)
                own = jnp.where(tril, sc[:, c * blk:], NEG)
                sc = own if c == 0 else jnp.concatenate([sc[:, :c * blk], own], axis=1)
                p = jnp.exp(sc - jnp.max(sc, axis=1, keepdims=True))
                outs.append(_dot(p.astype(BF16), vv) / jnp.sum(p, axis=1, keepdims=True))
            o_ref[:, cols] = jnp.where((lane // MOBA_HEAD_DIM) == 0, outs[0], outs[1]).astype(BF16)

    for c in range(n_blk):
        pl.when(i == c)(functools.partial(tile, c))


def _attn_prompt(q_all, k_p, v_p, *, batch, seq):
    aw = MOBA_HEADS * MOBA_HEAD_DIM
    n_blk = seq // MOBA_BLOCK
    assert n_blk <= SUBLANES
    ind = _block_indicator(seq)
    kern = functools.partial(_attn_prompt_kernel, n_blk=n_blk)
    return pl.pallas_call(
        kern,
        out_shape=jax.ShapeDtypeStruct((batch * seq, aw), BF16),
        grid=(batch, n_blk),
        in_specs=[pl.BlockSpec((MOBA_BLOCK, aw), lambda b, i: (b * n_blk + i, 0)),
                  pl.BlockSpec((seq, aw), lambda b, i: (b, 0)),
                  pl.BlockSpec((seq, aw), lambda b, i: (b, 0)),
                  pl.BlockSpec((seq, LANES), lambda b, i: (0, 0))],
        out_specs=pl.BlockSpec((MOBA_BLOCK, aw), lambda b, i: (b * n_blk + i, 0)),
        scratch_shapes=[pltpu.VMEM((SUBLANES, aw), F32)],
        compiler_params=_cparams(("arbitrary", "arbitrary")),
        name="moba_prompt",
    )(q_all, k_p, v_p, ind)


def _attn_sample_kernel(pt_ref, q_ref, kn_ref, vn_ref, ck_ref, cv_ref, o_ref, kbuf, vbuf, sem,
                        *, n_pages, page, n_blk, layer):
    b = pl.program_id(0)
    nb = pl.num_programs(0)
    blk = MOBA_BLOCK
    hd = MOBA_HEAD_DIM
    dec = q_ref.shape[0]
    scale = hd ** -0.5

    def page_copies(seq_idx, slot):
        cps = []
        for p in range(n_pages):
            pg = pt_ref[seq_idx * n_pages + p]
            cps.append(pltpu.make_async_copy(ck_ref.at[layer, pg], kbuf.at[slot, pl.ds(p * page, page)],
                                             sem.at[0, slot]))
            cps.append(pltpu.make_async_copy(cv_ref.at[layer, pg], vbuf.at[slot, pl.ds(p * page, page)],
                                             sem.at[1, slot]))
        return cps

    @pl.when(b == 0)
    def _():
        for c in page_copies(0, 0):
            c.start()

    @pl.when(b + 1 < nb)
    def _():
        for c in page_copies(b + 1, (b + 1) % 2):
            c.start()

    slot = b % 2
    for c in page_copies(b, slot):
        c.wait()

    nh = MOBA_HEADS
    nq = nh * dec
    past = n_blk * blk
    q = q_ref[...]
    kn = kn_ref[...]
    vn = vn_ref[...]
    head_rows = lambda a: jnp.concatenate([a[:, h * hd:(h + 1) * hd] for h in range(nh)], axis=0)
    qe = head_rows(q)
    kne = head_rows(kn)
    vne = head_rows(vn)
    kblk = lambda j: kbuf[slot, pl.ds(j * blk, blk)]
    vblk = lambda j: vbuf[slot, pl.ds(j * blk, blk)]

    km = jnp.stack([jnp.mean(kblk(j), axis=0) for j in range(n_blk)], axis=0)
    km_hj = jnp.concatenate([km[:, h, :] for h in range(nh)], axis=0)
    km_hj = jnp.concatenate([km_hj, jnp.zeros((LANES - nh * n_blk, hd), F32)], axis=0)
    gate = _dot_nt(qe, km_hj, precision=HIGHEST)
    lane = lax.broadcasted_iota(I32, gate.shape, 1)
    rowq = lax.broadcasted_iota(I32, gate.shape, 0)
    sel = _top_blocks(gate, lane < nh * n_blk)
    sel = (sel & ((lane // n_blk) == (rowq // dec))).astype(BF16)
    fold = (lax.broadcasted_iota(I32, (2 * SUBLANES, LANES), 1) % n_blk
            == lax.broadcasted_iota(I32, (2 * SUBLANES, LANES), 0)).astype(BF16)
    sel_jc = _dot_nt(fold, sel)[0:n_blk]

    hrow = lax.broadcasted_iota(I32, (nh, nq), 0)
    ccol = lax.broadcasted_iota(I32, (nh, nq), 1)
    diag = hrow == (ccol // dec)
    qs = (qe * scale).astype(BF16)
    g = []
    for j in range(n_blk):
        bias = jnp.where(diag & (sel_jc[j:j + 1, :] > 0.5), 0.0, NEG)
        g.append(_dot_nt(kblk(j).reshape(blk * nh, hd).astype(BF16), qs).reshape(blk, nh, nq) + bias[None])
    rn = lax.broadcasted_iota(I32, (nq, nq), 0)
    cn = lax.broadcasted_iota(I32, (nq, nq), 1)
    new_ok = ((rn // dec) == (cn // dec)) & ((rn % dec) <= (cn % dec))
    gn = jnp.where(new_ok, _dot_nt(kne.astype(BF16), qs), NEG).reshape(nh, dec, nq)

    m = jnp.max(gn, axis=1)
    for j in range(n_blk):
        m = jnp.maximum(m, jnp.max(g[j], axis=0))
    m = jnp.where(diag, m, 0.0)
    dot_tn = lambda a, b: lax.dot_general(a, b, (((0,), (0,)), ((), ())), preferred_element_type=F32)
    pn = jnp.exp(gn - m[:, None, :])
    l = jnp.sum(pn, axis=1)
    o = dot_tn(pn.reshape(nq, nq).astype(BF16), vne.astype(BF16))
    for j in range(n_blk):
        p = jnp.exp(g[j] - m[None])
        l = l + jnp.sum(p, axis=0)
        o = o + dot_tn(p.reshape(blk * nh, nq).astype(BF16), vblk(j).reshape(blk * nh, hd).astype(BF16))
    l_c = jnp.sum(jnp.where(diag, l, 0.0), axis=0, keepdims=True)
    l_r = jnp.sum(jnp.where(rn == cn, jnp.broadcast_to(l_c, (nq, nq)), 0.0), axis=1, keepdims=True)
    o = o / l_r
    for h in range(nh):
        o_ref[:, h * hd:(h + 1) * hd] = o[h * dec:(h + 1) * dec]


def _attn_sample(page_table, q_all, k_s, v_s, cache_k, cache_v, *, layer, dec_batch, dec_seq, q_row0):
    _, n_phys, page, n_heads, hd = cache_k.shape
    aw = MOBA_HEADS * MOBA_HEAD_DIM
    n_pages = page_table.shape[1]
    past_len = n_pages * page
    assert past_len % MOBA_BLOCK == 0 and dec_seq == SUBLANES and past_len // MOBA_BLOCK == SUBLANES
    assert n_heads == MOBA_HEADS and hd == MOBA_HEAD_DIM
    kern = functools.partial(_attn_sample_kernel, n_pages=n_pages, page=page, n_blk=past_len // MOBA_BLOCK,
                             layer=layer)
    q_blk0 = q_row0 // dec_seq
    grid_spec = pltpu.PrefetchScalarGridSpec(
        num_scalar_prefetch=1,
        grid=(dec_batch,),
        in_specs=[pl.BlockSpec((dec_seq, aw), lambda b, pt: (q_blk0 + b, 0)),
                  pl.BlockSpec((dec_seq, aw), lambda b, pt: (b, 0)),
                  pl.BlockSpec((dec_seq, aw), lambda b, pt: (b, 0)),
                  pl.BlockSpec(memory_space=pl.ANY),
                  pl.BlockSpec(memory_space=pl.ANY)],
        out_specs=pl.BlockSpec((dec_seq, aw), lambda b, pt: (b, 0)),
        scratch_shapes=[pltpu.VMEM((2, past_len, n_heads, hd), F32), pltpu.VMEM((2, past_len, n_heads, hd), F32),
                        pltpu.SemaphoreType.DMA((2, 2))],
    )
    return pl.pallas_call(
        kern,
        out_shape=jax.ShapeDtypeStruct((dec_batch * dec_seq, aw), F32),
        grid_spec=grid_spec,
        compiler_params=_cparams(("arbitrary",)),
        name="moba_sample",
    )(page_table.reshape(-1), q_all, k_s, v_s, cache_k, cache_v)


def _hgrn_levels():
    n = HG_CHUNK
    return int(np.log2(n))


def _hgrn_cumsum_matrix():
    c = HG_CHUNK
    t = np.arange(c)[:, None]
    s = np.arange(c)[None, :]
    mats = [(s <= t)]
    for l in range(_hgrn_levels()):
        m = 2 ** l
        sep = (t // (2 * m)) * (2 * m) + m - 1
        mats.append(s <= sep)
    return jnp.asarray(np.concatenate(mats, axis=0).astype(np.float32), dtype=BF16)


def _hgrn_post(o, g, nw):
    o = o * lax.rsqrt(jnp.mean(o * o, axis=-1, keepdims=True) + NORM_EPS) * nw
    return o * (g * _sigmoid(g))


def _hgrn_prompt_kernel(q_ref, f_ref, i_ref, g_ref, ms_ref, nw_ref, o_ref, s_ref):
    c = HG_CHUNK
    dk = HG_DK
    n_chunks = q_ref.shape[0] // c
    levels = _hgrn_levels()
    row = lax.broadcasted_iota(I32, (c, c), 0)
    col = lax.broadcasted_iota(I32, (c, c), 1)
    rowk = lax.broadcasted_iota(I32, (c, dk), 0)
    upper = [((rowk // (2 ** l)) % 2) == 1 for l in range(levels)]
    same = [(row // (2 ** (l + 1))) == (col // (2 ** (l + 1))) for l in range(levels)]
    diag = row == col
    nw = nw_ref[...]

    def chunk(ci, st):
        r0 = pl.multiple_of(ci * c, c)
        q = q_ref[pl.ds(r0, c), :]
        f = f_ref[pl.ds(r0, c), :]
        v = i_ref[pl.ds(r0, c), :]
        g = g_ref[pl.ds(r0, c), :]
        lf = jnp.log(f)
        k = 1.0 - f
        hi = lf.astype(BF16)
        r1 = lf - hi.astype(F32)
        mid = r1.astype(BF16)
        lo = (r1 - mid.astype(F32)).astype(BF16)
        bb = _dot(ms_ref[...], jnp.concatenate([hi, mid, lo], axis=1))
        bb = bb[:, 0:dk] + bb[:, dk:2 * dk] + bb[:, 2 * dk:3 * dk]
        b = bb[0:c]
        a = jnp.where(diag, jnp.sum(q * k, axis=1, keepdims=True), 0.0)
        for l in range(levels):
            bs = bb[(l + 1) * c:(l + 2) * c]
            e = jnp.exp(jnp.where(upper[l], b - bs, bs - b))
            qp = jnp.where(upper[l], q * e, 0.0).astype(BF16)
            kp = jnp.where(upper[l], 0.0, k * e).astype(BF16)
            a = a + jnp.where(same[l], _dot_nt(qp, kp), 0.0)
        vb = v.astype(BF16)
        o = _dot_nt((q * jnp.exp(b)).astype(BF16), st.astype(BF16)) + _dot(a.astype(BF16), vb)
        blast = b[c - 1:c, :]
        kl = (k * jnp.exp(blast - b)).astype(BF16)
        st_new = st * jnp.exp(blast) + _dot(v.T.astype(BF16), kl)
        o_ref[pl.ds(r0, c), :] = _hgrn_post(o, g, nw).astype(BF16)
        return st_new

    st = lax.fori_loop(0, n_chunks, chunk, jnp.zeros((dk, dk), F32))
    s_ref[0] = st.T


def _hgrn_prompt(hg, nw, *, batch, seq):
    hw = HG_HEADS * HG_DK
    mstack = _hgrn_cumsum_matrix()
    spec = lambda off: pl.BlockSpec((seq, HG_DK), lambda b, h: (b, off + h))
    return pl.pallas_call(
        _hgrn_prompt_kernel,
        out_shape=(jax.ShapeDtypeStruct((batch * seq, hw), BF16),
                   jax.ShapeDtypeStruct((batch * HG_HEADS, HG_DK, HG_DK), F32)),
        grid=(batch, HG_HEADS),
        in_specs=[spec(0), spec(HG_HEADS), spec(2 * HG_HEADS), spec(3 * HG_HEADS),
                  pl.BlockSpec(mstack.shape, lambda b, h: (0, 0)),
                  pl.BlockSpec((1, HG_DK), lambda b, h: (0, 0))],
        out_specs=(pl.BlockSpec((seq, HG_DK), lambda b, h: (b, h)),
                   pl.BlockSpec((1, HG_DK, HG_DK), lambda b, h: (b * HG_HEADS + h, 0, 0))),
        compiler_params=_cparams(("arbitrary", "arbitrary")),
        name="hgrn_prompt",
    )(hg, hg, hg, hg, mstack, nw)


def _hgrn_sample_kernel(hg_ref, s0_ref, nw_ref, o_ref, s_ref, *, dec, seqs):
    dk = HG_DK
    hw = HG_HEADS * HG_DK
    nw = nw_ref[...]
    zpad = jnp.zeros((dk - 3 * dec, dk), F32)

    def one_seq(si, carry):
        r0 = pl.multiple_of(si * dec, dec)
        for h in range(HG_HEADS):
            cols = slice(h * dk, (h + 1) * dk)
            q = hg_ref[pl.ds(r0, dec), h * dk:(h + 1) * dk]
            f = hg_ref[pl.ds(r0, dec), hw + h * dk:hw + (h + 1) * dk]
            v = hg_ref[pl.ds(r0, dec), 2 * hw + h * dk:2 * hw + (h + 1) * dk]
            g = hg_ref[pl.ds(r0, dec), 3 * hw + h * dk:3 * hw + (h + 1) * dk]
            colsT = jnp.concatenate([f, 1.0 - f, q, zpad], axis=0).T
            st = s0_ref[si * HG_HEADS + h]
            outs = []
            for t in range(dec):
                st = colsT[:, t:t + 1] * st + colsT[:, dec + t:dec + t + 1] * v[t:t + 1, :]
                outs.append(jnp.sum(colsT[:, 2 * dec + t:2 * dec + t + 1] * st, axis=0, keepdims=True))
            o = jnp.concatenate(outs, axis=0)
            o_ref[pl.ds(r0, dec), cols] = _hgrn_post(o, g, nw)
            s_ref[si * HG_HEADS + h] = st
        return carry

    lax.fori_loop(0, seqs, one_seq, 0)


def _hgrn_sample(hg, s0, nw, *, dec_batch, dec_seq, row0):
    hw = HG_HEADS * HG_DK
    seqs = 8
    rows = seqs * dec_seq
    kern = functools.partial(_hgrn_sample_kernel, dec=dec_seq, seqs=seqs)
    st_spec = pl.BlockSpec((seqs * HG_HEADS, HG_DK, HG_DK), lambda i: (i, 0, 0))
    return pl.pallas_call(
        kern,
        out_shape=(jax.ShapeDtypeStruct((dec_batch * dec_seq, hw), F32),
                   jax.ShapeDtypeStruct((dec_batch * HG_HEADS, HG_DK, HG_DK), F32)),
        grid=(dec_batch // seqs,),
        in_specs=[pl.BlockSpec((rows, 4 * hw), lambda i: (row0 // rows + i, 0)), st_spec,
                  pl.BlockSpec((1, HG_DK), lambda i: (0, 0))],
        out_specs=(pl.BlockSpec((rows, hw), lambda i: (i, 0)), st_spec),
        compiler_params=_cparams(("arbitrary",)),
        name="hgrn_sample",
    )(hg, s0, nw)


def _post_mixer_kernel(*refs, n_a, npt):
    hp_ref, hs_ref, mp_ref, ms_ref = refs[0:4]
    a_refs = refs[4:4 + 2 * n_a]
    w_refs = refs[4 + 2 * n_a:4 + 3 * n_a]
    nffn_ref, rw_ref, rb_ref, stril_ref = refs[4 + 3 * n_a:8 + 3 * n_a]
    hmid_ref, u_ref, rinfo_ref, counts_ref, carry = refs[8 + 3 * n_a:]
    i = pl.program_id(0)

    @pl.when(i == 0)
    def _():
        carry[...] = jnp.zeros_like(carry)

    def run(is_prompt):
        h = (hp_ref if is_prompt else hs_ref)[...]
        m3 = (mp_ref if is_prompt else ms_ref)[...]
        out = None
        for a in range(n_a):
            av = a_refs[2 * a + (0 if is_prompt else 1)][...].astype(BF16)
            t = _dot(av, w_refs[a][...])
            out = t if out is None else out + t
        hm = _gate_residual(h, m3, 2, out)
        hmid_ref[...] = hm
        u = _norm_mod(hm, nffn_ref[...], m3, 4, 3)
        for k in range(u.shape[1] // LANES):
            u_ref[:, k, :] = u[:, k * LANES:(k + 1) * LANES]

        u_hi = u.astype(BF16)
        u_lo = (u - u_hi.astype(F32)).astype(BF16)
        hh = _dot(u_hi, rw_ref[...])
        logits = hh[:, 0:LANES] + hh[:, LANES:2 * LANES] + _dot(u_lo, rw_ref[:, 0:LANES]) + rb_ref[...]
        col = lambda j: logits[:, j:j + 1]
        gl = [col(j) for j in range(N_GROUPS)]
        gmax = functools.reduce(jnp.maximum, gl)
        gsum = functools.reduce(lambda x, y: x + y, [jnp.exp(x - gmax) for x in gl])
        gval = 1.0 / gsum
        gidx = jnp.where(gl[0] == gmax, 0, jnp.where(gl[1] == gmax, 1, jnp.where(gl[2] == gmax, 2, 3)))
        el = []
        for k in range(EXPERTS_PER_GROUP):
            c = [col(N_GROUPS + g * EXPERTS_PER_GROUP + k) for g in range(N_GROUPS)]
            el.append(jnp.where(gidx == 0, c[0], jnp.where(gidx == 1, c[1], jnp.where(gidx == 2, c[2], c[3]))))
        emax = functools.reduce(jnp.maximum, el)
        pe = [jnp.exp(x - emax) for x in el]
        esum = functools.reduce(lambda x, y: x + y, pe)
        pk = [x / esum for x in pe]
        v1 = functools.reduce(jnp.maximum, pk)
        i1 = jnp.where(pk[0] == v1, 0, jnp.where(pk[1] == v1, 1, jnp.where(pk[2] == v1, 2, 3)))
        pk2 = [jnp.where(i1 == k, -1.0, pk[k]) for k in range(EXPERTS_PER_GROUP)]
        v2 = functools.reduce(jnp.maximum, pk2)
        i2 = jnp.where(pk2[0] == v2, 0, jnp.where(pk2[1] == v2, 1, jnp.where(pk2[2] == v2, 2, 3)))
        den = v1 + v2
        w1 = gval * v1 / den
        w2 = gval * v2 / den
        lo = jnp.minimum(i1, i2)
        hi = jnp.maximum(i1, i2)
        pair = jnp.where(lo == 0, hi - 1, jnp.where(lo == 1, hi + 1, 5))
        bucket = gidx * 6 + pair
        w_lo = jnp.where(i1 < i2, w1, w2)
        w_hi = jnp.where(i1 < i2, w2, w1)

        r = h.shape[0]
        lane = lax.broadcasted_iota(I32, (r, LANES), 1)
        onehot = lane == bucket
        before = _dot(stril_ref[...], onehot.astype(BF16)) + carry[...]
        rank = jnp.sum(jnp.where(onehot, before, 0.0), axis=1, keepdims=True)
        carry[...] = carry[...] + jnp.sum(onehot.astype(F32), axis=0, keepdims=True)
        rinfo_ref[...] = jnp.where(lane == 0, bucket.astype(F32),
                                   jnp.where(lane == 1, w_lo, jnp.where(lane == 2, w_hi,
                                                                        jnp.where(lane == 3, rank, 0.0))))
        counts_ref[...] = carry[...]

    _by_group(i, npt, run)


def _post_mixer(h_pair, h_off_s, mp, ms, a_pairs, ws, nffn, rw, rb, *, seq):
    hp, hs = h_pair
    d = hp.shape[1]
    tp = a_pairs[0][0].shape[0]
    ts = a_pairs[0][1].shape[0]
    npt, nst = tp // ROW_TILE, ts // ROW_TILE
    t = tp + ts
    n_a = len(a_pairs)
    tiles_per_seq = seq // ROW_TILE
    const = lambda shape: pl.BlockSpec(shape, lambda i: (0,) * len(shape))
    stril = jnp.asarray(np.tril(np.ones((ROW_TILE, ROW_TILE), np.float32), -1), dtype=BF16)
    in_specs = [*_pair_specs(d, npt, off_s=h_off_s), *_mod_specs(d, npt, tiles_per_seq)]
    args = [hp, hs, mp, ms]
    for ap, as_ in a_pairs:
        in_specs += list(_pair_specs(ap.shape[1], npt))
        args += [ap, as_]
    for w in ws:
        in_specs.append(const(w.shape))
        args.append(w)
    in_specs += [const((1, d)), const(rw.shape), const((1, LANES)), const(stril.shape)]
    args += [nffn, rw, rb, stril]
    tile = lambda c: pl.BlockSpec((ROW_TILE, c), lambda i: (i, 0))
    kern = functools.partial(_post_mixer_kernel, n_a=n_a, npt=npt)
    return pl.pallas_call(
        kern,
        out_shape=(jax.ShapeDtypeStruct((t, d), F32), jax.ShapeDtypeStruct((t, d // LANES, LANES), F32),
                   jax.ShapeDtypeStruct((t, LANES), F32), jax.ShapeDtypeStruct((1, LANES), F32)),
        grid=(npt + nst,),
        in_specs=in_specs,
        out_specs=(tile(d), pl.BlockSpec((ROW_TILE, d // LANES, LANES), lambda i: (i, 0, 0)), tile(LANES),
                   const((1, LANES))),
        scratch_shapes=[pltpu.VMEM((1, LANES), F32)],
        compiler_params=_cparams(("arbitrary",)),
        name="post_mixer",
    )(*args)


def _for_rows(rows, fn):
    def body(r, c):
        fn(r)
        return c

    lax.fori_loop(0, rows, body, 0, unroll=8)


def _scatter_kernel(dest_ref, src_ref, init_ref, o_ref, sem):
    del init_ref
    rows = src_ref.shape[0]
    base = pl.program_id(0) * rows
    copy = lambda r: pltpu.make_async_copy(src_ref.at[r], o_ref.at[dest_ref[base + r]], sem)
    _for_rows(rows, lambda r: copy(r).start())
    _for_rows(rows, lambda r: copy(r).wait())


def _scatter_tokens(dest, src, n_slots):
    t = src.shape[0]
    rows = ROW_TILE
    grid_spec = pltpu.PrefetchScalarGridSpec(
        num_scalar_prefetch=1,
        grid=(t // rows,),
        in_specs=[pl.BlockSpec((rows,) + src.shape[1:], lambda i, d: (i, 0, 0)), pl.BlockSpec(memory_space=pl.ANY)],
        out_specs=pl.BlockSpec(memory_space=pl.ANY),
        scratch_shapes=[pltpu.SemaphoreType.DMA(())],
    )
    return pl.pallas_call(
        _scatter_kernel,
        out_shape=jax.ShapeDtypeStruct((n_slots,) + src.shape[1:], src.dtype),
        grid_spec=grid_spec,
        input_output_aliases={2: 0},
        compiler_params=_cparams(("arbitrary",)),
        name="moe_scatter",
    )(dest, src, jnp.zeros((n_slots,) + src.shape[1:], src.dtype))


def _moe_kernel(ea_ref, eb_ref, valid_ref, fresh_ref, x_ref,
                w1a_ref, w3a_ref, w2a_ref, w1b_ref, w3b_ref, w2b_ref, o_ref, wbuf1, wbuf3, wbuf2):
    i = pl.program_id(0)
    nk = x_ref.shape[1]

    @pl.when(fresh_ref[i] == 1)
    def _():
        wbuf1[0] = w1a_ref[0, 0].astype(BF16)
        wbuf3[0] = w3a_ref[0, 0].astype(BF16)
        wbuf2[0] = w2a_ref[0, 0].astype(BF16)
        wbuf1[1] = w1b_ref[0, 0].astype(BF16)
        wbuf3[1] = w3b_ref[0, 0].astype(BF16)
        wbuf2[1] = w2b_ref[0, 0].astype(BF16)

    @pl.when(valid_ref[i] == 1)
    def _():
        xb = jnp.concatenate([x_ref[:, k, :] for k in range(nk)], axis=1).astype(BF16)
        for s in (0, 1):
            h1 = _dot(xb, wbuf1[s])
            h3 = _dot(xb, wbuf3[s])
            y = _dot(((h1 * _sigmoid(h1)) * h3).astype(BF16), wbuf2[s])
            for k in range(nk):
                o_ref[:, s * nk + k, :] = y[:, k * LANES:(k + 1) * LANES]

    @pl.when(valid_ref[i] == 0)
    def _():
        o_ref[...] = jnp.zeros_like(o_ref)


def _moe_experts(tile_ea, tile_eb, tile_valid, tile_fresh, xs, w1, w3, w2, layer):
    n_slots, nk, _ = xs.shape
    d = nk * LANES
    ff = w1.shape[-1]
    wa = lambda shape: pl.BlockSpec((1, 1) + shape, lambda i, ea, eb, va, fr: (layer, ea[i], 0, 0))
    wb = lambda shape: pl.BlockSpec((1, 1) + shape, lambda i, ea, eb, va, fr: (layer, eb[i], 0, 0))
    rows = lambda k: pl.BlockSpec((MOE_TILE, k, LANES), lambda i, ea, eb, va, fr: (i, 0, 0))
    grid_spec = pltpu.PrefetchScalarGridSpec(
        num_scalar_prefetch=4,
        grid=(n_slots // MOE_TILE,),
        in_specs=[rows(nk), wa((d, ff)), wa((d, ff)), wa((ff, d)), wb((d, ff)), wb((d, ff)), wb((ff, d))],
        out_specs=rows(2 * nk),
        scratch_shapes=[pltpu.VMEM((2, d, ff), BF16), pltpu.VMEM((2, d, ff), BF16), pltpu.VMEM((2, ff, d), BF16)],
    )
    return pl.pallas_call(
        _moe_kernel,
        out_shape=jax.ShapeDtypeStruct((n_slots, 2 * nk, LANES), F32),
        grid_spec=grid_spec,
        compiler_params=_cparams(("arbitrary",)),
        name="moe_experts",
    )(tile_ea, tile_eb, tile_valid, tile_fresh, xs, w1, w3, w2, w1, w3, w2)


def _moe_plan(rinfo, counts, n_slots):
    n_tiles = n_slots // MOE_TILE
    cnt = counts[0, :N_BUCKETS].astype(I32)
    padded = ((cnt + MOE_TILE - 1) // MOE_TILE) * MOE_TILE
    ends = jnp.cumsum(padded)
    starts = ends - padded
    bucket = rinfo[:, 0].astype(I32)
    dest = starts[bucket] + rinfo[:, 3].astype(I32)
    n_valid = ends[-1] // MOE_TILE
    tiles = jnp.arange(n_tiles, dtype=I32)
    tb = jnp.sum((ends[None, :] <= (tiles * MOE_TILE)[:, None]).astype(I32), axis=1)
    tb = jnp.minimum(tb, N_BUCKETS - 1)
    valid = tiles < n_valid
    tb = jnp.where(valid, tb, tb[jnp.maximum(n_valid - 1, 0)])
    lo = jnp.asarray(PAIR_LO, I32)
    hi = jnp.asarray(PAIR_HI, I32)
    ea = (tb // 6) * EXPERTS_PER_GROUP + lo[tb % 6]
    eb = (tb // 6) * EXPERTS_PER_GROUP + hi[tb % 6]
    fresh = jnp.concatenate([jnp.ones((1,), I32), (tb[1:] != tb[:-1]).astype(I32)])
    return dest, ea, eb, valid.astype(I32), fresh


def _moe(u_tiles, rinfo, counts, w1, w3, w2, layer):
    t = u_tiles.shape[0]
    n_slots = t + N_BUCKETS * MOE_TILE
    dest, ea, eb, valid, fresh = _moe_plan(rinfo, counts, n_slots)
    xs = _scatter_tokens(dest, u_tiles, n_slots)
    ys = _moe_experts(ea, eb, valid, fresh, xs, w1, w3, w2, layer)
    return dest, ys


def _combine_kernel(dest_ref, ys_ref, hm_ref, rinfo_ref, mp_ref, ms_ref, nf_ref, *rest, npt, final):
    if final:
        yp_ref, ysm_ref, gbuf, sem = rest
    else:
        h_ref, gbuf, sem = rest
    i = pl.program_id(0)
    n = pl.num_programs(0)
    rows = gbuf.shape[1]
    nk = gbuf.shape[2] // 2

    def copy(tile, r):
        slot = tile % 2
        return pltpu.make_async_copy(ys_ref.at[dest_ref[tile * rows + r]], gbuf.at[slot, r], sem.at[slot])

    @pl.when(i == 0)
    def _():
        _for_rows(rows, lambda r: copy(0, r).start())

    @pl.when(i + 1 < n)
    def _():
        _for_rows(rows, lambda r: copy(i + 1, r).start())

    _for_rows(rows, lambda r: copy(i, r).wait())
    slot = i % 2
    ya = jnp.concatenate([gbuf[slot, :, k, :] for k in range(nk)], axis=1)
    yb = jnp.concatenate([gbuf[slot, :, nk + k, :] for k in range(nk)], axis=1)
    rinfo = rinfo_ref[...]
    moe = rinfo[:, 1:2] * ya + rinfo[:, 2:3] * yb

    def run(is_prompt):
        m3 = (mp_ref if is_prompt else ms_ref)[...]
        h = _gate_residual(hm_ref[...], m3, 5, moe)
        if final:
            y = h * lax.rsqrt(jnp.mean(h * h, axis=-1, keepdims=True) + NORM_EPS) * nf_ref[...]
            (yp_ref if is_prompt else ysm_ref)[...] = y
        else:
            h_ref[...] = h

    _by_group(i, npt, run)


def _combine(dest, ys, hmid, rinfo, mp, ms, nf, *, tp, seq, final):
    t, d = hmid.shape
    npt = tp // ROW_TILE
    tiles_per_seq = seq // ROW_TILE
    mpm = lambda f: (lambda i, dref: f(i))
    mp_spec, ms_spec = _mod_specs(d, npt, tiles_per_seq)
    yp_spec, ysm_spec = _pair_specs(d, npt)
    wrap = lambda s: pl.BlockSpec(s.block_shape, mpm(s.index_map))
    if final:
        out_shape = (jax.ShapeDtypeStruct((tp, d), F32), jax.ShapeDtypeStruct((t - tp, d), F32))
        out_specs = (wrap(yp_spec), wrap(ysm_spec))
    else:
        out_shape = jax.ShapeDtypeStruct((t, d), F32)
        out_specs = pl.BlockSpec((ROW_TILE, d), lambda i, dref: (i, 0))
    grid_spec = pltpu.PrefetchScalarGridSpec(
        num_scalar_prefetch=1,
        grid=(t // ROW_TILE,),
        in_specs=[pl.BlockSpec(memory_space=pl.ANY),
                  pl.BlockSpec((ROW_TILE, d), lambda i, dref: (i, 0)),
                  pl.BlockSpec((ROW_TILE, LANES), lambda i, dref: (i, 0)),
                  wrap(mp_spec), wrap(ms_spec),
                  pl.BlockSpec((1, d), lambda i, dref: (0, 0))],
        out_specs=out_specs,
        scratch_shapes=[pltpu.VMEM((2, ROW_TILE) + ys.shape[1:], F32), pltpu.SemaphoreType.DMA((2,))],
    )
    kern = functools.partial(_combine_kernel, npt=npt, final=final)
    return pl.pallas_call(
        kern, out_shape=out_shape, grid_spec=grid_spec,
        compiler_params=_cparams(("arbitrary",)),
        name="moe_combine_final" if final else "moe_combine",
    )(dest, ys, hmid, rinfo, mp, ms, nf)


def _odd_pre_kernel(hp_ref, hs_ref, mp_ref, ms_ref, nw_ref, w_ref, gate_ref, rec_ref, *, npt):
    i = pl.program_id(0)
    width = gate_ref.shape[1]

    def run(is_prompt):
        h = (hp_ref if is_prompt else hs_ref)[...]
        m3 = (mp_ref if is_prompt else ms_ref)[...]
        ub = _norm_mod(h, nw_ref[...], m3, 1, 0).astype(BF16)
        x = _dot(ub, w_ref[:, 0:width])
        gate_ref[...] = 0.5 * x * (1.0 + jnp.tanh(np.sqrt(2.0 / np.pi) * (x + 0.044715 * (x * x * x))))
        rec_ref[...] = _dot(ub, w_ref[:, width:2 * width])

    _by_group(i, npt, run)


def _odd_pre(h, mp, ms, nw, w_in, *, tp, seq):
    t, d = h.shape
    npt = tp // ROW_TILE
    width = w_in.shape[1] // 2
    const = lambda shape: pl.BlockSpec(shape, lambda i: (0,) * len(shape))
    tile = lambda c: pl.BlockSpec((ROW_TILE, c), lambda i: (i, 0))
    kern = functools.partial(_odd_pre_kernel, npt=npt)
    return pl.pallas_call(
        kern,
        out_shape=(jax.ShapeDtypeStruct((t, width), F32), jax.ShapeDtypeStruct((t, width), F32)),
        grid=(t // ROW_TILE,),
        in_specs=[*_pair_specs(d, npt, off_s=npt), *_mod_specs(d, npt, seq // ROW_TILE),
                  const((1, d)), const(w_in.shape)],
        out_specs=(tile(width), tile(width)),
        compiler_params=_cparams(("arbitrary",)),
        name="odd_pre",
    )(h, h, mp, ms, nw, w_in)


def _lru_kernel(rec_ref, gate_ref, c0_ref, h0_ref, cw_ref, cb_ref, wa_ref, ba_ref, wx_ref, bx_ref, lam_ref,
                yg_ref, cs_ref, hl_ref, hist, hc, a_s, h_s, *, nb, tl):
    i = pl.program_id(1)
    r, width = rec_ref.shape
    taps = CONV_WIDTH
    blk = width // LRU_HEADS

    @pl.when(i == 0)
    def _():
        hist[:, SUBLANES - (taps - 1):SUBLANES, :] = c0_ref[...]
        hc[...] = h0_ref[...]

    rec3 = rec_ref[...].reshape(nb, tl, width)
    hist[:, SUBLANES:SUBLANES + tl, :] = rec3
    cw = cw_ref[...]
    conv = cb_ref[...] + rec3 * cw[taps - 1:taps, :]
    for back in range(1, taps):
        conv = conv + hist[:, SUBLANES - back:SUBLANES - back + tl, :] * cw[taps - 1 - back:taps - back, :]
    tail = hist[:, SUBLANES + tl - (taps - 1):SUBLANES + tl, :]
    cs_ref[...] = tail
    hist[:, SUBLANES - (taps - 1):SUBLANES, :] = tail

    cf = conv.reshape(r, width)
    cb16 = cf.astype(BF16)
    gr = jnp.concatenate([_dot(cb16[:, h * blk:(h + 1) * blk], wa_ref[h]) for h in range(LRU_HEADS)], axis=1)
    gi = jnp.concatenate([_dot(cb16[:, h * blk:(h + 1) * blk], wx_ref[h]) for h in range(LRU_HEADS)], axis=1)
    rg = _sigmoid(gr + ba_ref[...])
    ig = _sigmoid(gi + bx_ref[...])
    z = -lam_ref[...]
    softplus = jnp.maximum(z, 0.0) + jnp.log1p(jnp.exp(-jnp.abs(z)))
    log_a = -LRU_C * rg * softplus
    a = jnp.exp(log_a)
    gx = jnp.sqrt(1.0 - a * a) * (ig * cf)

    groups = r // SUBLANES
    sub = lax.broadcasted_iota(I32, (groups, SUBLANES, width), 1)
    aa, hh = a.reshape(groups, SUBLANES, width), gx.reshape(groups, SUBLANES, width)
    for dlt in (1, 2, 4):
        ok = sub >= dlt
        a_sh = pltpu.roll(aa, dlt, 1)
        h_sh = pltpu.roll(hh, dlt, 1)
        hh = jnp.where(ok, hh + aa * h_sh, hh)
        aa = jnp.where(ok, aa * a_sh, aa)
    aa, hh = aa.reshape(r, width), hh.reshape(r, width)

    if tl == SUBLANES:
        y3 = hh.reshape(nb, tl, width) + aa.reshape(nb, tl, width) * hc[...]
        hl_ref[...] = y3[:, tl - 1:tl, :]
        y = y3.reshape(r, width)
    else:
        a_s[...] = aa
        h_s[...] = hh

        def group(j, carry):
            r0 = pl.multiple_of(j * SUBLANES, SUBLANES)
            yj = h_s[pl.ds(r0, SUBLANES), :] + a_s[pl.ds(r0, SUBLANES), :] * carry
            h_s[pl.ds(r0, SUBLANES), :] = yj
            return yj[SUBLANES - 1:SUBLANES, :]

        last = lax.fori_loop(0, r // SUBLANES, group, hc[0])
        hc[0] = last
        hl_ref[0] = last
        y = h_s[...]
    yg_ref[...] = (y * gate_ref[...]).astype(BF16)


def _lru(rec, gate, conv0, h0, cw, cb, wa, ba, wx, bx, lam, *, batch, seq, row0):
    width = rec.shape[1]
    taps = CONV_WIDTH
    if seq == SUBLANES:
        nb, tl = ROW_TILE // SUBLANES, SUBLANES
    else:
        nb, tl = 1, ROW_TILE
    r = nb * tl
    n_l = seq // tl
    blk0 = row0 // r
    const = lambda shape: pl.BlockSpec(shape, lambda b, i: (0,) * len(shape))
    tile_in = pl.BlockSpec((r, width), lambda b, i: (blk0 + b * n_l + i, 0))
    kern = functools.partial(_lru_kernel, nb=nb, tl=tl)
    return pl.pallas_call(
        kern,
        out_shape=(jax.ShapeDtypeStruct((batch * seq, width), BF16),
                   jax.ShapeDtypeStruct((batch, taps - 1, width), F32),
                   jax.ShapeDtypeStruct((batch, 1, width), F32)),
        grid=(batch // nb, n_l),
        in_specs=[tile_in, tile_in,
                  pl.BlockSpec((nb, taps - 1, width), lambda b, i: (b, 0, 0)),
                  pl.BlockSpec((nb, 1, width), lambda b, i: (b, 0, 0)),
                  const((taps, width)), const((1, width)), const(wa.shape), const((1, width)),
                  const(wx.shape), const((1, width)), const((1, width))],
        out_specs=(pl.BlockSpec((r, width), lambda b, i: (b * n_l + i, 0)),
                   pl.BlockSpec((nb, taps - 1, width), lambda b, i: (b, 0, 0)),
                   pl.BlockSpec((nb, 1, width), lambda b, i: (b, 0, 0))),
        scratch_shapes=[pltpu.VMEM((nb, tl + SUBLANES, width), F32), pltpu.VMEM((nb, 1, width), F32),
                        pltpu.VMEM((r, width), F32), pltpu.VMEM((r, width), F32)],
        compiler_params=_cparams(("arbitrary", "arbitrary")),
        name="rg_lru_sample" if seq == SUBLANES else "rg_lru_prompt",
    )(rec, gate, conv0, h0, cw, cb, wa, ba, wx, bx, lam)


def kernel(x_prompt, x_sample, c_prompt, c_sample, cache_k, cache_v, state_hgrn, state_conv, state_lru, page_table,
           ada_w, ada_b, norm_mix, norm_ffn, norm_final, w_in_even, w_out_even, hg_lower_bounds, hg_norm_w, w_in_odd,
           conv_w, conv_b, lru_wa, lru_ba, lru_wx, lru_bx, lru_lambda, w_out_odd, router_group_w, router_group_b,
           router_expert_w, router_expert_b, moe_w1, moe_w3, moe_w2):
    batch, seq, d = x_prompt.shape
    dec_batch, dec_seq, _ = x_sample.shape
    depth = ada_w.shape[0]
    assert depth == 2 and seq % ROW_TILE == 0 and seq % MOBA_BLOCK == 0
    tp, ts = batch * seq, dec_batch * dec_seq
    n_pages, page = page_table.shape[1], cache_k.shape[2]
    past_len = n_pages * page
    aw = MOBA_HEADS * MOBA_HEAD_DIM
    hw = HG_HEADS * HG_DK

    xp = x_prompt.reshape(tp, d)
    xs = x_sample.reshape(ts, d)
    mods = _ada_mods(jnp.concatenate([c_prompt, c_sample], axis=0), ada_w, ada_b)
    mods = mods.reshape(depth, batch + dec_batch, 6, d)
    mods_p, mods_s = mods[:, :batch], mods[:, batch:]

    half = MOBA_HEAD_DIM // 2
    inv_freq = jnp.power(ROPE_THETA, -jnp.arange(half, dtype=F32) / half)
    invf = jnp.tile(inv_freq, LANES // half).reshape(1, LANES)

    def router_mats(l):
        rw = jnp.zeros((d, LANES), F32)
        rw = rw.at[:, 0:N_GROUPS].set(router_group_w[l])
        rw = rw.at[:, N_GROUPS:N_GROUPS + N_GROUPS * EXPERTS_PER_GROUP].set(router_expert_w[l])
        rb = jnp.zeros((1, LANES), F32)
        rb = rb.at[0, 0:N_GROUPS].set(router_group_b[l])
        rb = rb.at[0, N_GROUPS:N_GROUPS + N_GROUPS * EXPERTS_PER_GROUP].set(router_expert_b[l])
        rw_hi = rw.astype(BF16)
        rw_lo = (rw - rw_hi.astype(F32)).astype(BF16)
        return jnp.concatenate([rw_hi, rw_lo], axis=1), rb

    q_all, k_p, k_s, v_p, v_s, hg = _even_pre(
        xp, xs, mods_p[0], mods_s[0], norm_mix[0].reshape(1, d), w_in_even[0].astype(BF16), hg_lower_bounds, invf,
        seq=seq, past_len=past_len, layer_slot=0)
    oa_p = _attn_prompt(q_all, k_p, v_p, batch=batch, seq=seq)
    oa_s = _attn_sample(page_table, q_all, k_s, v_s, cache_k, cache_v,
                        layer=0, dec_batch=dec_batch, dec_seq=dec_seq, q_row0=tp)
    hnw = hg_norm_w[0].reshape(1, HG_DK)
    ob_p, hgrn_p = _hgrn_prompt(hg, hnw, batch=batch, seq=seq)
    ob_s, hgrn_s = _hgrn_sample(hg, state_hgrn[0].reshape(dec_batch * HG_HEADS, HG_DK, HG_DK), hnw,
                                dec_batch=dec_batch, dec_seq=dec_seq, row0=tp)
    w_out = w_out_even[0].astype(BF16)
    rw, rb = router_mats(0)
    hmid, u, rinfo, counts = _post_mixer((xp, xs), 0, mods_p[0], mods_s[0], [(oa_p, oa_s), (ob_p, ob_s)],
                                         [w_out[:aw], w_out[aw:]], norm_ffn[0].reshape(1, d), rw, rb, seq=seq)
    dest, ys = _moe(u, rinfo, counts, moe_w1, moe_w3, moe_w2, 0)
    h1 = _combine(dest, ys, hmid, rinfo, mods_p[0], mods_s[0], norm_final.reshape(1, d), tp=tp, seq=seq,
                  final=False)

    gate, rec = _odd_pre(h1, mods_p[1], mods_s[1], norm_mix[1].reshape(1, d), w_in_odd[0].astype(BF16), tp=tp, seq=seq)
    width = rec.shape[1]
    lru_args = (conv_w[0], conv_b[0].reshape(1, width), lru_wa[0].astype(BF16), lru_ba[0].reshape(1, width),
                lru_wx[0].astype(BF16), lru_bx[0].reshape(1, width), lru_lambda[0].reshape(1, width))
    yg_p, conv_p, lru_p = _lru(rec, gate, jnp.zeros((batch, CONV_WIDTH - 1, width), F32),
                               jnp.zeros((batch, 1, width), F32), *lru_args, batch=batch, seq=seq, row0=0)
    yg_s, conv_s, lru_s = _lru(rec, gate, state_conv[0], state_lru[0].reshape(dec_batch, 1, width), *lru_args,
                               batch=dec_batch, seq=dec_seq, row0=tp)
    rw, rb = router_mats(1)
    hmid, u, rinfo, counts = _post_mixer((h1, h1), tp // ROW_TILE, mods_p[1], mods_s[1], [(yg_p, yg_s)],
                                         [w_out_odd[0].astype(BF16)], norm_ffn[1].reshape(1, d), rw, rb, seq=seq)
    dest, ys = _moe(u, rinfo, counts, moe_w1, moe_w3, moe_w2, 1)
    y_p, y_s = _combine(dest, ys, hmid, rinfo, mods_p[1], mods_s[1], norm_final.reshape(1, d), tp=tp, seq=seq,
                        final=True)

    return (y_p.reshape(batch, seq, d), y_s.reshape(dec_batch, dec_seq, d),
            k_p.reshape(1, batch, seq, MOBA_HEADS, MOBA_HEAD_DIM), v_p.reshape(1, batch, seq, MOBA_HEADS, MOBA_HEAD_DIM),
            k_s.reshape(1, dec_batch, dec_seq, MOBA_HEADS, MOBA_HEAD_DIM),
            v_s.reshape(1, dec_batch, dec_seq, MOBA_HEADS, MOBA_HEAD_DIM),
            hgrn_p.reshape(1, batch, HG_HEADS, HG_DK, HG_DK), hgrn_s.reshape(1, dec_batch, HG_HEADS, HG_DK, HG_DK),
            conv_p.reshape(1, batch, CONV_WIDTH - 1, width), conv_s.reshape(1, dec_batch, CONV_WIDTH - 1, width),
            lru_p.reshape(1, batch, width), lru_s.reshape(1, dec_batch, width))
```

```python
import functools

import numpy as np
import jax
import jax.numpy as jnp
from jax import lax
from jax.experimental import pallas as pl
from jax.experimental.pallas import tpu as pltpu

F32 = jnp.float32
BF16 = jnp.bfloat16
I32 = jnp.int32
HIGHEST = lax.Precision.HIGHEST

MOBA_HEADS = 8
MOBA_HEAD_DIM = 64
MOBA_BLOCK = 256
MOBA_TOPK = 3
ROPE_THETA = 10000.0
HG_HEADS = 4
HG_DK = 128
LRU_HEADS = 4
CONV_WIDTH = 4
LRU_C = 8.0
N_GROUPS = 4
EXPERTS_PER_GROUP = 4
NORM_EPS = 1e-6

LANES = 128
SUBLANES = 8
VMEM_LIMIT = 56 * 1024 * 1024

ROW_TILE = 256
MOE_TILE = 256
HG_CHUNK = 128
N_BUCKETS = N_GROUPS * 6
PAIR_LO = (0, 0, 0, 1, 1, 2)
PAIR_HI = (1, 2, 3, 2, 3, 3)
NEG = -1e30


def _cparams(sem, vmem=VMEM_LIMIT):
    return pltpu.CompilerParams(dimension_semantics=sem, vmem_limit_bytes=vmem)


def _dot(a, b, **kw):
    return jnp.dot(a, b, preferred_element_type=F32, **kw)


def _dot_nt(a, b, **kw):
    return lax.dot_general(a, b, (((1,), (1,)), ((), ())), preferred_element_type=F32, **kw)


def _sigmoid(x):
    return jax.nn.sigmoid(x)


def _by_group(i, n_prompt_tiles, fn):
    @pl.when(i < n_prompt_tiles)
    def _():
        fn(True)

    @pl.when(i >= n_prompt_tiles)
    def _():
        fn(False)


def _pair_specs(cols, npt, rows=ROW_TILE, off_p=0, off_s=0):
    return (pl.BlockSpec((rows, cols), lambda i: (off_p + jnp.minimum(i, npt - 1), 0)),
            pl.BlockSpec((rows, cols), lambda i: (off_s + jnp.maximum(i - npt, 0), 0)))


def _mod_specs(d, npt, tiles_per_seq, rows=ROW_TILE):
    return (pl.BlockSpec((1, 6, d), lambda i: (jnp.minimum(i, npt - 1) // tiles_per_seq, 0, 0)),
            pl.BlockSpec((rows // SUBLANES, 6, d), lambda i: (jnp.maximum(i - npt, 0), 0, 0)))


def _norm_mod(x, nw, m3, scale_idx, shift_idx):
    r, d = x.shape
    nb = m3.shape[0]
    var = jnp.mean(x * x, axis=-1, keepdims=True)
    y = x * lax.rsqrt(var + NORM_EPS) * nw
    y3 = y.reshape(nb, r // nb, d)
    u3 = y3 * (1.0 + m3[:, scale_idx:scale_idx + 1, :]) + m3[:, shift_idx:shift_idx + 1, :]
    return u3.reshape(r, d)


def _gate_residual(h, m3, gate_idx, out):
    r, d = h.shape
    nb = m3.shape[0]
    return (h.reshape(nb, r // nb, d) + m3[:, gate_idx:gate_idx + 1, :] * out.reshape(nb, r // nb, d)).reshape(r, d)


def _ada_kernel(c_ref, w_ref, b_ref, o_ref):
    o_ref[0] = _dot(c_ref[...], w_ref[0], precision=HIGHEST) + b_ref[0]


def _ada_mods(c_all, ada_w, ada_b):
    depth, d, n6 = ada_w.shape
    nb = c_all.shape[0]
    tn = 1024
    return pl.pallas_call(
        _ada_kernel,
        out_shape=jax.ShapeDtypeStruct((depth, nb, n6), F32),
        grid=(depth, n6 // tn),
        in_specs=[pl.BlockSpec((nb, d), lambda l, n: (0, 0)),
                  pl.BlockSpec((1, d, tn), lambda l, n: (l, 0, n)),
                  pl.BlockSpec((1, 1, tn), lambda l, n: (l, 0, n))],
        out_specs=pl.BlockSpec((1, nb, tn), lambda l, n: (l, 0, n)),
        compiler_params=_cparams(("arbitrary", "arbitrary")),
        name="ada_mods",
    )(c_all, ada_w, ada_b.reshape(depth, 1, n6))


def _even_pre_kernel(xp_ref, xs_ref, mp_ref, ms_ref, nw_ref, w_ref, hb_ref, invf_ref,
                     q_ref, kp_ref, ks_ref, vp_ref, vs_ref, hg_ref, *, npt, tiles_per_seq, past_len, layer_slot):
    i = pl.program_id(0)
    aw = MOBA_HEADS * MOBA_HEAD_DIM
    hw = HG_HEADS * HG_DK

    def run(is_prompt):
        x = (xp_ref if is_prompt else xs_ref)[...]
        m3 = (mp_ref if is_prompt else ms_ref)[...]
        r = x.shape[0]
        nb = m3.shape[0]
        tl = r // nb
        ub = _norm_mod(x, nw_ref[...], m3, 1, 0).astype(BF16)

        pos0 = (i % tiles_per_seq) * r if is_prompt else past_len
        pos = (pos0 + lax.broadcasted_iota(I32, (nb, tl, LANES), 1)).astype(F32).reshape(r, LANES)
        ang = pos * invf_ref[...]
        reps = aw // LANES
        cos = jnp.concatenate([jnp.cos(ang)] * reps, axis=1)
        sin = jnp.concatenate([jnp.sin(ang)] * reps, axis=1)
        lane = lax.broadcasted_iota(I32, (r, aw), 1)
        half = MOBA_HEAD_DIM // 2
        first = (lane % MOBA_HEAD_DIM) < half
        sin = jnp.where(first, -sin, sin)

        def rope(t):
            rot = jnp.where(first, pltpu.roll(t, aw - half, 1), pltpu.roll(t, half, 1))
            return t * cos + rot * sin

        def seg(a, b):
            return _dot(ub, w_ref[:, a:b])

        q_ref[...] = rope(seg(0, aw))
        k = rope(seg(aw, 2 * aw))
        v = seg(2 * aw, 3 * aw)
        if is_prompt:
            kp_ref[...] = k
            vp_ref[...] = v
        else:
            ks_ref[...] = k
            vs_ref[...] = v
        o = 3 * aw
        qb = seg(o, o + hw)
        fb = seg(o + hw, o + 2 * hw)
        hb = hb_ref[...]
        e = jnp.exp(hb - jnp.max(hb, axis=0, keepdims=True))
        sm = e / jnp.sum(e, axis=0, keepdims=True)
        lb = jnp.sum(sm[0:layer_slot + 1, :], axis=0, keepdims=True)
        hg_ref[:, 0:hw] = qb * _sigmoid(qb)
        hg_ref[:, hw:2 * hw] = lb + (1.0 - lb) * _sigmoid(fb)
        hg_ref[:, 2 * hw:3 * hw] = seg(o + 2 * hw, o + 3 * hw)
        hg_ref[:, 3 * hw:4 * hw] = seg(o + 3 * hw, o + 4 * hw)

    _by_group(i, npt, run)


def _even_pre(xp, xs, mp, ms, nw, w_in, hb, invf, *, seq, past_len, layer_slot):
    tp, d = xp.shape
    ts = xs.shape[0]
    npt, nst = tp // ROW_TILE, ts // ROW_TILE
    tiles_per_seq = seq // ROW_TILE
    aw = MOBA_HEADS * MOBA_HEAD_DIM
    hw = HG_HEADS * HG_DK
    t = tp + ts
    const = lambda shape: pl.BlockSpec(shape, lambda i: (0,) * len(shape))
    kern = functools.partial(_even_pre_kernel, npt=npt, tiles_per_seq=tiles_per_seq, past_len=past_len,
                             layer_slot=layer_slot)
    kp_spec, ks_spec = _pair_specs(aw, npt)
    return pl.pallas_call(
        kern,
        out_shape=(jax.ShapeDtypeStruct((t, aw), F32),
                   jax.ShapeDtypeStruct((tp, aw), F32), jax.ShapeDtypeStruct((ts, aw), F32),
                   jax.ShapeDtypeStruct((tp, aw), F32), jax.ShapeDtypeStruct((ts, aw), F32),
                   jax.ShapeDtypeStruct((t, 4 * hw), F32)),
        grid=(npt + nst,),
        in_specs=[*_pair_specs(d, npt), *_mod_specs(d, npt, tiles_per_seq),
                  const((1, d)), const(w_in.shape), const(hb.shape), const((1, LANES))],
        out_specs=(pl.BlockSpec((ROW_TILE, aw), lambda i: (i, 0)), kp_spec, ks_spec, kp_spec, ks_spec,
                   pl.BlockSpec((ROW_TILE, 4 * hw), lambda i: (i, 0))),
        compiler_params=_cparams(("arbitrary",)),
        name="even_pre",
    )(xp, xs, mp, ms, nw, w_in, hb, invf)


def _top_blocks(g, valid):
    lane = lax.broadcasted_iota(I32, g.shape, 1)
    jl = lane % SUBLANES
    g = jnp.where(valid, g, -jnp.inf)
    cnt = jnp.zeros(g.shape, I32)
    for d in range(1, SUBLANES):
        wrapped = (jl + d) >= SUBLANES
        pg = jnp.where(wrapped, pltpu.roll(g, SUBLANES - d, 1), pltpu.roll(g, LANES - d, 1))
        beats = (pg > g) | ((pg == g) & wrapped)
        cnt = cnt + beats.astype(I32)
    return valid & (cnt < MOBA_TOPK)


def _head_expand(rows8, n_heads, width):
    x = jnp.concatenate([rows8] * n_heads, axis=0)
    r = lax.broadcasted_iota(I32, x.shape, 0) // SUBLANES
    l = lax.broadcasted_iota(I32, x.shape, 1) // (width // n_heads)
    return jnp.where(r == l, x, 0.0)


def _block_indicator(seq):
    key_blk = np.arange(seq)[:, None] // MOBA_BLOCK
    lane = np.arange(LANES)[None, :]
    ind = (lane < MOBA_HEADS * SUBLANES) & ((lane % SUBLANES) == key_blk)
    return jnp.asarray(ind.astype(np.float32), dtype=BF16)


def _attn_prompt_kernel(q_ref, k_ref, v_ref, ind_ref, o_ref, km_ref, *, n_blk):
    i = pl.program_id(1)
    blk = MOBA_BLOCK
    aw = MOBA_HEADS * MOBA_HEAD_DIM
    scale = MOBA_HEAD_DIM ** -0.5

    @pl.when(i == 0)
    def _():
        for j in range(n_blk):
            km_ref[j:j + 1, :] = jnp.mean(k_ref[j * blk:(j + 1) * blk, :], axis=0, keepdims=True)

    q = q_ref[...]
    kmexp = _head_expand(km_ref[...], MOBA_HEADS, aw)
    kmexp = jnp.concatenate([kmexp, jnp.zeros_like(kmexp)], axis=0)
    gate = _dot_nt(q, kmexp, precision=HIGHEST)
    lane = lax.broadcasted_iota(I32, gate.shape, 1)
    jl = lane % SUBLANES
    valid = (jl < i) & (lane < MOBA_HEADS * SUBLANES)
    keep = _top_blocks(gate, valid) | (jl == i)
    bias = jnp.where(keep, 0.0, NEG)

    row = lax.broadcasted_iota(I32, (blk, blk), 0)
    col = lax.broadcasted_iota(I32, (blk, blk), 1)
    tril = row >= col

    def tile(c):
        n_keys = (c + 1) * blk
        for hp in range(aw // LANES):
            cols = slice(hp * LANES, (hp + 1) * LANES)
            rhs = jnp.concatenate([k_ref[0:n_keys, cols].astype(BF16), ind_ref[0:n_keys, :]], axis=1)
            vv = v_ref[0:n_keys, cols].astype(BF16)
            qp = q[:, cols] * scale
            outs = []
            for s in (0, 1):
                h = 2 * hp + s
                qm = jnp.where((lane // MOBA_HEAD_DIM) == s, qp, 0.0).astype(BF16)
                hb = jnp.where((lane // SUBLANES) == h, bias, 0.0).astype(BF16)
                sc = _dot_nt(jnp.concatenate([qm, hb], axis=1), rhs)
                own = jnp.where(tril, sc[:, c * blk:], NEG)
                sc = own if c == 0 else jnp.concatenate([sc[:, :c * blk], own], axis=1)
                p = jnp.exp(sc - jnp.max(sc, axis=1, keepdims=True))
                outs.append(_dot(p.astype(BF16), vv) / jnp.sum(p, axis=1, keepdims=True))
            o_ref[:, cols] = jnp.where((lane // MOBA_HEAD_DIM) == 0, outs[0], outs[1]).astype(BF16)

    for c in range(n_blk):
        pl.when(i == c)(functools.partial(tile, c))


def _attn_prompt(q_all, k_p, v_p, *, batch, seq):
    aw = MOBA_HEADS * MOBA_HEAD_DIM
    n_blk = seq // MOBA_BLOCK
    assert n_blk <= SUBLANES
    ind = _block_indicator(seq)
    kern = functools.partial(_attn_prompt_kernel, n_blk=n_blk)
    return pl.pallas_call(
        kern,
        out_shape=jax.ShapeDtypeStruct((batch * seq, aw), BF16),
        grid=(batch, n_blk),
        in_specs=[pl.BlockSpec((MOBA_BLOCK, aw), lambda b, i: (b * n_blk + i, 0)),
                  pl.BlockSpec((seq, aw), lambda b, i: (b, 0)),
                  pl.BlockSpec((seq, aw), lambda b, i: (b, 0)),
                  pl.BlockSpec((seq, LANES), lambda b, i: (0, 0))],
        out_specs=pl.BlockSpec((MOBA_BLOCK, aw), lambda b, i: (b * n_blk + i, 0)),
        scratch_shapes=[pltpu.VMEM((SUBLANES, aw), F32)],
        compiler_params=_cparams(("arbitrary", "arbitrary")),
        name="moba_prompt",
    )(q_all, k_p, v_p, ind)


def _attn_sample_kernel(pt_ref, q_ref, kn_ref, vn_ref, ck_ref, cv_ref, o_ref, kbuf, vbuf, sem,
                        *, n_pages, page, n_blk, layer):
    b = pl.program_id(0)
    nb = pl.num_programs(0)
    blk = MOBA_BLOCK
    hd = MOBA_HEAD_DIM
    dec = q_ref.shape[0]
    scale = hd ** -0.5

    def page_copies(seq_idx, slot):
        cps = []
        for p in range(n_pages):
            pg = pt_ref[seq_idx * n_pages + p]
            cps.append(pltpu.make_async_copy(ck_ref.at[layer, pg], kbuf.at[slot, :, :, pl.ds(p * page, page)],
                                             sem.at[0, slot]))
            cps.append(pltpu.make_async_copy(cv_ref.at[layer, pg], vbuf.at[slot, :, :, pl.ds(p * page, page)],
                                             sem.at[1, slot]))
        return cps

    @pl.when(b == 0)
    def _():
        for c in page_copies(0, 0):
            c.start()

    @pl.when(b + 1 < nb)
    def _():
        for c in page_copies(b + 1, (b + 1) % 2):
            c.start()

    slot = b % 2
    for c in page_copies(b, slot):
        c.wait()

    nh = MOBA_HEADS
    aw = nh * hd
    past = n_blk * blk
    kt = kbuf[slot].reshape(aw, past)
    vt = vbuf[slot].reshape(aw, past)
    qexp = _head_expand(q_ref[...], nh, aw)

    colj = lax.broadcasted_iota(I32, (aw, LANES), 1)
    km = jnp.zeros((aw, LANES), F32)
    for j in range(n_blk):
        km = jnp.where(colj == j, jnp.mean(kt[:, j * blk:(j + 1) * blk], axis=1, keepdims=True), km)
    gate = _dot(qexp, km, precision=HIGHEST)
    lane = lax.broadcasted_iota(I32, gate.shape, 1)
    sel = _top_blocks(gate, lane < n_blk).astype(F32)

    qs = (qexp * scale).astype(BF16)
    sc = _dot(qs, kt.astype(BF16))
    key_blk = lax.broadcasted_iota(I32, sc.shape, 1) // blk
    mask = jnp.zeros(sc.shape, F32)
    for j in range(n_blk):
        mask = jnp.where(key_blk == j, sel[:, j:j + 1], mask)
    sc = jnp.where(mask > 0.5, sc, NEG)

    pad = jnp.zeros((LANES - dec, aw), F32)
    kn = jnp.concatenate([kn_ref[...], pad], axis=0).astype(BF16)
    vn = jnp.concatenate([vn_ref[...], pad], axis=0).astype(BF16)
    qi = lax.broadcasted_iota(I32, gate.shape, 0) % dec
    sn = jnp.where(lane <= qi, _dot_nt(qs, kn), NEG)

    m = jnp.maximum(jnp.max(sc, axis=1, keepdims=True), jnp.max(sn, axis=1, keepdims=True))
    p = jnp.exp(sc - m)
    pn = jnp.exp(sn - m)
    l = jnp.sum(p, axis=1, keepdims=True) + jnp.sum(pn, axis=1, keepdims=True)
    o = (_dot_nt(p.astype(BF16), vt.astype(BF16)) + _dot(pn.astype(BF16), vn)) / l
    r = lax.broadcasted_iota(I32, o.shape, 0) // dec
    c = lax.broadcasted_iota(I32, o.shape, 1) // hd
    o = jnp.where(r == c, o, 0.0)
    acc = o[0:dec]
    for h in range(1, nh):
        acc = acc + o[h * dec:(h + 1) * dec]
    o_ref[...] = acc


def _attn_sample(page_table, q_all, k_s, v_s, cache_k, cache_v, *, layer, dec_batch, dec_seq, q_row0):
    _, n_phys, page, n_heads, hd = cache_k.shape
    aw = MOBA_HEADS * MOBA_HEAD_DIM
    n_pages = page_table.shape[1]
    past_len = n_pages * page
    assert past_len % MOBA_BLOCK == 0 and dec_seq == SUBLANES and past_len // MOBA_BLOCK == SUBLANES
    assert n_heads == MOBA_HEADS and hd == MOBA_HEAD_DIM
    kern = functools.partial(_attn_sample_kernel, n_pages=n_pages, page=page, n_blk=past_len // MOBA_BLOCK,
                             layer=layer)
    q_blk0 = q_row0 // dec_seq
    grid_spec = pltpu.PrefetchScalarGridSpec(
        num_scalar_prefetch=1,
        grid=(dec_batch,),
        in_specs=[pl.BlockSpec((dec_seq, aw), lambda b, pt: (q_blk0 + b, 0)),
                  pl.BlockSpec((dec_seq, aw), lambda b, pt: (b, 0)),
                  pl.BlockSpec((dec_seq, aw), lambda b, pt: (b, 0)),
                  pl.BlockSpec(memory_space=pl.ANY),
                  pl.BlockSpec(memory_space=pl.ANY)],
        out_specs=pl.BlockSpec((dec_seq, aw), lambda b, pt: (b, 0)),
        scratch_shapes=[pltpu.VMEM((2, n_heads, hd, past_len), F32), pltpu.VMEM((2, n_heads, hd, past_len), F32),
                        pltpu.SemaphoreType.DMA((2, 2))],
    )
    to_pos_minor = lambda c: jnp.transpose(c, (0, 1, 3, 4, 2))
    return pl.pallas_call(
        kern,
        out_shape=jax.ShapeDtypeStruct((dec_batch * dec_seq, aw), F32),
        grid_spec=grid_spec,
        compiler_params=_cparams(("arbitrary",)),
        name="moba_sample",
    )(page_table.reshape(-1), q_all, k_s, v_s, to_pos_minor(cache_k), to_pos_minor(cache_v))


def _hgrn_levels():
    n = HG_CHUNK
    return int(np.log2(n))


def _hgrn_cumsum_matrix():
    c = HG_CHUNK
    t = np.arange(c)[:, None]
    s = np.arange(c)[None, :]
    mats = [(s <= t)]
    for l in range(_hgrn_levels()):
        m = 2 ** l
        sep = (t // (2 * m)) * (2 * m) + m - 1
        mats.append(s <= sep)
    return jnp.asarray(np.concatenate(mats, axis=0).astype(np.float32), dtype=BF16)


def _hgrn_post(o, g, nw):
    o = o * lax.rsqrt(jnp.mean(o * o, axis=-1, keepdims=True) + NORM_EPS) * nw
    return o * (g * _sigmoid(g))


def _hgrn_prompt_kernel(q_ref, f_ref, i_ref, g_ref, ms_ref, nw_ref, o_ref, s_ref):
    c = HG_CHUNK
    dk = HG_DK
    n_chunks = q_ref.shape[0] // c
    levels = _hgrn_levels()
    row = lax.broadcasted_iota(I32, (c, c), 0)
    col = lax.broadcasted_iota(I32, (c, c), 1)
    rowk = lax.broadcasted_iota(I32, (c, dk), 0)
    upper = [((rowk // (2 ** l)) % 2) == 1 for l in range(levels)]
    same = [(row // (2 ** (l + 1))) == (col // (2 ** (l + 1))) for l in range(levels)]
    diag = row == col
    nw = nw_ref[...]

    def chunk(ci, st):
        r0 = pl.multiple_of(ci * c, c)
        q = q_ref[pl.ds(r0, c), :]
        f = f_ref[pl.ds(r0, c), :]
        v = i_ref[pl.ds(r0, c), :]
        g = g_ref[pl.ds(r0, c), :]
        lf = jnp.log(f)
        k = 1.0 - f
        hi = lf.astype(BF16)
        r1 = lf - hi.astype(F32)
        mid = r1.astype(BF16)
        lo = (r1 - mid.astype(F32)).astype(BF16)
        bb = _dot(ms_ref[...], jnp.concatenate([hi, mid, lo], axis=1))
        bb = bb[:, 0:dk] + bb[:, dk:2 * dk] + bb[:, 2 * dk:3 * dk]
        b = bb[0:c]
        a = jnp.where(diag, jnp.sum(q * k, axis=1, keepdims=True), 0.0)
        for l in range(levels):
            bs = bb[(l + 1) * c:(l + 2) * c]
            e = jnp.exp(jnp.where(upper[l], b - bs, bs - b))
            qp = jnp.where(upper[l], q * e, 0.0).astype(BF16)
            kp = jnp.where(upper[l], 0.0, k * e).astype(BF16)
            a = a + jnp.where(same[l], _dot_nt(qp, kp), 0.0)
        vb = v.astype(BF16)
        o = _dot_nt((q * jnp.exp(b)).astype(BF16), st.astype(BF16)) + _dot(a.astype(BF16), vb)
        blast = b[c - 1:c, :]
        kl = (k * jnp.exp(blast - b)).astype(BF16)
        st_new = st * jnp.exp(blast) + _dot(v.T.astype(BF16), kl)
        o_ref[pl.ds(r0, c), :] = _hgrn_post(o, g, nw).astype(BF16)
        return st_new

    st = lax.fori_loop(0, n_chunks, chunk, jnp.zeros((dk, dk), F32))
    s_ref[0] = st.T


def _hgrn_prompt(hg, nw, *, batch, seq):
    hw = HG_HEADS * HG_DK
    mstack = _hgrn_cumsum_matrix()
    spec = lambda off: pl.BlockSpec((seq, HG_DK), lambda b, h: (b, off + h))
    return pl.pallas_call(
        _hgrn_prompt_kernel,
        out_shape=(jax.ShapeDtypeStruct((batch * seq, hw), BF16),
                   jax.ShapeDtypeStruct((batch * HG_HEADS, HG_DK, HG_DK), F32)),
        grid=(batch, HG_HEADS),
        in_specs=[spec(0), spec(HG_HEADS), spec(2 * HG_HEADS), spec(3 * HG_HEADS),
                  pl.BlockSpec(mstack.shape, lambda b, h: (0, 0)),
                  pl.BlockSpec((1, HG_DK), lambda b, h: (0, 0))],
        out_specs=(pl.BlockSpec((seq, HG_DK), lambda b, h: (b, h)),
                   pl.BlockSpec((1, HG_DK, HG_DK), lambda b, h: (b * HG_HEADS + h, 0, 0))),
        compiler_params=_cparams(("arbitrary", "arbitrary")),
        name="hgrn_prompt",
    )(hg, hg, hg, hg, mstack, nw)


def _hgrn_sample_kernel(hg_ref, s0_ref, nw_ref, o_ref, s_ref, *, dec, seqs):
    dk = HG_DK
    hw = HG_HEADS * HG_DK
    nw = nw_ref[...]
    zpad = jnp.zeros((dk - 3 * dec, dk), F32)

    def one_seq(si, carry):
        r0 = pl.multiple_of(si * dec, dec)
        for h in range(HG_HEADS):
            cols = slice(h * dk, (h + 1) * dk)
            q = hg_ref[pl.ds(r0, dec), h * dk:(h + 1) * dk]
            f = hg_ref[pl.ds(r0, dec), hw + h * dk:hw + (h + 1) * dk]
            v = hg_ref[pl.ds(r0, dec), 2 * hw + h * dk:2 * hw + (h + 1) * dk]
            g = hg_ref[pl.ds(r0, dec), 3 * hw + h * dk:3 * hw + (h + 1) * dk]
            colsT = jnp.concatenate([f, 1.0 - f, q, zpad], axis=0).T
            st = s0_ref[si * HG_HEADS + h]
            outs = []
            for t in range(dec):
                st = colsT[:, t:t + 1] * st + colsT[:, dec + t:dec + t + 1] * v[t:t + 1, :]
                outs.append(jnp.sum(colsT[:, 2 * dec + t:2 * dec + t + 1] * st, axis=0, keepdims=True))
            o = jnp.concatenate(outs, axis=0)
            o_ref[pl.ds(r0, dec), cols] = _hgrn_post(o, g, nw)
            s_ref[si * HG_HEADS + h] = st
        return carry

    lax.fori_loop(0, seqs, one_seq, 0)


def _hgrn_sample(hg, s0, nw, *, dec_batch, dec_seq, row0):
    hw = HG_HEADS * HG_DK
    seqs = 8
    rows = seqs * dec_seq
    kern = functools.partial(_hgrn_sample_kernel, dec=dec_seq, seqs=seqs)
    st_spec = pl.BlockSpec((seqs * HG_HEADS, HG_DK, HG_DK), lambda i: (i, 0, 0))
    return pl.pallas_call(
        kern,
        out_shape=(jax.ShapeDtypeStruct((dec_batch * dec_seq, hw), F32),
                   jax.ShapeDtypeStruct((dec_batch * HG_HEADS, HG_DK, HG_DK), F32)),
        grid=(dec_batch // seqs,),
        in_specs=[pl.BlockSpec((rows, 4 * hw), lambda i: (row0 // rows + i, 0)), st_spec,
                  pl.BlockSpec((1, HG_DK), lambda i: (0, 0))],
        out_specs=(pl.BlockSpec((rows, hw), lambda i: (i, 0)), st_spec),
        compiler_params=_cparams(("arbitrary",)),
        name="hgrn_sample",
    )(hg, s0, nw)


def _post_mixer_kernel(*refs, n_a, npt):
    hp_ref, hs_ref, mp_ref, ms_ref = refs[0:4]
    a_refs = refs[4:4 + 2 * n_a]
    w_refs = refs[4 + 2 * n_a:4 + 3 * n_a]
    nffn_ref, rw_ref, rb_ref, stril_ref = refs[4 + 3 * n_a:8 + 3 * n_a]
    hmid_ref, u_ref, rinfo_ref, counts_ref, carry = refs[8 + 3 * n_a:]
    i = pl.program_id(0)

    @pl.when(i == 0)
    def _():
        carry[...] = jnp.zeros_like(carry)

    def run(is_prompt):
        h = (hp_ref if is_prompt else hs_ref)[...]
        m3 = (mp_ref if is_prompt else ms_ref)[...]
        out = None
        for a in range(n_a):
            av = a_refs[2 * a + (0 if is_prompt else 1)][...].astype(BF16)
            t = _dot(av, w_refs[a][...])
            out = t if out is None else out + t
        hm = _gate_residual(h, m3, 2, out)
        hmid_ref[...] = hm
        u = _norm_mod(hm, nffn_ref[...], m3, 4, 3)
        for k in range(u.shape[1] // LANES):
            u_ref[:, k, :] = u[:, k * LANES:(k + 1) * LANES]

        u_hi = u.astype(BF16)
        u_lo = (u - u_hi.astype(F32)).astype(BF16)
        hh = _dot(u_hi, rw_ref[...])
        logits = hh[:, 0:LANES] + hh[:, LANES:2 * LANES] + _dot(u_lo, rw_ref[:, 0:LANES]) + rb_ref[...]
        col = lambda j: logits[:, j:j + 1]
        gl = [col(j) for j in range(N_GROUPS)]
        gmax = functools.reduce(jnp.maximum, gl)
        gsum = functools.reduce(lambda x, y: x + y, [jnp.exp(x - gmax) for x in gl])
        gval = 1.0 / gsum
        gidx = jnp.where(gl[0] == gmax, 0, jnp.where(gl[1] == gmax, 1, jnp.where(gl[2] == gmax, 2, 3)))
        el = []
        for k in range(EXPERTS_PER_GROUP):
            c = [col(N_GROUPS + g * EXPERTS_PER_GROUP + k) for g in range(N_GROUPS)]
            el.append(jnp.where(gidx == 0, c[0], jnp.where(gidx == 1, c[1], jnp.where(gidx == 2, c[2], c[3]))))
        emax = functools.reduce(jnp.maximum, el)
        pe = [jnp.exp(x - emax) for x in el]
        esum = functools.reduce(lambda x, y: x + y, pe)
        pk = [x / esum for x in pe]
        v1 = functools.reduce(jnp.maximum, pk)
        i1 = jnp.where(pk[0] == v1, 0, jnp.where(pk[1] == v1, 1, jnp.where(pk[2] == v1, 2, 3)))
        pk2 = [jnp.where(i1 == k, -1.0, pk[k]) for k in range(EXPERTS_PER_GROUP)]
        v2 = functools.reduce(jnp.maximum, pk2)
        i2 = jnp.where(pk2[0] == v2, 0, jnp.where(pk2[1] == v2, 1, jnp.where(pk2[2] == v2, 2, 3)))
        den = v1 + v2
        w1 = gval * v1 / den
        w2 = gval * v2 / den
        lo = jnp.minimum(i1, i2)
        hi = jnp.maximum(i1, i2)
        pair = jnp.where(lo == 0, hi - 1, jnp.where(lo == 1, hi + 1, 5))
        bucket = gidx * 6 + pair
        w_lo = jnp.where(i1 < i2, w1, w2)
        w_hi = jnp.where(i1 < i2, w2, w1)

        r = h.shape[0]
        lane = lax.broadcasted_iota(I32, (r, LANES), 1)
        onehot = lane == bucket
        before = _dot(stril_ref[...], onehot.astype(BF16)) + carry[...]
        rank = jnp.sum(jnp.where(onehot, before, 0.0), axis=1, keepdims=True)
        carry[...] = carry[...] + jnp.sum(onehot.astype(F32), axis=0, keepdims=True)
        rinfo_ref[...] = jnp.where(lane == 0, bucket.astype(F32),
                                   jnp.where(lane == 1, w_lo, jnp.where(lane == 2, w_hi,
                                                                        jnp.where(lane == 3, rank, 0.0))))
        counts_ref[...] = carry[...]

    _by_group(i, npt, run)


def _post_mixer(h_pair, h_off_s, mp, ms, a_pairs, ws, nffn, rw, rb, *, seq):
    hp, hs = h_pair
    d = hp.shape[1]
    tp = a_pairs[0][0].shape[0]
    ts = a_pairs[0][1].shape[0]
    npt, nst = tp // ROW_TILE, ts // ROW_TILE
    t = tp + ts
    n_a = len(a_pairs)
    tiles_per_seq = seq // ROW_TILE
    const = lambda shape: pl.BlockSpec(shape, lambda i: (0,) * len(shape))
    stril = jnp.asarray(np.tril(np.ones((ROW_TILE, ROW_TILE), np.float32), -1), dtype=BF16)
    in_specs = [*_pair_specs(d, npt, off_s=h_off_s), *_mod_specs(d, npt, tiles_per_seq)]
    args = [hp, hs, mp, ms]
    for ap, as_ in a_pairs:
        in_specs += list(_pair_specs(ap.shape[1], npt))
        args += [ap, as_]
    for w in ws:
        in_specs.append(const(w.shape))
        args.append(w)
    in_specs += [const((1, d)), const(rw.shape), const((1, LANES)), const(stril.shape)]
    args += [nffn, rw, rb, stril]
    tile = lambda c: pl.BlockSpec((ROW_TILE, c), lambda i: (i, 0))
    kern = functools.partial(_post_mixer_kernel, n_a=n_a, npt=npt)
    return pl.pallas_call(
        kern,
        out_shape=(jax.ShapeDtypeStruct((t, d), F32), jax.ShapeDtypeStruct((t, d // LANES, LANES), F32),
                   jax.ShapeDtypeStruct((t, LANES), F32), jax.ShapeDtypeStruct((1, LANES), F32)),
        grid=(npt + nst,),
        in_specs=in_specs,
        out_specs=(tile(d), pl.BlockSpec((ROW_TILE, d // LANES, LANES), lambda i: (i, 0, 0)), tile(LANES),
                   const((1, LANES))),
        scratch_shapes=[pltpu.VMEM((1, LANES), F32)],
        compiler_params=_cparams(("arbitrary",)),
        name="post_mixer",
    )(*args)


def _for_rows(rows, fn):
    def body(r, c):
        fn(r)
        return c

    lax.fori_loop(0, rows, body, 0, unroll=8)


def _scatter_kernel(dest_ref, src_ref, init_ref, o_ref, sem):
    del init_ref
    rows = src_ref.shape[0]
    base = pl.program_id(0) * rows
    copy = lambda r: pltpu.make_async_copy(src_ref.at[r], o_ref.at[dest_ref[base + r]], sem)
    _for_rows(rows, lambda r: copy(r).start())
    _for_rows(rows, lambda r: copy(r).wait())


def _scatter_tokens(dest, src, n_slots):
    t = src.shape[0]
    rows = ROW_TILE
    grid_spec = pltpu.PrefetchScalarGridSpec(
        num_scalar_prefetch=1,
        grid=(t // rows,),
        in_specs=[pl.BlockSpec((rows,) + src.shape[1:], lambda i, d: (i, 0, 0)), pl.BlockSpec(memory_space=pl.ANY)],
        out_specs=pl.BlockSpec(memory_space=pl.ANY),
        scratch_shapes=[pltpu.SemaphoreType.DMA(())],
    )
    return pl.pallas_call(
        _scatter_kernel,
        out_shape=jax.ShapeDtypeStruct((n_slots,) + src.shape[1:], src.dtype),
        grid_spec=grid_spec,
        input_output_aliases={2: 0},
        compiler_params=_cparams(("arbitrary",)),
        name="moe_scatter",
    )(dest, src, jnp.zeros((n_slots,) + src.shape[1:], src.dtype))


def _moe_kernel(ea_ref, eb_ref, valid_ref, fresh_ref, x_ref,
                w1a_ref, w3a_ref, w2a_ref, w1b_ref, w3b_ref, w2b_ref, o_ref, wbuf1, wbuf3, wbuf2):
    i = pl.program_id(0)
    nk = x_ref.shape[1]

    @pl.when(fresh_ref[i] == 1)
    def _():
        wbuf1[0] = w1a_ref[0, 0].astype(BF16)
        wbuf3[0] = w3a_ref[0, 0].astype(BF16)
        wbuf2[0] = w2a_ref[0, 0].astype(BF16)
        wbuf1[1] = w1b_ref[0, 0].astype(BF16)
        wbuf3[1] = w3b_ref[0, 0].astype(BF16)
        wbuf2[1] = w2b_ref[0, 0].astype(BF16)

    @pl.when(valid_ref[i] == 1)
    def _():
        xb = jnp.concatenate([x_ref[:, k, :] for k in range(nk)], axis=1).astype(BF16)
        for s in (0, 1):
            h1 = _dot(xb, wbuf1[s])
            h3 = _dot(xb, wbuf3[s])
            y = _dot(((h1 * _sigmoid(h1)) * h3).astype(BF16), wbuf2[s])
            for k in range(nk):
                o_ref[:, s * nk + k, :] = y[:, k * LANES:(k + 1) * LANES]

    @pl.when(valid_ref[i] == 0)
    def _():
        o_ref[...] = jnp.zeros_like(o_ref)


def _moe_experts(tile_ea, tile_eb, tile_valid, tile_fresh, xs, w1, w3, w2, layer):
    n_slots, nk, _ = xs.shape
    d = nk * LANES
    ff = w1.shape[-1]
    wa = lambda shape: pl.BlockSpec((1, 1) + shape, lambda i, ea, eb, va, fr: (layer, ea[i], 0, 0))
    wb = lambda shape: pl.BlockSpec((1, 1) + shape, lambda i, ea, eb, va, fr: (layer, eb[i], 0, 0))
    rows = lambda k: pl.BlockSpec((MOE_TILE, k, LANES), lambda i, ea, eb, va, fr: (i, 0, 0))
    grid_spec = pltpu.PrefetchScalarGridSpec(
        num_scalar_prefetch=4,
        grid=(n_slots // MOE_TILE,),
        in_specs=[rows(nk), wa((d, ff)), wa((d, ff)), wa((ff, d)), wb((d, ff)), wb((d, ff)), wb((ff, d))],
        out_specs=rows(2 * nk),
        scratch_shapes=[pltpu.VMEM((2, d, ff), BF16), pltpu.VMEM((2, d, ff), BF16), pltpu.VMEM((2, ff, d), BF16)],
    )
    return pl.pallas_call(
        _moe_kernel,
        out_shape=jax.ShapeDtypeStruct((n_slots, 2 * nk, LANES), F32),
        grid_spec=grid_spec,
        compiler_params=_cparams(("arbitrary",)),
        name="moe_experts",
    )(tile_ea, tile_eb, tile_valid, tile_fresh, xs, w1, w3, w2, w1, w3, w2)


def _moe_plan(rinfo, counts, n_slots):
    n_tiles = n_slots // MOE_TILE
    cnt = counts[0, :N_BUCKETS].astype(I32)
    padded = ((cnt + MOE_TILE - 1) // MOE_TILE) * MOE_TILE
    ends = jnp.cumsum(padded)
    starts = ends - padded
    bucket = rinfo[:, 0].astype(I32)
    dest = starts[bucket] + rinfo[:, 3].astype(I32)
    n_valid = ends[-1] // MOE_TILE
    tiles = jnp.arange(n_tiles, dtype=I32)
    tb = jnp.sum((ends[None, :] <= (tiles * MOE_TILE)[:, None]).astype(I32), axis=1)
    tb = jnp.minimum(tb, N_BUCKETS - 1)
    valid = tiles < n_valid
    tb = jnp.where(valid, tb, tb[jnp.maximum(n_valid - 1, 0)])
    lo = jnp.asarray(PAIR_LO, I32)
    hi = jnp.asarray(PAIR_HI, I32)
    ea = (tb // 6) * EXPERTS_PER_GROUP + lo[tb % 6]
    eb = (tb // 6) * EXPERTS_PER_GROUP + hi[tb % 6]
    fresh = jnp.concatenate([jnp.ones((1,), I32), (tb[1:] != tb[:-1]).astype(I32)])
    return dest, ea, eb, valid.astype(I32), fresh


def _moe(u_tiles, rinfo, counts, w1, w3, w2, layer):
    t = u_tiles.shape[0]
    n_slots = t + N_BUCKETS * MOE_TILE
    dest, ea, eb, valid, fresh = _moe_plan(rinfo, counts, n_slots)
    xs = _scatter_tokens(dest, u_tiles, n_slots)
    ys = _moe_experts(ea, eb, valid, fresh, xs, w1, w3, w2, layer)
    return dest, ys


def _combine_kernel(dest_ref, ys_ref, hm_ref, rinfo_ref, mp_ref, ms_ref, nf_ref, *rest, npt, final):
    if final:
        yp_ref, ysm_ref, gbuf, sem = rest
    else:
        h_ref, gbuf, sem = rest
    i = pl.program_id(0)
    n = pl.num_programs(0)
    rows = gbuf.shape[1]
    nk = gbuf.shape[2] // 2

    def copy(tile, r):
        slot = tile % 2
        return pltpu.make_async_copy(ys_ref.at[dest_ref[tile * rows + r]], gbuf.at[slot, r], sem.at[slot])

    @pl.when(i == 0)
    def _():
        _for_rows(rows, lambda r: copy(0, r).start())

    @pl.when(i + 1 < n)
    def _():
        _for_rows(rows, lambda r: copy(i + 1, r).start())

    _for_rows(rows, lambda r: copy(i, r).wait())
    slot = i % 2
    ya = jnp.concatenate([gbuf[slot, :, k, :] for k in range(nk)], axis=1)
    yb = jnp.concatenate([gbuf[slot, :, nk + k, :] for k in range(nk)], axis=1)
    rinfo = rinfo_ref[...]
    moe = rinfo[:, 1:2] * ya + rinfo[:, 2:3] * yb

    def run(is_prompt):
        m3 = (mp_ref if is_prompt else ms_ref)[...]
        h = _gate_residual(hm_ref[...], m3, 5, moe)
        if final:
            y = h * lax.rsqrt(jnp.mean(h * h, axis=-1, keepdims=True) + NORM_EPS) * nf_ref[...]
            (yp_ref if is_prompt else ysm_ref)[...] = y
        else:
            h_ref[...] = h

    _by_group(i, npt, run)


def _combine(dest, ys, hmid, rinfo, mp, ms, nf, *, tp, seq, final):
    t, d = hmid.shape
    npt = tp // ROW_TILE
    tiles_per_seq = seq // ROW_TILE
    mpm = lambda f: (lambda i, dref: f(i))
    mp_spec, ms_spec = _mod_specs(d, npt, tiles_per_seq)
    yp_spec, ysm_spec = _pair_specs(d, npt)
    wrap = lambda s: pl.BlockSpec(s.block_shape, mpm(s.index_map))
    if final:
        out_shape = (jax.ShapeDtypeStruct((tp, d), F32), jax.ShapeDtypeStruct((t - tp, d), F32))
        out_specs = (wrap(yp_spec), wrap(ysm_spec))
    else:
        out_shape = jax.ShapeDtypeStruct((t, d), F32)
        out_specs = pl.BlockSpec((ROW_TILE, d), lambda i, dref: (i, 0))
    grid_spec = pltpu.PrefetchScalarGridSpec(
        num_scalar_prefetch=1,
        grid=(t // ROW_TILE,),
        in_specs=[pl.BlockSpec(memory_space=pl.ANY),
                  pl.BlockSpec((ROW_TILE, d), lambda i, dref: (i, 0)),
                  pl.BlockSpec((ROW_TILE, LANES), lambda i, dref: (i, 0)),
                  wrap(mp_spec), wrap(ms_spec),
                  pl.BlockSpec((1, d), lambda i, dref: (0, 0))],
        out_specs=out_specs,
        scratch_shapes=[pltpu.VMEM((2, ROW_TILE) + ys.shape[1:], F32), pltpu.SemaphoreType.DMA((2,))],
    )
    kern = functools.partial(_combine_kernel, npt=npt, final=final)
    return pl.pallas_call(
        kern, out_shape=out_shape, grid_spec=grid_spec,
        compiler_params=_cparams(("arbitrary",)),
        name="moe_combine_final" if final else "moe_combine",
    )(dest, ys, hmid, rinfo, mp, ms, nf)


def _odd_pre_kernel(hp_ref, hs_ref, mp_ref, ms_ref, nw_ref, w_ref, gate_ref, rec_ref, *, npt):
    i = pl.program_id(0)
    width = gate_ref.shape[1]

    def run(is_prompt):
        h = (hp_ref if is_prompt else hs_ref)[...]
        m3 = (mp_ref if is_prompt else ms_ref)[...]
        ub = _norm_mod(h, nw_ref[...], m3, 1, 0).astype(BF16)
        x = _dot(ub, w_ref[:, 0:width])
        gate_ref[...] = 0.5 * x * (1.0 + jnp.tanh(np.sqrt(2.0 / np.pi) * (x + 0.044715 * (x * x * x))))
        rec_ref[...] = _dot(ub, w_ref[:, width:2 * width])

    _by_group(i, npt, run)


def _odd_pre(h, mp, ms, nw, w_in, *, tp, seq):
    t, d = h.shape
    npt = tp // ROW_TILE
    width = w_in.shape[1] // 2
    const = lambda shape: pl.BlockSpec(shape, lambda i: (0,) * len(shape))
    tile = lambda c: pl.BlockSpec((ROW_TILE, c), lambda i: (i, 0))
    kern = functools.partial(_odd_pre_kernel, npt=npt)
    return pl.pallas_call(
        kern,
        out_shape=(jax.ShapeDtypeStruct((t, width), F32), jax.ShapeDtypeStruct((t, width), F32)),
        grid=(t // ROW_TILE,),
        in_specs=[*_pair_specs(d, npt, off_s=npt), *_mod_specs(d, npt, seq // ROW_TILE),
                  const((1, d)), const(w_in.shape)],
        out_specs=(tile(width), tile(width)),
        compiler_params=_cparams(("arbitrary",)),
        name="odd_pre",
    )(h, h, mp, ms, nw, w_in)


def _lru_kernel(rec_ref, gate_ref, c0_ref, h0_ref, cw_ref, cb_ref, wa_ref, ba_ref, wx_ref, bx_ref, lam_ref,
                yg_ref, cs_ref, hl_ref, hist, hc, a_s, h_s, *, nb, tl):
    i = pl.program_id(1)
    r, width = rec_ref.shape
    taps = CONV_WIDTH
    blk = width // LRU_HEADS

    @pl.when(i == 0)
    def _():
        hist[:, SUBLANES - (taps - 1):SUBLANES, :] = c0_ref[...]
        hc[...] = h0_ref[...]

    rec3 = rec_ref[...].reshape(nb, tl, width)
    hist[:, SUBLANES:SUBLANES + tl, :] = rec3
    cw = cw_ref[...]
    conv = cb_ref[...] + rec3 * cw[taps - 1:taps, :]
    for back in range(1, taps):
        conv = conv + hist[:, SUBLANES - back:SUBLANES - back + tl, :] * cw[taps - 1 - back:taps - back, :]
    tail = hist[:, SUBLANES + tl - (taps - 1):SUBLANES + tl, :]
    cs_ref[...] = tail
    hist[:, SUBLANES - (taps - 1):SUBLANES, :] = tail

    cf = conv.reshape(r, width)
    cb16 = cf.astype(BF16)
    gr = jnp.concatenate([_dot(cb16[:, h * blk:(h + 1) * blk], wa_ref[h]) for h in range(LRU_HEADS)], axis=1)
    gi = jnp.concatenate([_dot(cb16[:, h * blk:(h + 1) * blk], wx_ref[h]) for h in range(LRU_HEADS)], axis=1)
    rg = _sigmoid(gr + ba_ref[...])
    ig = _sigmoid(gi + bx_ref[...])
    z = -lam_ref[...]
    softplus = jnp.maximum(z, 0.0) + jnp.log1p(jnp.exp(-jnp.abs(z)))
    log_a = -LRU_C * rg * softplus
    a = jnp.exp(log_a)
    gx = jnp.sqrt(1.0 - a * a) * (ig * cf)

    groups = r // SUBLANES
    sub = lax.broadcasted_iota(I32, (groups, SUBLANES, width), 1)
    aa, hh = a.reshape(groups, SUBLANES, width), gx.reshape(groups, SUBLANES, width)
    for dlt in (1, 2, 4):
        ok = sub >= dlt
        a_sh = pltpu.roll(aa, dlt, 1)
        h_sh = pltpu.roll(hh, dlt, 1)
        hh = jnp.where(ok, hh + aa * h_sh, hh)
        aa = jnp.where(ok, aa * a_sh, aa)
    aa, hh = aa.reshape(r, width), hh.reshape(r, width)

    if tl == SUBLANES:
        y3 = hh.reshape(nb, tl, width) + aa.reshape(nb, tl, width) * hc[...]
        hl_ref[...] = y3[:, tl - 1:tl, :]
        y = y3.reshape(r, width)
    else:
        a_s[...] = aa
        h_s[...] = hh

        def group(j, carry):
            r0 = pl.multiple_of(j * SUBLANES, SUBLANES)
            yj = h_s[pl.ds(r0, SUBLANES), :] + a_s[pl.ds(r0, SUBLANES), :] * carry
            h_s[pl.ds(r0, SUBLANES), :] = yj
            return yj[SUBLANES - 1:SUBLANES, :]

        last = lax.fori_loop(0, r // SUBLANES, group, hc[0])
        hc[0] = last
        hl_ref[0] = last
        y = h_s[...]
    yg_ref[...] = (y * gate_ref[...]).astype(BF16)


def _lru(rec, gate, conv0, h0, cw, cb, wa, ba, wx, bx, lam, *, batch, seq, row0):
    width = rec.shape[1]
    taps = CONV_WIDTH
    if seq == SUBLANES:
        nb, tl = ROW_TILE // SUBLANES, SUBLANES
    else:
        nb, tl = 1, ROW_TILE
    r = nb * tl
    n_l = seq // tl
    blk0 = row0 // r
    const = lambda shape: pl.BlockSpec(shape, lambda b, i: (0,) * len(shape))
    tile_in = pl.BlockSpec((r, width), lambda b, i: (blk0 + b * n_l + i, 0))
    kern = functools.partial(_lru_kernel, nb=nb, tl=tl)
    return pl.pallas_call(
        kern,
        out_shape=(jax.ShapeDtypeStruct((batch * seq, width), BF16),
                   jax.ShapeDtypeStruct((batch, taps - 1, width), F32),
                   jax.ShapeDtypeStruct((batch, 1, width), F32)),
        grid=(batch // nb, n_l),
        in_specs=[tile_in, tile_in,
                  pl.BlockSpec((nb, taps - 1, width), lambda b, i: (b, 0, 0)),
                  pl.BlockSpec((nb, 1, width), lambda b, i: (b, 0, 0)),
                  const((taps, width)), const((1, width)), const(wa.shape), const((1, width)),
                  const(wx.shape), const((1, width)), const((1, width))],
        out_specs=(pl.BlockSpec((r, width), lambda b, i: (b * n_l + i, 0)),
                   pl.BlockSpec((nb, taps - 1, width), lambda b, i: (b, 0, 0)),
                   pl.BlockSpec((nb, 1, width), lambda b, i: (b, 0, 0))),
        scratch_shapes=[pltpu.VMEM((nb, tl + SUBLANES, width), F32), pltpu.VMEM((nb, 1, width), F32),
                        pltpu.VMEM((r, width), F32), pltpu.VMEM((r, width), F32)],
        compiler_params=_cparams(("arbitrary", "arbitrary")),
        name="rg_lru_sample" if seq == SUBLANES else "rg_lru_prompt",
    )(rec, gate, conv0, h0, cw, cb, wa, ba, wx, bx, lam)


def kernel(x_prompt, x_sample, c_prompt, c_sample, cache_k, cache_v, state_hgrn, state_conv, state_lru, page_table,
           ada_w, ada_b, norm_mix, norm_ffn, norm_final, w_in_even, w_out_even, hg_lower_bounds, hg_norm_w, w_in_odd,
           conv_w, conv_b, lru_wa, lru_ba, lru_wx, lru_bx, lru_lambda, w_out_odd, router_group_w, router_group_b,
           router_expert_w, router_expert_b, moe_w1, moe_w3, moe_w2):
    batch, seq, d = x_prompt.shape
    dec_batch, dec_seq, _ = x_sample.shape
    depth = ada_w.shape[0]
    assert depth == 2 and seq % ROW_TILE == 0 and seq % MOBA_BLOCK == 0
    tp, ts = batch * seq, dec_batch * dec_seq
    n_pages, page = page_table.shape[1], cache_k.shape[2]
    past_len = n_pages * page
    aw = MOBA_HEADS * MOBA_HEAD_DIM
    hw = HG_HEADS * HG_DK

    xp = x_prompt.reshape(tp, d)
    xs = x_sample.reshape(ts, d)
    mods = _ada_mods(jnp.concatenate([c_prompt, c_sample], axis=0), ada_w, ada_b)
    mods = mods.reshape(depth, batch + dec_batch, 6, d)
    mods_p, mods_s = mods[:, :batch], mods[:, batch:]

    half = MOBA_HEAD_DIM // 2
    inv_freq = jnp.power(ROPE_THETA, -jnp.arange(half, dtype=F32) / half)
    invf = jnp.tile(inv_freq, LANES // half).reshape(1, LANES)

    def router_mats(l):
        rw = jnp.zeros((d, LANES), F32)
        rw = rw.at[:, 0:N_GROUPS].set(router_group_w[l])
        rw = rw.at[:, N_GROUPS:N_GROUPS + N_GROUPS * EXPERTS_PER_GROUP].set(router_expert_w[l])
        rb = jnp.zeros((1, LANES), F32)
        rb = rb.at[0, 0:N_GROUPS].set(router_group_b[l])
        rb = rb.at[0, N_GROUPS:N_GROUPS + N_GROUPS * EXPERTS_PER_GROUP].set(router_expert_b[l])
        rw_hi = rw.astype(BF16)
        rw_lo = (rw - rw_hi.astype(F32)).astype(BF16)
        return jnp.concatenate([rw_hi, rw_lo], axis=1), rb

    q_all, k_p, k_s, v_p, v_s, hg = _even_pre(
        xp, xs, mods_p[0], mods_s[0], norm_mix[0].reshape(1, d), w_in_even[0].astype(BF16), hg_lower_bounds, invf,
        seq=seq, past_len=past_len, layer_slot=0)
    oa_p = _attn_prompt(q_all, k_p, v_p, batch=batch, seq=seq)
    oa_s = _attn_sample(page_table, q_all, k_s, v_s, cache_k, cache_v,
                        layer=0, dec_batch=dec_batch, dec_seq=dec_seq, q_row0=tp)
    hnw = hg_norm_w[0].reshape(1, HG_DK)
    ob_p, hgrn_p = _hgrn_prompt(hg, hnw, batch=batch, seq=seq)
    ob_s, hgrn_s = _hgrn_sample(hg, state_hgrn[0].reshape(dec_batch * HG_HEADS, HG_DK, HG_DK), hnw,
                                dec_batch=dec_batch, dec_seq=dec_seq, row0=tp)
    w_out = w_out_even[0].astype(BF16)
    rw, rb = router_mats(0)
    hmid, u, rinfo, counts = _post_mixer((xp, xs), 0, mods_p[0], mods_s[0], [(oa_p, oa_s), (ob_p, ob_s)],
                                         [w_out[:aw], w_out[aw:]], norm_ffn[0].reshape(1, d), rw, rb, seq=seq)
    dest, ys = _moe(u, rinfo, counts, moe_w1, moe_w3, moe_w2, 0)
    h1 = _combine(dest, ys, hmid, rinfo, mods_p[0], mods_s[0], norm_final.reshape(1, d), tp=tp, seq=seq,
                  final=False)

    gate, rec = _odd_pre(h1, mods_p[1], mods_s[1], norm_mix[1].reshape(1, d), w_in_odd[0].astype(BF16), tp=tp, seq=seq)
    width = rec.shape[1]
    lru_args = (conv_w[0], conv_b[0].reshape(1, width), lru_wa[0].astype(BF16), lru_ba[0].reshape(1, width),
                lru_wx[0].astype(BF16), lru_bx[0].reshape(1, width), lru_lambda[0].reshape(1, width))
    yg_p, conv_p, lru_p = _lru(rec, gate, jnp.zeros((batch, CONV_WIDTH - 1, width), F32),
                               jnp.zeros((batch, 1, width), F32), *lru_args, batch=batch, seq=seq, row0=0)
    yg_s, conv_s, lru_s = _lru(rec, gate, state_conv[0], state_lru[0].reshape(dec_batch, 1, width), *lru_args,
                               batch=dec_batch, seq=dec_seq, row0=tp)
    rw, rb = router_mats(1)
    hmid, u, rinfo, counts = _post_mixer((h1, h1), tp // ROW_TILE, mods_p[1], mods_s[1], [(yg_p, yg_s)],
                                         [w_out_odd[0].astype(BF16)], norm_ffn[1].reshape(1, d), rw, rb, seq=seq)
    dest, ys = _moe(u, rinfo, counts, moe_w1, moe_w3, moe_w2, 1)
    y_p, y_s = _combine(dest, ys, hmid, rinfo, mods_p[1], mods_s[1], norm_final.reshape(1, d), tp=tp, seq=seq,
                        final=True)

    return (y_p.reshape(batch, seq, d), y_s.reshape(dec_batch, dec_seq, d),
            k_p.reshape(1, batch, seq, MOBA_HEADS, MOBA_HEAD_DIM), v_p.reshape(1, batch, seq, MOBA_HEADS, MOBA_HEAD_DIM),
            k_s.reshape(1, dec_batch, dec_seq, MOBA_HEADS, MOBA_HEAD_DIM),
            v_s.reshape(1, dec_batch, dec_seq, MOBA_HEADS, MOBA_HEAD_DIM),
            hgrn_p.reshape(1, batch, HG_HEADS, HG_DK, HG_DK), hgrn_s.reshape(1, dec_batch, HG_HEADS, HG_DK, HG_DK),
            conv_p.reshape(1, batch, CONV_WIDTH - 1, width), conv_s.reshape(1, dec_batch, CONV_WIDTH - 1, width),
            lru_p.reshape(1, batch, width), lru_s.reshape(1, dec_batch, width))
```

```python
import functools

import numpy as np
import jax
import jax.numpy as jnp
from jax import lax
from jax.experimental import pallas as pl
from jax.experimental.pallas import tpu as pltpu

F32 = jnp.float32
BF16 = jnp.bfloat16
I32 = jnp.int32
HIGHEST = lax.Precision.HIGHEST

MOBA_HEADS = 8
MOBA_HEAD_DIM = 64
MOBA_BLOCK = 256
MOBA_TOPK = 3
ROPE_THETA = 10000.0
HG_HEADS = 4
HG_DK = 128
LRU_HEADS = 4
CONV_WIDTH = 4
LRU_C = 8.0
N_GROUPS = 4
EXPERTS_PER_GROUP = 4
NORM_EPS = 1e-6

LANES = 128
SUBLANES = 8
VMEM_LIMIT = 56 * 1024 * 1024

ROW_TILE = 256
MOE_TILE = 256
HG_CHUNK = 128
N_BUCKETS = N_GROUPS * 6
PAIR_LO = (0, 0, 0, 1, 1, 2)
PAIR_HI = (1, 2, 3, 2, 3, 3)
NEG = -1e30


def _cparams(sem, vmem=VMEM_LIMIT):
    return pltpu.CompilerParams(dimension_semantics=sem, vmem_limit_bytes=vmem)


def _dot(a, b, **kw):
    return jnp.dot(a, b, preferred_element_type=F32, **kw)


def _dot_nt(a, b, **kw):
    return lax.dot_general(a, b, (((1,), (1,)), ((), ())), preferred_element_type=F32, **kw)


def _sigmoid(x):
    return jax.nn.sigmoid(x)


def _by_group(i, n_prompt_tiles, fn):
    @pl.when(i < n_prompt_tiles)
    def _():
        fn(True)

    @pl.when(i >= n_prompt_tiles)
    def _():
        fn(False)


def _pair_specs(cols, npt, rows=ROW_TILE, off_p=0, off_s=0):
    return (pl.BlockSpec((rows, cols), lambda i: (off_p + jnp.minimum(i, npt - 1), 0)),
            pl.BlockSpec((rows, cols), lambda i: (off_s + jnp.maximum(i - npt, 0), 0)))


def _mod_specs(d, npt, tiles_per_seq, rows=ROW_TILE):
    return (pl.BlockSpec((1, 6, d), lambda i: (jnp.minimum(i, npt - 1) // tiles_per_seq, 0, 0)),
            pl.BlockSpec((rows // SUBLANES, 6, d), lambda i: (jnp.maximum(i - npt, 0), 0, 0)))


def _norm_mod(x, nw, m3, scale_idx, shift_idx):
    r, d = x.shape
    nb = m3.shape[0]
    var = jnp.mean(x * x, axis=-1, keepdims=True)
    y = x * lax.rsqrt(var + NORM_EPS) * nw
    y3 = y.reshape(nb, r // nb, d)
    u3 = y3 * (1.0 + m3[:, scale_idx:scale_idx + 1, :]) + m3[:, shift_idx:shift_idx + 1, :]
    return u3.reshape(r, d)


def _gate_residual(h, m3, gate_idx, out):
    r, d = h.shape
    nb = m3.shape[0]
    return (h.reshape(nb, r // nb, d) + m3[:, gate_idx:gate_idx + 1, :] * out.reshape(nb, r // nb, d)).reshape(r, d)


def _ada_kernel(c_ref, w_ref, b_ref, o_ref):
    o_ref[0] = _dot(c_ref[...], w_ref[0], precision=HIGHEST) + b_ref[0]


def _ada_mods(c_all, ada_w, ada_b):
    depth, d, n6 = ada_w.shape
    nb = c_all.shape[0]
    tn = 1024
    return pl.pallas_call(
        _ada_kernel,
        out_shape=jax.ShapeDtypeStruct((depth, nb, n6), F32),
        grid=(depth, n6 // tn),
        in_specs=[pl.BlockSpec((nb, d), lambda l, n: (0, 0)),
                  pl.BlockSpec((1, d, tn), lambda l, n: (l, 0, n)),
                  pl.BlockSpec((1, 1, tn), lambda l, n: (l, 0, n))],
        out_specs=pl.BlockSpec((1, nb, tn), lambda l, n: (l, 0, n)),
        compiler_params=_cparams(("arbitrary", "arbitrary")),
        name="ada_mods",
    )(c_all, ada_w, ada_b.reshape(depth, 1, n6))


def _even_pre_kernel(xp_ref, xs_ref, mp_ref, ms_ref, nw_ref, w_ref, hb_ref, invf_ref,
                     q_ref, kp_ref, ks_ref, vp_ref, vs_ref, hg_ref, *, npt, tiles_per_seq, past_len, layer_slot):
    i = pl.program_id(0)
    aw = MOBA_HEADS * MOBA_HEAD_DIM
    hw = HG_HEADS * HG_DK

    def run(is_prompt):
        x = (xp_ref if is_prompt else xs_ref)[...]
        m3 = (mp_ref if is_prompt else ms_ref)[...]
        r = x.shape[0]
        nb = m3.shape[0]
        tl = r // nb
        ub = _norm_mod(x, nw_ref[...], m3, 1, 0).astype(BF16)

        pos0 = (i % tiles_per_seq) * r if is_prompt else past_len
        pos = (pos0 + lax.broadcasted_iota(I32, (nb, tl, LANES), 1)).astype(F32).reshape(r, LANES)
        ang = pos * invf_ref[...]
        reps = aw // LANES
        cos = jnp.concatenate([jnp.cos(ang)] * reps, axis=1)
        sin = jnp.concatenate([jnp.sin(ang)] * reps, axis=1)
        lane = lax.broadcasted_iota(I32, (r, aw), 1)
        half = MOBA_HEAD_DIM // 2
        first = (lane % MOBA_HEAD_DIM) < half
        sin = jnp.where(first, -sin, sin)

        def rope(t):
            rot = jnp.where(first, pltpu.roll(t, aw - half, 1), pltpu.roll(t, half, 1))
            return t * cos + rot * sin

        def seg(a, b):
            return _dot(ub, w_ref[:, a:b])

        q_ref[...] = rope(seg(0, aw))
        k = rope(seg(aw, 2 * aw))
        v = seg(2 * aw, 3 * aw)
        if is_prompt:
            kp_ref[...] = k
            vp_ref[...] = v
        else:
            ks_ref[...] = k
            vs_ref[...] = v
        o = 3 * aw
        qb = seg(o, o + hw)
        fb = seg(o + hw, o + 2 * hw)
        hb = hb_ref[...]
        e = jnp.exp(hb - jnp.max(hb, axis=0, keepdims=True))
        sm = e / jnp.sum(e, axis=0, keepdims=True)
        lb = jnp.sum(sm[0:layer_slot + 1, :], axis=0, keepdims=True)
        hg_ref[:, 0:hw] = qb * _sigmoid(qb)
        hg_ref[:, hw:2 * hw] = lb + (1.0 - lb) * _sigmoid(fb)
        hg_ref[:, 2 * hw:3 * hw] = seg(o + 2 * hw, o + 3 * hw)
        hg_ref[:, 3 * hw:4 * hw] = seg(o + 3 * hw, o + 4 * hw)

    _by_group(i, npt, run)


def _even_pre(xp, xs, mp, ms, nw, w_in, hb, invf, *, seq, past_len, layer_slot):
    tp, d = xp.shape
    ts = xs.shape[0]
    npt, nst = tp // ROW_TILE, ts // ROW_TILE
    tiles_per_seq = seq // ROW_TILE
    aw = MOBA_HEADS * MOBA_HEAD_DIM
    hw = HG_HEADS * HG_DK
    t = tp + ts
    const = lambda shape: pl.BlockSpec(shape, lambda i: (0,) * len(shape))
    kern = functools.partial(_even_pre_kernel, npt=npt, tiles_per_seq=tiles_per_seq, past_len=past_len,
                             layer_slot=layer_slot)
    kp_spec, ks_spec = _pair_specs(aw, npt)
    return pl.pallas_call(
        kern,
        out_shape=(jax.ShapeDtypeStruct((t, aw), F32),
                   jax.ShapeDtypeStruct((tp, aw), F32), jax.ShapeDtypeStruct((ts, aw), F32),
                   jax.ShapeDtypeStruct((tp, aw), F32), jax.ShapeDtypeStruct((ts, aw), F32),
                   jax.ShapeDtypeStruct((t, 4 * hw), F32)),
        grid=(npt + nst,),
        in_specs=[*_pair_specs(d, npt), *_mod_specs(d, npt, tiles_per_seq),
                  const((1, d)), const(w_in.shape), const(hb.shape), const((1, LANES))],
        out_specs=(pl.BlockSpec((ROW_TILE, aw), lambda i: (i, 0)), kp_spec, ks_spec, kp_spec, ks_spec,
                   pl.BlockSpec((ROW_TILE, 4 * hw), lambda i: (i, 0))),
        compiler_params=_cparams(("arbitrary",)),
        name="even_pre",
    )(xp, xs, mp, ms, nw, w_in, hb, invf)


def _top_blocks(g, valid):
    lane = lax.broadcasted_iota(I32, g.shape, 1)
    jl = lane % SUBLANES
    g = jnp.where(valid, g, -jnp.inf)
    cnt = jnp.zeros(g.shape, I32)
    for d in range(1, SUBLANES):
        wrapped = (jl + d) >= SUBLANES
        pg = jnp.where(wrapped, pltpu.roll(g, SUBLANES - d, 1), pltpu.roll(g, LANES - d, 1))
        beats = (pg > g) | ((pg == g) & wrapped)
        cnt = cnt + beats.astype(I32)
    return valid & (cnt < MOBA_TOPK)


def _head_expand(rows8, n_heads, width):
    x = jnp.concatenate([rows8] * n_heads, axis=0)
    r = lax.broadcasted_iota(I32, x.shape, 0) // SUBLANES
    l = lax.broadcasted_iota(I32, x.shape, 1) // (width // n_heads)
    return jnp.where(r == l, x, 0.0)


def _block_indicator(seq):
    key_blk = np.arange(seq)[:, None] // MOBA_BLOCK
    lane = np.arange(LANES)[None, :]
    ind = (lane < MOBA_HEADS * SUBLANES) & ((lane % SUBLANES) == key_blk)
    return jnp.asarray(ind.astype(np.float32), dtype=BF16)


def _attn_prompt_kernel(q_ref, k_ref, v_ref, ind_ref, o_ref, km_ref, *, n_blk):
    i = pl.program_id(1)
    blk = MOBA_BLOCK
    aw = MOBA_HEADS * MOBA_HEAD_DIM
    scale = MOBA_HEAD_DIM ** -0.5

    @pl.when(i == 0)
    def _():
        for j in range(n_blk):
            km_ref[j:j + 1, :] = jnp.mean(k_ref[j * blk:(j + 1) * blk, :], axis=0, keepdims=True)

    q = q_ref[...]
    kmexp = _head_expand(km_ref[...], MOBA_HEADS, aw)
    kmexp = jnp.concatenate([kmexp, jnp.zeros_like(kmexp)], axis=0)
    gate = _dot_nt(q, kmexp, precision=HIGHEST)
    lane = lax.broadcasted_iota(I32, gate.shape, 1)
    jl = lane % SUBLANES
    valid = (jl < i) & (lane < MOBA_HEADS * SUBLANES)
    keep = _top_blocks(gate, valid) | (jl == i)
    bias = jnp.where(keep, 0.0, NEG)

    row = lax.broadcasted_iota(I32, (blk, blk), 0)
    col = lax.broadcasted_iota(I32, (blk, blk), 1)
    tril = row >= col

    def tile(c):
        n_keys = (c + 1) * blk
        for hp in range(aw // LANES):
            cols = slice(hp * LANES, (hp + 1) * LANES)
            rhs = jnp.concatenate([k_ref[0:n_keys, cols].astype(BF16), ind_ref[0:n_keys, :]], axis=1)
            vv = v_ref[0:n_keys, cols].astype(BF16)
            qp = q[:, cols] * scale
            outs = []
            for s in (0, 1):
                h = 2 * hp + s
                qm = jnp.where((lane // MOBA_HEAD_DIM) == s, qp, 0.0).astype(BF16)
                hb = jnp.where((lane // SUBLANES) == h, bias, 0.0).astype(BF16)
                sc = _dot_nt(jnp.concatenate([qm, hb], axis=1), rhs)
                own = jnp.where(tril, sc[:, c * blk:], NEG)
                sc = own if c == 0 else jnp.concatenate([sc[:, :c * blk], own], axis=1)
                p = jnp.exp(sc - jnp.max(sc, axis=1, keepdims=True))
                outs.append(_dot(p.astype(BF16), vv) / jnp.sum(p, axis=1, keepdims=True))
            o_ref[:, cols] = jnp.where((lane // MOBA_HEAD_DIM) == 0, outs[0], outs[1]).astype(BF16)

    for c in range(n_blk):
        pl.when(i == c)(functools.partial(tile, c))


def _attn_prompt(q_all, k_p, v_p, *, batch, seq):
    aw = MOBA_HEADS * MOBA_HEAD_DIM
    n_blk = seq // MOBA_BLOCK
    assert n_blk <= SUBLANES
    ind = _block_indicator(seq)
    kern = functools.partial(_attn_prompt_kernel, n_blk=n_blk)
    return pl.pallas_call(
        kern,
        out_shape=jax.ShapeDtypeStruct((batch * seq, aw), BF16),
        grid=(batch, n_blk),
        in_specs=[pl.BlockSpec((MOBA_BLOCK, aw), lambda b, i: (b * n_blk + i, 0)),
                  pl.BlockSpec((seq, aw), lambda b, i: (b, 0)),
                  pl.BlockSpec((seq, aw), lambda b, i: (b, 0)),
                  pl.BlockSpec((seq, LANES), lambda b, i: (0, 0))],
        out_specs=pl.BlockSpec((MOBA_BLOCK, aw), lambda b, i: (b * n_blk + i, 0)),
        scratch_shapes=[pltpu.VMEM((SUBLANES, aw), F32)],
        compiler_params=_cparams(("arbitrary", "arbitrary")),
        name="moba_prompt",
    )(q_all, k_p, v_p, ind)


def _attn_sample_kernel(pt_ref, q_ref, kn_ref, vn_ref, ck_ref, cv_ref, o_ref, kbuf, vbuf, sem,
                        *, n_pages, page, n_blk, layer):
    b = pl.program_id(0)
    nb = pl.num_programs(0)
    blk = MOBA_BLOCK
    hd = MOBA_HEAD_DIM
    dec = q_ref.shape[0]
    scale = hd ** -0.5

    def page_copies(seq_idx, slot):
        cps = []
        for p in range(n_pages):
            pg = pt_ref[seq_idx * n_pages + p]
            cps.append(pltpu.make_async_copy(ck_ref.at[layer, pg], kbuf.at[slot, :, :, pl.ds(p * page, page)],
                                             sem.at[0, slot]))
            cps.append(pltpu.make_async_copy(cv_ref.at[layer, pg], vbuf.at[slot, :, :, pl.ds(p * page, page)],
                                             sem.at[1, slot]))
        return cps

    @pl.when(b == 0)
    def _():
        for c in page_copies(0, 0):
            c.start()

    @pl.when(b + 1 < nb)
    def _():
        for c in page_copies(b + 1, (b + 1) % 2):
            c.start()

    slot = b % 2
    for c in page_copies(b, slot):
        c.wait()

    nh = MOBA_HEADS
    aw = nh * hd
    past = n_blk * blk
    kt = kbuf[slot].reshape(aw, past)
    vt = vbuf[slot].reshape(aw, past)
    qexp = _head_expand(q_ref[...], nh, aw)

    colj = lax.broadcasted_iota(I32, (aw, LANES), 1)
    km = jnp.zeros((aw, LANES), F32)
    for j in range(n_blk):
        km = jnp.where(colj == j, jnp.mean(kt[:, j * blk:(j + 1) * blk], axis=1, keepdims=True), km)
    gate = _dot(qexp, km, precision=HIGHEST)
    lane = lax.broadcasted_iota(I32, gate.shape, 1)
    sel = _top_blocks(gate, lane < n_blk).astype(F32)

    qs = (qexp * scale).astype(BF16)
    sc = _dot(qs, kt.astype(BF16))
    key_blk = lax.broadcasted_iota(I32, sc.shape, 1) // blk
    mask = jnp.zeros(sc.shape, F32)
    for j in range(n_blk):
        mask = jnp.where(key_blk == j, sel[:, j:j + 1], mask)
    sc = jnp.where(mask > 0.5, sc, NEG)

    pad = jnp.zeros((LANES - dec, aw), F32)
    kn = jnp.concatenate([kn_ref[...], pad], axis=0).astype(BF16)
    vn = jnp.concatenate([vn_ref[...], pad], axis=0).astype(BF16)
    qi = lax.broadcasted_iota(I32, gate.shape, 0) % dec
    sn = jnp.where(lane <= qi, _dot_nt(qs, kn), NEG)

    m = jnp.maximum(jnp.max(sc, axis=1, keepdims=True), jnp.max(sn, axis=1, keepdims=True))
    p = jnp.exp(sc - m)
    pn = jnp.exp(sn - m)
    l = jnp.sum(p, axis=1, keepdims=True) + jnp.sum(pn, axis=1, keepdims=True)
    o = (_dot_nt(p.astype(BF16), vt.astype(BF16)) + _dot(pn.astype(BF16), vn)) / l
    r = lax.broadcasted_iota(I32, o.shape, 0) // dec
    c = lax.broadcasted_iota(I32, o.shape, 1) // hd
    o = jnp.where(r == c, o, 0.0)
    acc = o[0:dec]
    for h in range(1, nh):
        acc = acc + o[h * dec:(h + 1) * dec]
    o_ref[...] = acc


def _attn_sample(page_table, q_all, k_s, v_s, cache_k, cache_v, *, layer, dec_batch, dec_seq, q_row0):
    _, n_phys, page, n_heads, hd = cache_k.shape
    aw = MOBA_HEADS * MOBA_HEAD_DIM
    n_pages = page_table.shape[1]
    past_len = n_pages * page
    assert past_len % MOBA_BLOCK == 0 and dec_seq == SUBLANES and past_len // MOBA_BLOCK == SUBLANES
    assert n_heads == MOBA_HEADS and hd == MOBA_HEAD_DIM
    kern = functools.partial(_attn_sample_kernel, n_pages=n_pages, page=page, n_blk=past_len // MOBA_BLOCK,
                             layer=layer)
    q_blk0 = q_row0 // dec_seq
    grid_spec = pltpu.PrefetchScalarGridSpec(
        num_scalar_prefetch=1,
        grid=(dec_batch,),
        in_specs=[pl.BlockSpec((dec_seq, aw), lambda b, pt: (q_blk0 + b, 0)),
                  pl.BlockSpec((dec_seq, aw), lambda b, pt: (b, 0)),
                  pl.BlockSpec((dec_seq, aw), lambda b, pt: (b, 0)),
                  pl.BlockSpec(memory_space=pl.ANY),
                  pl.BlockSpec(memory_space=pl.ANY)],
        out_specs=pl.BlockSpec((dec_seq, aw), lambda b, pt: (b, 0)),
        scratch_shapes=[pltpu.VMEM((2, n_heads, hd, past_len), F32), pltpu.VMEM((2, n_heads, hd, past_len), F32),
                        pltpu.SemaphoreType.DMA((2, 2))],
    )
    to_pos_minor = lambda c: jnp.transpose(c, (0, 1, 3, 4, 2))
    return pl.pallas_call(
        kern,
        out_shape=jax.ShapeDtypeStruct((dec_batch * dec_seq, aw), F32),
        grid_spec=grid_spec,
        compiler_params=_cparams(("arbitrary",)),
        name="moba_sample",
    )(page_table.reshape(-1), q_all, k_s, v_s, to_pos_minor(cache_k), to_pos_minor(cache_v))


def _hgrn_levels():
    n = HG_CHUNK
    return int(np.log2(n))


def _hgrn_cumsum_matrix():
    return jnp.asarray(np.tril(np.ones((HG_CHUNK, HG_CHUNK), np.float32)), dtype=BF16)


def _separator_rows(b, level):
    c, dk = b.shape
    m = 2 ** level
    if 2 * m >= SUBLANES:
        b3 = b.reshape(c // (2 * m), 2 * m, dk)
        return jnp.broadcast_to(b3[:, m - 1:m, :], b3.shape).reshape(c, dk)
    b3 = b.reshape(c // SUBLANES, SUBLANES, dk)
    sub = lax.broadcasted_iota(I32, b3.shape, 1)
    if m == 1:
        out = jnp.where(sub % 2 == 1, pltpu.roll(b3, 1, 1), b3)
    else:
        out = jnp.where(sub < 4, b3[:, 1:2, :], b3[:, 5:6, :])
    return out.reshape(c, dk)


def _hgrn_post(o, g, nw):
    o = o * lax.rsqrt(jnp.mean(o * o, axis=-1, keepdims=True) + NORM_EPS) * nw
    return o * (g * _sigmoid(g))


def _hgrn_prompt_kernel(q_ref, f_ref, i_ref, g_ref, ms_ref, nw_ref, o_ref, s_ref):
    c = HG_CHUNK
    dk = HG_DK
    n_chunks = q_ref.shape[0] // c
    levels = _hgrn_levels()
    row = lax.broadcasted_iota(I32, (c, c), 0)
    col = lax.broadcasted_iota(I32, (c, c), 1)
    rowk = lax.broadcasted_iota(I32, (c, dk), 0)
    upper = [((rowk // (2 ** l)) % 2) == 1 for l in range(levels)]
    same = [(row // (2 ** (l + 1))) == (col // (2 ** (l + 1))) for l in range(levels)]
    diag = row == col
    nw = nw_ref[...]

    heads = q_ref.shape[1] // dk

    def chunk(ci, sts):
        r0 = pl.multiple_of(ci * c, c)
        return tuple(one_head(r0, slice(hh * dk, (hh + 1) * dk), sts[hh]) for hh in range(heads))

    def one_head(r0, cols, st):
        q = q_ref[pl.ds(r0, c), cols]
        f = f_ref[pl.ds(r0, c), cols]
        v = i_ref[pl.ds(r0, c), cols]
        g = g_ref[pl.ds(r0, c), cols]
        lf = jnp.log(f)
        k = 1.0 - f
        hi = lf.astype(BF16)
        r1 = lf - hi.astype(F32)
        mid = r1.astype(BF16)
        lo = (r1 - mid.astype(F32)).astype(BF16)
        bb = _dot(ms_ref[...], jnp.concatenate([hi, mid, lo], axis=1))
        b = bb[:, 0:dk] + bb[:, dk:2 * dk] + bb[:, 2 * dk:3 * dk]
        a = jnp.where(diag, jnp.sum(q * k, axis=1, keepdims=True), 0.0)
        for l in range(levels):
            bs = _separator_rows(b, l)
            e = jnp.exp(jnp.where(upper[l], b - bs, bs - b))
            qp = jnp.where(upper[l], q * e, 0.0).astype(BF16)
            kp = jnp.where(upper[l], 0.0, k * e).astype(BF16)
            a = a + jnp.where(same[l], _dot_nt(qp, kp), 0.0)
        vb = v.astype(BF16)
        o = _dot_nt((q * jnp.exp(b)).astype(BF16), st.astype(BF16)) + _dot(a.astype(BF16), vb)
        blast = b[c - 1:c, :]
        kl = (k * jnp.exp(blast - b)).astype(BF16)
        st_new = st * jnp.exp(blast) + _dot(v.T.astype(BF16), kl)
        o_ref[pl.ds(r0, c), cols] = _hgrn_post(o, g, nw).astype(BF16)
        return st_new

    sts = lax.fori_loop(0, n_chunks, chunk, tuple(jnp.zeros((dk, dk), F32) for _ in range(heads)))
    for hh in range(heads):
        s_ref[hh] = sts[hh].T


def _hgrn_prompt(hg, nw, *, batch, seq):
    hw = HG_HEADS * HG_DK
    mstack = _hgrn_cumsum_matrix()
    hps = HG_HEADS
    groups = HG_HEADS // hps
    spec = lambda off: pl.BlockSpec((seq, hps * HG_DK), lambda b, h: (b, off + h))
    return pl.pallas_call(
        _hgrn_prompt_kernel,
        out_shape=(jax.ShapeDtypeStruct((batch * seq, hw), BF16),
                   jax.ShapeDtypeStruct((batch * HG_HEADS, HG_DK, HG_DK), F32)),
        grid=(batch, groups),
        in_specs=[spec(0), spec(groups), spec(2 * groups), spec(3 * groups),
                  pl.BlockSpec(mstack.shape, lambda b, h: (0, 0)),
                  pl.BlockSpec((1, HG_DK), lambda b, h: (0, 0))],
        out_specs=(pl.BlockSpec((seq, hps * HG_DK), lambda b, h: (b, h)),
                   pl.BlockSpec((hps, HG_DK, HG_DK), lambda b, h: (b * groups + h, 0, 0))),
        compiler_params=_cparams(("arbitrary", "arbitrary")),
        name="hgrn_prompt",
    )(hg, hg, hg, hg, mstack, nw)


def _hgrn_sample_kernel(hg_ref, s0_ref, nw_ref, o_ref, s_ref, *, dec, seqs):
    dk = HG_DK
    hw = HG_HEADS * HG_DK
    nw = nw_ref[...]
    trow = lax.broadcasted_iota(I32, (dec, dk), 0)
    srow = lax.broadcasted_iota(I32, (dk, dk), 0)
    zrows = jnp.zeros((dec, dk), F32)
    pad16 = lambda a: jnp.concatenate([a, zrows], axis=0).astype(BF16)
    dot_tn = lambda a, b: lax.dot_general(a, b, (((0,), (0,)), ((), ())), preferred_element_type=F32)

    def one_seq(si, carry):
        r0 = pl.multiple_of(si * dec, dec)
        for h in range(HG_HEADS):
            cols = slice(h * dk, (h + 1) * dk)
            q = hg_ref[pl.ds(r0, dec), h * dk:(h + 1) * dk]
            f = hg_ref[pl.ds(r0, dec), hw + h * dk:hw + (h + 1) * dk]
            v = hg_ref[pl.ds(r0, dec), 2 * hw + h * dk:2 * hw + (h + 1) * dk]
            g = hg_ref[pl.ds(r0, dec), 3 * hw + h * dk:3 * hw + (h + 1) * dk]
            k = 1.0 - f
            b = jnp.log(f)
            for dlt in (1, 2, 4):
                b = b + jnp.where(trow >= dlt, pltpu.roll(b, dlt, 0), 0.0)
            s0 = s0_ref[si * HG_HEADS + h]
            o = _dot(pad16(q * jnp.exp(b)), s0.astype(BF16))[0:dec]
            for s in range(dec):
                live = trow >= s
                e = jnp.exp(jnp.where(live, b - b[s:s + 1, :], 0.0))
                a_s = jnp.sum(jnp.where(live, q * e * k[s:s + 1, :], 0.0), axis=1, keepdims=True)
                o = o + a_s * v[s:s + 1, :]
            o_ref[pl.ds(r0, dec), cols] = _hgrn_post(o, g, nw)
            blast = b[dec - 1:dec, :]
            upd = dot_tn(pad16(k * jnp.exp(blast - b)), pad16(v))
            ecol = jnp.where(srow == 0, jnp.exp(blast), 0.0).T[:, 0:1]
            s_ref[si * HG_HEADS + h] = ecol * s0 + upd
        return carry

    lax.fori_loop(0, seqs, one_seq, 0)


def _hgrn_sample(hg, s0, nw, *, dec_batch, dec_seq, row0):
    hw = HG_HEADS * HG_DK
    seqs = 8
    rows = seqs * dec_seq
    kern = functools.partial(_hgrn_sample_kernel, dec=dec_seq, seqs=seqs)
    st_spec = pl.BlockSpec((seqs * HG_HEADS, HG_DK, HG_DK), lambda i: (i, 0, 0))
    return pl.pallas_call(
        kern,
        out_shape=(jax.ShapeDtypeStruct((dec_batch * dec_seq, hw), F32),
                   jax.ShapeDtypeStruct((dec_batch * HG_HEADS, HG_DK, HG_DK), F32)),
        grid=(dec_batch // seqs,),
        in_specs=[pl.BlockSpec((rows, 4 * hw), lambda i: (row0 // rows + i, 0)), st_spec,
                  pl.BlockSpec((1, HG_DK), lambda i: (0, 0))],
        out_specs=(pl.BlockSpec((rows, hw), lambda i: (i, 0)), st_spec),
        compiler_params=_cparams(("arbitrary",)),
        name="hgrn_sample",
    )(hg, s0, nw)


def _post_mixer_kernel(*refs, n_a, npt):
    hp_ref, hs_ref, mp_ref, ms_ref = refs[0:4]
    a_refs = refs[4:4 + 2 * n_a]
    w_refs = refs[4 + 2 * n_a:4 + 3 * n_a]
    nffn_ref, rw_ref, rb_ref, stril_ref = refs[4 + 3 * n_a:8 + 3 * n_a]
    hmid_ref, u_ref, rinfo_ref, counts_ref, carry = refs[8 + 3 * n_a:]
    i = pl.program_id(0)

    @pl.when(i == 0)
    def _():
        carry[...] = jnp.zeros_like(carry)

    def run(is_prompt):
        h = (hp_ref if is_prompt else hs_ref)[...]
        m3 = (mp_ref if is_prompt else ms_ref)[...]
        out = None
        for a in range(n_a):
            av = a_refs[2 * a + (0 if is_prompt else 1)][...].astype(BF16)
            t = _dot(av, w_refs[a][...])
            out = t if out is None else out + t
        hm = _gate_residual(h, m3, 2, out)
        hmid_ref[...] = hm
        u = _norm_mod(hm, nffn_ref[...], m3, 4, 3)
        for k in range(u.shape[1] // LANES):
            u_ref[:, k, :] = u[:, k * LANES:(k + 1) * LANES]

        u_hi = u.astype(BF16)
        u_lo = (u - u_hi.astype(F32)).astype(BF16)
        hh = _dot(u_hi, rw_ref[...])
        logits = hh[:, 0:LANES] + hh[:, LANES:2 * LANES] + _dot(u_lo, rw_ref[:, 0:LANES]) + rb_ref[...]
        col = lambda j: logits[:, j:j + 1]
        gl = [col(j) for j in range(N_GROUPS)]
        gmax = functools.reduce(jnp.maximum, gl)
        gsum = functools.reduce(lambda x, y: x + y, [jnp.exp(x - gmax) for x in gl])
        gval = 1.0 / gsum
        gidx = jnp.where(gl[0] == gmax, 0, jnp.where(gl[1] == gmax, 1, jnp.where(gl[2] == gmax, 2, 3)))
        el = []
        for k in range(EXPERTS_PER_GROUP):
            c = [col(N_GROUPS + g * EXPERTS_PER_GROUP + k) for g in range(N_GROUPS)]
            el.append(jnp.where(gidx == 0, c[0], jnp.where(gidx == 1, c[1], jnp.where(gidx == 2, c[2], c[3]))))
        emax = functools.reduce(jnp.maximum, el)
        pe = [jnp.exp(x - emax) for x in el]
        esum = functools.reduce(lambda x, y: x + y, pe)
        pk = [x / esum for x in pe]
        v1 = functools.reduce(jnp.maximum, pk)
        i1 = jnp.where(pk[0] == v1, 0, jnp.where(pk[1] == v1, 1, jnp.where(pk[2] == v1, 2, 3)))
        pk2 = [jnp.where(i1 == k, -1.0, pk[k]) for k in range(EXPERTS_PER_GROUP)]
        v2 = functools.reduce(jnp.maximum, pk2)
        i2 = jnp.where(pk2[0] == v2, 0, jnp.where(pk2[1] == v2, 1, jnp.where(pk2[2] == v2, 2, 3)))
        den = v1 + v2
        w1 = gval * v1 / den
        w2 = gval * v2 / den
        lo = jnp.minimum(i1, i2)
        hi = jnp.maximum(i1, i2)
        pair = jnp.where(lo == 0, hi - 1, jnp.where(lo == 1, hi + 1, 5))
        bucket = gidx * 6 + pair
        w_lo = jnp.where(i1 < i2, w1, w2)
        w_hi = jnp.where(i1 < i2, w2, w1)

        r = h.shape[0]
        lane = lax.broadcasted_iota(I32, (r, LANES), 1)
        onehot = lane == bucket
        before = _dot(stril_ref[...], onehot.astype(BF16)) + carry[...]
        rank = jnp.sum(jnp.where(onehot, before, 0.0), axis=1, keepdims=True)
        carry[...] = carry[...] + jnp.sum(onehot.astype(F32), axis=0, keepdims=True)
        rinfo_ref[...] = jnp.where(lane == 0, bucket.astype(F32),
                                   jnp.where(lane == 1, w_lo, jnp.where(lane == 2, w_hi,
                                                                        jnp.where(lane == 3, rank, 0.0))))
        counts_ref[...] = carry[...]

    _by_group(i, npt, run)


def _post_mixer(h_pair, h_off_s, mp, ms, a_pairs, ws, nffn, rw, rb, *, seq):
    hp, hs = h_pair
    d = hp.shape[1]
    tp = a_pairs[0][0].shape[0]
    ts = a_pairs[0][1].shape[0]
    npt, nst = tp // ROW_TILE, ts // ROW_TILE
    t = tp + ts
    n_a = len(a_pairs)
    tiles_per_seq = seq // ROW_TILE
    const = lambda shape: pl.BlockSpec(shape, lambda i: (0,) * len(shape))
    stril = jnp.asarray(np.tril(np.ones((ROW_TILE, ROW_TILE), np.float32), -1), dtype=BF16)
    in_specs = [*_pair_specs(d, npt, off_s=h_off_s), *_mod_specs(d, npt, tiles_per_seq)]
    args = [hp, hs, mp, ms]
    for ap, as_ in a_pairs:
        in_specs += list(_pair_specs(ap.shape[1], npt))
        args += [ap, as_]
    for w in ws:
        in_specs.append(const(w.shape))
        args.append(w)
    in_specs += [const((1, d)), const(rw.shape), const((1, LANES)), const(stril.shape)]
    args += [nffn, rw, rb, stril]
    tile = lambda c: pl.BlockSpec((ROW_TILE, c), lambda i: (i, 0))
    kern = functools.partial(_post_mixer_kernel, n_a=n_a, npt=npt)
    return pl.pallas_call(
        kern,
        out_shape=(jax.ShapeDtypeStruct((t, d), F32), jax.ShapeDtypeStruct((t, d // LANES, LANES), F32),
                   jax.ShapeDtypeStruct((t, LANES), F32), jax.ShapeDtypeStruct((1, LANES), F32)),
        grid=(npt + nst,),
        in_specs=in_specs,
        out_specs=(tile(d), pl.BlockSpec((ROW_TILE, d // LANES, LANES), lambda i: (i, 0, 0)), tile(LANES),
                   const((1, LANES))),
        scratch_shapes=[pltpu.VMEM((1, LANES), F32)],
        compiler_params=_cparams(("arbitrary",)),
        name="post_mixer",
    )(*args)


def _for_rows(rows, fn):
    def body(r, c):
        fn(r)
        return c

    lax.fori_loop(0, rows, body, 0, unroll=8)


def _slot_of(bucket_ref, rank_ref, starts_ref, t):
    return starts_ref[bucket_ref[t]] + rank_ref[t]


def _scatter_kernel(bucket_ref, rank_ref, starts_ref, src_ref, init_ref, o_ref, sem):
    del init_ref
    rows = src_ref.shape[0]
    base = pl.program_id(0) * rows
    copy = lambda r: pltpu.make_async_copy(
        src_ref.at[r], o_ref.at[_slot_of(bucket_ref, rank_ref, starts_ref, base + r)], sem)
    _for_rows(rows, lambda r: copy(r).start())
    _for_rows(rows, lambda r: copy(r).wait())


def _scatter_tokens(slots, src, n_slots):
    t = src.shape[0]
    rows = ROW_TILE
    grid_spec = pltpu.PrefetchScalarGridSpec(
        num_scalar_prefetch=3,
        grid=(t // rows,),
        in_specs=[pl.BlockSpec((rows,) + src.shape[1:], lambda i, *_: (i, 0, 0)), pl.BlockSpec(memory_space=pl.ANY)],
        out_specs=pl.BlockSpec(memory_space=pl.ANY),
        scratch_shapes=[pltpu.SemaphoreType.DMA(())],
    )
    return pl.pallas_call(
        _scatter_kernel,
        out_shape=jax.ShapeDtypeStruct((n_slots,) + src.shape[1:], src.dtype),
        grid_spec=grid_spec,
        input_output_aliases={4: 0},
        compiler_params=_cparams(("arbitrary",)),
        name="moe_scatter",
    )(*slots, src, jnp.zeros((n_slots,) + src.shape[1:], src.dtype))


def _moe_kernel(ea_ref, eb_ref, valid_ref, fresh_ref, x_ref,
                w1a_ref, w3a_ref, w2a_ref, w1b_ref, w3b_ref, w2b_ref, o_ref, wbuf1, wbuf3, wbuf2):
    i = pl.program_id(0)
    nk = x_ref.shape[1]

    @pl.when(fresh_ref[i] == 1)
    def _():
        wbuf1[0] = w1a_ref[0, 0].astype(BF16)
        wbuf3[0] = w3a_ref[0, 0].astype(BF16)
        wbuf2[0] = w2a_ref[0, 0].astype(BF16)
        wbuf1[1] = w1b_ref[0, 0].astype(BF16)
        wbuf3[1] = w3b_ref[0, 0].astype(BF16)
        wbuf2[1] = w2b_ref[0, 0].astype(BF16)

    @pl.when(valid_ref[i] == 1)
    def _():
        xb = jnp.concatenate([x_ref[:, k, :] for k in range(nk)], axis=1).astype(BF16)
        for s in (0, 1):
            h1 = _dot(xb, wbuf1[s])
            h3 = _dot(xb, wbuf3[s])
            y = _dot(((h1 * _sigmoid(h1)) * h3).astype(BF16), wbuf2[s])
            for k in range(nk):
                o_ref[:, s * nk + k, :] = y[:, k * LANES:(k + 1) * LANES]

    @pl.when(valid_ref[i] == 0)
    def _():
        o_ref[...] = jnp.zeros_like(o_ref)


def _moe_experts(tile_ea, tile_eb, tile_valid, tile_fresh, xs, w1, w3, w2, layer):
    n_slots, nk, _ = xs.shape
    d = nk * LANES
    ff = w1.shape[-1]
    wa = lambda shape: pl.BlockSpec((1, 1) + shape, lambda i, ea, eb, va, fr: (layer, ea[i], 0, 0))
    wb = lambda shape: pl.BlockSpec((1, 1) + shape, lambda i, ea, eb, va, fr: (layer, eb[i], 0, 0))
    rows = lambda k: pl.BlockSpec((MOE_TILE, k, LANES), lambda i, ea, eb, va, fr: (i, 0, 0))
    grid_spec = pltpu.PrefetchScalarGridSpec(
        num_scalar_prefetch=4,
        grid=(n_slots // MOE_TILE,),
        in_specs=[rows(nk), wa((d, ff)), wa((d, ff)), wa((ff, d)), wb((d, ff)), wb((d, ff)), wb((ff, d))],
        out_specs=rows(2 * nk),
        scratch_shapes=[pltpu.VMEM((2, d, ff), BF16), pltpu.VMEM((2, d, ff), BF16), pltpu.VMEM((2, ff, d), BF16)],
    )
    return pl.pallas_call(
        _moe_kernel,
        out_shape=jax.ShapeDtypeStruct((n_slots, 2 * nk, LANES), F32),
        grid_spec=grid_spec,
        compiler_params=_cparams(("arbitrary",)),
        name="moe_experts",
    )(tile_ea, tile_eb, tile_valid, tile_fresh, xs, w1, w3, w2, w1, w3, w2)


def _moe_plan(rinfo, counts, n_slots):
    n_tiles = n_slots // MOE_TILE
    cnt = counts[0, :N_BUCKETS].astype(I32)
    padded = ((cnt + MOE_TILE - 1) // MOE_TILE) * MOE_TILE
    ends = jnp.cumsum(padded)
    starts = ends - padded
    slots = (rinfo[:, 0].astype(I32), rinfo[:, 3].astype(I32), starts)
    n_valid = ends[-1] // MOE_TILE
    tiles = jnp.arange(n_tiles, dtype=I32)
    tb = jnp.sum((ends[None, :] <= (tiles * MOE_TILE)[:, None]).astype(I32), axis=1)
    tb = jnp.minimum(tb, N_BUCKETS - 1)
    valid = tiles < n_valid
    tb = jnp.where(valid, tb, tb[jnp.maximum(n_valid - 1, 0)])
    lo = jnp.asarray(PAIR_LO, I32)
    hi = jnp.asarray(PAIR_HI, I32)
    ea = (tb // 6) * EXPERTS_PER_GROUP + lo[tb % 6]
    eb = (tb // 6) * EXPERTS_PER_GROUP + hi[tb % 6]
    fresh = jnp.concatenate([jnp.ones((1,), I32), (tb[1:] != tb[:-1]).astype(I32)])
    return slots, ea, eb, valid.astype(I32), fresh


def _moe(u_tiles, rinfo, counts, w1, w3, w2, layer):
    t = u_tiles.shape[0]
    n_slots = t + N_BUCKETS * MOE_TILE
    slots, ea, eb, valid, fresh = _moe_plan(rinfo, counts, n_slots)
    xs = _scatter_tokens(slots, u_tiles, n_slots)
    ys = _moe_experts(ea, eb, valid, fresh, xs, w1, w3, w2, layer)
    return slots, ys


def _combine_kernel(bucket_ref, rank_ref, starts_ref, ys_ref, hm_ref, rinfo_ref, mp_ref, ms_ref, nf_ref, *rest,
                    npt, final):
    if final:
        yp_ref, ysm_ref, gbuf, sem = rest
    else:
        h_ref, gbuf, sem = rest
    i = pl.program_id(0)
    n = pl.num_programs(0)
    rows = gbuf.shape[1]
    nk = gbuf.shape[2] // 2

    def copy(tile, r):
        slot = tile % 2
        src = ys_ref.at[_slot_of(bucket_ref, rank_ref, starts_ref, tile * rows + r)]
        return pltpu.make_async_copy(src, gbuf.at[slot, r], sem.at[slot])

    @pl.when(i == 0)
    def _():
        _for_rows(rows, lambda r: copy(0, r).start())

    @pl.when(i + 1 < n)
    def _():
        _for_rows(rows, lambda r: copy(i + 1, r).start())

    _for_rows(rows, lambda r: copy(i, r).wait())
    slot = i % 2
    ya = jnp.concatenate([gbuf[slot, :, k, :] for k in range(nk)], axis=1)
    yb = jnp.concatenate([gbuf[slot, :, nk + k, :] for k in range(nk)], axis=1)
    rinfo = rinfo_ref[...]
    moe = rinfo[:, 1:2] * ya + rinfo[:, 2:3] * yb

    def run(is_prompt):
        m3 = (mp_ref if is_prompt else ms_ref)[...]
        h = _gate_residual(hm_ref[...], m3, 5, moe)
        if final:
            y = h * lax.rsqrt(jnp.mean(h * h, axis=-1, keepdims=True) + NORM_EPS) * nf_ref[...]
            (yp_ref if is_prompt else ysm_ref)[...] = y
        else:
            h_ref[...] = h

    _by_group(i, npt, run)


def _combine(slots, ys, hmid, rinfo, mp, ms, nf, *, tp, seq, final):
    t, d = hmid.shape
    npt = tp // ROW_TILE
    tiles_per_seq = seq // ROW_TILE
    mpm = lambda f: (lambda i, *_: f(i))
    mp_spec, ms_spec = _mod_specs(d, npt, tiles_per_seq)
    yp_spec, ysm_spec = _pair_specs(d, npt)
    wrap = lambda s: pl.BlockSpec(s.block_shape, mpm(s.index_map))
    if final:
        out_shape = (jax.ShapeDtypeStruct((tp, d), F32), jax.ShapeDtypeStruct((t - tp, d), F32))
        out_specs = (wrap(yp_spec), wrap(ysm_spec))
    else:
        out_shape = jax.ShapeDtypeStruct((t, d), F32)
        out_specs = pl.BlockSpec((ROW_TILE, d), lambda i, *_: (i, 0))
    grid_spec = pltpu.PrefetchScalarGridSpec(
        num_scalar_prefetch=3,
        grid=(t // ROW_TILE,),
        in_specs=[pl.BlockSpec(memory_space=pl.ANY),
                  pl.BlockSpec((ROW_TILE, d), lambda i, *_: (i, 0)),
                  pl.BlockSpec((ROW_TILE, LANES), lambda i, *_: (i, 0)),
                  wrap(mp_spec), wrap(ms_spec),
                  pl.BlockSpec((1, d), lambda i, *_: (0, 0))],
        out_specs=out_specs,
        scratch_shapes=[pltpu.VMEM((2, ROW_TILE) + ys.shape[1:], F32), pltpu.SemaphoreType.DMA((2,))],
    )
    kern = functools.partial(_combine_kernel, npt=npt, final=final)
    return pl.pallas_call(
        kern, out_shape=out_shape, grid_spec=grid_spec,
        compiler_params=_cparams(("arbitrary",)),
        name="moe_combine_final" if final else "moe_combine",
    )(*slots, ys, hmid, rinfo, mp, ms, nf)


def _odd_pre_kernel(hp_ref, hs_ref, mp_ref, ms_ref, nw_ref, w_ref, gate_ref, rec_ref, *, npt):
    i = pl.program_id(0)
    width = gate_ref.shape[1]

    def run(is_prompt):
        h = (hp_ref if is_prompt else hs_ref)[...]
        m3 = (mp_ref if is_prompt else ms_ref)[...]
        ub = _norm_mod(h, nw_ref[...], m3, 1, 0).astype(BF16)
        x = _dot(ub, w_ref[:, 0:width])
        gate_ref[...] = 0.5 * x * (1.0 + jnp.tanh(np.sqrt(2.0 / np.pi) * (x + 0.044715 * (x * x * x))))
        rec_ref[...] = _dot(ub, w_ref[:, width:2 * width])

    _by_group(i, npt, run)


def _odd_pre(h, mp, ms, nw, w_in, *, tp, seq):
    t, d = h.shape
    npt = tp // ROW_TILE
    width = w_in.shape[1] // 2
    const = lambda shape: pl.BlockSpec(shape, lambda i: (0,) * len(shape))
    tile = lambda c: pl.BlockSpec((ROW_TILE, c), lambda i: (i, 0))
    kern = functools.partial(_odd_pre_kernel, npt=npt)
    return pl.pallas_call(
        kern,
        out_shape=(jax.ShapeDtypeStruct((t, width), F32), jax.ShapeDtypeStruct((t, width), F32)),
        grid=(t // ROW_TILE,),
        in_specs=[*_pair_specs(d, npt, off_s=npt), *_mod_specs(d, npt, seq // ROW_TILE),
                  const((1, d)), const(w_in.shape)],
        out_specs=(tile(width), tile(width)),
        compiler_params=_cparams(("arbitrary",)),
        name="odd_pre",
    )(h, h, mp, ms, nw, w_in)


def _lru_kernel(rec_ref, gate_ref, c0_ref, h0_ref, cw_ref, cb_ref, wa_ref, ba_ref, wx_ref, bx_ref, lam_ref,
                yg_ref, cs_ref, hl_ref, hist, hc, a_s, h_s, *, nb, tl):
    i = pl.program_id(1)
    r, width = rec_ref.shape
    taps = CONV_WIDTH
    blk = width // LRU_HEADS

    @pl.when(i == 0)
    def _():
        hist[:, SUBLANES - (taps - 1):SUBLANES, :] = c0_ref[...]
        hc[...] = h0_ref[...]

    rec3 = rec_ref[...].reshape(nb, tl, width)
    hist[:, SUBLANES:SUBLANES + tl, :] = rec3
    cw = cw_ref[...]
    conv = cb_ref[...] + rec3 * cw[taps - 1:taps, :]
    for back in range(1, taps):
        conv = conv + hist[:, SUBLANES - back:SUBLANES - back + tl, :] * cw[taps - 1 - back:taps - back, :]
    tail = hist[:, SUBLANES + tl - (taps - 1):SUBLANES + tl, :]
    cs_ref[...] = tail
    hist[:, SUBLANES - (taps - 1):SUBLANES, :] = tail

    cf = conv.reshape(r, width)
    cb16 = cf.astype(BF16)
    gr = jnp.concatenate([_dot(cb16[:, h * blk:(h + 1) * blk], wa_ref[h]) for h in range(LRU_HEADS)], axis=1)
    gi = jnp.concatenate([_dot(cb16[:, h * blk:(h + 1) * blk], wx_ref[h]) for h in range(LRU_HEADS)], axis=1)
    rg = _sigmoid(gr + ba_ref[...])
    ig = _sigmoid(gi + bx_ref[...])
    z = -lam_ref[...]
    softplus = jnp.maximum(z, 0.0) + jnp.log1p(jnp.exp(-jnp.abs(z)))
    log_a = -LRU_C * rg * softplus
    a = jnp.exp(log_a)
    gx = jnp.sqrt(1.0 - a * a) * (ig * cf)

    groups = r // SUBLANES
    sub = lax.broadcasted_iota(I32, (groups, SUBLANES, width), 1)
    aa, hh = a.reshape(groups, SUBLANES, width), gx.reshape(groups, SUBLANES, width)
    for dlt in (1, 2, 4):
        ok = sub >= dlt
        a_sh = pltpu.roll(aa, dlt, 1)
        h_sh = pltpu.roll(hh, dlt, 1)
        hh = jnp.where(ok, hh + aa * h_sh, hh)
        aa = jnp.where(ok, aa * a_sh, aa)
    aa, hh = aa.reshape(r, width), hh.reshape(r, width)

    if tl == SUBLANES:
        y3 = hh.reshape(nb, tl, width) + aa.reshape(nb, tl, width) * hc[...]
        hl_ref[...] = y3[:, tl - 1:tl, :]
        y = y3.reshape(r, width)
    else:
        a_s[...] = aa
        h_s[...] = hh

        def group(j, carry):
            r0 = pl.multiple_of(j * SUBLANES, SUBLANES)
            yj = h_s[pl.ds(r0, SUBLANES), :] + a_s[pl.ds(r0, SUBLANES), :] * carry
            h_s[pl.ds(r0, SUBLANES), :] = yj
            return yj[SUBLANES - 1:SUBLANES, :]

        last = lax.fori_loop(0, r // SUBLANES, group, hc[0])
        hc[0] = last
        hl_ref[0] = last
        y = h_s[...]
    yg_ref[...] = (y * gate_ref[...]).astype(BF16)


def _lru(rec, gate, conv0, h0, cw, cb, wa, ba, wx, bx, lam, *, batch, seq, row0):
    width = rec.shape[1]
    taps = CONV_WIDTH
    if seq == SUBLANES:
        nb, tl = ROW_TILE // SUBLANES, SUBLANES
    else:
        nb, tl = 1, ROW_TILE
    r = nb * tl
    n_l = seq // tl
    blk0 = row0 // r
    const = lambda shape: pl.BlockSpec(shape, lambda b, i: (0,) * len(shape))
    tile_in = pl.BlockSpec((r, width), lambda b, i: (blk0 + b * n_l + i, 0))
    kern = functools.partial(_lru_kernel, nb=nb, tl=tl)
    return pl.pallas_call(
        kern,
        out_shape=(jax.ShapeDtypeStruct((batch * seq, width), BF16),
                   jax.ShapeDtypeStruct((batch, taps - 1, width), F32),
                   jax.ShapeDtypeStruct((batch, 1, width), F32)),
        grid=(batch // nb, n_l),
        in_specs=[tile_in, tile_in,
                  pl.BlockSpec((nb, taps - 1, width), lambda b, i: (b, 0, 0)),
                  pl.BlockSpec((nb, 1, width), lambda b, i: (b, 0, 0)),
                  const((taps, width)), const((1, width)), const(wa.shape), const((1, width)),
                  const(wx.shape), const((1, width)), const((1, width))],
        out_specs=(pl.BlockSpec((r, width), lambda b, i: (b * n_l + i, 0)),
                   pl.BlockSpec((nb, taps - 1, width), lambda b, i: (b, 0, 0)),
                   pl.BlockSpec((nb, 1, width), lambda b, i: (b, 0, 0))),
        scratch_shapes=[pltpu.VMEM((nb, tl + SUBLANES, width), F32), pltpu.VMEM((nb, 1, width), F32),
                        pltpu.VMEM((r, width), F32), pltpu.VMEM((r, width), F32)],
        compiler_params=_cparams(("arbitrary", "arbitrary")),
        name="rg_lru_sample" if seq == SUBLANES else "rg_lru_prompt",
    )(rec, gate, conv0, h0, cw, cb, wa, ba, wx, bx, lam)


def kernel(x_prompt, x_sample, c_prompt, c_sample, cache_k, cache_v, state_hgrn, state_conv, state_lru, page_table,
           ada_w, ada_b, norm_mix, norm_ffn, norm_final, w_in_even, w_out_even, hg_lower_bounds, hg_norm_w, w_in_odd,
           conv_w, conv_b, lru_wa, lru_ba, lru_wx, lru_bx, lru_lambda, w_out_odd, router_group_w, router_group_b,
           router_expert_w, router_expert_b, moe_w1, moe_w3, moe_w2):
    batch, seq, d = x_prompt.shape
    dec_batch, dec_seq, _ = x_sample.shape
    depth = ada_w.shape[0]
    assert depth == 2 and seq % ROW_TILE == 0 and seq % MOBA_BLOCK == 0
    tp, ts = batch * seq, dec_batch * dec_seq
    n_pages, page = page_table.shape[1], cache_k.shape[2]
    past_len = n_pages * page
    aw = MOBA_HEADS * MOBA_HEAD_DIM
    hw = HG_HEADS * HG_DK

    xp = x_prompt.reshape(tp, d)
    xs = x_sample.reshape(ts, d)
    mods = _ada_mods(jnp.concatenate([c_prompt, c_sample], axis=0), ada_w, ada_b)
    mods = mods.reshape(depth, batch + dec_batch, 6, d)
    mods_p, mods_s = mods[:, :batch], mods[:, batch:]

    half = MOBA_HEAD_DIM // 2
    inv_freq = jnp.power(ROPE_THETA, -jnp.arange(half, dtype=F32) / half)
    invf = jnp.tile(inv_freq, LANES // half).reshape(1, LANES)

    def router_mats(l):
        rw = jnp.zeros((d, LANES), F32)
        rw = rw.at[:, 0:N_GROUPS].set(router_group_w[l])
        rw = rw.at[:, N_GROUPS:N_GROUPS + N_GROUPS * EXPERTS_PER_GROUP].set(router_expert_w[l])
        rb = jnp.zeros((1, LANES), F32)
        rb = rb.at[0, 0:N_GROUPS].set(router_group_b[l])
        rb = rb.at[0, N_GROUPS:N_GROUPS + N_GROUPS * EXPERTS_PER_GROUP].set(router_expert_b[l])
        rw_hi = rw.astype(BF16)
        rw_lo = (rw - rw_hi.astype(F32)).astype(BF16)
        return jnp.concatenate([rw_hi, rw_lo], axis=1), rb

    q_all, k_p, k_s, v_p, v_s, hg = _even_pre(
        xp, xs, mods_p[0], mods_s[0], norm_mix[0].reshape(1, d), w_in_even[0].astype(BF16), hg_lower_bounds, invf,
        seq=seq, past_len=past_len, layer_slot=0)
    oa_p = _attn_prompt(q_all, k_p, v_p, batch=batch, seq=seq)
    oa_s = _attn_sample(page_table, q_all, k_s, v_s, cache_k, cache_v,
                        layer=0, dec_batch=dec_batch, dec_seq=dec_seq, q_row0=tp)
    hnw = hg_norm_w[0].reshape(1, HG_DK)
    ob_p, hgrn_p = _hgrn_prompt(hg, hnw, batch=batch, seq=seq)
    ob_s, hgrn_s = _hgrn_sample(hg, state_hgrn[0].reshape(dec_batch * HG_HEADS, HG_DK, HG_DK), hnw,
                                dec_batch=dec_batch, dec_seq=dec_seq, row0=tp)
    w_out = w_out_even[0].astype(BF16)
    rw, rb = router_mats(0)
    hmid, u, rinfo, counts = _post_mixer((xp, xs), 0, mods_p[0], mods_s[0], [(oa_p, oa_s), (ob_p, ob_s)],
                                         [w_out[:aw], w_out[aw:]], norm_ffn[0].reshape(1, d), rw, rb, seq=seq)
    dest, ys = _moe(u, rinfo, counts, moe_w1, moe_w3, moe_w2, 0)
    h1 = _combine(dest, ys, hmid, rinfo, mods_p[0], mods_s[0], norm_final.reshape(1, d), tp=tp, seq=seq,
                  final=False)

    gate, rec = _odd_pre(h1, mods_p[1], mods_s[1], norm_mix[1].reshape(1, d), w_in_odd[0].astype(BF16), tp=tp, seq=seq)
    width = rec.shape[1]
    lru_args = (conv_w[0], conv_b[0].reshape(1, width), lru_wa[0].astype(BF16), lru_ba[0].reshape(1, width),
                lru_wx[0].astype(BF16), lru_bx[0].reshape(1, width), lru_lambda[0].reshape(1, width))
    yg_p, conv_p, lru_p = _lru(rec, gate, jnp.zeros((batch, CONV_WIDTH - 1, width), F32),
                               jnp.zeros((batch, 1, width), F32), *lru_args, batch=batch, seq=seq, row0=0)
    yg_s, conv_s, lru_s = _lru(rec, gate, state_conv[0], state_lru[0].reshape(dec_batch, 1, width), *lru_args,
                               batch=dec_batch, seq=dec_seq, row0=tp)
    rw, rb = router_mats(1)
    hmid, u, rinfo, counts = _post_mixer((h1, h1), tp // ROW_TILE, mods_p[1], mods_s[1], [(yg_p, yg_s)],
                                         [w_out_odd[0].astype(BF16)], norm_ffn[1].reshape(1, d), rw, rb, seq=seq)
    dest, ys = _moe(u, rinfo, counts, moe_w1, moe_w3, moe_w2, 1)
    y_p, y_s = _combine(dest, ys, hmid, rinfo, mods_p[1], mods_s[1], norm_final.reshape(1, d), tp=tp, seq=seq,
                        final=True)

    return (y_p.reshape(batch, seq, d), y_s.reshape(dec_batch, dec_seq, d),
            k_p.reshape(1, batch, seq, MOBA_HEADS, MOBA_HEAD_DIM), v_p.reshape(1, batch, seq, MOBA_HEADS, MOBA_HEAD_DIM),
            k_s.reshape(1, dec_batch, dec_seq, MOBA_HEADS, MOBA_HEAD_DIM),
            v_s.reshape(1, dec_batch, dec_seq, MOBA_HEADS, MOBA_HEAD_DIM),
            hgrn_p.reshape(1, batch, HG_HEADS, HG_DK, HG_DK), hgrn_s.reshape(1, dec_batch, HG_HEADS, HG_DK, HG_DK),
            conv_p.reshape(1, batch, CONV_WIDTH - 1, width), conv_s.reshape(1, dec_batch, CONV_WIDTH - 1, width),
            lru_p.reshape(1, batch, width), lru_s.reshape(1, dec_batch, width))
```

```python
import functools

import numpy as np
import jax
import jax.numpy as jnp
from jax import lax
from jax.experimental import pallas as pl
from jax.experimental.pallas import tpu as pltpu

F32 = jnp.float32
BF16 = jnp.bfloat16
I32 = jnp.int32
HIGHEST = lax.Precision.HIGHEST

MOBA_HEADS = 8
MOBA_HEAD_DIM = 64
MOBA_BLOCK = 256
MOBA_TOPK = 3
ROPE_THETA = 10000.0
HG_HEADS = 4
HG_DK = 128
LRU_HEADS = 4
CONV_WIDTH = 4
LRU_C = 8.0
N_GROUPS = 4
EXPERTS_PER_GROUP = 4
NORM_EPS = 1e-6

LANES = 128
SUBLANES = 8
VMEM_LIMIT = 56 * 1024 * 1024

ROW_TILE = 256
MOE_TILE = 256
HG_CHUNK = 128
N_BUCKETS = N_GROUPS * 6
PAIR_LO = (0, 0, 0, 1, 1, 2)
PAIR_HI = (1, 2, 3, 2, 3, 3)
NEG = -1e30


def _cparams(sem, vmem=VMEM_LIMIT):
    return pltpu.CompilerParams(dimension_semantics=sem, vmem_limit_bytes=vmem)


def _dot(a, b, **kw):
    return jnp.dot(a, b, preferred_element_type=F32, **kw)


def _dot_nt(a, b, **kw):
    return lax.dot_general(a, b, (((1,), (1,)), ((), ())), preferred_element_type=F32, **kw)


def _sigmoid(x):
    return jax.nn.sigmoid(x)


def _by_group(i, n_prompt_tiles, fn):
    @pl.when(i < n_prompt_tiles)
    def _():
        fn(True)

    @pl.when(i >= n_prompt_tiles)
    def _():
        fn(False)


def _pair_specs(cols, npt, rows=ROW_TILE, off_p=0, off_s=0):
    return (pl.BlockSpec((rows, cols), lambda i: (off_p + jnp.minimum(i, npt - 1), 0)),
            pl.BlockSpec((rows, cols), lambda i: (off_s + jnp.maximum(i - npt, 0), 0)))


def _mod_specs(d, npt, tiles_per_seq, rows=ROW_TILE):
    return (pl.BlockSpec((1, 6, d), lambda i: (jnp.minimum(i, npt - 1) // tiles_per_seq, 0, 0)),
            pl.BlockSpec((rows // SUBLANES, 6, d), lambda i: (jnp.maximum(i - npt, 0), 0, 0)))


def _norm_mod(x, nw, m3, scale_idx, shift_idx):
    r, d = x.shape
    nb = m3.shape[0]
    var = jnp.mean(x * x, axis=-1, keepdims=True)
    y = x * lax.rsqrt(var + NORM_EPS) * nw
    y3 = y.reshape(nb, r // nb, d)
    u3 = y3 * (1.0 + m3[:, scale_idx:scale_idx + 1, :]) + m3[:, shift_idx:shift_idx + 1, :]
    return u3.reshape(r, d)


def _gate_residual(h, m3, gate_idx, out):
    r, d = h.shape
    nb = m3.shape[0]
    return (h.reshape(nb, r // nb, d) + m3[:, gate_idx:gate_idx + 1, :] * out.reshape(nb, r // nb, d)).reshape(r, d)


def _ada_kernel(c_ref, w_ref, b_ref, o_ref):
    o_ref[0] = _dot(c_ref[...], w_ref[0], precision=HIGHEST) + b_ref[0]


def _ada_mods(c_all, ada_w, ada_b):
    depth, d, n6 = ada_w.shape
    nb = c_all.shape[0]
    tn = 1024
    return pl.pallas_call(
        _ada_kernel,
        out_shape=jax.ShapeDtypeStruct((depth, nb, n6), F32),
        grid=(depth, n6 // tn),
        in_specs=[pl.BlockSpec((nb, d), lambda l, n: (0, 0)),
                  pl.BlockSpec((1, d, tn), lambda l, n: (l, 0, n)),
                  pl.BlockSpec((1, 1, tn), lambda l, n: (l, 0, n))],
        out_specs=pl.BlockSpec((1, nb, tn), lambda l, n: (l, 0, n)),
        compiler_params=_cparams(("arbitrary", "arbitrary")),
        name="ada_mods",
    )(c_all, ada_w, ada_b.reshape(depth, 1, n6))


def _even_pre_kernel(xp_ref, xs_ref, mp_ref, ms_ref, nw_ref, w_ref, hb_ref, invf_ref,
                     q_ref, kp_ref, ks_ref, vp_ref, vs_ref, hg_ref, *, npt, tiles_per_seq, past_len, layer_slot):
    i = pl.program_id(0)
    aw = MOBA_HEADS * MOBA_HEAD_DIM
    hw = HG_HEADS * HG_DK

    def run(is_prompt):
        x = (xp_ref if is_prompt else xs_ref)[...]
        m3 = (mp_ref if is_prompt else ms_ref)[...]
        r = x.shape[0]
        nb = m3.shape[0]
        tl = r // nb
        ub = _norm_mod(x, nw_ref[...], m3, 1, 0).astype(BF16)

        pos0 = (i % tiles_per_seq) * r if is_prompt else past_len
        pos = (pos0 + lax.broadcasted_iota(I32, (nb, tl, LANES), 1)).astype(F32).reshape(r, LANES)
        ang = pos * invf_ref[...]
        reps = aw // LANES
        cos = jnp.concatenate([jnp.cos(ang)] * reps, axis=1)
        sin = jnp.concatenate([jnp.sin(ang)] * reps, axis=1)
        lane = lax.broadcasted_iota(I32, (r, aw), 1)
        half = MOBA_HEAD_DIM // 2
        first = (lane % MOBA_HEAD_DIM) < half
        sin = jnp.where(first, -sin, sin)

        def rope(t):
            rot = jnp.where(first, pltpu.roll(t, aw - half, 1), pltpu.roll(t, half, 1))
            return t * cos + rot * sin

        def seg(a, b):
            return _dot(ub, w_ref[:, a:b])

        q_ref[...] = rope(seg(0, aw))
        k = rope(seg(aw, 2 * aw))
        v = seg(2 * aw, 3 * aw)
        if is_prompt:
            kp_ref[0] = k.T
            vp_ref[0] = v.T
        else:
            ks_ref[...] = k
            vs_ref[...] = v
        o = 3 * aw
        qb = seg(o, o + hw)
        fb = seg(o + hw, o + 2 * hw)
        hb = hb_ref[...]
        e = jnp.exp(hb - jnp.max(hb, axis=0, keepdims=True))
        sm = e / jnp.sum(e, axis=0, keepdims=True)
        lb = jnp.sum(sm[0:layer_slot + 1, :], axis=0, keepdims=True)
        hg_ref[:, 0:hw] = qb * _sigmoid(qb)
        hg_ref[:, hw:2 * hw] = lb + (1.0 - lb) * _sigmoid(fb)
        hg_ref[:, 2 * hw:3 * hw] = seg(o + 2 * hw, o + 3 * hw)
        hg_ref[:, 3 * hw:4 * hw] = seg(o + 3 * hw, o + 4 * hw)

    _by_group(i, npt, run)


def _even_pre(xp, xs, mp, ms, nw, w_in, hb, invf, *, seq, past_len, layer_slot):
    tp, d = xp.shape
    ts = xs.shape[0]
    npt, nst = tp // ROW_TILE, ts // ROW_TILE
    tiles_per_seq = seq // ROW_TILE
    aw = MOBA_HEADS * MOBA_HEAD_DIM
    hw = HG_HEADS * HG_DK
    t = tp + ts
    const = lambda shape: pl.BlockSpec(shape, lambda i: (0,) * len(shape))
    kern = functools.partial(_even_pre_kernel, npt=npt, tiles_per_seq=tiles_per_seq, past_len=past_len,
                             layer_slot=layer_slot)
    _, ks_spec = _pair_specs(aw, npt)
    kp_spec = pl.BlockSpec((1, aw, ROW_TILE), lambda i: (jnp.minimum(i, npt - 1) // tiles_per_seq, 0,
                                                          jnp.minimum(i, npt - 1) % tiles_per_seq))
    return pl.pallas_call(
        kern,
        out_shape=(jax.ShapeDtypeStruct((t, aw), F32),
                   jax.ShapeDtypeStruct((tp // seq, aw, seq), F32), jax.ShapeDtypeStruct((ts, aw), F32),
                   jax.ShapeDtypeStruct((tp // seq, aw, seq), F32), jax.ShapeDtypeStruct((ts, aw), F32),
                   jax.ShapeDtypeStruct((t, 4 * hw), F32)),
        grid=(npt + nst,),
        in_specs=[*_pair_specs(d, npt), *_mod_specs(d, npt, tiles_per_seq),
                  const((1, d)), const(w_in.shape), const(hb.shape), const((1, LANES))],
        out_specs=(pl.BlockSpec((ROW_TILE, aw), lambda i: (i, 0)), kp_spec, ks_spec, kp_spec, ks_spec,
                   pl.BlockSpec((ROW_TILE, 4 * hw), lambda i: (i, 0))),
        compiler_params=_cparams(("arbitrary",)),
        name="even_pre",
    )(xp, xs, mp, ms, nw, w_in, hb, invf)


def _top_blocks(g, valid):
    lane = lax.broadcasted_iota(I32, g.shape, 1)
    jl = lane % SUBLANES
    g = jnp.where(valid, g, -jnp.inf)
    cnt = jnp.zeros(g.shape, I32)
    for d in range(1, SUBLANES):
        wrapped = (jl + d) >= SUBLANES
        pg = jnp.where(wrapped, pltpu.roll(g, SUBLANES - d, 1), pltpu.roll(g, LANES - d, 1))
        beats = (pg > g) | ((pg == g) & wrapped)
        cnt = cnt + beats.astype(I32)
    return valid & (cnt < MOBA_TOPK)


def _head_expand(rows8, n_heads, width):
    x = jnp.concatenate([rows8] * n_heads, axis=0)
    r = lax.broadcasted_iota(I32, x.shape, 0) // SUBLANES
    l = lax.broadcasted_iota(I32, x.shape, 1) // (width // n_heads)
    return jnp.where(r == l, x, 0.0)


def _block_indicator(seq):
    key_blk = np.arange(seq)[None, :] // MOBA_BLOCK
    row = np.arange(LANES)[:, None]
    ind = (row < MOBA_HEADS * SUBLANES) & ((row % SUBLANES) == key_blk)
    return jnp.asarray(ind.astype(np.float32), dtype=BF16)


def _attn_prompt_kernel(q_ref, k_ref, v_ref, ind_ref, o_ref, km_ref, *, n_blk):
    i = pl.program_id(1)
    blk = MOBA_BLOCK
    aw = MOBA_HEADS * MOBA_HEAD_DIM
    scale = MOBA_HEAD_DIM ** -0.5

    @pl.when(i == 0)
    def _():
        r = lax.broadcasted_iota(I32, (aw, LANES), 0) // MOBA_HEAD_DIM
        c = lax.broadcasted_iota(I32, (aw, LANES), 1)
        km = jnp.zeros((aw, LANES), F32)
        for j in range(n_blk):
            mean_j = jnp.mean(k_ref[0, :, j * blk:(j + 1) * blk], axis=1, keepdims=True)
            km = jnp.where(((c % SUBLANES) == j) & ((c // SUBLANES) == r), mean_j, km)
        km_ref[...] = km

    q = q_ref[...]
    gate = _dot(q, km_ref[...], precision=HIGHEST)
    lane = lax.broadcasted_iota(I32, gate.shape, 1)
    jl = lane % SUBLANES
    valid = (jl < i) & (lane < MOBA_HEADS * SUBLANES)
    keep = _top_blocks(gate, valid) | (jl == i)
    bias = jnp.where(keep, 0.0, NEG)

    row = lax.broadcasted_iota(I32, (blk, blk), 0)
    col = lax.broadcasted_iota(I32, (blk, blk), 1)
    tril = row >= col

    def tile(c):
        n_keys = (c + 1) * blk
        for hp in range(aw // LANES):
            cols = slice(hp * LANES, (hp + 1) * LANES)
            rhs = jnp.concatenate([k_ref[0, cols, 0:n_keys].astype(BF16), ind_ref[:, 0:n_keys]], axis=0)
            vv = v_ref[0, cols, 0:n_keys].astype(BF16)
            qp = q[:, cols] * scale
            outs = []
            for s in (0, 1):
                h = 2 * hp + s
                qm = jnp.where((lane // MOBA_HEAD_DIM) == s, qp, 0.0).astype(BF16)
                hb = jnp.where((lane // SUBLANES) == h, bias, 0.0).astype(BF16)
                sc = _dot(jnp.concatenate([qm, hb], axis=1), rhs)
                own = jnp.where(tril, sc[:, c * blk:], NEG)
                sc = own if c == 0 else jnp.concatenate([sc[:, :c * blk], own], axis=1)
                p = jnp.exp(sc - jnp.max(sc, axis=1, keepdims=True))
                outs.append(_dot_nt(p.astype(BF16), vv) / jnp.sum(p, axis=1, keepdims=True))
            o_ref[:, cols] = jnp.where((lane // MOBA_HEAD_DIM) == 0, outs[0], outs[1]).astype(BF16)

    for c in range(n_blk):
        pl.when(i == c)(functools.partial(tile, c))


def _attn_prompt(q_all, k_p, v_p, *, batch, seq):
    aw = MOBA_HEADS * MOBA_HEAD_DIM
    n_blk = seq // MOBA_BLOCK
    assert n_blk <= SUBLANES
    ind = _block_indicator(seq)
    kern = functools.partial(_attn_prompt_kernel, n_blk=n_blk)
    return pl.pallas_call(
        kern,
        out_shape=jax.ShapeDtypeStruct((batch * seq, aw), BF16),
        grid=(batch, n_blk),
        in_specs=[pl.BlockSpec((MOBA_BLOCK, aw), lambda b, i: (b * n_blk + i, 0)),
                  pl.BlockSpec((1, aw, seq), lambda b, i: (b, 0, 0)),
                  pl.BlockSpec((1, aw, seq), lambda b, i: (b, 0, 0)),
                  pl.BlockSpec((LANES, seq), lambda b, i: (0, 0))],
        out_specs=pl.BlockSpec((MOBA_BLOCK, aw), lambda b, i: (b * n_blk + i, 0)),
        scratch_shapes=[pltpu.VMEM((aw, LANES), F32)],
        compiler_params=_cparams(("arbitrary", "arbitrary")),
        name="moba_prompt",
    )(q_all, k_p, v_p, ind)


def _attn_sample_kernel(pt_ref, q_ref, kn_ref, vn_ref, ck_ref, cv_ref, o_ref, kbuf, vbuf, sem,
                        *, n_pages, page, n_blk, layer):
    b = pl.program_id(0)
    nb = pl.num_programs(0)
    blk = MOBA_BLOCK
    hd = MOBA_HEAD_DIM
    dec = q_ref.shape[0]
    scale = hd ** -0.5

    def page_copies(seq_idx, slot):
        cps = []
        for p in range(n_pages):
            pg = pt_ref[seq_idx * n_pages + p]
            cps.append(pltpu.make_async_copy(ck_ref.at[layer, pg], kbuf.at[slot, :, :, pl.ds(p * page, page)],
                                             sem.at[0, slot]))
            cps.append(pltpu.make_async_copy(cv_ref.at[layer, pg], vbuf.at[slot, :, :, pl.ds(p * page, page)],
                                             sem.at[1, slot]))
        return cps

    @pl.when(b == 0)
    def _():
        for c in page_copies(0, 0):
            c.start()

    @pl.when(b + 1 < nb)
    def _():
        for c in page_copies(b + 1, (b + 1) % 2):
            c.start()

    slot = b % 2
    for c in page_copies(b, slot):
        c.wait()

    nh = MOBA_HEADS
    aw = nh * hd
    past = n_blk * blk
    kt = kbuf[slot].reshape(aw, past)
    vt = vbuf[slot].reshape(aw, past)
    qexp = _head_expand(q_ref[...], nh, aw)

    colj = lax.broadcasted_iota(I32, (aw, LANES), 1)
    km = jnp.zeros((aw, LANES), F32)
    for j in range(n_blk):
        km = jnp.where(colj == j, jnp.mean(kt[:, j * blk:(j + 1) * blk], axis=1, keepdims=True), km)
    gate = _dot(qexp, km, precision=HIGHEST)
    lane = lax.broadcasted_iota(I32, gate.shape, 1)
    sel = _top_blocks(gate, lane < n_blk).astype(F32)

    qs = (qexp * scale).astype(BF16)
    sc = _dot(qs, kt.astype(BF16))
    key_blk = lax.broadcasted_iota(I32, sc.shape, 1) // blk
    mask = jnp.zeros(sc.shape, F32)
    for j in range(n_blk):
        mask = jnp.where(key_blk == j, sel[:, j:j + 1], mask)
    sc = jnp.where(mask > 0.5, sc, NEG)

    pad = jnp.zeros((LANES - dec, aw), F32)
    kn = jnp.concatenate([kn_ref[...], pad], axis=0).astype(BF16)
    vn = jnp.concatenate([vn_ref[...], pad], axis=0).astype(BF16)
    qi = lax.broadcasted_iota(I32, gate.shape, 0) % dec
    sn = jnp.where(lane <= qi, _dot_nt(qs, kn), NEG)

    m = jnp.maximum(jnp.max(sc, axis=1, keepdims=True), jnp.max(sn, axis=1, keepdims=True))
    p = jnp.exp(sc - m)
    pn = jnp.exp(sn - m)
    l = jnp.sum(p, axis=1, keepdims=True) + jnp.sum(pn, axis=1, keepdims=True)
    o = (_dot_nt(p.astype(BF16), vt.astype(BF16)) + _dot(pn.astype(BF16), vn)) / l
    r = lax.broadcasted_iota(I32, o.shape, 0) // dec
    c = lax.broadcasted_iota(I32, o.shape, 1) // hd
    o = jnp.where(r == c, o, 0.0)
    acc = o[0:dec]
    for h in range(1, nh):
        acc = acc + o[h * dec:(h + 1) * dec]
    o_ref[...] = acc


def _attn_sample(page_table, q_all, k_s, v_s, cache_k, cache_v, *, layer, dec_batch, dec_seq, q_row0):
    _, n_phys, page, n_heads, hd = cache_k.shape
    aw = MOBA_HEADS * MOBA_HEAD_DIM
    n_pages = page_table.shape[1]
    past_len = n_pages * page
    assert past_len % MOBA_BLOCK == 0 and dec_seq == SUBLANES and past_len // MOBA_BLOCK == SUBLANES
    assert n_heads == MOBA_HEADS and hd == MOBA_HEAD_DIM
    kern = functools.partial(_attn_sample_kernel, n_pages=n_pages, page=page, n_blk=past_len // MOBA_BLOCK,
                             layer=layer)
    q_blk0 = q_row0 // dec_seq
    grid_spec = pltpu.PrefetchScalarGridSpec(
        num_scalar_prefetch=1,
        grid=(dec_batch,),
        in_specs=[pl.BlockSpec((dec_seq, aw), lambda b, pt: (q_blk0 + b, 0)),
                  pl.BlockSpec((dec_seq, aw), lambda b, pt: (b, 0)),
                  pl.BlockSpec((dec_seq, aw), lambda b, pt: (b, 0)),
                  pl.BlockSpec(memory_space=pl.ANY),
                  pl.BlockSpec(memory_space=pl.ANY)],
        out_specs=pl.BlockSpec((dec_seq, aw), lambda b, pt: (b, 0)),
        scratch_shapes=[pltpu.VMEM((2, n_heads, hd, past_len), F32), pltpu.VMEM((2, n_heads, hd, past_len), F32),
                        pltpu.SemaphoreType.DMA((2, 2))],
    )
    to_pos_minor = lambda c: jnp.transpose(c, (0, 1, 3, 4, 2))
    return pl.pallas_call(
        kern,
        out_shape=jax.ShapeDtypeStruct((dec_batch * dec_seq, aw), F32),
        grid_spec=grid_spec,
        compiler_params=_cparams(("arbitrary",)),
        name="moba_sample",
    )(page_table.reshape(-1), q_all, k_s, v_s, to_pos_minor(cache_k), to_pos_minor(cache_v))


def _hgrn_levels():
    n = HG_CHUNK
    return int(np.log2(n))


def _hgrn_cumsum_matrix():
    return jnp.asarray(np.tril(np.ones((HG_CHUNK, HG_CHUNK), np.float32)), dtype=BF16)


def _separator_rows(b, level):
    c, dk = b.shape
    m = 2 ** level
    if 2 * m >= SUBLANES:
        b3 = b.reshape(c // (2 * m), 2 * m, dk)
        return jnp.broadcast_to(b3[:, m - 1:m, :], b3.shape).reshape(c, dk)
    b3 = b.reshape(c // SUBLANES, SUBLANES, dk)
    sub = lax.broadcasted_iota(I32, b3.shape, 1)
    if m == 1:
        out = jnp.where(sub % 2 == 1, pltpu.roll(b3, 1, 1), b3)
    else:
        out = jnp.where(sub < 4, b3[:, 1:2, :], b3[:, 5:6, :])
    return out.reshape(c, dk)


def _hgrn_post(o, g, nw):
    o = o * lax.rsqrt(jnp.mean(o * o, axis=-1, keepdims=True) + NORM_EPS) * nw
    return o * (g * _sigmoid(g))


def _hgrn_prompt_kernel(q_ref, f_ref, i_ref, g_ref, ms_ref, nw_ref, o_ref, s_ref):
    c = HG_CHUNK
    dk = HG_DK
    n_chunks = q_ref.shape[0] // c
    levels = _hgrn_levels()
    row = lax.broadcasted_iota(I32, (c, c), 0)
    col = lax.broadcasted_iota(I32, (c, c), 1)
    rowk = lax.broadcasted_iota(I32, (c, dk), 0)
    upper = [((rowk // (2 ** l)) % 2) == 1 for l in range(levels)]
    same = [(row // (2 ** (l + 1))) == (col // (2 ** (l + 1))) for l in range(levels)]
    diag = row == col
    nw = nw_ref[...]

    heads = q_ref.shape[1] // dk

    def chunk(ci, sts):
        r0 = pl.multiple_of(ci * c, c)
        return tuple(one_head(r0, slice(hh * dk, (hh + 1) * dk), sts[hh]) for hh in range(heads))

    def one_head(r0, cols, st):
        q = q_ref[pl.ds(r0, c), cols]
        f = f_ref[pl.ds(r0, c), cols]
        v = i_ref[pl.ds(r0, c), cols]
        g = g_ref[pl.ds(r0, c), cols]
        lf = jnp.log(f)
        k = 1.0 - f
        hi = lf.astype(BF16)
        r1 = lf - hi.astype(F32)
        mid = r1.astype(BF16)
        lo = (r1 - mid.astype(F32)).astype(BF16)
        bb = _dot(ms_ref[...], jnp.concatenate([hi, mid, lo], axis=1))
        b = bb[:, 0:dk] + bb[:, dk:2 * dk] + bb[:, 2 * dk:3 * dk]
        a = jnp.where(diag, jnp.sum(q * k, axis=1, keepdims=True), 0.0)
        for l in range(levels):
            bs = _separator_rows(b, l)
            e = jnp.exp(jnp.where(upper[l], b - bs, bs - b))
            qp = jnp.where(upper[l], q * e, 0.0).astype(BF16)
            kp = jnp.where(upper[l], 0.0, k * e).astype(BF16)
            a = a + jnp.where(same[l], _dot_nt(qp, kp), 0.0)
        vb = v.astype(BF16)
        o = _dot_nt((q * jnp.exp(b)).astype(BF16), st.astype(BF16)) + _dot(a.astype(BF16), vb)
        blast = b[c - 1:c, :]
        kl = (k * jnp.exp(blast - b)).astype(BF16)
        st_new = st * jnp.exp(blast) + _dot(v.T.astype(BF16), kl)
        o_ref[pl.ds(r0, c), cols] = _hgrn_post(o, g, nw).astype(BF16)
        return st_new

    sts = lax.fori_loop(0, n_chunks, chunk, tuple(jnp.zeros((dk, dk), F32) for _ in range(heads)))
    for hh in range(heads):
        s_ref[hh] = sts[hh].T


def _hgrn_prompt(hg, nw, *, batch, seq):
    hw = HG_HEADS * HG_DK
    mstack = _hgrn_cumsum_matrix()
    hps = HG_HEADS
    groups = HG_HEADS // hps
    spec = lambda off: pl.BlockSpec((seq, hps * HG_DK), lambda b, h: (b, off + h))
    return pl.pallas_call(
        _hgrn_prompt_kernel,
        out_shape=(jax.ShapeDtypeStruct((batch * seq, hw), BF16),
                   jax.ShapeDtypeStruct((batch * HG_HEADS, HG_DK, HG_DK), F32)),
        grid=(batch, groups),
        in_specs=[spec(0), spec(groups), spec(2 * groups), spec(3 * groups),
                  pl.BlockSpec(mstack.shape, lambda b, h: (0, 0)),
                  pl.BlockSpec((1, HG_DK), lambda b, h: (0, 0))],
        out_specs=(pl.BlockSpec((seq, hps * HG_DK), lambda b, h: (b, h)),
                   pl.BlockSpec((hps, HG_DK, HG_DK), lambda b, h: (b * groups + h, 0, 0))),
        compiler_params=_cparams(("arbitrary", "arbitrary")),
        name="hgrn_prompt",
    )(hg, hg, hg, hg, mstack, nw)


def _hgrn_sample_kernel(hg_ref, s0_ref, nw_ref, o_ref, s_ref, *, dec, seqs):
    dk = HG_DK
    hw = HG_HEADS * HG_DK
    nw = nw_ref[...]
    trow = lax.broadcasted_iota(I32, (dec, dk), 0)
    srow = lax.broadcasted_iota(I32, (dk, dk), 0)
    zrows = jnp.zeros((dec, dk), F32)
    pad16 = lambda a: jnp.concatenate([a, zrows], axis=0).astype(BF16)
    dot_tn = lambda a, b: lax.dot_general(a, b, (((0,), (0,)), ((), ())), preferred_element_type=F32)

    def one_seq(si, carry):
        r0 = pl.multiple_of(si * dec, dec)
        for h in range(HG_HEADS):
            cols = slice(h * dk, (h + 1) * dk)
            q = hg_ref[pl.ds(r0, dec), h * dk:(h + 1) * dk]
            f = hg_ref[pl.ds(r0, dec), hw + h * dk:hw + (h + 1) * dk]
            v = hg_ref[pl.ds(r0, dec), 2 * hw + h * dk:2 * hw + (h + 1) * dk]
            g = hg_ref[pl.ds(r0, dec), 3 * hw + h * dk:3 * hw + (h + 1) * dk]
            k = 1.0 - f
            b = jnp.log(f)
            for dlt in (1, 2, 4):
                b = b + jnp.where(trow >= dlt, pltpu.roll(b, dlt, 0), 0.0)
            s0 = s0_ref[si * HG_HEADS + h]
            o = _dot(pad16(q * jnp.exp(b)), s0.astype(BF16))[0:dec]
            for s in range(dec):
                live = trow >= s
                e = jnp.exp(jnp.where(live, b - b[s:s + 1, :], 0.0))
                a_s = jnp.sum(jnp.where(live, q * e * k[s:s + 1, :], 0.0), axis=1, keepdims=True)
                o = o + a_s * v[s:s + 1, :]
            o_ref[pl.ds(r0, dec), cols] = _hgrn_post(o, g, nw)
            blast = b[dec - 1:dec, :]
            upd = dot_tn(pad16(k * jnp.exp(blast - b)), pad16(v))
            ecol = jnp.where(srow == 0, jnp.exp(blast), 0.0).T[:, 0:1]
            s_ref[si * HG_HEADS + h] = ecol * s0 + upd
        return carry

    lax.fori_loop(0, seqs, one_seq, 0)


def _hgrn_sample(hg, s0, nw, *, dec_batch, dec_seq, row0):
    hw = HG_HEADS * HG_DK
    seqs = 8
    rows = seqs * dec_seq
    kern = functools.partial(_hgrn_sample_kernel, dec=dec_seq, seqs=seqs)
    st_spec = pl.BlockSpec((seqs * HG_HEADS, HG_DK, HG_DK), lambda i: (i, 0, 0))
    return pl.pallas_call(
        kern,
        out_shape=(jax.ShapeDtypeStruct((dec_batch * dec_seq, hw), F32),
                   jax.ShapeDtypeStruct((dec_batch * HG_HEADS, HG_DK, HG_DK), F32)),
        grid=(dec_batch // seqs,),
        in_specs=[pl.BlockSpec((rows, 4 * hw), lambda i: (row0 // rows + i, 0)), st_spec,
                  pl.BlockSpec((1, HG_DK), lambda i: (0, 0))],
        out_specs=(pl.BlockSpec((rows, hw), lambda i: (i, 0)), st_spec),
        compiler_params=_cparams(("arbitrary",)),
        name="hgrn_sample",
    )(hg, s0, nw)


def _post_mixer_kernel(*refs, n_a, npt):
    hp_ref, hs_ref, mp_ref, ms_ref = refs[0:4]
    a_refs = refs[4:4 + 2 * n_a]
    w_refs = refs[4 + 2 * n_a:4 + 3 * n_a]
    nffn_ref, rw_ref, rb_ref, striu_ref = refs[4 + 3 * n_a:8 + 3 * n_a]
    hmid_ref, u_ref, rinfo_ref, counts_ref, carry = refs[8 + 3 * n_a:]
    i = pl.program_id(0)

    @pl.when(i == 0)
    def _():
        carry[...] = jnp.zeros_like(carry)

    def run(is_prompt):
        h = (hp_ref if is_prompt else hs_ref)[...]
        m3 = (mp_ref if is_prompt else ms_ref)[...]
        out = None
        for a in range(n_a):
            av = a_refs[2 * a + (0 if is_prompt else 1)][...].astype(BF16)
            t = _dot(av, w_refs[a][...])
            out = t if out is None else out + t
        hm = _gate_residual(h, m3, 2, out)
        hmid_ref[...] = hm
        u = _norm_mod(hm, nffn_ref[...], m3, 4, 3)
        for k in range(u.shape[1] // LANES):
            u_ref[:, k, :] = u[:, k * LANES:(k + 1) * LANES]

        u_hi = u.astype(BF16)
        u_lo = (u - u_hi.astype(F32)).astype(BF16)
        hh = _dot(u_hi, rw_ref[...])
        logits = hh[:, 0:LANES] + hh[:, LANES:2 * LANES] + _dot(u_lo, rw_ref[:, 0:LANES]) + rb_ref[...]
        lt = logits.T
        col = lambda j: lt[j:j + 1, :]
        gl = [col(j) for j in range(N_GROUPS)]
        gmax = functools.reduce(jnp.maximum, gl)
        gsum = functools.reduce(lambda x, y: x + y, [jnp.exp(x - gmax) for x in gl])
        gval = 1.0 / gsum
        gidx = jnp.where(gl[0] == gmax, 0, jnp.where(gl[1] == gmax, 1, jnp.where(gl[2] == gmax, 2, 3)))
        el = []
        for k in range(EXPERTS_PER_GROUP):
            c = [col(N_GROUPS + g * EXPERTS_PER_GROUP + k) for g in range(N_GROUPS)]
            el.append(jnp.where(gidx == 0, c[0], jnp.where(gidx == 1, c[1], jnp.where(gidx == 2, c[2], c[3]))))
        emax = functools.reduce(jnp.maximum, el)
        pe = [jnp.exp(x - emax) for x in el]
        esum = functools.reduce(lambda x, y: x + y, pe)
        pk = [x / esum for x in pe]
        v1 = functools.reduce(jnp.maximum, pk)
        i1 = jnp.where(pk[0] == v1, 0, jnp.where(pk[1] == v1, 1, jnp.where(pk[2] == v1, 2, 3)))
        pk2 = [jnp.where(i1 == k, -1.0, pk[k]) for k in range(EXPERTS_PER_GROUP)]
        v2 = functools.reduce(jnp.maximum, pk2)
        i2 = jnp.where(pk2[0] == v2, 0, jnp.where(pk2[1] == v2, 1, jnp.where(pk2[2] == v2, 2, 3)))
        den = v1 + v2
        w1 = gval * v1 / den
        w2 = gval * v2 / den
        lo = jnp.minimum(i1, i2)
        hi = jnp.maximum(i1, i2)
        pair = jnp.where(lo == 0, hi - 1, jnp.where(lo == 1, hi + 1, 5))
        bucket = gidx * 6 + pair
        w_lo = jnp.where(i1 < i2, w1, w2)
        w_hi = jnp.where(i1 < i2, w2, w1)

        r = h.shape[0]
        sub = lax.broadcasted_iota(I32, (LANES, r), 0)
        onehot = sub == bucket
        before = _dot(onehot.astype(BF16), striu_ref[...]) + carry[:, 0:1]
        rank = jnp.sum(jnp.where(onehot, before, 0.0), axis=0, keepdims=True)
        carry[...] = carry[...] + jnp.sum(onehot.astype(F32), axis=1, keepdims=True)
        info = jnp.where(sub == 0, bucket.astype(F32),
                         jnp.where(sub == 1, w_lo, jnp.where(sub == 2, w_hi, jnp.where(sub == 3, rank, 0.0))))
        rinfo_ref[...] = info.T
        counts_ref[...] = carry[...]

    _by_group(i, npt, run)


def _post_mixer(h_pair, h_off_s, mp, ms, a_pairs, ws, nffn, rw, rb, *, seq):
    hp, hs = h_pair
    d = hp.shape[1]
    tp = a_pairs[0][0].shape[0]
    ts = a_pairs[0][1].shape[0]
    npt, nst = tp // ROW_TILE, ts // ROW_TILE
    t = tp + ts
    n_a = len(a_pairs)
    tiles_per_seq = seq // ROW_TILE
    const = lambda shape: pl.BlockSpec(shape, lambda i: (0,) * len(shape))
    stril = jnp.asarray(np.triu(np.ones((ROW_TILE, ROW_TILE), np.float32), 1), dtype=BF16)
    in_specs = [*_pair_specs(d, npt, off_s=h_off_s), *_mod_specs(d, npt, tiles_per_seq)]
    args = [hp, hs, mp, ms]
    for ap, as_ in a_pairs:
        in_specs += list(_pair_specs(ap.shape[1], npt))
        args += [ap, as_]
    for w in ws:
        in_specs.append(const(w.shape))
        args.append(w)
    in_specs += [const((1, d)), const(rw.shape), const((1, LANES)), const(stril.shape)]
    args += [nffn, rw, rb, stril]
    tile = lambda c: pl.BlockSpec((ROW_TILE, c), lambda i: (i, 0))
    kern = functools.partial(_post_mixer_kernel, n_a=n_a, npt=npt)
    return pl.pallas_call(
        kern,
        out_shape=(jax.ShapeDtypeStruct((t, d), F32), jax.ShapeDtypeStruct((t, d // LANES, LANES), F32),
                   jax.ShapeDtypeStruct((t, LANES), F32), jax.ShapeDtypeStruct((LANES, LANES), F32)),
        grid=(npt + nst,),
        in_specs=in_specs,
        out_specs=(tile(d), pl.BlockSpec((ROW_TILE, d // LANES, LANES), lambda i: (i, 0, 0)), tile(LANES),
                   const((LANES, LANES))),
        scratch_shapes=[pltpu.VMEM((LANES, LANES), F32)],
        compiler_params=_cparams(("arbitrary",)),
        name="post_mixer",
    )(*args)


def _for_rows(rows, fn):
    def body(r, c):
        fn(r)
        return c

    lax.fori_loop(0, rows, body, 0, unroll=8)


def _slot_of(bucket_ref, rank_ref, starts_ref, t):
    return starts_ref[bucket_ref[t]] + rank_ref[t]


def _scatter_kernel(bucket_ref, rank_ref, starts_ref, src_ref, init_ref, o_ref, sem):
    del init_ref
    rows = src_ref.shape[0]
    base = pl.program_id(0) * rows
    copy = lambda r: pltpu.make_async_copy(
        src_ref.at[r], o_ref.at[_slot_of(bucket_ref, rank_ref, starts_ref, base + r)], sem)
    _for_rows(rows, lambda r: copy(r).start())
    _for_rows(rows, lambda r: copy(r).wait())


def _scatter_tokens(slots, src, n_slots):
    t = src.shape[0]
    rows = ROW_TILE
    grid_spec = pltpu.PrefetchScalarGridSpec(
        num_scalar_prefetch=3,
        grid=(t // rows,),
        in_specs=[pl.BlockSpec((rows,) + src.shape[1:], lambda i, *_: (i, 0, 0)), pl.BlockSpec(memory_space=pl.ANY)],
        out_specs=pl.BlockSpec(memory_space=pl.ANY),
        scratch_shapes=[pltpu.SemaphoreType.DMA(())],
    )
    return pl.pallas_call(
        _scatter_kernel,
        out_shape=jax.ShapeDtypeStruct((n_slots,) + src.shape[1:], src.dtype),
        grid_spec=grid_spec,
        input_output_aliases={4: 0},
        compiler_params=_cparams(("arbitrary",)),
        name="moe_scatter",
    )(*slots, src, jnp.zeros((n_slots,) + src.shape[1:], src.dtype))


def _moe_kernel(ea_ref, eb_ref, valid_ref, fresh_ref, x_ref,
                w1a_ref, w3a_ref, w2a_ref, w1b_ref, w3b_ref, w2b_ref, o_ref, wbuf1, wbuf3, wbuf2):
    i = pl.program_id(0)
    nk = x_ref.shape[1]

    @pl.when(fresh_ref[i] == 1)
    def _():
        wbuf1[0] = w1a_ref[0, 0].astype(BF16)
        wbuf3[0] = w3a_ref[0, 0].astype(BF16)
        wbuf2[0] = w2a_ref[0, 0].astype(BF16)
        wbuf1[1] = w1b_ref[0, 0].astype(BF16)
        wbuf3[1] = w3b_ref[0, 0].astype(BF16)
        wbuf2[1] = w2b_ref[0, 0].astype(BF16)

    @pl.when(valid_ref[i] == 1)
    def _():
        xb = jnp.concatenate([x_ref[:, k, :] for k in range(nk)], axis=1).astype(BF16)
        for s in (0, 1):
            h1 = _dot(xb, wbuf1[s])
            h3 = _dot(xb, wbuf3[s])
            y = _dot(((h1 * _sigmoid(h1)) * h3).astype(BF16), wbuf2[s])
            for k in range(nk):
                o_ref[:, s * nk + k, :] = y[:, k * LANES:(k + 1) * LANES]

    @pl.when(valid_ref[i] == 0)
    def _():
        o_ref[...] = jnp.zeros_like(o_ref)


def _moe_experts(tile_ea, tile_eb, tile_valid, tile_fresh, xs, w1, w3, w2, layer):
    n_slots, nk, _ = xs.shape
    d = nk * LANES
    ff = w1.shape[-1]
    wa = lambda shape: pl.BlockSpec((1, 1) + shape, lambda i, ea, eb, va, fr: (layer, ea[i], 0, 0))
    wb = lambda shape: pl.BlockSpec((1, 1) + shape, lambda i, ea, eb, va, fr: (layer, eb[i], 0, 0))
    rows = lambda k: pl.BlockSpec((MOE_TILE, k, LANES), lambda i, ea, eb, va, fr: (i, 0, 0))
    grid_spec = pltpu.PrefetchScalarGridSpec(
        num_scalar_prefetch=4,
        grid=(n_slots // MOE_TILE,),
        in_specs=[rows(nk), wa((d, ff)), wa((d, ff)), wa((ff, d)), wb((d, ff)), wb((d, ff)), wb((ff, d))],
        out_specs=rows(2 * nk),
        scratch_shapes=[pltpu.VMEM((2, d, ff), BF16), pltpu.VMEM((2, d, ff), BF16), pltpu.VMEM((2, ff, d), BF16)],
    )
    return pl.pallas_call(
        _moe_kernel,
        out_shape=jax.ShapeDtypeStruct((n_slots, 2 * nk, LANES), F32),
        grid_spec=grid_spec,
        compiler_params=_cparams(("arbitrary",)),
        name="moe_experts",
    )(tile_ea, tile_eb, tile_valid, tile_fresh, xs, w1, w3, w2, w1, w3, w2)


def _moe_plan(rinfo, counts, n_slots):
    n_tiles = n_slots // MOE_TILE
    cnt = counts[:N_BUCKETS, 0].astype(I32)
    padded = ((cnt + MOE_TILE - 1) // MOE_TILE) * MOE_TILE
    ends = jnp.cumsum(padded)
    starts = ends - padded
    slots = (rinfo[:, 0].astype(I32), rinfo[:, 3].astype(I32), starts)
    n_valid = ends[-1] // MOE_TILE
    tiles = jnp.arange(n_tiles, dtype=I32)
    tb = jnp.sum((ends[None, :] <= (tiles * MOE_TILE)[:, None]).astype(I32), axis=1)
    tb = jnp.minimum(tb, N_BUCKETS - 1)
    valid = tiles < n_valid
    tb = jnp.where(valid, tb, tb[jnp.maximum(n_valid - 1, 0)])
    lo = jnp.asarray(PAIR_LO, I32)
    hi = jnp.asarray(PAIR_HI, I32)
    ea = (tb // 6) * EXPERTS_PER_GROUP + lo[tb % 6]
    eb = (tb // 6) * EXPERTS_PER_GROUP + hi[tb % 6]
    fresh = jnp.concatenate([jnp.ones((1,), I32), (tb[1:] != tb[:-1]).astype(I32)])
    return slots, ea, eb, valid.astype(I32), fresh


def _moe(u_tiles, rinfo, counts, w1, w3, w2, layer):
    t = u_tiles.shape[0]
    n_slots = t + N_BUCKETS * MOE_TILE
    slots, ea, eb, valid, fresh = _moe_plan(rinfo, counts, n_slots)
    xs = _scatter_tokens(slots, u_tiles, n_slots)
    ys = _moe_experts(ea, eb, valid, fresh, xs, w1, w3, w2, layer)
    return slots, ys


def _combine_kernel(bucket_ref, rank_ref, starts_ref, ys_ref, hm_ref, rinfo_ref, mp_ref, ms_ref, nf_ref, *rest,
                    npt, final):
    if final:
        yp_ref, ysm_ref, gbuf, sem = rest
    else:
        h_ref, gbuf, sem = rest
    i = pl.program_id(0)
    n = pl.num_programs(0)
    rows = gbuf.shape[1]
    nk = gbuf.shape[2] // 2

    def copy(tile, r):
        slot = tile % 2
        src = ys_ref.at[_slot_of(bucket_ref, rank_ref, starts_ref, tile * rows + r)]
        return pltpu.make_async_copy(src, gbuf.at[slot, r], sem.at[slot])

    @pl.when(i == 0)
    def _():
        _for_rows(rows, lambda r: copy(0, r).start())

    @pl.when(i + 1 < n)
    def _():
        _for_rows(rows, lambda r: copy(i + 1, r).start())

    _for_rows(rows, lambda r: copy(i, r).wait())
    slot = i % 2
    ya = jnp.concatenate([gbuf[slot, :, k, :] for k in range(nk)], axis=1)
    yb = jnp.concatenate([gbuf[slot, :, nk + k, :] for k in range(nk)], axis=1)
    rinfo = rinfo_ref[...]
    moe = rinfo[:, 1:2] * ya + rinfo[:, 2:3] * yb

    def run(is_prompt):
        m3 = (mp_ref if is_prompt else ms_ref)[...]
        h = _gate_residual(hm_ref[...], m3, 5, moe)
        if final:
            y = h * lax.rsqrt(jnp.mean(h * h, axis=-1, keepdims=True) + NORM_EPS) * nf_ref[...]
            (yp_ref if is_prompt else ysm_ref)[...] = y
        else:
            h_ref[...] = h

    _by_group(i, npt, run)


def _combine(slots, ys, hmid, rinfo, mp, ms, nf, *, tp, seq, final):
    t, d = hmid.shape
    npt = tp // ROW_TILE
    tiles_per_seq = seq // ROW_TILE
    mpm = lambda f: (lambda i, *_: f(i))
    mp_spec, ms_spec = _mod_specs(d, npt, tiles_per_seq)
    yp_spec, ysm_spec = _pair_specs(d, npt)
    wrap = lambda s: pl.BlockSpec(s.block_shape, mpm(s.index_map))
    if final:
        out_shape = (jax.ShapeDtypeStruct((tp, d), F32), jax.ShapeDtypeStruct((t - tp, d), F32))
        out_specs = (wrap(yp_spec), wrap(ysm_spec))
    else:
        out_shape = jax.ShapeDtypeStruct((t, d), F32)
        out_specs = pl.BlockSpec((ROW_TILE, d), lambda i, *_: (i, 0))
    grid_spec = pltpu.PrefetchScalarGridSpec(
        num_scalar_prefetch=3,
        grid=(t // ROW_TILE,),
        in_specs=[pl.BlockSpec(memory_space=pl.ANY),
                  pl.BlockSpec((ROW_TILE, d), lambda i, *_: (i, 0)),
                  pl.BlockSpec((ROW_TILE, LANES), lambda i, *_: (i, 0)),
                  wrap(mp_spec), wrap(ms_spec),
                  pl.BlockSpec((1, d), lambda i, *_: (0, 0))],
        out_specs=out_specs,
        scratch_shapes=[pltpu.VMEM((2, ROW_TILE) + ys.shape[1:], F32), pltpu.SemaphoreType.DMA((2,))],
    )
    kern = functools.partial(_combine_kernel, npt=npt, final=final)
    return pl.pallas_call(
        kern, out_shape=out_shape, grid_spec=grid_spec,
        compiler_params=_cparams(("arbitrary",)),
        name="moe_combine_final" if final else "moe_combine",
    )(*slots, ys, hmid, rinfo, mp, ms, nf)


def _odd_pre_kernel(hp_ref, hs_ref, mp_ref, ms_ref, nw_ref, w_ref, gate_ref, rec_ref, *, npt):
    i = pl.program_id(0)
    width = gate_ref.shape[1]

    def run(is_prompt):
        h = (hp_ref if is_prompt else hs_ref)[...]
        m3 = (mp_ref if is_prompt else ms_ref)[...]
        ub = _norm_mod(h, nw_ref[...], m3, 1, 0).astype(BF16)
        x = _dot(ub, w_ref[:, 0:width])
        gate_ref[...] = 0.5 * x * (1.0 + jnp.tanh(np.sqrt(2.0 / np.pi) * (x + 0.044715 * (x * x * x))))
        rec_ref[...] = _dot(ub, w_ref[:, width:2 * width])

    _by_group(i, npt, run)


def _odd_pre(h, mp, ms, nw, w_in, *, tp, seq):
    t, d = h.shape
    npt = tp // ROW_TILE
    width = w_in.shape[1] // 2
    const = lambda shape: pl.BlockSpec(shape, lambda i: (0,) * len(shape))
    tile = lambda c: pl.BlockSpec((ROW_TILE, c), lambda i: (i, 0))
    kern = functools.partial(_odd_pre_kernel, npt=npt)
    return pl.pallas_call(
        kern,
        out_shape=(jax.ShapeDtypeStruct((t, width), F32), jax.ShapeDtypeStruct((t, width), F32)),
        grid=(t // ROW_TILE,),
        in_specs=[*_pair_specs(d, npt, off_s=npt), *_mod_specs(d, npt, seq // ROW_TILE),
                  const((1, d)), const(w_in.shape)],
        out_specs=(tile(width), tile(width)),
        compiler_params=_cparams(("arbitrary",)),
        name="odd_pre",
    )(h, h, mp, ms, nw, w_in)


def _lru_kernel(rec_ref, gate_ref, c0_ref, h0_ref, cw_ref, cb_ref, wa_ref, ba_ref, wx_ref, bx_ref, lam_ref,
                yg_ref, cs_ref, hl_ref, hist, hc, a_s, h_s, *, nb, tl):
    i = pl.program_id(1)
    r, width = rec_ref.shape
    taps = CONV_WIDTH
    blk = width // LRU_HEADS

    @pl.when(i == 0)
    def _():
        hist[:, SUBLANES - (taps - 1):SUBLANES, :] = c0_ref[...]
        hc[...] = h0_ref[...]

    rec3 = rec_ref[...].reshape(nb, tl, width)
    hist[:, SUBLANES:SUBLANES + tl, :] = rec3
    cw = cw_ref[...]
    conv = cb_ref[...] + rec3 * cw[taps - 1:taps, :]
    for back in range(1, taps):
        conv = conv + hist[:, SUBLANES - back:SUBLANES - back + tl, :] * cw[taps - 1 - back:taps - back, :]
    tail = hist[:, SUBLANES + tl - (taps - 1):SUBLANES + tl, :]
    cs_ref[...] = tail
    hist[:, SUBLANES - (taps - 1):SUBLANES, :] = tail

    cf = conv.reshape(r, width)
    cb16 = cf.astype(BF16)
    gr = jnp.concatenate([_dot(cb16[:, h * blk:(h + 1) * blk], wa_ref[h]) for h in range(LRU_HEADS)], axis=1)
    gi = jnp.concatenate([_dot(cb16[:, h * blk:(h + 1) * blk], wx_ref[h]) for h in range(LRU_HEADS)], axis=1)
    rg = _sigmoid(gr + ba_ref[...])
    ig = _sigmoid(gi + bx_ref[...])
    z = -lam_ref[...]
    softplus = jnp.maximum(z, 0.0) + jnp.log1p(jnp.exp(-jnp.abs(z)))
    log_a = -LRU_C * rg * softplus
    a = jnp.exp(log_a)
    gx = jnp.sqrt(1.0 - a * a) * (ig * cf)

    groups = r // SUBLANES
    sub = lax.broadcasted_iota(I32, (groups, SUBLANES, width), 1)
    aa, hh = a.reshape(groups, SUBLANES, width), gx.reshape(groups, SUBLANES, width)
    for dlt in (1, 2, 4):
        ok = sub >= dlt
        a_sh = pltpu.roll(aa, dlt, 1)
        h_sh = pltpu.roll(hh, dlt, 1)
        hh = jnp.where(ok, hh + aa * h_sh, hh)
        aa = jnp.where(ok, aa * a_sh, aa)
    aa, hh = aa.reshape(r, width), hh.reshape(r, width)

    if tl == SUBLANES:
        y3 = hh.reshape(nb, tl, width) + aa.reshape(nb, tl, width) * hc[...]
        hl_ref[...] = y3[:, tl - 1:tl, :]
        y = y3.reshape(r, width)
    else:
        a_s[...] = aa
        h_s[...] = hh

        def group(j, carry):
            r0 = pl.multiple_of(j * SUBLANES, SUBLANES)
            yj = h_s[pl.ds(r0, SUBLANES), :] + a_s[pl.ds(r0, SUBLANES), :] * carry
            h_s[pl.ds(r0, SUBLANES), :] = yj
            return yj[SUBLANES - 1:SUBLANES, :]

        last = lax.fori_loop(0, r // SUBLANES, group, hc[0])
        hc[0] = last
        hl_ref[0] = last
        y = h_s[...]
    yg_ref[...] = (y * gate_ref[...]).astype(BF16)


def _lru(rec, gate, conv0, h0, cw, cb, wa, ba, wx, bx, lam, *, batch, seq, row0):
    width = rec.shape[1]
    taps = CONV_WIDTH
    if seq == SUBLANES:
        nb, tl = ROW_TILE // SUBLANES, SUBLANES
    else:
        nb, tl = 1, ROW_TILE
    r = nb * tl
    n_l = seq // tl
    blk0 = row0 // r
    const = lambda shape: pl.BlockSpec(shape, lambda b, i: (0,) * len(shape))
    tile_in = pl.BlockSpec((r, width), lambda b, i: (blk0 + b * n_l + i, 0))
    kern = functools.partial(_lru_kernel, nb=nb, tl=tl)
    return pl.pallas_call(
        kern,
        out_shape=(jax.ShapeDtypeStruct((batch * seq, width), BF16),
                   jax.ShapeDtypeStruct((batch, taps - 1, width), F32),
                   jax.ShapeDtypeStruct((batch, 1, width), F32)),
        grid=(batch // nb, n_l),
        in_specs=[tile_in, tile_in,
                  pl.BlockSpec((nb, taps - 1, width), lambda b, i: (b, 0, 0)),
                  pl.BlockSpec((nb, 1, width), lambda b, i: (b, 0, 0)),
                  const((taps, width)), const((1, width)), const(wa.shape), const((1, width)),
                  const(wx.shape), const((1, width)), const((1, width))],
        out_specs=(pl.BlockSpec((r, width), lambda b, i: (b * n_l + i, 0)),
                   pl.BlockSpec((nb, taps - 1, width), lambda b, i: (b, 0, 0)),
                   pl.BlockSpec((nb, 1, width), lambda b, i: (b, 0, 0))),
        scratch_shapes=[pltpu.VMEM((nb, tl + SUBLANES, width), F32), pltpu.VMEM((nb, 1, width), F32),
                        pltpu.VMEM((r, width), F32), pltpu.VMEM((r, width), F32)],
        compiler_params=_cparams(("arbitrary", "arbitrary")),
        name="rg_lru_sample" if seq == SUBLANES else "rg_lru_prompt",
    )(rec, gate, conv0, h0, cw, cb, wa, ba, wx, bx, lam)


def kernel(x_prompt, x_sample, c_prompt, c_sample, cache_k, cache_v, state_hgrn, state_conv, state_lru, page_table,
           ada_w, ada_b, norm_mix, norm_ffn, norm_final, w_in_even, w_out_even, hg_lower_bounds, hg_norm_w, w_in_odd,
           conv_w, conv_b, lru_wa, lru_ba, lru_wx, lru_bx, lru_lambda, w_out_odd, router_group_w, router_group_b,
           router_expert_w, router_expert_b, moe_w1, moe_w3, moe_w2):
    batch, seq, d = x_prompt.shape
    dec_batch, dec_seq, _ = x_sample.shape
    depth = ada_w.shape[0]
    assert depth == 2 and seq % ROW_TILE == 0 and seq % MOBA_BLOCK == 0
    tp, ts = batch * seq, dec_batch * dec_seq
    n_pages, page = page_table.shape[1], cache_k.shape[2]
    past_len = n_pages * page
    aw = MOBA_HEADS * MOBA_HEAD_DIM
    hw = HG_HEADS * HG_DK

    xp = x_prompt.reshape(tp, d)
    xs = x_sample.reshape(ts, d)
    mods = _ada_mods(jnp.concatenate([c_prompt, c_sample], axis=0), ada_w, ada_b)
    mods = mods.reshape(depth, batch + dec_batch, 6, d)
    mods_p, mods_s = mods[:, :batch], mods[:, batch:]

    half = MOBA_HEAD_DIM // 2
    inv_freq = jnp.power(ROPE_THETA, -jnp.arange(half, dtype=F32) / half)
    invf = jnp.tile(inv_freq, LANES // half).reshape(1, LANES)

    def router_mats(l):
        rw = jnp.zeros((d, LANES), F32)
        rw = rw.at[:, 0:N_GROUPS].set(router_group_w[l])
        rw = rw.at[:, N_GROUPS:N_GROUPS + N_GROUPS * EXPERTS_PER_GROUP].set(router_expert_w[l])
        rb = jnp.zeros((1, LANES), F32)
        rb = rb.at[0, 0:N_GROUPS].set(router_group_b[l])
        rb = rb.at[0, N_GROUPS:N_GROUPS + N_GROUPS * EXPERTS_PER_GROUP].set(router_expert_b[l])
        rw_hi = rw.astype(BF16)
        rw_lo = (rw - rw_hi.astype(F32)).astype(BF16)
        return jnp.concatenate([rw_hi, rw_lo], axis=1), rb

    q_all, k_p, k_s, v_p, v_s, hg = _even_pre(
        xp, xs, mods_p[0], mods_s[0], norm_mix[0].reshape(1, d), w_in_even[0].astype(BF16), hg_lower_bounds, invf,
        seq=seq, past_len=past_len, layer_slot=0)
    oa_p = _attn_prompt(q_all, k_p, v_p, batch=batch, seq=seq)
    oa_s = _attn_sample(page_table, q_all, k_s, v_s, cache_k, cache_v,
                        layer=0, dec_batch=dec_batch, dec_seq=dec_seq, q_row0=tp)
    hnw = hg_norm_w[0].reshape(1, HG_DK)
    ob_p, hgrn_p = _hgrn_prompt(hg, hnw, batch=batch, seq=seq)
    ob_s, hgrn_s = _hgrn_sample(hg, state_hgrn[0].reshape(dec_batch * HG_HEADS, HG_DK, HG_DK), hnw,
                                dec_batch=dec_batch, dec_seq=dec_seq, row0=tp)
    w_out = w_out_even[0].astype(BF16)
    rw, rb = router_mats(0)
    hmid, u, rinfo, counts = _post_mixer((xp, xs), 0, mods_p[0], mods_s[0], [(oa_p, oa_s), (ob_p, ob_s)],
                                         [w_out[:aw], w_out[aw:]], norm_ffn[0].reshape(1, d), rw, rb, seq=seq)
    dest, ys = _moe(u, rinfo, counts, moe_w1, moe_w3, moe_w2, 0)
    h1 = _combine(dest, ys, hmid, rinfo, mods_p[0], mods_s[0], norm_final.reshape(1, d), tp=tp, seq=seq,
                  final=False)

    gate, rec = _odd_pre(h1, mods_p[1], mods_s[1], norm_mix[1].reshape(1, d), w_in_odd[0].astype(BF16), tp=tp, seq=seq)
    width = rec.shape[1]
    lru_args = (conv_w[0], conv_b[0].reshape(1, width), lru_wa[0].astype(BF16), lru_ba[0].reshape(1, width),
                lru_wx[0].astype(BF16), lru_bx[0].reshape(1, width), lru_lambda[0].reshape(1, width))
    yg_p, conv_p, lru_p = _lru(rec, gate, jnp.zeros((batch, CONV_WIDTH - 1, width), F32),
                               jnp.zeros((batch, 1, width), F32), *lru_args, batch=batch, seq=seq, row0=0)
    yg_s, conv_s, lru_s = _lru(rec, gate, state_conv[0], state_lru[0].reshape(dec_batch, 1, width), *lru_args,
                               batch=dec_batch, seq=dec_seq, row0=tp)
    rw, rb = router_mats(1)
    hmid, u, rinfo, counts = _post_mixer((h1, h1), tp // ROW_TILE, mods_p[1], mods_s[1], [(yg_p, yg_s)],
                                         [w_out_odd[0].astype(BF16)], norm_ffn[1].reshape(1, d), rw, rb, seq=seq)
    dest, ys = _moe(u, rinfo, counts, moe_w1, moe_w3, moe_w2, 1)
    y_p, y_s = _combine(dest, ys, hmid, rinfo, mods_p[1], mods_s[1], norm_final.reshape(1, d), tp=tp, seq=seq,
                        final=True)

    pos_major = lambda a: jnp.transpose(a.reshape(batch, MOBA_HEADS, MOBA_HEAD_DIM, seq), (0, 3, 1, 2))[None]
    return (y_p.reshape(batch, seq, d), y_s.reshape(dec_batch, dec_seq, d),
            pos_major(k_p), pos_major(v_p),
            k_s.reshape(1, dec_batch, dec_seq, MOBA_HEADS, MOBA_HEAD_DIM),
            v_s.reshape(1, dec_batch, dec_seq, MOBA_HEADS, MOBA_HEAD_DIM),
            hgrn_p.reshape(1, batch, HG_HEADS, HG_DK, HG_DK), hgrn_s.reshape(1, dec_batch, HG_HEADS, HG_DK, HG_DK),
            conv_p.reshape(1, batch, CONV_WIDTH - 1, width), conv_s.reshape(1, dec_batch, CONV_WIDTH - 1, width),
            lru_p.reshape(1, batch, width), lru_s.reshape(1, dec_batch, width))
```

```python
import functools

import numpy as np
import jax
import jax.numpy as jnp
from jax import lax
from jax.experimental import pallas as pl
from jax.experimental.pallas import tpu as pltpu

F32 = jnp.float32
BF16 = jnp.bfloat16
I32 = jnp.int32
HIGHEST = lax.Precision.HIGHEST

MOBA_HEADS = 8
MOBA_HEAD_DIM = 64
MOBA_BLOCK = 256
MOBA_TOPK = 3
ROPE_THETA = 10000.0
HG_HEADS = 4
HG_DK = 128
LRU_HEADS = 4
CONV_WIDTH = 4
LRU_C = 8.0
N_GROUPS = 4
EXPERTS_PER_GROUP = 4
NORM_EPS = 1e-6

LANES = 128
SUBLANES = 8
VMEM_LIMIT = 56 * 1024 * 1024

ROW_TILE = 256
MOE_TILE = 256
HG_CHUNK = 128
N_BUCKETS = N_GROUPS * 6
PAIR_LO = (0, 0, 0, 1, 1, 2)
PAIR_HI = (1, 2, 3, 2, 3, 3)
NEG = -1e30


def _cparams(sem, vmem=VMEM_LIMIT):
    return pltpu.CompilerParams(dimension_semantics=sem, vmem_limit_bytes=vmem)


def _dot(a, b, **kw):
    return jnp.dot(a, b, preferred_element_type=F32, **kw)


def _dot_nt(a, b, **kw):
    return lax.dot_general(a, b, (((1,), (1,)), ((), ())), preferred_element_type=F32, **kw)


def _sigmoid(x):
    return jax.nn.sigmoid(x)


def _by_group(i, n_prompt_tiles, fn):
    @pl.when(i < n_prompt_tiles)
    def _():
        fn(True)

    @pl.when(i >= n_prompt_tiles)
    def _():
        fn(False)


def _pair_specs(cols, npt, rows=ROW_TILE, off_p=0, off_s=0):
    return (pl.BlockSpec((rows, cols), lambda i: (off_p + jnp.minimum(i, npt - 1), 0)),
            pl.BlockSpec((rows, cols), lambda i: (off_s + jnp.maximum(i - npt, 0), 0)))


def _mod_specs(d, npt, tiles_per_seq, rows=ROW_TILE):
    return (pl.BlockSpec((1, 6, d), lambda i: (jnp.minimum(i, npt - 1) // tiles_per_seq, 0, 0)),
            pl.BlockSpec((rows // SUBLANES, 6, d), lambda i: (jnp.maximum(i - npt, 0), 0, 0)))


def _norm_mod(x, nw, m3, scale_idx, shift_idx):
    r, d = x.shape
    nb = m3.shape[0]
    var = jnp.mean(x * x, axis=-1, keepdims=True)
    y = x * lax.rsqrt(var + NORM_EPS) * nw
    y3 = y.reshape(nb, r // nb, d)
    u3 = y3 * (1.0 + m3[:, scale_idx:scale_idx + 1, :]) + m3[:, shift_idx:shift_idx + 1, :]
    return u3.reshape(r, d)


def _gate_residual(h, m3, gate_idx, out):
    r, d = h.shape
    nb = m3.shape[0]
    return (h.reshape(nb, r // nb, d) + m3[:, gate_idx:gate_idx + 1, :] * out.reshape(nb, r // nb, d)).reshape(r, d)


def _ada_kernel(c_ref, w_ref, b_ref, o_ref):
    o_ref[0] = _dot(c_ref[...], w_ref[0], precision=HIGHEST) + b_ref[0]


def _ada_mods(c_all, ada_w, ada_b):
    depth, d, n6 = ada_w.shape
    nb = c_all.shape[0]
    tn = 1024
    return pl.pallas_call(
        _ada_kernel,
        out_shape=jax.ShapeDtypeStruct((depth, nb, n6), F32),
        grid=(depth, n6 // tn),
        in_specs=[pl.BlockSpec((nb, d), lambda l, n: (0, 0)),
                  pl.BlockSpec((1, d, tn), lambda l, n: (l, 0, n)),
                  pl.BlockSpec((1, 1, tn), lambda l, n: (l, 0, n))],
        out_specs=pl.BlockSpec((1, nb, tn), lambda l, n: (l, 0, n)),
        compiler_params=_cparams(("arbitrary", "arbitrary")),
        name="ada_mods",
    )(c_all, ada_w, ada_b.reshape(depth, 1, n6))


def _even_pre_kernel(xp_ref, xs_ref, mp_ref, ms_ref, nw_ref, w_ref, hb_ref, invf_ref,
                     q_ref, kp_ref, ks_ref, vp_ref, vs_ref, hg_ref, *, npt, tiles_per_seq, past_len, layer_slot):
    i = pl.program_id(0)
    aw = MOBA_HEADS * MOBA_HEAD_DIM
    hw = HG_HEADS * HG_DK

    def run(is_prompt):
        x = (xp_ref if is_prompt else xs_ref)[...]
        m3 = (mp_ref if is_prompt else ms_ref)[...]
        r = x.shape[0]
        nb = m3.shape[0]
        tl = r // nb
        ub = _norm_mod(x, nw_ref[...], m3, 1, 0).astype(BF16)

        pos0 = (i % tiles_per_seq) * r if is_prompt else past_len
        pos = (pos0 + lax.broadcasted_iota(I32, (nb, tl, LANES), 1)).astype(F32).reshape(r, LANES)
        ang = pos * invf_ref[...]
        reps = aw // LANES
        cos = jnp.concatenate([jnp.cos(ang)] * reps, axis=1)
        sin = jnp.concatenate([jnp.sin(ang)] * reps, axis=1)
        lane = lax.broadcasted_iota(I32, (r, aw), 1)
        half = MOBA_HEAD_DIM // 2
        first = (lane % MOBA_HEAD_DIM) < half
        sin = jnp.where(first, -sin, sin)

        def rope(t):
            rot = jnp.where(first, pltpu.roll(t, aw - half, 1), pltpu.roll(t, half, 1))
            return t * cos + rot * sin

        def seg(a, b):
            return _dot(ub, w_ref[:, a:b])

        q_ref[...] = rope(seg(0, aw))
        k = rope(seg(aw, 2 * aw))
        v = seg(2 * aw, 3 * aw)
        if is_prompt:
            kp_ref[0] = k.T
            vp_ref[0] = v.T
        else:
            ks_ref[...] = k
            vs_ref[...] = v
        o = 3 * aw
        qb = seg(o, o + hw)
        fb = seg(o + hw, o + 2 * hw)
        hb = hb_ref[...]
        e = jnp.exp(hb - jnp.max(hb, axis=0, keepdims=True))
        sm = e / jnp.sum(e, axis=0, keepdims=True)
        lb = jnp.sum(sm[0:layer_slot + 1, :], axis=0, keepdims=True)
        hg_ref[:, 0:hw] = qb * _sigmoid(qb)
        hg_ref[:, hw:2 * hw] = lb + (1.0 - lb) * _sigmoid(fb)
        hg_ref[:, 2 * hw:3 * hw] = seg(o + 2 * hw, o + 3 * hw)
        hg_ref[:, 3 * hw:4 * hw] = seg(o + 3 * hw, o + 4 * hw)

    _by_group(i, npt, run)


def _even_pre(xp, xs, mp, ms, nw, w_in, hb, invf, *, seq, past_len, layer_slot):
    tp, d = xp.shape
    ts = xs.shape[0]
    npt, nst = tp // ROW_TILE, ts // ROW_TILE
    tiles_per_seq = seq // ROW_TILE
    aw = MOBA_HEADS * MOBA_HEAD_DIM
    hw = HG_HEADS * HG_DK
    t = tp + ts
    const = lambda shape: pl.BlockSpec(shape, lambda i: (0,) * len(shape))
    kern = functools.partial(_even_pre_kernel, npt=npt, tiles_per_seq=tiles_per_seq, past_len=past_len,
                             layer_slot=layer_slot)
    _, ks_spec = _pair_specs(aw, npt)
    kp_spec = pl.BlockSpec((1, aw, ROW_TILE), lambda i: (jnp.minimum(i, npt - 1) // tiles_per_seq, 0,
                                                          jnp.minimum(i, npt - 1) % tiles_per_seq))
    return pl.pallas_call(
        kern,
        out_shape=(jax.ShapeDtypeStruct((t, aw), F32),
                   jax.ShapeDtypeStruct((tp // seq, aw, seq), F32), jax.ShapeDtypeStruct((ts, aw), F32),
                   jax.ShapeDtypeStruct((tp // seq, aw, seq), F32), jax.ShapeDtypeStruct((ts, aw), F32),
                   jax.ShapeDtypeStruct((t, 4 * hw), F32)),
        grid=(npt + nst,),
        in_specs=[*_pair_specs(d, npt), *_mod_specs(d, npt, tiles_per_seq),
                  const((1, d)), const(w_in.shape), const(hb.shape), const((1, LANES))],
        out_specs=(pl.BlockSpec((ROW_TILE, aw), lambda i: (i, 0)), kp_spec, ks_spec, kp_spec, ks_spec,
                   pl.BlockSpec((ROW_TILE, 4 * hw), lambda i: (i, 0))),
        compiler_params=_cparams(("arbitrary",)),
        name="even_pre",
    )(xp, xs, mp, ms, nw, w_in, hb, invf)


def _top_blocks(g, valid):
    lane = lax.broadcasted_iota(I32, g.shape, 1)
    jl = lane % SUBLANES
    g = jnp.where(valid, g, -jnp.inf)
    cnt = jnp.zeros(g.shape, I32)
    for d in range(1, SUBLANES):
        wrapped = (jl + d) >= SUBLANES
        pg = jnp.where(wrapped, pltpu.roll(g, SUBLANES - d, 1), pltpu.roll(g, LANES - d, 1))
        beats = (pg > g) | ((pg == g) & wrapped)
        cnt = cnt + beats.astype(I32)
    return valid & (cnt < MOBA_TOPK)


def _head_expand(rows8, n_heads, width):
    x = jnp.concatenate([rows8] * n_heads, axis=0)
    r = lax.broadcasted_iota(I32, x.shape, 0) // SUBLANES
    l = lax.broadcasted_iota(I32, x.shape, 1) // (width // n_heads)
    return jnp.where(r == l, x, 0.0)


def _block_indicator(seq):
    key_blk = np.arange(seq)[None, :] // MOBA_BLOCK
    row = np.arange(LANES)[:, None]
    ind = (row < MOBA_HEADS * SUBLANES) & ((row % SUBLANES) == key_blk)
    return jnp.asarray(ind.astype(np.float32), dtype=BF16)


def _attn_prompt_kernel(q_ref, k_ref, v_ref, ind_ref, o_ref, km_ref, *, n_blk):
    i = pl.program_id(1)
    blk = MOBA_BLOCK
    aw = MOBA_HEADS * MOBA_HEAD_DIM
    scale = MOBA_HEAD_DIM ** -0.5

    @pl.when(i == 0)
    def _():
        r = lax.broadcasted_iota(I32, (aw, LANES), 0) // MOBA_HEAD_DIM
        c = lax.broadcasted_iota(I32, (aw, LANES), 1)
        km = jnp.zeros((aw, LANES), F32)
        for j in range(n_blk):
            mean_j = jnp.mean(k_ref[0, :, j * blk:(j + 1) * blk], axis=1, keepdims=True)
            km = jnp.where(((c % SUBLANES) == j) & ((c // SUBLANES) == r), mean_j, km)
        km_ref[...] = km

    q = q_ref[...]
    gate = _dot(q, km_ref[...], precision=HIGHEST)
    lane = lax.broadcasted_iota(I32, gate.shape, 1)
    jl = lane % SUBLANES
    valid = (jl < i) & (lane < MOBA_HEADS * SUBLANES)
    keep = _top_blocks(gate, valid) | (jl == i)
    bias = jnp.where(keep, 0.0, NEG)

    row = lax.broadcasted_iota(I32, (blk, blk), 0)
    col = lax.broadcasted_iota(I32, (blk, blk), 1)
    tril = row >= col

    def tile(c):
        n_keys = (c + 1) * blk
        for hp in range(aw // LANES):
            cols = slice(hp * LANES, (hp + 1) * LANES)
            rhs = jnp.concatenate([k_ref[0, cols, 0:n_keys].astype(BF16), ind_ref[:, 0:n_keys]], axis=0)
            vv = v_ref[0, cols, 0:n_keys].astype(BF16)
            qp = q[:, cols] * scale
            outs = []
            for s in (0, 1):
                h = 2 * hp + s
                qm = jnp.where((lane // MOBA_HEAD_DIM) == s, qp, 0.0).astype(BF16)
                hb = jnp.where((lane // SUBLANES) == h, bias, 0.0).astype(BF16)
                sc = _dot(jnp.concatenate([qm, hb], axis=1), rhs)
                own = jnp.where(tril, sc[:, c * blk:], NEG)
                sc = own if c == 0 else jnp.concatenate([sc[:, :c * blk], own], axis=1)
                p = jnp.exp(sc - jnp.max(sc, axis=1, keepdims=True))
                outs.append(_dot_nt(p.astype(BF16), vv) / jnp.sum(p, axis=1, keepdims=True))
            o_ref[:, cols] = jnp.where((lane // MOBA_HEAD_DIM) == 0, outs[0], outs[1]).astype(BF16)

    for c in range(n_blk):
        pl.when(i == c)(functools.partial(tile, c))


def _attn_prompt(q_all, k_p, v_p, *, batch, seq):
    aw = MOBA_HEADS * MOBA_HEAD_DIM
    n_blk = seq // MOBA_BLOCK
    assert n_blk <= SUBLANES
    ind = _block_indicator(seq)
    kern = functools.partial(_attn_prompt_kernel, n_blk=n_blk)
    return pl.pallas_call(
        kern,
        out_shape=jax.ShapeDtypeStruct((batch * seq, aw), BF16),
        grid=(batch, n_blk),
        in_specs=[pl.BlockSpec((MOBA_BLOCK, aw), lambda b, i: (b * n_blk + i, 0)),
                  pl.BlockSpec((1, aw, seq), lambda b, i: (b, 0, 0)),
                  pl.BlockSpec((1, aw, seq), lambda b, i: (b, 0, 0)),
                  pl.BlockSpec((LANES, seq), lambda b, i: (0, 0))],
        out_specs=pl.BlockSpec((MOBA_BLOCK, aw), lambda b, i: (b * n_blk + i, 0)),
        scratch_shapes=[pltpu.VMEM((aw, LANES), F32)],
        compiler_params=_cparams(("arbitrary", "arbitrary")),
        name="moba_prompt",
    )(q_all, k_p, v_p, ind)


def _attn_sample_kernel(pt_ref, q_ref, kn_ref, vn_ref, ck_ref, cv_ref, o_ref, kbuf, vbuf, sem,
                        *, n_pages, page, n_blk, layer):
    b = pl.program_id(0)
    nb = pl.num_programs(0)
    blk = MOBA_BLOCK
    hd = MOBA_HEAD_DIM
    dec = q_ref.shape[0]
    scale = hd ** -0.5

    def page_copies(seq_idx, slot):
        cps = []
        for p in range(n_pages):
            pg = pt_ref[seq_idx * n_pages + p]
            cps.append(pltpu.make_async_copy(ck_ref.at[layer, pg], kbuf.at[slot, :, :, pl.ds(p * page, page)],
                                             sem.at[0, slot]))
            cps.append(pltpu.make_async_copy(cv_ref.at[layer, pg], vbuf.at[slot, :, :, pl.ds(p * page, page)],
                                             sem.at[1, slot]))
        return cps

    @pl.when(b == 0)
    def _():
        for c in page_copies(0, 0):
            c.start()

    @pl.when(b + 1 < nb)
    def _():
        for c in page_copies(b + 1, (b + 1) % 2):
            c.start()

    slot = b % 2
    for c in page_copies(b, slot):
        c.wait()

    nh = MOBA_HEADS
    aw = nh * hd
    past = n_blk * blk
    kt = kbuf[slot].reshape(aw, past)
    vt = vbuf[slot].reshape(aw, past)
    qexp = _head_expand(q_ref[...], nh, aw)

    colj = lax.broadcasted_iota(I32, (aw, LANES), 1)
    km = jnp.zeros((aw, LANES), F32)
    for j in range(n_blk):
        km = jnp.where(colj == j, jnp.mean(kt[:, j * blk:(j + 1) * blk], axis=1, keepdims=True), km)
    gate = _dot(qexp, km, precision=HIGHEST)
    lane = lax.broadcasted_iota(I32, gate.shape, 1)
    sel = _top_blocks(gate, lane < n_blk).astype(F32)

    qs = (qexp * scale).astype(BF16)
    sc = _dot(qs, kt.astype(BF16))
    key_blk = lax.broadcasted_iota(I32, sc.shape, 1) // blk
    mask = jnp.zeros(sc.shape, F32)
    for j in range(n_blk):
        mask = jnp.where(key_blk == j, sel[:, j:j + 1], mask)
    sc = jnp.where(mask > 0.5, sc, NEG)

    pad = jnp.zeros((LANES - dec, aw), F32)
    kn = jnp.concatenate([kn_ref[...], pad], axis=0).astype(BF16)
    vn = jnp.concatenate([vn_ref[...], pad], axis=0).astype(BF16)
    qi = lax.broadcasted_iota(I32, gate.shape, 0) % dec
    sn = jnp.where(lane <= qi, _dot_nt(qs, kn), NEG)

    m = jnp.maximum(jnp.max(sc, axis=1, keepdims=True), jnp.max(sn, axis=1, keepdims=True))
    p = jnp.exp(sc - m)
    pn = jnp.exp(sn - m)
    l = jnp.sum(p, axis=1, keepdims=True) + jnp.sum(pn, axis=1, keepdims=True)
    o = (_dot_nt(p.astype(BF16), vt.astype(BF16)) + _dot(pn.astype(BF16), vn)) / l
    r = lax.broadcasted_iota(I32, o.shape, 0) // dec
    c = lax.broadcasted_iota(I32, o.shape, 1) // hd
    o = jnp.where(r == c, o, 0.0)
    acc = o[0:dec]
    for h in range(1, nh):
        acc = acc + o[h * dec:(h + 1) * dec]
    o_ref[...] = acc


def _attn_sample(page_table, q_all, k_s, v_s, cache_k, cache_v, *, layer, dec_batch, dec_seq, q_row0):
    _, n_phys, page, n_heads, hd = cache_k.shape
    aw = MOBA_HEADS * MOBA_HEAD_DIM
    n_pages = page_table.shape[1]
    past_len = n_pages * page
    assert past_len % MOBA_BLOCK == 0 and dec_seq == SUBLANES and past_len // MOBA_BLOCK == SUBLANES
    assert n_heads == MOBA_HEADS and hd == MOBA_HEAD_DIM
    kern = functools.partial(_attn_sample_kernel, n_pages=n_pages, page=page, n_blk=past_len // MOBA_BLOCK,
                             layer=layer)
    q_blk0 = q_row0 // dec_seq
    grid_spec = pltpu.PrefetchScalarGridSpec(
        num_scalar_prefetch=1,
        grid=(dec_batch,),
        in_specs=[pl.BlockSpec((dec_seq, aw), lambda b, pt: (q_blk0 + b, 0)),
                  pl.BlockSpec((dec_seq, aw), lambda b, pt: (b, 0)),
                  pl.BlockSpec((dec_seq, aw), lambda b, pt: (b, 0)),
                  pl.BlockSpec(memory_space=pl.ANY),
                  pl.BlockSpec(memory_space=pl.ANY)],
        out_specs=pl.BlockSpec((dec_seq, aw), lambda b, pt: (b, 0)),
        scratch_shapes=[pltpu.VMEM((2, n_heads, hd, past_len), F32), pltpu.VMEM((2, n_heads, hd, past_len), F32),
                        pltpu.SemaphoreType.DMA((2, 2))],
    )
    to_pos_minor = lambda c: jnp.transpose(c, (0, 1, 3, 4, 2))
    return pl.pallas_call(
        kern,
        out_shape=jax.ShapeDtypeStruct((dec_batch * dec_seq, aw), F32),
        grid_spec=grid_spec,
        compiler_params=_cparams(("arbitrary",)),
        name="moba_sample",
    )(page_table.reshape(-1), q_all, k_s, v_s, to_pos_minor(cache_k), to_pos_minor(cache_v))


def _hgrn_levels():
    n = HG_CHUNK
    return int(np.log2(n))


def _hgrn_cumsum_matrix():
    return jnp.asarray(np.tril(np.ones((HG_CHUNK, HG_CHUNK), np.float32)), dtype=BF16)


def _separator_rows(b, level):
    c, dk = b.shape
    m = 2 ** level
    if 2 * m >= SUBLANES:
        b3 = b.reshape(c // (2 * m), 2 * m, dk)
        return jnp.broadcast_to(b3[:, m - 1:m, :], b3.shape).reshape(c, dk)
    b3 = b.reshape(c // SUBLANES, SUBLANES, dk)
    sub = lax.broadcasted_iota(I32, b3.shape, 1)
    if m == 1:
        out = jnp.where(sub % 2 == 1, pltpu.roll(b3, 1, 1), b3)
    else:
        out = jnp.where(sub < 4, b3[:, 1:2, :], b3[:, 5:6, :])
    return out.reshape(c, dk)


def _hgrn_post(o, g, nw):
    o = o * lax.rsqrt(jnp.mean(o * o, axis=-1, keepdims=True) + NORM_EPS) * nw
    return o * (g * _sigmoid(g))


def _hgrn_prompt_kernel(q_ref, f_ref, i_ref, g_ref, ms_ref, nw_ref, o_ref, s_ref):
    c = HG_CHUNK
    dk = HG_DK
    n_chunks = q_ref.shape[0] // c
    levels = _hgrn_levels()
    row = lax.broadcasted_iota(I32, (c, c), 0)
    col = lax.broadcasted_iota(I32, (c, c), 1)
    rowk = lax.broadcasted_iota(I32, (c, dk), 0)
    upper = [((rowk // (2 ** l)) % 2) == 1 for l in range(levels)]
    same = [(row // (2 ** (l + 1))) == (col // (2 ** (l + 1))) for l in range(levels)]
    diag = row == col
    nw = nw_ref[...]

    heads = q_ref.shape[1] // dk

    def chunk(ci, sts):
        r0 = pl.multiple_of(ci * c, c)
        return tuple(one_head(r0, slice(hh * dk, (hh + 1) * dk), sts[hh]) for hh in range(heads))

    def one_head(r0, cols, st):
        q = q_ref[pl.ds(r0, c), cols]
        f = f_ref[pl.ds(r0, c), cols]
        v = i_ref[pl.ds(r0, c), cols]
        g = g_ref[pl.ds(r0, c), cols]
        lf = jnp.log(f)
        k = 1.0 - f
        hi = lf.astype(BF16)
        r1 = lf - hi.astype(F32)
        mid = r1.astype(BF16)
        lo = (r1 - mid.astype(F32)).astype(BF16)
        bb = _dot(ms_ref[...], jnp.concatenate([hi, mid, lo], axis=1))
        b = bb[:, 0:dk] + bb[:, dk:2 * dk] + bb[:, 2 * dk:3 * dk]
        a = jnp.where(diag, jnp.sum(q * k, axis=1, keepdims=True), 0.0)
        for l in range(levels):
            bs = _separator_rows(b, l)
            e = jnp.exp(jnp.where(upper[l], b - bs, bs - b))
            qp = jnp.where(upper[l], q * e, 0.0).astype(BF16)
            kp = jnp.where(upper[l], 0.0, k * e).astype(BF16)
            a = a + jnp.where(same[l], _dot_nt(qp, kp), 0.0)
        vb = v.astype(BF16)
        o = _dot_nt((q * jnp.exp(b)).astype(BF16), st.astype(BF16)) + _dot(a.astype(BF16), vb)
        blast = b[c - 1:c, :]
        kl = (k * jnp.exp(blast - b)).astype(BF16)
        st_new = st * jnp.exp(blast) + _dot(v.T.astype(BF16), kl)
        o_ref[pl.ds(r0, c), cols] = _hgrn_post(o, g, nw).astype(BF16)
        return st_new

    sts = lax.fori_loop(0, n_chunks, chunk, tuple(jnp.zeros((dk, dk), F32) for _ in range(heads)))
    for hh in range(heads):
        s_ref[hh] = sts[hh].T


def _hgrn_prompt(hg, nw, *, batch, seq):
    hw = HG_HEADS * HG_DK
    mstack = _hgrn_cumsum_matrix()
    hps = HG_HEADS
    groups = HG_HEADS // hps
    spec = lambda off: pl.BlockSpec((seq, hps * HG_DK), lambda b, h: (b, off + h))
    return pl.pallas_call(
        _hgrn_prompt_kernel,
        out_shape=(jax.ShapeDtypeStruct((batch * seq, hw), BF16),
                   jax.ShapeDtypeStruct((batch * HG_HEADS, HG_DK, HG_DK), F32)),
        grid=(batch, groups),
        in_specs=[spec(0), spec(groups), spec(2 * groups), spec(3 * groups),
                  pl.BlockSpec(mstack.shape, lambda b, h: (0, 0)),
                  pl.BlockSpec((1, HG_DK), lambda b, h: (0, 0))],
        out_specs=(pl.BlockSpec((seq, hps * HG_DK), lambda b, h: (b, h)),
                   pl.BlockSpec((hps, HG_DK, HG_DK), lambda b, h: (b * groups + h, 0, 0))),
        compiler_params=_cparams(("arbitrary", "arbitrary")),
        name="hgrn_prompt",
    )(hg, hg, hg, hg, mstack, nw)


def _hgrn_sample_kernel(hg_ref, s0_ref, nw_ref, o_ref, s_ref, *, dec, seqs):
    dk = HG_DK
    hw = HG_HEADS * HG_DK
    nw = nw_ref[...]
    trow = lax.broadcasted_iota(I32, (dec, dk), 0)
    srow = lax.broadcasted_iota(I32, (dk, dk), 0)
    zrows = jnp.zeros((dec, dk), F32)
    pad16 = lambda a: jnp.concatenate([a, zrows], axis=0).astype(BF16)
    dot_tn = lambda a, b: lax.dot_general(a, b, (((0,), (0,)), ((), ())), preferred_element_type=F32)

    def one_seq(si, carry):
        r0 = pl.multiple_of(si * dec, dec)
        for h in range(HG_HEADS):
            cols = slice(h * dk, (h + 1) * dk)
            q = hg_ref[pl.ds(r0, dec), h * dk:(h + 1) * dk]
            f = hg_ref[pl.ds(r0, dec), hw + h * dk:hw + (h + 1) * dk]
            v = hg_ref[pl.ds(r0, dec), 2 * hw + h * dk:2 * hw + (h + 1) * dk]
            g = hg_ref[pl.ds(r0, dec), 3 * hw + h * dk:3 * hw + (h + 1) * dk]
            k = 1.0 - f
            b = jnp.log(f)
            for dlt in (1, 2, 4):
                b = b + jnp.where(trow >= dlt, pltpu.roll(b, dlt, 0), 0.0)
            s0 = s0_ref[si * HG_HEADS + h]
            o = _dot(pad16(q * jnp.exp(b)), s0.astype(BF16))[0:dec]
            for s in range(dec):
                live = trow >= s
                e = jnp.exp(jnp.where(live, b - b[s:s + 1, :], 0.0))
                a_s = jnp.sum(jnp.where(live, q * e * k[s:s + 1, :], 0.0), axis=1, keepdims=True)
                o = o + a_s * v[s:s + 1, :]
            o_ref[pl.ds(r0, dec), cols] = _hgrn_post(o, g, nw)
            blast = b[dec - 1:dec, :]
            upd = dot_tn(pad16(k * jnp.exp(blast - b)), pad16(v))
            ecol = jnp.where(srow == 0, jnp.exp(blast), 0.0).T[:, 0:1]
            s_ref[si * HG_HEADS + h] = ecol * s0 + upd
        return carry

    lax.fori_loop(0, seqs, one_seq, 0)


def _hgrn_sample(hg, s0, nw, *, dec_batch, dec_seq, row0):
    hw = HG_HEADS * HG_DK
    seqs = 8
    rows = seqs * dec_seq
    kern = functools.partial(_hgrn_sample_kernel, dec=dec_seq, seqs=seqs)
    st_spec = pl.BlockSpec((seqs * HG_HEADS, HG_DK, HG_DK), lambda i: (i, 0, 0))
    return pl.pallas_call(
        kern,
        out_shape=(jax.ShapeDtypeStruct((dec_batch * dec_seq, hw), F32),
                   jax.ShapeDtypeStruct((dec_batch * HG_HEADS, HG_DK, HG_DK), F32)),
        grid=(dec_batch // seqs,),
        in_specs=[pl.BlockSpec((rows, 4 * hw), lambda i: (row0 // rows + i, 0)), st_spec,
                  pl.BlockSpec((1, HG_DK), lambda i: (0, 0))],
        out_specs=(pl.BlockSpec((rows, hw), lambda i: (i, 0)), st_spec),
        compiler_params=_cparams(("arbitrary",)),
        name="hgrn_sample",
    )(hg, s0, nw)


def _post_mixer_kernel(*refs, n_a, npt):
    hp_ref, hs_ref, mp_ref, ms_ref = refs[0:4]
    a_refs = refs[4:4 + 2 * n_a]
    w_refs = refs[4 + 2 * n_a:4 + 3 * n_a]
    nffn_ref, rw_ref, rb_ref, striu_ref = refs[4 + 3 * n_a:8 + 3 * n_a]
    hmid_ref, u_ref, rinfo_ref, counts_ref, carry = refs[8 + 3 * n_a:]
    i = pl.program_id(0)

    @pl.when(i == 0)
    def _():
        carry[...] = jnp.zeros_like(carry)

    def run(is_prompt):
        h = (hp_ref if is_prompt else hs_ref)[...]
        m3 = (mp_ref if is_prompt else ms_ref)[...]
        out = None
        for a in range(n_a):
            av = a_refs[2 * a + (0 if is_prompt else 1)][...].astype(BF16)
            t = _dot(av, w_refs[a][...])
            out = t if out is None else out + t
        hm = _gate_residual(h, m3, 2, out)
        hmid_ref[...] = hm
        u = _norm_mod(hm, nffn_ref[...], m3, 4, 3)
        u_ref[...] = u.reshape(u_ref.shape)

        u_hi = u.astype(BF16)
        u_lo = (u - u_hi.astype(F32)).astype(BF16)
        hh = _dot(u_hi, rw_ref[...])
        logits = hh[:, 0:LANES] + hh[:, LANES:2 * LANES] + _dot(u_lo, rw_ref[:, 0:LANES]) + rb_ref[...]
        lt = logits.T
        col = lambda j: lt[j:j + 1, :]
        gl = [col(j) for j in range(N_GROUPS)]
        gmax = functools.reduce(jnp.maximum, gl)
        gsum = functools.reduce(lambda x, y: x + y, [jnp.exp(x - gmax) for x in gl])
        gval = 1.0 / gsum
        gidx = jnp.where(gl[0] == gmax, 0, jnp.where(gl[1] == gmax, 1, jnp.where(gl[2] == gmax, 2, 3)))
        el = []
        for k in range(EXPERTS_PER_GROUP):
            c = [col(N_GROUPS + g * EXPERTS_PER_GROUP + k) for g in range(N_GROUPS)]
            el.append(jnp.where(gidx == 0, c[0], jnp.where(gidx == 1, c[1], jnp.where(gidx == 2, c[2], c[3]))))
        emax = functools.reduce(jnp.maximum, el)
        pe = [jnp.exp(x - emax) for x in el]
        esum = functools.reduce(lambda x, y: x + y, pe)
        pk = [x / esum for x in pe]
        v1 = functools.reduce(jnp.maximum, pk)
        i1 = jnp.where(pk[0] == v1, 0, jnp.where(pk[1] == v1, 1, jnp.where(pk[2] == v1, 2, 3)))
        pk2 = [jnp.where(i1 == k, -1.0, pk[k]) for k in range(EXPERTS_PER_GROUP)]
        v2 = functools.reduce(jnp.maximum, pk2)
        i2 = jnp.where(pk2[0] == v2, 0, jnp.where(pk2[1] == v2, 1, jnp.where(pk2[2] == v2, 2, 3)))
        den = v1 + v2
        w1 = gval * v1 / den
        w2 = gval * v2 / den
        lo = jnp.minimum(i1, i2)
        hi = jnp.maximum(i1, i2)
        pair = jnp.where(lo == 0, hi - 1, jnp.where(lo == 1, hi + 1, 5))
        bucket = gidx * 6 + pair
        w_lo = jnp.where(i1 < i2, w1, w2)
        w_hi = jnp.where(i1 < i2, w2, w1)

        r = h.shape[0]
        sub = lax.broadcasted_iota(I32, (LANES, r), 0)
        onehot = sub == bucket
        before = _dot(onehot.astype(BF16), striu_ref[...]) + carry[:, 0:1]
        rank = jnp.sum(jnp.where(onehot, before, 0.0), axis=0, keepdims=True)
        carry[...] = carry[...] + jnp.sum(onehot.astype(F32), axis=1, keepdims=True)
        info = jnp.where(sub == 0, bucket.astype(F32),
                         jnp.where(sub == 1, w_lo, jnp.where(sub == 2, w_hi, jnp.where(sub == 3, rank, 0.0))))
        rinfo_ref[...] = info.T
        counts_ref[...] = carry[...]

    _by_group(i, npt, run)


def _post_mixer(h_pair, h_off_s, mp, ms, a_pairs, ws, nffn, rw, rb, *, seq):
    hp, hs = h_pair
    d = hp.shape[1]
    tp = a_pairs[0][0].shape[0]
    ts = a_pairs[0][1].shape[0]
    npt, nst = tp // ROW_TILE, ts // ROW_TILE
    t = tp + ts
    n_a = len(a_pairs)
    tiles_per_seq = seq // ROW_TILE
    const = lambda shape: pl.BlockSpec(shape, lambda i: (0,) * len(shape))
    stril = jnp.asarray(np.triu(np.ones((ROW_TILE, ROW_TILE), np.float32), 1), dtype=BF16)
    in_specs = [*_pair_specs(d, npt, off_s=h_off_s), *_mod_specs(d, npt, tiles_per_seq)]
    args = [hp, hs, mp, ms]
    for ap, as_ in a_pairs:
        in_specs += list(_pair_specs(ap.shape[1], npt))
        args += [ap, as_]
    for w in ws:
        in_specs.append(const(w.shape))
        args.append(w)
    in_specs += [const((1, d)), const(rw.shape), const((1, LANES)), const(stril.shape)]
    args += [nffn, rw, rb, stril]
    tile = lambda c: pl.BlockSpec((ROW_TILE, c), lambda i: (i, 0))
    kern = functools.partial(_post_mixer_kernel, n_a=n_a, npt=npt)
    return pl.pallas_call(
        kern,
        out_shape=(jax.ShapeDtypeStruct((t, d), F32), jax.ShapeDtypeStruct((t, d // LANES, LANES), F32),
                   jax.ShapeDtypeStruct((t, LANES), F32), jax.ShapeDtypeStruct((LANES, LANES), F32)),
        grid=(npt + nst,),
        in_specs=in_specs,
        out_specs=(tile(d), pl.BlockSpec((ROW_TILE, d // LANES, LANES), lambda i: (i, 0, 0)), tile(LANES),
                   const((LANES, LANES))),
        scratch_shapes=[pltpu.VMEM((LANES, LANES), F32)],
        compiler_params=_cparams(("arbitrary",)),
        name="post_mixer",
    )(*args)


def _for_rows(rows, fn):
    def body(r, c):
        fn(r)
        return c

    lax.fori_loop(0, rows, body, 0, unroll=8)


def _slot_of(bucket_ref, rank_ref, starts_ref, t):
    return starts_ref[bucket_ref[t]] + rank_ref[t]


def _scatter_kernel(bucket_ref, rank_ref, starts_ref, src_ref, init_ref, o_ref, sem):
    del init_ref
    rows = src_ref.shape[0]
    base = pl.program_id(0) * rows
    copy = lambda r: pltpu.make_async_copy(
        src_ref.at[r], o_ref.at[_slot_of(bucket_ref, rank_ref, starts_ref, base + r)], sem)
    _for_rows(rows, lambda r: copy(r).start())
    _for_rows(rows, lambda r: copy(r).wait())


def _scatter_tokens(slots, src, n_slots):
    t = src.shape[0]
    rows = ROW_TILE
    grid_spec = pltpu.PrefetchScalarGridSpec(
        num_scalar_prefetch=3,
        grid=(t // rows,),
        in_specs=[pl.BlockSpec((rows,) + src.shape[1:], lambda i, *_: (i, 0, 0)), pl.BlockSpec(memory_space=pl.ANY)],
        out_specs=pl.BlockSpec(memory_space=pl.ANY),
        scratch_shapes=[pltpu.SemaphoreType.DMA(())],
    )
    return pl.pallas_call(
        _scatter_kernel,
        out_shape=jax.ShapeDtypeStruct((n_slots,) + src.shape[1:], src.dtype),
        grid_spec=grid_spec,
        input_output_aliases={4: 0},
        compiler_params=_cparams(("arbitrary",)),
        name="moe_scatter",
    )(*slots, src, jnp.zeros((n_slots,) + src.shape[1:], src.dtype))


def _moe_kernel(ea_ref, eb_ref, valid_ref, fresh_ref, x_ref,
                w1a_ref, w3a_ref, w2a_ref, w1b_ref, w3b_ref, w2b_ref, o_ref, wbuf1, wbuf3, wbuf2):
    i = pl.program_id(0)
    nk = x_ref.shape[1]

    @pl.when(fresh_ref[i] == 1)
    def _():
        wbuf1[0] = w1a_ref[0, 0].astype(BF16)
        wbuf3[0] = w3a_ref[0, 0].astype(BF16)
        wbuf2[0] = w2a_ref[0, 0].astype(BF16)
        wbuf1[1] = w1b_ref[0, 0].astype(BF16)
        wbuf3[1] = w3b_ref[0, 0].astype(BF16)
        wbuf2[1] = w2b_ref[0, 0].astype(BF16)

    @pl.when(valid_ref[i] == 1)
    def _():
        xb = x_ref[...].reshape(x_ref.shape[0], nk * LANES).astype(BF16)
        for s in (0, 1):
            h1 = _dot(xb, wbuf1[s])
            h3 = _dot(xb, wbuf3[s])
            y = _dot(((h1 * _sigmoid(h1)) * h3).astype(BF16), wbuf2[s])
            o_ref[:, s * nk:(s + 1) * nk, :] = y.reshape(y.shape[0], nk, LANES)

    @pl.when(valid_ref[i] == 0)
    def _():
        o_ref[...] = jnp.zeros_like(o_ref)


def _moe_experts(tile_ea, tile_eb, tile_valid, tile_fresh, xs, w1, w3, w2, layer):
    n_slots, nk, _ = xs.shape
    d = nk * LANES
    ff = w1.shape[-1]
    wa = lambda shape: pl.BlockSpec((1, 1) + shape, lambda i, ea, eb, va, fr: (layer, ea[i], 0, 0))
    wb = lambda shape: pl.BlockSpec((1, 1) + shape, lambda i, ea, eb, va, fr: (layer, eb[i], 0, 0))
    rows = lambda k: pl.BlockSpec((MOE_TILE, k, LANES), lambda i, ea, eb, va, fr: (i, 0, 0))
    grid_spec = pltpu.PrefetchScalarGridSpec(
        num_scalar_prefetch=4,
        grid=(n_slots // MOE_TILE,),
        in_specs=[rows(nk), wa((d, ff)), wa((d, ff)), wa((ff, d)), wb((d, ff)), wb((d, ff)), wb((ff, d))],
        out_specs=rows(2 * nk),
        scratch_shapes=[pltpu.VMEM((2, d, ff), BF16), pltpu.VMEM((2, d, ff), BF16), pltpu.VMEM((2, ff, d), BF16)],
    )
    return pl.pallas_call(
        _moe_kernel,
        out_shape=jax.ShapeDtypeStruct((n_slots, 2 * nk, LANES), F32),
        grid_spec=grid_spec,
        compiler_params=_cparams(("arbitrary",)),
        name="moe_experts",
    )(tile_ea, tile_eb, tile_valid, tile_fresh, xs, w1, w3, w2, w1, w3, w2)


def _moe_plan(rinfo, counts, n_slots):
    n_tiles = n_slots // MOE_TILE
    cnt = counts[:N_BUCKETS, 0].astype(I32)
    padded = ((cnt + MOE_TILE - 1) // MOE_TILE) * MOE_TILE
    ends = jnp.cumsum(padded)
    starts = ends - padded
    slots = (rinfo[:, 0].astype(I32), rinfo[:, 3].astype(I32), starts)
    n_valid = ends[-1] // MOE_TILE
    tiles = jnp.arange(n_tiles, dtype=I32)
    tb = jnp.sum((ends[None, :] <= (tiles * MOE_TILE)[:, None]).astype(I32), axis=1)
    tb = jnp.minimum(tb, N_BUCKETS - 1)
    valid = tiles < n_valid
    tb = jnp.where(valid, tb, tb[jnp.maximum(n_valid - 1, 0)])
    lo = jnp.asarray(PAIR_LO, I32)
    hi = jnp.asarray(PAIR_HI, I32)
    ea = (tb // 6) * EXPERTS_PER_GROUP + lo[tb % 6]
    eb = (tb // 6) * EXPERTS_PER_GROUP + hi[tb % 6]
    fresh = jnp.concatenate([jnp.ones((1,), I32), (tb[1:] != tb[:-1]).astype(I32)])
    return slots, ea, eb, valid.astype(I32), fresh


def _moe(u_tiles, rinfo, counts, w1, w3, w2, layer):
    t = u_tiles.shape[0]
    n_slots = t + N_BUCKETS * MOE_TILE
    slots, ea, eb, valid, fresh = _moe_plan(rinfo, counts, n_slots)
    xs = _scatter_tokens(slots, u_tiles, n_slots)
    ys = _moe_experts(ea, eb, valid, fresh, xs, w1, w3, w2, layer)
    return slots, ys


def _combine_kernel(bucket_ref, rank_ref, starts_ref, ys_ref, hm_ref, rinfo_ref, mp_ref, ms_ref, nf_ref, *rest,
                    npt, final):
    if final:
        yp_ref, ysm_ref, gbuf, sem = rest
    else:
        h_ref, gbuf, sem = rest
    i = pl.program_id(0)
    n = pl.num_programs(0)
    rows = gbuf.shape[1]
    nk = gbuf.shape[2] // 2

    def copy(tile, r):
        slot = tile % 2
        src = ys_ref.at[_slot_of(bucket_ref, rank_ref, starts_ref, tile * rows + r)]
        return pltpu.make_async_copy(src, gbuf.at[slot, r], sem.at[slot])

    @pl.when(i == 0)
    def _():
        _for_rows(rows, lambda r: copy(0, r).start())

    @pl.when(i + 1 < n)
    def _():
        _for_rows(rows, lambda r: copy(i + 1, r).start())

    _for_rows(rows, lambda r: copy(i, r).wait())
    slot = i % 2
    ya = gbuf[slot, :, 0:nk, :].reshape(rows, nk * LANES)
    yb = gbuf[slot, :, nk:2 * nk, :].reshape(rows, nk * LANES)
    rinfo = rinfo_ref[...]
    moe = rinfo[:, 1:2] * ya + rinfo[:, 2:3] * yb

    def run(is_prompt):
        m3 = (mp_ref if is_prompt else ms_ref)[...]
        h = _gate_residual(hm_ref[...], m3, 5, moe)
        if final:
            y = h * lax.rsqrt(jnp.mean(h * h, axis=-1, keepdims=True) + NORM_EPS) * nf_ref[...]
            (yp_ref if is_prompt else ysm_ref)[...] = y
        else:
            h_ref[...] = h

    _by_group(i, npt, run)


def _combine(slots, ys, hmid, rinfo, mp, ms, nf, *, tp, seq, final):
    t, d = hmid.shape
    npt = tp // ROW_TILE
    tiles_per_seq = seq // ROW_TILE
    mpm = lambda f: (lambda i, *_: f(i))
    mp_spec, ms_spec = _mod_specs(d, npt, tiles_per_seq)
    yp_spec, ysm_spec = _pair_specs(d, npt)
    wrap = lambda s: pl.BlockSpec(s.block_shape, mpm(s.index_map))
    if final:
        out_shape = (jax.ShapeDtypeStruct((tp, d), F32), jax.ShapeDtypeStruct((t - tp, d), F32))
        out_specs = (wrap(yp_spec), wrap(ysm_spec))
    else:
        out_shape = jax.ShapeDtypeStruct((t, d), F32)
        out_specs = pl.BlockSpec((ROW_TILE, d), lambda i, *_: (i, 0))
    grid_spec = pltpu.PrefetchScalarGridSpec(
        num_scalar_prefetch=3,
        grid=(t // ROW_TILE,),
        in_specs=[pl.BlockSpec(memory_space=pl.ANY),
                  pl.BlockSpec((ROW_TILE, d), lambda i, *_: (i, 0)),
                  pl.BlockSpec((ROW_TILE, LANES), lambda i, *_: (i, 0)),
                  wrap(mp_spec), wrap(ms_spec),
                  pl.BlockSpec((1, d), lambda i, *_: (0, 0))],
        out_specs=out_specs,
        scratch_shapes=[pltpu.VMEM((2, ROW_TILE) + ys.shape[1:], F32), pltpu.SemaphoreType.DMA((2,))],
    )
    kern = functools.partial(_combine_kernel, npt=npt, final=final)
    return pl.pallas_call(
        kern, out_shape=out_shape, grid_spec=grid_spec,
        compiler_params=_cparams(("arbitrary",)),
        name="moe_combine_final" if final else "moe_combine",
    )(*slots, ys, hmid, rinfo, mp, ms, nf)


def _odd_pre_kernel(hp_ref, hs_ref, mp_ref, ms_ref, nw_ref, w_ref, gate_ref, rec_ref, *, npt):
    i = pl.program_id(0)
    width = gate_ref.shape[1]

    def run(is_prompt):
        h = (hp_ref if is_prompt else hs_ref)[...]
        m3 = (mp_ref if is_prompt else ms_ref)[...]
        ub = _norm_mod(h, nw_ref[...], m3, 1, 0).astype(BF16)
        x = _dot(ub, w_ref[:, 0:width])
        gate_ref[...] = 0.5 * x * (1.0 + jnp.tanh(np.sqrt(2.0 / np.pi) * (x + 0.044715 * (x * x * x))))
        rec_ref[...] = _dot(ub, w_ref[:, width:2 * width])

    _by_group(i, npt, run)


def _odd_pre(h, mp, ms, nw, w_in, *, tp, seq):
    t, d = h.shape
    npt = tp // ROW_TILE
    width = w_in.shape[1] // 2
    const = lambda shape: pl.BlockSpec(shape, lambda i: (0,) * len(shape))
    tile = lambda c: pl.BlockSpec((ROW_TILE, c), lambda i: (i, 0))
    kern = functools.partial(_odd_pre_kernel, npt=npt)
    return pl.pallas_call(
        kern,
        out_shape=(jax.ShapeDtypeStruct((t, width), F32), jax.ShapeDtypeStruct((t, width), F32)),
        grid=(t // ROW_TILE,),
        in_specs=[*_pair_specs(d, npt, off_s=npt), *_mod_specs(d, npt, seq // ROW_TILE),
                  const((1, d)), const(w_in.shape)],
        out_specs=(tile(width), tile(width)),
        compiler_params=_cparams(("arbitrary",)),
        name="odd_pre",
    )(h, h, mp, ms, nw, w_in)


def _lru_kernel(rec_ref, gate_ref, c0_ref, h0_ref, cw_ref, cb_ref, wa_ref, ba_ref, wx_ref, bx_ref, lam_ref,
                yg_ref, cs_ref, hl_ref, hist, hc, a_s, h_s, *, nb, tl):
    i = pl.program_id(1)
    r, width = rec_ref.shape
    taps = CONV_WIDTH
    blk = width // LRU_HEADS

    @pl.when(i == 0)
    def _():
        hist[:, SUBLANES - (taps - 1):SUBLANES, :] = c0_ref[...]
        hc[...] = h0_ref[...]

    rec3 = rec_ref[...].reshape(nb, tl, width)
    hist[:, SUBLANES:SUBLANES + tl, :] = rec3
    cw = cw_ref[...]
    conv = cb_ref[...] + rec3 * cw[taps - 1:taps, :]
    for back in range(1, taps):
        conv = conv + hist[:, SUBLANES - back:SUBLANES - back + tl, :] * cw[taps - 1 - back:taps - back, :]
    tail = hist[:, SUBLANES + tl - (taps - 1):SUBLANES + tl, :]
    cs_ref[...] = tail
    hist[:, SUBLANES - (taps - 1):SUBLANES, :] = tail

    cf = conv.reshape(r, width)
    cb16 = cf.astype(BF16)
    gr = jnp.concatenate([_dot(cb16[:, h * blk:(h + 1) * blk], wa_ref[h]) for h in range(LRU_HEADS)], axis=1)
    gi = jnp.concatenate([_dot(cb16[:, h * blk:(h + 1) * blk], wx_ref[h]) for h in range(LRU_HEADS)], axis=1)
    rg = _sigmoid(gr + ba_ref[...])
    ig = _sigmoid(gi + bx_ref[...])
    z = -lam_ref[...]
    softplus = jnp.maximum(z, 0.0) + jnp.log1p(jnp.exp(-jnp.abs(z)))
    log_a = -LRU_C * rg * softplus
    a = jnp.exp(log_a)
    gx = jnp.sqrt(1.0 - a * a) * (ig * cf)

    groups = r // SUBLANES
    sub = lax.broadcasted_iota(I32, (groups, SUBLANES, width), 1)
    aa, hh = a.reshape(groups, SUBLANES, width), gx.reshape(groups, SUBLANES, width)
    for dlt in (1, 2, 4):
        ok = sub >= dlt
        a_sh = pltpu.roll(aa, dlt, 1)
        h_sh = pltpu.roll(hh, dlt, 1)
        hh = jnp.where(ok, hh + aa * h_sh, hh)
        aa = jnp.where(ok, aa * a_sh, aa)
    aa, hh = aa.reshape(r, width), hh.reshape(r, width)

    if tl == SUBLANES:
        y3 = hh.reshape(nb, tl, width) + aa.reshape(nb, tl, width) * hc[...]
        hl_ref[...] = y3[:, tl - 1:tl, :]
        y = y3.reshape(r, width)
    else:
        a_s[...] = aa
        h_s[...] = hh

        def group(j, carry):
            r0 = pl.multiple_of(j * SUBLANES, SUBLANES)
            yj = h_s[pl.ds(r0, SUBLANES), :] + a_s[pl.ds(r0, SUBLANES), :] * carry
            h_s[pl.ds(r0, SUBLANES), :] = yj
            return yj[SUBLANES - 1:SUBLANES, :]

        last = lax.fori_loop(0, r // SUBLANES, group, hc[0])
        hc[0] = last
        hl_ref[0] = last
        y = h_s[...]
    yg_ref[...] = (y * gate_ref[...]).astype(BF16)


def _lru(rec, gate, conv0, h0, cw, cb, wa, ba, wx, bx, lam, *, batch, seq, row0):
    width = rec.shape[1]
    taps = CONV_WIDTH
    if seq == SUBLANES:
        nb, tl = ROW_TILE // SUBLANES, SUBLANES
    else:
        nb, tl = 1, ROW_TILE
    r = nb * tl
    n_l = seq // tl
    blk0 = row0 // r
    const = lambda shape: pl.BlockSpec(shape, lambda b, i: (0,) * len(shape))
    tile_in = pl.BlockSpec((r, width), lambda b, i: (blk0 + b * n_l + i, 0))
    kern = functools.partial(_lru_kernel, nb=nb, tl=tl)
    return pl.pallas_call(
        kern,
        out_shape=(jax.ShapeDtypeStruct((batch * seq, width), BF16),
                   jax.ShapeDtypeStruct((batch, taps - 1, width), F32),
                   jax.ShapeDtypeStruct((batch, 1, width), F32)),
        grid=(batch // nb, n_l),
        in_specs=[tile_in, tile_in,
                  pl.BlockSpec((nb, taps - 1, width), lambda b, i: (b, 0, 0)),
                  pl.BlockSpec((nb, 1, width), lambda b, i: (b, 0, 0)),
                  const((taps, width)), const((1, width)), const(wa.shape), const((1, width)),
                  const(wx.shape), const((1, width)), const((1, width))],
        out_specs=(pl.BlockSpec((r, width), lambda b, i: (b * n_l + i, 0)),
                   pl.BlockSpec((nb, taps - 1, width), lambda b, i: (b, 0, 0)),
                   pl.BlockSpec((nb, 1, width), lambda b, i: (b, 0, 0))),
        scratch_shapes=[pltpu.VMEM((nb, tl + SUBLANES, width), F32), pltpu.VMEM((nb, 1, width), F32),
                        pltpu.VMEM((r, width), F32), pltpu.VMEM((r, width), F32)],
        compiler_params=_cparams(("arbitrary", "arbitrary")),
        name="rg_lru_sample" if seq == SUBLANES else "rg_lru_prompt",
    )(rec, gate, conv0, h0, cw, cb, wa, ba, wx, bx, lam)


def kernel(x_prompt, x_sample, c_prompt, c_sample, cache_k, cache_v, state_hgrn, state_conv, state_lru, page_table,
           ada_w, ada_b, norm_mix, norm_ffn, norm_final, w_in_even, w_out_even, hg_lower_bounds, hg_norm_w, w_in_odd,
           conv_w, conv_b, lru_wa, lru_ba, lru_wx, lru_bx, lru_lambda, w_out_odd, router_group_w, router_group_b,
           router_expert_w, router_expert_b, moe_w1, moe_w3, moe_w2):
    batch, seq, d = x_prompt.shape
    dec_batch, dec_seq, _ = x_sample.shape
    depth = ada_w.shape[0]
    assert depth == 2 and seq % ROW_TILE == 0 and seq % MOBA_BLOCK == 0
    tp, ts = batch * seq, dec_batch * dec_seq
    n_pages, page = page_table.shape[1], cache_k.shape[2]
    past_len = n_pages * page
    aw = MOBA_HEADS * MOBA_HEAD_DIM
    hw = HG_HEADS * HG_DK

    xp = x_prompt.reshape(tp, d)
    xs = x_sample.reshape(ts, d)
    mods = _ada_mods(jnp.concatenate([c_prompt, c_sample], axis=0), ada_w, ada_b)
    mods = mods.reshape(depth, batch + dec_batch, 6, d)
    mods_p, mods_s = mods[:, :batch], mods[:, batch:]

    half = MOBA_HEAD_DIM // 2
    inv_freq = jnp.power(ROPE_THETA, -jnp.arange(half, dtype=F32) / half)
    invf = jnp.tile(inv_freq, LANES // half).reshape(1, LANES)

    def router_mats(l):
        rw = jnp.zeros((d, LANES), F32)
        rw = rw.at[:, 0:N_GROUPS].set(router_group_w[l])
        rw = rw.at[:, N_GROUPS:N_GROUPS + N_GROUPS * EXPERTS_PER_GROUP].set(router_expert_w[l])
        rb = jnp.zeros((1, LANES), F32)
        rb = rb.at[0, 0:N_GROUPS].set(router_group_b[l])
        rb = rb.at[0, N_GROUPS:N_GROUPS + N_GROUPS * EXPERTS_PER_GROUP].set(router_expert_b[l])
        rw_hi = rw.astype(BF16)
        rw_lo = (rw - rw_hi.astype(F32)).astype(BF16)
        return jnp.concatenate([rw_hi, rw_lo], axis=1), rb

    q_all, k_p, k_s, v_p, v_s, hg = _even_pre(
        xp, xs, mods_p[0], mods_s[0], norm_mix[0].reshape(1, d), w_in_even[0].astype(BF16), hg_lower_bounds, invf,
        seq=seq, past_len=past_len, layer_slot=0)
    oa_p = _attn_prompt(q_all, k_p, v_p, batch=batch, seq=seq)
    oa_s = _attn_sample(page_table, q_all, k_s, v_s, cache_k, cache_v,
                        layer=0, dec_batch=dec_batch, dec_seq=dec_seq, q_row0=tp)
    hnw = hg_norm_w[0].reshape(1, HG_DK)
    ob_p, hgrn_p = _hgrn_prompt(hg, hnw, batch=batch, seq=seq)
    ob_s, hgrn_s = _hgrn_sample(hg, state_hgrn[0].reshape(dec_batch * HG_HEADS, HG_DK, HG_DK), hnw,
                                dec_batch=dec_batch, dec_seq=dec_seq, row0=tp)
    w_out = w_out_even[0].astype(BF16)
    rw, rb = router_mats(0)
    hmid, u, rinfo, counts = _post_mixer((xp, xs), 0, mods_p[0], mods_s[0], [(oa_p, oa_s), (ob_p, ob_s)],
                                         [w_out[:aw], w_out[aw:]], norm_ffn[0].reshape(1, d), rw, rb, seq=seq)
    dest, ys = _moe(u, rinfo, counts, moe_w1, moe_w3, moe_w2, 0)
    h1 = _combine(dest, ys, hmid, rinfo, mods_p[0], mods_s[0], norm_final.reshape(1, d), tp=tp, seq=seq,
                  final=False)

    gate, rec = _odd_pre(h1, mods_p[1], mods_s[1], norm_mix[1].reshape(1, d), w_in_odd[0].astype(BF16), tp=tp, seq=seq)
    width = rec.shape[1]
    lru_args = (conv_w[0], conv_b[0].reshape(1, width), lru_wa[0].astype(BF16), lru_ba[0].reshape(1, width),
                lru_wx[0].astype(BF16), lru_bx[0].reshape(1, width), lru_lambda[0].reshape(1, width))
    yg_p, conv_p, lru_p = _lru(rec, gate, jnp.zeros((batch, CONV_WIDTH - 1, width), F32),
                               jnp.zeros((batch, 1, width), F32), *lru_args, batch=batch, seq=seq, row0=0)
    yg_s, conv_s, lru_s = _lru(rec, gate, state_conv[0], state_lru[0].reshape(dec_batch, 1, width), *lru_args,
                               batch=dec_batch, seq=dec_seq, row0=tp)
    rw, rb = router_mats(1)
    hmid, u, rinfo, counts = _post_mixer((h1, h1), tp // ROW_TILE, mods_p[1], mods_s[1], [(yg_p, yg_s)],
                                         [w_out_odd[0].astype(BF16)], norm_ffn[1].reshape(1, d), rw, rb, seq=seq)
    dest, ys = _moe(u, rinfo, counts, moe_w1, moe_w3, moe_w2, 1)
    y_p, y_s = _combine(dest, ys, hmid, rinfo, mods_p[1], mods_s[1], norm_final.reshape(1, d), tp=tp, seq=seq,
                        final=True)

    pos_major = lambda a: jnp.transpose(a.reshape(batch, MOBA_HEADS, MOBA_HEAD_DIM, seq), (0, 3, 1, 2))[None]
    return (y_p.reshape(batch, seq, d), y_s.reshape(dec_batch, dec_seq, d),
            pos_major(k_p), pos_major(v_p),
            k_s.reshape(1, dec_batch, dec_seq, MOBA_HEADS, MOBA_HEAD_DIM),
            v_s.reshape(1, dec_batch, dec_seq, MOBA_HEADS, MOBA_HEAD_DIM),
            hgrn_p.reshape(1, batch, HG_HEADS, HG_DK, HG_DK), hgrn_s.reshape(1, dec_batch, HG_HEADS, HG_DK, HG_DK),
            conv_p.reshape(1, batch, CONV_WIDTH - 1, width), conv_s.reshape(1, dec_batch, CONV_WIDTH - 1, width),
            lru_p.reshape(1, batch, width), lru_s.reshape(1, dec_batch, width))
```

```python
import functools

import numpy as np
import jax
import jax.numpy as jnp
from jax import lax
from jax.experimental import pallas as pl
from jax.experimental.pallas import tpu as pltpu

F32 = jnp.float32
BF16 = jnp.bfloat16
I32 = jnp.int32
HIGHEST = lax.Precision.HIGHEST

MOBA_HEADS = 8
MOBA_HEAD_DIM = 64
MOBA_BLOCK = 256
MOBA_TOPK = 3
ROPE_THETA = 10000.0
HG_HEADS = 4
HG_DK = 128
LRU_HEADS = 4
CONV_WIDTH = 4
LRU_C = 8.0
N_GROUPS = 4
EXPERTS_PER_GROUP = 4
NORM_EPS = 1e-6

LANES = 128
SUBLANES = 8
VMEM_LIMIT = 56 * 1024 * 1024

ROW_TILE = 256
MOE_TILE = 256
HG_CHUNK = 128
N_BUCKETS = N_GROUPS * 6
PAIR_LO = (0, 0, 0, 1, 1, 2)
PAIR_HI = (1, 2, 3, 2, 3, 3)
NEG = -1e30


def _cparams(sem, vmem=VMEM_LIMIT):
    return pltpu.CompilerParams(dimension_semantics=sem, vmem_limit_bytes=vmem)


def _dot(a, b, **kw):
    return jnp.dot(a, b, preferred_element_type=F32, **kw)


def _dot_nt(a, b, **kw):
    return lax.dot_general(a, b, (((1,), (1,)), ((), ())), preferred_element_type=F32, **kw)


def _sigmoid(x):
    return jax.nn.sigmoid(x)


def _by_group(i, n_prompt_tiles, fn):
    @pl.when(i < n_prompt_tiles)
    def _():
        fn(True)

    @pl.when(i >= n_prompt_tiles)
    def _():
        fn(False)


def _pair_specs(cols, npt, rows=ROW_TILE, off_p=0, off_s=0):
    return (pl.BlockSpec((rows, cols), lambda i: (off_p + jnp.minimum(i, npt - 1), 0)),
            pl.BlockSpec((rows, cols), lambda i: (off_s + jnp.maximum(i - npt, 0), 0)))


def _mod_specs(d, npt, tiles_per_seq, rows=ROW_TILE):
    return (pl.BlockSpec((1, 6, d), lambda i: (jnp.minimum(i, npt - 1) // tiles_per_seq, 0, 0)),
            pl.BlockSpec((rows // SUBLANES, 6, d), lambda i: (jnp.maximum(i - npt, 0), 0, 0)))


def _norm_mod(x, nw, m3, scale_idx, shift_idx):
    r, d = x.shape
    nb = m3.shape[0]
    var = jnp.mean(x * x, axis=-1, keepdims=True)
    y = x * lax.rsqrt(var + NORM_EPS) * nw
    y3 = y.reshape(nb, r // nb, d)
    u3 = y3 * (1.0 + m3[:, scale_idx:scale_idx + 1, :]) + m3[:, shift_idx:shift_idx + 1, :]
    return u3.reshape(r, d)


def _gate_residual(h, m3, gate_idx, out):
    r, d = h.shape
    nb = m3.shape[0]
    return (h.reshape(nb, r // nb, d) + m3[:, gate_idx:gate_idx + 1, :] * out.reshape(nb, r // nb, d)).reshape(r, d)


def _ada_kernel(c_ref, w_ref, b_ref, o_ref):
    o_ref[0] = _dot(c_ref[...], w_ref[0], precision=HIGHEST) + b_ref[0]


def _ada_mods(c_all, ada_w, ada_b):
    depth, d, n6 = ada_w.shape
    nb = c_all.shape[0]
    tn = 1024
    return pl.pallas_call(
        _ada_kernel,
        out_shape=jax.ShapeDtypeStruct((depth, nb, n6), F32),
        grid=(depth, n6 // tn),
        in_specs=[pl.BlockSpec((nb, d), lambda l, n: (0, 0)),
                  pl.BlockSpec((1, d, tn), lambda l, n: (l, 0, n)),
                  pl.BlockSpec((1, 1, tn), lambda l, n: (l, 0, n))],
        out_specs=pl.BlockSpec((1, nb, tn), lambda l, n: (l, 0, n)),
        compiler_params=_cparams(("arbitrary", "arbitrary")),
        name="ada_mods",
    )(c_all, ada_w, ada_b.reshape(depth, 1, n6))


def _even_pre_kernel(xp_ref, xs_ref, mp_ref, ms_ref, nw_ref, w_ref, hb_ref, invf_ref,
                     q_ref, kp_ref, ks_ref, vp_ref, vs_ref, hg_ref, *, npt, tiles_per_seq, past_len, layer_slot):
    i = pl.program_id(0)
    aw = MOBA_HEADS * MOBA_HEAD_DIM
    hw = HG_HEADS * HG_DK

    def run(is_prompt):
        x = (xp_ref if is_prompt else xs_ref)[...]
        m3 = (mp_ref if is_prompt else ms_ref)[...]
        r = x.shape[0]
        nb = m3.shape[0]
        tl = r // nb
        ub = _norm_mod(x, nw_ref[...], m3, 1, 0).astype(BF16)

        pos0 = (i % tiles_per_seq) * r if is_prompt else past_len
        pos = (pos0 + lax.broadcasted_iota(I32, (nb, tl, LANES), 1)).astype(F32).reshape(r, LANES)
        ang = pos * invf_ref[...]
        reps = aw // LANES
        cos = jnp.concatenate([jnp.cos(ang)] * reps, axis=1)
        sin = jnp.concatenate([jnp.sin(ang)] * reps, axis=1)
        lane = lax.broadcasted_iota(I32, (r, aw), 1)
        half = MOBA_HEAD_DIM // 2
        first = (lane % MOBA_HEAD_DIM) < half
        sin = jnp.where(first, -sin, sin)

        def rope(t):
            rot = jnp.where(first, pltpu.roll(t, aw - half, 1), pltpu.roll(t, half, 1))
            return t * cos + rot * sin

        def seg(a, b):
            return _dot(ub, w_ref[:, a:b])

        q_ref[...] = rope(seg(0, aw))
        k = rope(seg(aw, 2 * aw))
        v = seg(2 * aw, 3 * aw)
        if is_prompt:
            kp_ref[0] = k.T
            vp_ref[0] = v.T
        else:
            ks_ref[...] = k
            vs_ref[...] = v
        o = 3 * aw
        qb = seg(o, o + hw)
        fb = seg(o + hw, o + 2 * hw)
        hb = hb_ref[...]
        e = jnp.exp(hb - jnp.max(hb, axis=0, keepdims=True))
        sm = e / jnp.sum(e, axis=0, keepdims=True)
        lb = jnp.sum(sm[0:layer_slot + 1, :], axis=0, keepdims=True)
        hg_ref[:, 0:hw] = qb * _sigmoid(qb)
        hg_ref[:, hw:2 * hw] = lb + (1.0 - lb) * _sigmoid(fb)
        hg_ref[:, 2 * hw:3 * hw] = seg(o + 2 * hw, o + 3 * hw)
        hg_ref[:, 3 * hw:4 * hw] = seg(o + 3 * hw, o + 4 * hw)

    _by_group(i, npt, run)


def _even_pre(xp, xs, mp, ms, nw, w_in, hb, invf, *, seq, past_len, layer_slot):
    tp, d = xp.shape
    ts = xs.shape[0]
    npt, nst = tp // ROW_TILE, ts // ROW_TILE
    tiles_per_seq = seq // ROW_TILE
    aw = MOBA_HEADS * MOBA_HEAD_DIM
    hw = HG_HEADS * HG_DK
    t = tp + ts
    const = lambda shape: pl.BlockSpec(shape, lambda i: (0,) * len(shape))
    kern = functools.partial(_even_pre_kernel, npt=npt, tiles_per_seq=tiles_per_seq, past_len=past_len,
                             layer_slot=layer_slot)
    _, ks_spec = _pair_specs(aw, npt)
    kp_spec = pl.BlockSpec((1, aw, ROW_TILE), lambda i: (jnp.minimum(i, npt - 1) // tiles_per_seq, 0,
                                                          jnp.minimum(i, npt - 1) % tiles_per_seq))
    return pl.pallas_call(
        kern,
        out_shape=(jax.ShapeDtypeStruct((t, aw), F32),
                   jax.ShapeDtypeStruct((tp // seq, aw, seq), F32), jax.ShapeDtypeStruct((ts, aw), F32),
                   jax.ShapeDtypeStruct((tp // seq, aw, seq), F32), jax.ShapeDtypeStruct((ts, aw), F32),
                   jax.ShapeDtypeStruct((t, 4 * hw), F32)),
        grid=(npt + nst,),
        in_specs=[*_pair_specs(d, npt), *_mod_specs(d, npt, tiles_per_seq),
                  const((1, d)), const(w_in.shape), const(hb.shape), const((1, LANES))],
        out_specs=(pl.BlockSpec((ROW_TILE, aw), lambda i: (i, 0)), kp_spec, ks_spec, kp_spec, ks_spec,
                   pl.BlockSpec((ROW_TILE, 4 * hw), lambda i: (i, 0))),
        compiler_params=_cparams(("arbitrary",)),
        name="even_pre",
    )(xp, xs, mp, ms, nw, w_in, hb, invf)


def _top_blocks(g, valid):
    lane = lax.broadcasted_iota(I32, g.shape, 1)
    jl = lane % SUBLANES
    g = jnp.where(valid, g, -jnp.inf)
    cnt = jnp.zeros(g.shape, I32)
    for d in range(1, SUBLANES):
        wrapped = (jl + d) >= SUBLANES
        pg = jnp.where(wrapped, pltpu.roll(g, SUBLANES - d, 1), pltpu.roll(g, LANES - d, 1))
        beats = (pg > g) | ((pg == g) & wrapped)
        cnt = cnt + beats.astype(I32)
    return valid & (cnt < MOBA_TOPK)


def _head_expand(rows8, n_heads, width):
    x = jnp.concatenate([rows8] * n_heads, axis=0)
    r = lax.broadcasted_iota(I32, x.shape, 0) // SUBLANES
    l = lax.broadcasted_iota(I32, x.shape, 1) // (width // n_heads)
    return jnp.where(r == l, x, 0.0)


def _block_indicator(seq):
    key_blk = np.arange(seq)[None, :] // MOBA_BLOCK
    row = np.arange(LANES)[:, None]
    ind = (row < MOBA_HEADS * SUBLANES) & ((row % SUBLANES) == key_blk)
    return jnp.asarray(ind.astype(np.float32), dtype=BF16)


def _attn_prompt_kernel(q_ref, k_ref, v_ref, ind_ref, o_ref, km_ref, *, n_blk):
    i = pl.program_id(1)
    blk = MOBA_BLOCK
    aw = MOBA_HEADS * MOBA_HEAD_DIM
    scale = MOBA_HEAD_DIM ** -0.5

    @pl.when(i == 0)
    def _():
        r = lax.broadcasted_iota(I32, (aw, LANES), 0) // MOBA_HEAD_DIM
        c = lax.broadcasted_iota(I32, (aw, LANES), 1)
        km = jnp.zeros((aw, LANES), F32)
        for j in range(n_blk):
            mean_j = jnp.mean(k_ref[0, :, j * blk:(j + 1) * blk], axis=1, keepdims=True)
            km = jnp.where(((c % SUBLANES) == j) & ((c // SUBLANES) == r), mean_j, km)
        km_ref[...] = km

    q = q_ref[...]
    gate = _dot(q, km_ref[...], precision=HIGHEST)
    lane = lax.broadcasted_iota(I32, gate.shape, 1)
    jl = lane % SUBLANES
    valid = (jl < i) & (lane < MOBA_HEADS * SUBLANES)
    keep = _top_blocks(gate, valid) | (jl == i)
    bias = jnp.where(keep, 0.0, NEG)

    row = lax.broadcasted_iota(I32, (blk, blk), 0)
    col = lax.broadcasted_iota(I32, (blk, blk), 1)
    tril = row >= col

    def tile(c):
        n_keys = (c + 1) * blk
        for hp in range(aw // LANES):
            cols = slice(hp * LANES, (hp + 1) * LANES)
            rhs = jnp.concatenate([k_ref[0, cols, 0:n_keys].astype(BF16), ind_ref[:, 0:n_keys]], axis=0)
            vv = v_ref[0, cols, 0:n_keys].astype(BF16)
            qp = q[:, cols] * scale
            outs = []
            for s in (0, 1):
                h = 2 * hp + s
                qm = jnp.where((lane // MOBA_HEAD_DIM) == s, qp, 0.0).astype(BF16)
                hb = jnp.where((lane // SUBLANES) == h, bias, 0.0).astype(BF16)
                sc = _dot(jnp.concatenate([qm, hb], axis=1), rhs)
                own = jnp.where(tril, sc[:, c * blk:], NEG)
                sc = own if c == 0 else jnp.concatenate([sc[:, :c * blk], own], axis=1)
                p = jnp.exp(sc - jnp.max(sc, axis=1, keepdims=True))
                outs.append(_dot_nt(p.astype(BF16), vv) / jnp.sum(p, axis=1, keepdims=True))
            o_ref[:, cols] = jnp.where((lane // MOBA_HEAD_DIM) == 0, outs[0], outs[1]).astype(BF16)

    for c in range(n_blk):
        pl.when(i == c)(functools.partial(tile, c))


def _attn_prompt(q_all, k_p, v_p, *, batch, seq):
    aw = MOBA_HEADS * MOBA_HEAD_DIM
    n_blk = seq // MOBA_BLOCK
    assert n_blk <= SUBLANES
    ind = _block_indicator(seq)
    kern = functools.partial(_attn_prompt_kernel, n_blk=n_blk)
    return pl.pallas_call(
        kern,
        out_shape=jax.ShapeDtypeStruct((batch * seq, aw), BF16),
        grid=(batch, n_blk),
        in_specs=[pl.BlockSpec((MOBA_BLOCK, aw), lambda b, i: (b * n_blk + i, 0)),
                  pl.BlockSpec((1, aw, seq), lambda b, i: (b, 0, 0)),
                  pl.BlockSpec((1, aw, seq), lambda b, i: (b, 0, 0)),
                  pl.BlockSpec((LANES, seq), lambda b, i: (0, 0))],
        out_specs=pl.BlockSpec((MOBA_BLOCK, aw), lambda b, i: (b * n_blk + i, 0)),
        scratch_shapes=[pltpu.VMEM((aw, LANES), F32)],
        compiler_params=_cparams(("arbitrary", "arbitrary")),
        name="moba_prompt",
    )(q_all, k_p, v_p, ind)


def _attn_sample_kernel(pt_ref, q_ref, kn_ref, vn_ref, ck_ref, cv_ref, o_ref, kbuf, vbuf, sem,
                        *, n_pages, page, n_blk, layer):
    b = pl.program_id(0)
    nb = pl.num_programs(0)
    blk = MOBA_BLOCK
    hd = MOBA_HEAD_DIM
    dec = q_ref.shape[0]
    scale = hd ** -0.5

    def page_copies(seq_idx, slot):
        cps = []
        for p in range(n_pages):
            pg = pt_ref[seq_idx * n_pages + p]
            cps.append(pltpu.make_async_copy(ck_ref.at[layer, pg], kbuf.at[slot, :, :, pl.ds(p * page, page)],
                                             sem.at[0, slot]))
            cps.append(pltpu.make_async_copy(cv_ref.at[layer, pg], vbuf.at[slot, :, :, pl.ds(p * page, page)],
                                             sem.at[1, slot]))
        return cps

    @pl.when(b == 0)
    def _():
        for c in page_copies(0, 0):
            c.start()

    @pl.when(b + 1 < nb)
    def _():
        for c in page_copies(b + 1, (b + 1) % 2):
            c.start()

    slot = b % 2
    for c in page_copies(b, slot):
        c.wait()

    nh = MOBA_HEADS
    aw = nh * hd
    past = n_blk * blk
    kt = kbuf[slot].reshape(aw, past)
    vt = vbuf[slot].reshape(aw, past)
    qexp = _head_expand(q_ref[...], nh, aw)

    colj = lax.broadcasted_iota(I32, (aw, LANES), 1)
    km = jnp.zeros((aw, LANES), F32)
    for j in range(n_blk):
        km = jnp.where(colj == j, jnp.mean(kt[:, j * blk:(j + 1) * blk], axis=1, keepdims=True), km)
    gate = _dot(qexp, km, precision=HIGHEST)
    lane = lax.broadcasted_iota(I32, gate.shape, 1)
    sel = _top_blocks(gate, lane < n_blk).astype(F32)

    qs = (qexp * scale).astype(BF16)
    sc = _dot(qs, kt.astype(BF16))
    key_blk = lax.broadcasted_iota(I32, sc.shape, 1) // blk
    mask = jnp.zeros(sc.shape, F32)
    for j in range(n_blk):
        mask = jnp.where(key_blk == j, sel[:, j:j + 1], mask)
    sc = jnp.where(mask > 0.5, sc, NEG)

    pad = jnp.zeros((LANES - dec, aw), F32)
    kn = jnp.concatenate([kn_ref[...], pad], axis=0).astype(BF16)
    vn = jnp.concatenate([vn_ref[...], pad], axis=0).astype(BF16)
    qi = lax.broadcasted_iota(I32, gate.shape, 0) % dec
    sn = jnp.where(lane <= qi, _dot_nt(qs, kn), NEG)

    m = jnp.maximum(jnp.max(sc, axis=1, keepdims=True), jnp.max(sn, axis=1, keepdims=True))
    p = jnp.exp(sc - m)
    pn = jnp.exp(sn - m)
    l = jnp.sum(p, axis=1, keepdims=True) + jnp.sum(pn, axis=1, keepdims=True)
    o = (_dot_nt(p.astype(BF16), vt.astype(BF16)) + _dot(pn.astype(BF16), vn)) / l
    r = lax.broadcasted_iota(I32, o.shape, 0) // dec
    c = lax.broadcasted_iota(I32, o.shape, 1) // hd
    o = jnp.where(r == c, o, 0.0)
    acc = o[0:dec]
    for h in range(1, nh):
        acc = acc + o[h * dec:(h + 1) * dec]
    o_ref[...] = acc


def _attn_sample(page_table, q_all, k_s, v_s, cache_k, cache_v, *, layer, dec_batch, dec_seq, q_row0):
    _, n_phys, page, n_heads, hd = cache_k.shape
    aw = MOBA_HEADS * MOBA_HEAD_DIM
    n_pages = page_table.shape[1]
    past_len = n_pages * page
    assert past_len % MOBA_BLOCK == 0 and dec_seq == SUBLANES and past_len // MOBA_BLOCK == SUBLANES
    assert n_heads == MOBA_HEADS and hd == MOBA_HEAD_DIM
    kern = functools.partial(_attn_sample_kernel, n_pages=n_pages, page=page, n_blk=past_len // MOBA_BLOCK,
                             layer=layer)
    q_blk0 = q_row0 // dec_seq
    grid_spec = pltpu.PrefetchScalarGridSpec(
        num_scalar_prefetch=1,
        grid=(dec_batch,),
        in_specs=[pl.BlockSpec((dec_seq, aw), lambda b, pt: (q_blk0 + b, 0)),
                  pl.BlockSpec((dec_seq, aw), lambda b, pt: (b, 0)),
                  pl.BlockSpec((dec_seq, aw), lambda b, pt: (b, 0)),
                  pl.BlockSpec(memory_space=pl.ANY),
                  pl.BlockSpec(memory_space=pl.ANY)],
        out_specs=pl.BlockSpec((dec_seq, aw), lambda b, pt: (b, 0)),
        scratch_shapes=[pltpu.VMEM((2, n_heads, hd, past_len), F32), pltpu.VMEM((2, n_heads, hd, past_len), F32),
                        pltpu.SemaphoreType.DMA((2, 2))],
    )
    to_pos_minor = lambda c: jnp.transpose(c, (0, 1, 3, 4, 2))
    return pl.pallas_call(
        kern,
        out_shape=jax.ShapeDtypeStruct((dec_batch * dec_seq, aw), F32),
        grid_spec=grid_spec,
        compiler_params=_cparams(("arbitrary",)),
        name="moba_sample",
    )(page_table.reshape(-1), q_all, k_s, v_s, to_pos_minor(cache_k), to_pos_minor(cache_v))


def _hgrn_levels():
    n = HG_CHUNK
    return int(np.log2(n))


def _hgrn_cumsum_matrix():
    return jnp.asarray(np.tril(np.ones((HG_CHUNK, HG_CHUNK), np.float32)), dtype=BF16)


def _separator_rows(b, level):
    c, dk = b.shape
    m = 2 ** level
    if 2 * m >= SUBLANES:
        b3 = b.reshape(c // (2 * m), 2 * m, dk)
        return jnp.broadcast_to(b3[:, m - 1:m, :], b3.shape).reshape(c, dk)
    b3 = b.reshape(c // SUBLANES, SUBLANES, dk)
    sub = lax.broadcasted_iota(I32, b3.shape, 1)
    if m == 1:
        out = jnp.where(sub % 2 == 1, pltpu.roll(b3, 1, 1), b3)
    else:
        out = jnp.where(sub < 4, b3[:, 1:2, :], b3[:, 5:6, :])
    return out.reshape(c, dk)


def _hgrn_post(o, g, nw):
    o = o * lax.rsqrt(jnp.mean(o * o, axis=-1, keepdims=True) + NORM_EPS) * nw
    return o * (g * _sigmoid(g))


def _hgrn_prompt_kernel(q_ref, f_ref, i_ref, g_ref, ms_ref, nw_ref, o_ref, s_ref):
    c = HG_CHUNK
    dk = HG_DK
    n_chunks = q_ref.shape[0] // c
    levels = _hgrn_levels()
    row = lax.broadcasted_iota(I32, (c, c), 0)
    col = lax.broadcasted_iota(I32, (c, c), 1)
    rowk = lax.broadcasted_iota(I32, (c, dk), 0)
    upper = [((rowk // (2 ** l)) % 2) == 1 for l in range(levels)]
    same = [(row // (2 ** (l + 1))) == (col // (2 ** (l + 1))) for l in range(levels)]
    diag = row == col
    nw = nw_ref[...]

    heads = q_ref.shape[1] // dk

    def chunk(ci, sts):
        r0 = pl.multiple_of(ci * c, c)
        return tuple(one_head(r0, slice(hh * dk, (hh + 1) * dk), sts[hh]) for hh in range(heads))

    def one_head(r0, cols, st):
        q = q_ref[pl.ds(r0, c), cols]
        f = f_ref[pl.ds(r0, c), cols]
        v = i_ref[pl.ds(r0, c), cols]
        g = g_ref[pl.ds(r0, c), cols]
        lf = jnp.log(f)
        k = 1.0 - f
        hi = lf.astype(BF16)
        r1 = lf - hi.astype(F32)
        mid = r1.astype(BF16)
        lo = (r1 - mid.astype(F32)).astype(BF16)
        bb = _dot(ms_ref[...], jnp.concatenate([hi, mid, lo], axis=1))
        b = bb[:, 0:dk] + bb[:, dk:2 * dk] + bb[:, 2 * dk:3 * dk]
        a = jnp.where(diag, jnp.sum(q * k, axis=1, keepdims=True), 0.0)
        for l in range(levels):
            bs = _separator_rows(b, l)
            e = jnp.exp(jnp.where(upper[l], b - bs, bs - b))
            qp = jnp.where(upper[l], q * e, 0.0).astype(BF16)
            kp = jnp.where(upper[l], 0.0, k * e).astype(BF16)
            a = a + jnp.where(same[l], _dot_nt(qp, kp), 0.0)
        vb = v.astype(BF16)
        o = _dot_nt((q * jnp.exp(b)).astype(BF16), st.astype(BF16)) + _dot(a.astype(BF16), vb)
        blast = b[c - 1:c, :]
        kl = (k * jnp.exp(blast - b)).astype(BF16)
        st_new = st * jnp.exp(blast) + _dot(v.T.astype(BF16), kl)
        o_ref[pl.ds(r0, c), cols] = _hgrn_post(o, g, nw).astype(BF16)
        return st_new

    sts = lax.fori_loop(0, n_chunks, chunk, tuple(jnp.zeros((dk, dk), F32) for _ in range(heads)))
    for hh in range(heads):
        s_ref[hh] = sts[hh].T


def _hgrn_prompt(hg, nw, *, batch, seq):
    hw = HG_HEADS * HG_DK
    mstack = _hgrn_cumsum_matrix()
    hps = HG_HEADS
    groups = HG_HEADS // hps
    spec = lambda off: pl.BlockSpec((seq, hps * HG_DK), lambda b, h: (b, off + h))
    return pl.pallas_call(
        _hgrn_prompt_kernel,
        out_shape=(jax.ShapeDtypeStruct((batch * seq, hw), BF16),
                   jax.ShapeDtypeStruct((batch * HG_HEADS, HG_DK, HG_DK), F32)),
        grid=(batch, groups),
        in_specs=[spec(0), spec(groups), spec(2 * groups), spec(3 * groups),
                  pl.BlockSpec(mstack.shape, lambda b, h: (0, 0)),
                  pl.BlockSpec((1, HG_DK), lambda b, h: (0, 0))],
        out_specs=(pl.BlockSpec((seq, hps * HG_DK), lambda b, h: (b, h)),
                   pl.BlockSpec((hps, HG_DK, HG_DK), lambda b, h: (b * groups + h, 0, 0))),
        compiler_params=_cparams(("arbitrary", "arbitrary")),
        name="hgrn_prompt",
    )(hg, hg, hg, hg, mstack, nw)


def _hgrn_sample_kernel(hg_ref, s0_ref, nw_ref, o_ref, s_ref, *, dec, seqs):
    dk = HG_DK
    hw = HG_HEADS * HG_DK
    nw = nw_ref[...]
    trow = lax.broadcasted_iota(I32, (dec, dk), 0)
    srow = lax.broadcasted_iota(I32, (dk, dk), 0)
    zrows = jnp.zeros((dec, dk), F32)
    pad16 = lambda a: jnp.concatenate([a, zrows], axis=0).astype(BF16)
    dot_tn = lambda a, b: lax.dot_general(a, b, (((0,), (0,)), ((), ())), preferred_element_type=F32)

    def one_seq(si, carry):
        r0 = pl.multiple_of(si * dec, dec)
        for h in range(HG_HEADS):
            cols = slice(h * dk, (h + 1) * dk)
            q = hg_ref[pl.ds(r0, dec), h * dk:(h + 1) * dk]
            f = hg_ref[pl.ds(r0, dec), hw + h * dk:hw + (h + 1) * dk]
            v = hg_ref[pl.ds(r0, dec), 2 * hw + h * dk:2 * hw + (h + 1) * dk]
            g = hg_ref[pl.ds(r0, dec), 3 * hw + h * dk:3 * hw + (h + 1) * dk]
            k = 1.0 - f
            b = jnp.log(f)
            for dlt in (1, 2, 4):
                b = b + jnp.where(trow >= dlt, pltpu.roll(b, dlt, 0), 0.0)
            s0 = s0_ref[si * HG_HEADS + h]
            o = _dot(pad16(q * jnp.exp(b)), s0.astype(BF16))[0:dec]
            for s in range(dec):
                live = trow >= s
                e = jnp.exp(jnp.where(live, b - b[s:s + 1, :], 0.0))
                a_s = jnp.sum(jnp.where(live, q * e * k[s:s + 1, :], 0.0), axis=1, keepdims=True)
                o = o + a_s * v[s:s + 1, :]
            o_ref[pl.ds(r0, dec), cols] = _hgrn_post(o, g, nw)
            blast = b[dec - 1:dec, :]
            upd = dot_tn(pad16(k * jnp.exp(blast - b)), pad16(v))
            ecol = jnp.where(srow == 0, jnp.exp(blast), 0.0).T[:, 0:1]
            s_ref[si * HG_HEADS + h] = ecol * s0 + upd
        return carry

    lax.fori_loop(0, seqs, one_seq, 0)


def _hgrn_sample(hg, s0, nw, *, dec_batch, dec_seq, row0):
    hw = HG_HEADS * HG_DK
    seqs = 8
    rows = seqs * dec_seq
    kern = functools.partial(_hgrn_sample_kernel, dec=dec_seq, seqs=seqs)
    st_spec = pl.BlockSpec((seqs * HG_HEADS, HG_DK, HG_DK), lambda i: (i, 0, 0))
    return pl.pallas_call(
        kern,
        out_shape=(jax.ShapeDtypeStruct((dec_batch * dec_seq, hw), F32),
                   jax.ShapeDtypeStruct((dec_batch * HG_HEADS, HG_DK, HG_DK), F32)),
        grid=(dec_batch // seqs,),
        in_specs=[pl.BlockSpec((rows, 4 * hw), lambda i: (row0 // rows + i, 0)), st_spec,
                  pl.BlockSpec((1, HG_DK), lambda i: (0, 0))],
        out_specs=(pl.BlockSpec((rows, hw), lambda i: (i, 0)), st_spec),
        compiler_params=_cparams(("arbitrary",)),
        name="hgrn_sample",
    )(hg, s0, nw)


def _post_mixer_kernel(*refs, n_a, npt):
    hp_ref, hs_ref, mp_ref, ms_ref = refs[0:4]
    a_refs = refs[4:4 + 2 * n_a]
    w_refs = refs[4 + 2 * n_a:4 + 3 * n_a]
    nffn_ref, rw_ref, rb_ref, striu_ref = refs[4 + 3 * n_a:8 + 3 * n_a]
    hmid_ref, u_ref, rinfo_ref, counts_ref, carry = refs[8 + 3 * n_a:]
    i = pl.program_id(0)

    @pl.when(i == 0)
    def _():
        carry[...] = jnp.zeros_like(carry)

    def run(is_prompt):
        h = (hp_ref if is_prompt else hs_ref)[...]
        m3 = (mp_ref if is_prompt else ms_ref)[...]
        out = None
        for a in range(n_a):
            av = a_refs[2 * a + (0 if is_prompt else 1)][...].astype(BF16)
            t = _dot(av, w_refs[a][...])
            out = t if out is None else out + t
        hm = _gate_residual(h, m3, 2, out)
        hmid_ref[...] = hm
        u = _norm_mod(hm, nffn_ref[...], m3, 4, 3)
        u_ref[...] = u.reshape(u_ref.shape)

        u_hi = u.astype(BF16)
        u_lo = (u - u_hi.astype(F32)).astype(BF16)
        hh = _dot(u_hi, rw_ref[...])
        logits = hh[:, 0:LANES] + hh[:, LANES:2 * LANES] + _dot(u_lo, rw_ref[:, 0:LANES]) + rb_ref[...]
        lt = logits.T
        col = lambda j: lt[j:j + 1, :]
        gl = [col(j) for j in range(N_GROUPS)]
        gmax = functools.reduce(jnp.maximum, gl)
        gsum = functools.reduce(lambda x, y: x + y, [jnp.exp(x - gmax) for x in gl])
        gval = 1.0 / gsum
        gidx = jnp.where(gl[0] == gmax, 0, jnp.where(gl[1] == gmax, 1, jnp.where(gl[2] == gmax, 2, 3)))
        el = []
        for k in range(EXPERTS_PER_GROUP):
            c = [col(N_GROUPS + g * EXPERTS_PER_GROUP + k) for g in range(N_GROUPS)]
            el.append(jnp.where(gidx == 0, c[0], jnp.where(gidx == 1, c[1], jnp.where(gidx == 2, c[2], c[3]))))
        emax = functools.reduce(jnp.maximum, el)
        pe = [jnp.exp(x - emax) for x in el]
        esum = functools.reduce(lambda x, y: x + y, pe)
        pk = [x / esum for x in pe]
        v1 = functools.reduce(jnp.maximum, pk)
        i1 = jnp.where(pk[0] == v1, 0, jnp.where(pk[1] == v1, 1, jnp.where(pk[2] == v1, 2, 3)))
        pk2 = [jnp.where(i1 == k, -1.0, pk[k]) for k in range(EXPERTS_PER_GROUP)]
        v2 = functools.reduce(jnp.maximum, pk2)
        i2 = jnp.where(pk2[0] == v2, 0, jnp.where(pk2[1] == v2, 1, jnp.where(pk2[2] == v2, 2, 3)))
        den = v1 + v2
        w1 = gval * v1 / den
        w2 = gval * v2 / den
        lo = jnp.minimum(i1, i2)
        hi = jnp.maximum(i1, i2)
        pair = jnp.where(lo == 0, hi - 1, jnp.where(lo == 1, hi + 1, 5))
        bucket = gidx * 6 + pair
        w_lo = jnp.where(i1 < i2, w1, w2)
        w_hi = jnp.where(i1 < i2, w2, w1)

        r = h.shape[0]
        sub = lax.broadcasted_iota(I32, (LANES, r), 0)
        onehot = sub == bucket
        before = _dot(onehot.astype(BF16), striu_ref[...]) + carry[:, 0:1]
        rank = jnp.sum(jnp.where(onehot, before, 0.0), axis=0, keepdims=True)
        carry[...] = carry[...] + jnp.sum(onehot.astype(F32), axis=1, keepdims=True)
        info = jnp.where(sub == 0, bucket.astype(F32),
                         jnp.where(sub == 1, w_lo, jnp.where(sub == 2, w_hi, jnp.where(sub == 3, rank, 0.0))))
        rinfo_ref[...] = info.T
        counts_ref[...] = carry[...]

    _by_group(i, npt, run)


def _post_mixer(h_pair, h_off_s, mp, ms, a_pairs, ws, nffn, rw, rb, *, seq):
    hp, hs = h_pair
    d = hp.shape[1]
    tp = a_pairs[0][0].shape[0]
    ts = a_pairs[0][1].shape[0]
    npt, nst = tp // ROW_TILE, ts // ROW_TILE
    t = tp + ts
    n_a = len(a_pairs)
    tiles_per_seq = seq // ROW_TILE
    const = lambda shape: pl.BlockSpec(shape, lambda i: (0,) * len(shape))
    stril = jnp.asarray(np.triu(np.ones((ROW_TILE, ROW_TILE), np.float32), 1), dtype=BF16)
    in_specs = [*_pair_specs(d, npt, off_s=h_off_s), *_mod_specs(d, npt, tiles_per_seq)]
    args = [hp, hs, mp, ms]
    for ap, as_ in a_pairs:
        in_specs += list(_pair_specs(ap.shape[1], npt))
        args += [ap, as_]
    for w in ws:
        in_specs.append(const(w.shape))
        args.append(w)
    in_specs += [const((1, d)), const(rw.shape), const((1, LANES)), const(stril.shape)]
    args += [nffn, rw, rb, stril]
    tile = lambda c: pl.BlockSpec((ROW_TILE, c), lambda i: (i, 0))
    kern = functools.partial(_post_mixer_kernel, n_a=n_a, npt=npt)
    return pl.pallas_call(
        kern,
        out_shape=(jax.ShapeDtypeStruct((t, d), F32), jax.ShapeDtypeStruct((t, d // LANES, LANES), F32),
                   jax.ShapeDtypeStruct((t, LANES), F32), jax.ShapeDtypeStruct((LANES, LANES), F32)),
        grid=(npt + nst,),
        in_specs=in_specs,
        out_specs=(tile(d), pl.BlockSpec((ROW_TILE, d // LANES, LANES), lambda i: (i, 0, 0)), tile(LANES),
                   const((LANES, LANES))),
        scratch_shapes=[pltpu.VMEM((LANES, LANES), F32)],
        compiler_params=_cparams(("arbitrary",)),
        name="post_mixer",
    )(*args)


def _for_rows(rows, fn):
    def body(r, c):
        fn(r)
        return c

    lax.fori_loop(0, rows, body, 0, unroll=8)


def _slot_of(bucket_ref, rank_ref, starts_ref, t):
    return starts_ref[bucket_ref[t]] + rank_ref[t]


def _scatter_kernel(bucket_ref, rank_ref, starts_ref, src_ref, init_ref, o_ref, sem):
    del init_ref
    rows = src_ref.shape[0]
    base = pl.program_id(0) * rows
    copy = lambda r: pltpu.make_async_copy(
        src_ref.at[r], o_ref.at[_slot_of(bucket_ref, rank_ref, starts_ref, base + r)], sem)
    _for_rows(rows, lambda r: copy(r).start())
    _for_rows(rows, lambda r: copy(r).wait())


def _scatter_tokens(slots, src, n_slots):
    t = src.shape[0]
    rows = 4 * ROW_TILE if t % (4 * ROW_TILE) == 0 else ROW_TILE
    grid_spec = pltpu.PrefetchScalarGridSpec(
        num_scalar_prefetch=3,
        grid=(t // rows,),
        in_specs=[pl.BlockSpec((rows,) + src.shape[1:], lambda i, *_: (i, 0, 0)), pl.BlockSpec(memory_space=pl.ANY)],
        out_specs=pl.BlockSpec(memory_space=pl.ANY),
        scratch_shapes=[pltpu.SemaphoreType.DMA(())],
    )
    return pl.pallas_call(
        _scatter_kernel,
        out_shape=jax.ShapeDtypeStruct((n_slots,) + src.shape[1:], src.dtype),
        grid_spec=grid_spec,
        input_output_aliases={4: 0},
        compiler_params=_cparams(("arbitrary",)),
        name="moe_scatter",
    )(*slots, src, jnp.zeros((n_slots,) + src.shape[1:], src.dtype))


def _moe_kernel(ea_ref, eb_ref, valid_ref, fresh_ref, slot_ref, na_ref, nb_ref, more_ref, x_ref,
                w1_ref, w3_ref, w2_ref, o_ref, wf1, wf3, wf2, wbuf1, wbuf3, wbuf2, sem, *, layer):
    i = pl.program_id(0)
    nk = x_ref.shape[1]

    def weight_copies(e_a, e_b, slot):
        cps = []
        for s, e in ((0, e_a), (1, e_b)):
            for src, dst in ((w1_ref, wf1), (w3_ref, wf3), (w2_ref, wf2)):
                cps.append(pltpu.make_async_copy(src.at[layer, e], dst.at[slot, s], sem.at[slot]))
        return cps

    @pl.when(i == 0)
    def _():
        for c in weight_copies(ea_ref[0], eb_ref[0], slot_ref[0]):
            c.start()

    @pl.when(fresh_ref[i] == 1)
    def _():
        slot = slot_ref[i]
        for c in weight_copies(ea_ref[i], eb_ref[i], slot):
            c.wait()

        @pl.when(more_ref[i] == 1)
        def _():
            for c in weight_copies(na_ref[i], nb_ref[i], 1 - slot):
                c.start()

        for s in (0, 1):
            wbuf1[s] = wf1[slot, s].astype(BF16)
            wbuf3[s] = wf3[slot, s].astype(BF16)
            wbuf2[s] = wf2[slot, s].astype(BF16)

    @pl.when(valid_ref[i] == 1)
    def _():
        xb = x_ref[...].reshape(x_ref.shape[0], nk * LANES).astype(BF16)
        for s in (0, 1):
            h1 = _dot(xb, wbuf1[s])
            h3 = _dot(xb, wbuf3[s])
            y = _dot(((h1 * _sigmoid(h1)) * h3).astype(BF16), wbuf2[s])
            o_ref[:, s * nk:(s + 1) * nk, :] = y.reshape(y.shape[0], nk, LANES)

    @pl.when(valid_ref[i] == 0)
    def _():
        o_ref[...] = jnp.zeros_like(o_ref)


def _moe_experts(plan, xs, w1, w3, w2, layer):
    n_slots, nk, _ = xs.shape
    d = nk * LANES
    ff = w1.shape[-1]
    rows = lambda k: pl.BlockSpec((MOE_TILE, k, LANES), lambda i, *_: (i, 0, 0))
    hbm = pl.BlockSpec(memory_space=pl.ANY)
    grid_spec = pltpu.PrefetchScalarGridSpec(
        num_scalar_prefetch=len(plan),
        grid=(n_slots // MOE_TILE,),
        in_specs=[rows(nk), hbm, hbm, hbm],
        out_specs=rows(2 * nk),
        scratch_shapes=[pltpu.VMEM((2, 2, d, ff), F32), pltpu.VMEM((2, 2, d, ff), F32), pltpu.VMEM((2, 2, ff, d), F32),
                        pltpu.VMEM((2, d, ff), BF16), pltpu.VMEM((2, d, ff), BF16), pltpu.VMEM((2, ff, d), BF16),
                        pltpu.SemaphoreType.DMA((2,))],
    )
    return pl.pallas_call(
        functools.partial(_moe_kernel, layer=layer),
        out_shape=jax.ShapeDtypeStruct((n_slots, 2 * nk, LANES), F32),
        grid_spec=grid_spec,
        compiler_params=_cparams(("arbitrary",)),
        name="moe_experts",
    )(*plan, xs, w1, w3, w2)


def _moe_plan(rinfo, counts, n_slots):
    n_tiles = n_slots // MOE_TILE
    cnt = counts[:N_BUCKETS, 0].astype(I32)
    padded = ((cnt + MOE_TILE - 1) // MOE_TILE) * MOE_TILE
    ends = jnp.cumsum(padded)
    starts = ends - padded
    slots = (rinfo[:, 0].astype(I32), rinfo[:, 3].astype(I32), starts)
    n_valid = ends[-1] // MOE_TILE
    tiles = jnp.arange(n_tiles, dtype=I32)
    tb = jnp.sum((ends[None, :] <= (tiles * MOE_TILE)[:, None]).astype(I32), axis=1)
    tb = jnp.minimum(tb, N_BUCKETS - 1)
    valid = tiles < n_valid
    tb = jnp.where(valid, tb, tb[jnp.maximum(n_valid - 1, 0)])
    lo = jnp.asarray(PAIR_LO, I32)
    hi = jnp.asarray(PAIR_HI, I32)
    ea = (tb // 6) * EXPERTS_PER_GROUP + lo[tb % 6]
    eb = (tb // 6) * EXPERTS_PER_GROUP + hi[tb % 6]
    fresh = jnp.concatenate([jnp.ones((1,), I32), (tb[1:] != tb[:-1]).astype(I32)])
    run_slot = (jnp.cumsum(fresh) - 1) % 2
    later_start = (tiles[None, :] > tiles[:, None]) & (fresh[None, :] == 1)
    nxt = jnp.min(jnp.where(later_start, tiles[None, :], n_tiles), axis=1)
    more = (nxt < n_tiles).astype(I32)
    nxt = jnp.minimum(nxt, n_tiles - 1)
    plan = (ea, eb, valid.astype(I32), fresh, run_slot.astype(I32), ea[nxt], eb[nxt], more)
    return slots, plan


def _moe(u_tiles, rinfo, counts, w1, w3, w2, layer):
    t = u_tiles.shape[0]
    n_slots = t + N_BUCKETS * MOE_TILE
    slots, plan = _moe_plan(rinfo, counts, n_slots)
    xs = _scatter_tokens(slots, u_tiles, n_slots)
    ys = _moe_experts(plan, xs, w1, w3, w2, layer)
    return slots, ys


def _combine_kernel(bucket_ref, rank_ref, starts_ref, ys_ref, hm_ref, rinfo_ref, mp_ref, ms_ref, nf_ref, *rest,
                    npt, final):
    if final:
        yp_ref, ysm_ref, gbuf, sem = rest
    else:
        h_ref, gbuf, sem = rest
    i = pl.program_id(0)
    n = pl.num_programs(0)
    rows = gbuf.shape[1]
    nk = gbuf.shape[2] // 2

    def copy(tile, r):
        slot = tile % 2
        src = ys_ref.at[_slot_of(bucket_ref, rank_ref, starts_ref, tile * rows + r)]
        return pltpu.make_async_copy(src, gbuf.at[slot, r], sem.at[slot])

    @pl.when(i == 0)
    def _():
        _for_rows(rows, lambda r: copy(0, r).start())

    @pl.when(i + 1 < n)
    def _():
        _for_rows(rows, lambda r: copy(i + 1, r).start())

    _for_rows(rows, lambda r: copy(i, r).wait())
    slot = i % 2
    ya = gbuf[slot, :, 0:nk, :].reshape(rows, nk * LANES)
    yb = gbuf[slot, :, nk:2 * nk, :].reshape(rows, nk * LANES)
    rinfo = rinfo_ref[...]
    moe = rinfo[:, 1:2] * ya + rinfo[:, 2:3] * yb

    def run(is_prompt):
        m3 = (mp_ref if is_prompt else ms_ref)[...]
        h = _gate_residual(hm_ref[...], m3, 5, moe)
        if final:
            y = h * lax.rsqrt(jnp.mean(h * h, axis=-1, keepdims=True) + NORM_EPS) * nf_ref[...]
            (yp_ref if is_prompt else ysm_ref)[...] = y
        else:
            h_ref[...] = h

    _by_group(i, npt, run)


def _combine(slots, ys, hmid, rinfo, mp, ms, nf, *, tp, seq, final):
    t, d = hmid.shape
    npt = tp // ROW_TILE
    tiles_per_seq = seq // ROW_TILE
    mpm = lambda f: (lambda i, *_: f(i))
    mp_spec, ms_spec = _mod_specs(d, npt, tiles_per_seq)
    yp_spec, ysm_spec = _pair_specs(d, npt)
    wrap = lambda s: pl.BlockSpec(s.block_shape, mpm(s.index_map))
    if final:
        out_shape = (jax.ShapeDtypeStruct((tp, d), F32), jax.ShapeDtypeStruct((t - tp, d), F32))
        out_specs = (wrap(yp_spec), wrap(ysm_spec))
    else:
        out_shape = jax.ShapeDtypeStruct((t, d), F32)
        out_specs = pl.BlockSpec((ROW_TILE, d), lambda i, *_: (i, 0))
    grid_spec = pltpu.PrefetchScalarGridSpec(
        num_scalar_prefetch=3,
        grid=(t // ROW_TILE,),
        in_specs=[pl.BlockSpec(memory_space=pl.ANY),
                  pl.BlockSpec((ROW_TILE, d), lambda i, *_: (i, 0)),
                  pl.BlockSpec((ROW_TILE, LANES), lambda i, *_: (i, 0)),
                  wrap(mp_spec), wrap(ms_spec),
                  pl.BlockSpec((1, d), lambda i, *_: (0, 0))],
        out_specs=out_specs,
        scratch_shapes=[pltpu.VMEM((2, ROW_TILE) + ys.shape[1:], F32), pltpu.SemaphoreType.DMA((2,))],
    )
    kern = functools.partial(_combine_kernel, npt=npt, final=final)
    return pl.pallas_call(
        kern, out_shape=out_shape, grid_spec=grid_spec,
        compiler_params=_cparams(("arbitrary",)),
        name="moe_combine_final" if final else "moe_combine",
    )(*slots, ys, hmid, rinfo, mp, ms, nf)


def _odd_pre_kernel(hp_ref, hs_ref, mp_ref, ms_ref, nw_ref, w_ref, gate_ref, rec_ref, *, npt):
    i = pl.program_id(0)
    width = gate_ref.shape[1]

    def run(is_prompt):
        h = (hp_ref if is_prompt else hs_ref)[...]
        m3 = (mp_ref if is_prompt else ms_ref)[...]
        ub = _norm_mod(h, nw_ref[...], m3, 1, 0).astype(BF16)
        x = _dot(ub, w_ref[:, 0:width])
        gate_ref[...] = 0.5 * x * (1.0 + jnp.tanh(np.sqrt(2.0 / np.pi) * (x + 0.044715 * (x * x * x))))
        rec_ref[...] = _dot(ub, w_ref[:, width:2 * width])

    _by_group(i, npt, run)


def _odd_pre(h, mp, ms, nw, w_in, *, tp, seq):
    t, d = h.shape
    npt = tp // ROW_TILE
    width = w_in.shape[1] // 2
    const = lambda shape: pl.BlockSpec(shape, lambda i: (0,) * len(shape))
    tile = lambda c: pl.BlockSpec((ROW_TILE, c), lambda i: (i, 0))
    kern = functools.partial(_odd_pre_kernel, npt=npt)
    return pl.pallas_call(
        kern,
        out_shape=(jax.ShapeDtypeStruct((t, width), F32), jax.ShapeDtypeStruct((t, width), F32)),
        grid=(t // ROW_TILE,),
        in_specs=[*_pair_specs(d, npt, off_s=npt), *_mod_specs(d, npt, seq // ROW_TILE),
                  const((1, d)), const(w_in.shape)],
        out_specs=(tile(width), tile(width)),
        compiler_params=_cparams(("arbitrary",)),
        name="odd_pre",
    )(h, h, mp, ms, nw, w_in)


def _lru_kernel(rec_ref, gate_ref, c0_ref, h0_ref, cw_ref, cb_ref, wa_ref, ba_ref, wx_ref, bx_ref, lam_ref,
                yg_ref, cs_ref, hl_ref, hist, hc, a_s, h_s, *, nb, tl):
    i = pl.program_id(1)
    r, width = rec_ref.shape
    taps = CONV_WIDTH
    blk = width // LRU_HEADS

    @pl.when(i == 0)
    def _():
        hist[:, SUBLANES - (taps - 1):SUBLANES, :] = c0_ref[...]
        hc[...] = h0_ref[...]

    rec3 = rec_ref[...].reshape(nb, tl, width)
    hist[:, SUBLANES:SUBLANES + tl, :] = rec3
    cw = cw_ref[...]
    conv = cb_ref[...] + rec3 * cw[taps - 1:taps, :]
    for back in range(1, taps):
        conv = conv + hist[:, SUBLANES - back:SUBLANES - back + tl, :] * cw[taps - 1 - back:taps - back, :]
    tail = hist[:, SUBLANES + tl - (taps - 1):SUBLANES + tl, :]
    cs_ref[...] = tail
    hist[:, SUBLANES - (taps - 1):SUBLANES, :] = tail

    cf = conv.reshape(r, width)
    cb16 = cf.astype(BF16)
    gr = jnp.concatenate([_dot(cb16[:, h * blk:(h + 1) * blk], wa_ref[h]) for h in range(LRU_HEADS)], axis=1)
    gi = jnp.concatenate([_dot(cb16[:, h * blk:(h + 1) * blk], wx_ref[h]) for h in range(LRU_HEADS)], axis=1)
    rg = _sigmoid(gr + ba_ref[...])
    ig = _sigmoid(gi + bx_ref[...])
    z = -lam_ref[...]
    softplus = jnp.maximum(z, 0.0) + jnp.log1p(jnp.exp(-jnp.abs(z)))
    log_a = -LRU_C * rg * softplus
    a = jnp.exp(log_a)
    gx = jnp.sqrt(1.0 - a * a) * (ig * cf)

    groups = r // SUBLANES
    sub = lax.broadcasted_iota(I32, (groups, SUBLANES, width), 1)
    aa, hh = a.reshape(groups, SUBLANES, width), gx.reshape(groups, SUBLANES, width)
    for dlt in (1, 2, 4):
        ok = sub >= dlt
        a_sh = pltpu.roll(aa, dlt, 1)
        h_sh = pltpu.roll(hh, dlt, 1)
        hh = jnp.where(ok, hh + aa * h_sh, hh)
        aa = jnp.where(ok, aa * a_sh, aa)
    aa, hh = aa.reshape(r, width), hh.reshape(r, width)

    if tl == SUBLANES:
        y3 = hh.reshape(nb, tl, width) + aa.reshape(nb, tl, width) * hc[...]
        hl_ref[...] = y3[:, tl - 1:tl, :]
        y = y3.reshape(r, width)
    else:
        a_s[...] = aa
        h_s[...] = hh

        def group(j, carry):
            r0 = pl.multiple_of(j * SUBLANES, SUBLANES)
            yj = h_s[pl.ds(r0, SUBLANES), :] + a_s[pl.ds(r0, SUBLANES), :] * carry
            h_s[pl.ds(r0, SUBLANES), :] = yj
            return yj[SUBLANES - 1:SUBLANES, :]

        last = lax.fori_loop(0, r // SUBLANES, group, hc[0])
        hc[0] = last
        hl_ref[0] = last
        y = h_s[...]
    yg_ref[...] = (y * gate_ref[...]).astype(BF16)


def _lru(rec, gate, conv0, h0, cw, cb, wa, ba, wx, bx, lam, *, batch, seq, row0):
    width = rec.shape[1]
    taps = CONV_WIDTH
    if seq == SUBLANES:
        nb, tl = ROW_TILE // SUBLANES, SUBLANES
    else:
        nb, tl = 1, ROW_TILE
    r = nb * tl
    n_l = seq // tl
    blk0 = row0 // r
    const = lambda shape: pl.BlockSpec(shape, lambda b, i: (0,) * len(shape))
    tile_in = pl.BlockSpec((r, width), lambda b, i: (blk0 + b * n_l + i, 0))
    kern = functools.partial(_lru_kernel, nb=nb, tl=tl)
    return pl.pallas_call(
        kern,
        out_shape=(jax.ShapeDtypeStruct((batch * seq, width), BF16),
                   jax.ShapeDtypeStruct((batch, taps - 1, width), F32),
                   jax.ShapeDtypeStruct((batch, 1, width), F32)),
        grid=(batch // nb, n_l),
        in_specs=[tile_in, tile_in,
                  pl.BlockSpec((nb, taps - 1, width), lambda b, i: (b, 0, 0)),
                  pl.BlockSpec((nb, 1, width), lambda b, i: (b, 0, 0)),
                  const((taps, width)), const((1, width)), const(wa.shape), const((1, width)),
                  const(wx.shape), const((1, width)), const((1, width))],
        out_specs=(pl.BlockSpec((r, width), lambda b, i: (b * n_l + i, 0)),
                   pl.BlockSpec((nb, taps - 1, width), lambda b, i: (b, 0, 0)),
                   pl.BlockSpec((nb, 1, width), lambda b, i: (b, 0, 0))),
        scratch_shapes=[pltpu.VMEM((nb, tl + SUBLANES, width), F32), pltpu.VMEM((nb, 1, width), F32),
                        pltpu.VMEM((r, width), F32), pltpu.VMEM((r, width), F32)],
        compiler_params=_cparams(("arbitrary", "arbitrary")),
        name="rg_lru_sample" if seq == SUBLANES else "rg_lru_prompt",
    )(rec, gate, conv0, h0, cw, cb, wa, ba, wx, bx, lam)


def kernel(x_prompt, x_sample, c_prompt, c_sample, cache_k, cache_v, state_hgrn, state_conv, state_lru, page_table,
           ada_w, ada_b, norm_mix, norm_ffn, norm_final, w_in_even, w_out_even, hg_lower_bounds, hg_norm_w, w_in_odd,
           conv_w, conv_b, lru_wa, lru_ba, lru_wx, lru_bx, lru_lambda, w_out_odd, router_group_w, router_group_b,
           router_expert_w, router_expert_b, moe_w1, moe_w3, moe_w2):
    batch, seq, d = x_prompt.shape
    dec_batch, dec_seq, _ = x_sample.shape
    depth = ada_w.shape[0]
    assert depth == 2 and seq % ROW_TILE == 0 and seq % MOBA_BLOCK == 0
    tp, ts = batch * seq, dec_batch * dec_seq
    n_pages, page = page_table.shape[1], cache_k.shape[2]
    past_len = n_pages * page
    aw = MOBA_HEADS * MOBA_HEAD_DIM
    hw = HG_HEADS * HG_DK

    xp = x_prompt.reshape(tp, d)
    xs = x_sample.reshape(ts, d)
    mods = _ada_mods(jnp.concatenate([c_prompt, c_sample], axis=0), ada_w, ada_b)
    mods = mods.reshape(depth, batch + dec_batch, 6, d)
    mods_p, mods_s = mods[:, :batch], mods[:, batch:]

    half = MOBA_HEAD_DIM // 2
    inv_freq = jnp.power(ROPE_THETA, -jnp.arange(half, dtype=F32) / half)
    invf = jnp.tile(inv_freq, LANES // half).reshape(1, LANES)

    def router_mats(l):
        rw = jnp.zeros((d, LANES), F32)
        rw = rw.at[:, 0:N_GROUPS].set(router_group_w[l])
        rw = rw.at[:, N_GROUPS:N_GROUPS + N_GROUPS * EXPERTS_PER_GROUP].set(router_expert_w[l])
        rb = jnp.zeros((1, LANES), F32)
        rb = rb.at[0, 0:N_GROUPS].set(router_group_b[l])
        rb = rb.at[0, N_GROUPS:N_GROUPS + N_GROUPS * EXPERTS_PER_GROUP].set(router_expert_b[l])
        rw_hi = rw.astype(BF16)
        rw_lo = (rw - rw_hi.astype(F32)).astype(BF16)
        return jnp.concatenate([rw_hi, rw_lo], axis=1), rb

    q_all, k_p, k_s, v_p, v_s, hg = _even_pre(
        xp, xs, mods_p[0], mods_s[0], norm_mix[0].reshape(1, d), w_in_even[0].astype(BF16), hg_lower_bounds, invf,
        seq=seq, past_len=past_len, layer_slot=0)
    oa_p = _attn_prompt(q_all, k_p, v_p, batch=batch, seq=seq)
    oa_s = _attn_sample(page_table, q_all, k_s, v_s, cache_k, cache_v,
                        layer=0, dec_batch=dec_batch, dec_seq=dec_seq, q_row0=tp)
    hnw = hg_norm_w[0].reshape(1, HG_DK)
    ob_p, hgrn_p = _hgrn_prompt(hg, hnw, batch=batch, seq=seq)
    ob_s, hgrn_s = _hgrn_sample(hg, state_hgrn[0].reshape(dec_batch * HG_HEADS, HG_DK, HG_DK), hnw,
                                dec_batch=dec_batch, dec_seq=dec_seq, row0=tp)
    w_out = w_out_even[0].astype(BF16)
    rw, rb = router_mats(0)
    hmid, u, rinfo, counts = _post_mixer((xp, xs), 0, mods_p[0], mods_s[0], [(oa_p, oa_s), (ob_p, ob_s)],
                                         [w_out[:aw], w_out[aw:]], norm_ffn[0].reshape(1, d), rw, rb, seq=seq)
    dest, ys = _moe(u, rinfo, counts, moe_w1, moe_w3, moe_w2, 0)
    h1 = _combine(dest, ys, hmid, rinfo, mods_p[0], mods_s[0], norm_final.reshape(1, d), tp=tp, seq=seq,
                  final=False)

    gate, rec = _odd_pre(h1, mods_p[1], mods_s[1], norm_mix[1].reshape(1, d), w_in_odd[0].astype(BF16), tp=tp, seq=seq)
    width = rec.shape[1]
    lru_args = (conv_w[0], conv_b[0].reshape(1, width), lru_wa[0].astype(BF16), lru_ba[0].reshape(1, width),
                lru_wx[0].astype(BF16), lru_bx[0].reshape(1, width), lru_lambda[0].reshape(1, width))
    yg_p, conv_p, lru_p = _lru(rec, gate, jnp.zeros((batch, CONV_WIDTH - 1, width), F32),
                               jnp.zeros((batch, 1, width), F32), *lru_args, batch=batch, seq=seq, row0=0)
    yg_s, conv_s, lru_s = _lru(rec, gate, state_conv[0], state_lru[0].reshape(dec_batch, 1, width), *lru_args,
                               batch=dec_batch, seq=dec_seq, row0=tp)
    rw, rb = router_mats(1)
    hmid, u, rinfo, counts = _post_mixer((h1, h1), tp // ROW_TILE, mods_p[1], mods_s[1], [(yg_p, yg_s)],
                                         [w_out_odd[0].astype(BF16)], norm_ffn[1].reshape(1, d), rw, rb, seq=seq)
    dest, ys = _moe(u, rinfo, counts, moe_w1, moe_w3, moe_w2, 1)
    y_p, y_s = _combine(dest, ys, hmid, rinfo, mods_p[1], mods_s[1], norm_final.reshape(1, d), tp=tp, seq=seq,
                        final=True)

    pos_major = lambda a: jnp.transpose(a.reshape(batch, MOBA_HEADS, MOBA_HEAD_DIM, seq), (0, 3, 1, 2))[None]
    return (y_p.reshape(batch, seq, d), y_s.reshape(dec_batch, dec_seq, d),
            pos_major(k_p), pos_major(v_p),
            k_s.reshape(1, dec_batch, dec_seq, MOBA_HEADS, MOBA_HEAD_DIM),
            v_s.reshape(1, dec_batch, dec_seq, MOBA_HEADS, MOBA_HEAD_DIM),
            hgrn_p.reshape(1, batch, HG_HEADS, HG_DK, HG_DK), hgrn_s.reshape(1, dec_batch, HG_HEADS, HG_DK, HG_DK),
            conv_p.reshape(1, batch, CONV_WIDTH - 1, width), conv_s.reshape(1, dec_batch, CONV_WIDTH - 1, width),
            lru_p.reshape(1, batch, width), lru_s.reshape(1, dec_batch, width))
```

```python
import functools

import numpy as np
import jax
import jax.numpy as jnp
from jax import lax
from jax.experimental import pallas as pl
from jax.experimental.pallas import tpu as pltpu

F32 = jnp.float32
BF16 = jnp.bfloat16
I32 = jnp.int32
HIGHEST = lax.Precision.HIGHEST

MOBA_HEADS = 8
MOBA_HEAD_DIM = 64
MOBA_BLOCK = 256
MOBA_TOPK = 3
ROPE_THETA = 10000.0
HG_HEADS = 4
HG_DK = 128
LRU_HEADS = 4
CONV_WIDTH = 4
LRU_C = 8.0
N_GROUPS = 4
EXPERTS_PER_GROUP = 4
NORM_EPS = 1e-6

LANES = 128
SUBLANES = 8
VMEM_LIMIT = 56 * 1024 * 1024

ROW_TILE = 256
MOE_TILE = 256
HG_CHUNK = 128
N_BUCKETS = N_GROUPS * 6
PAIR_LO = (0, 0, 0, 1, 1, 2)
PAIR_HI = (1, 2, 3, 2, 3, 3)
NEG = -1e30


def _cparams(sem, vmem=VMEM_LIMIT):
    return pltpu.CompilerParams(dimension_semantics=sem, vmem_limit_bytes=vmem)


def _dot(a, b, **kw):
    return jnp.dot(a, b, preferred_element_type=F32, **kw)


def _dot_nt(a, b, **kw):
    return lax.dot_general(a, b, (((1,), (1,)), ((), ())), preferred_element_type=F32, **kw)


def _sigmoid(x):
    return jax.nn.sigmoid(x)


def _by_group(i, n_prompt_tiles, fn):
    @pl.when(i < n_prompt_tiles)
    def _():
        fn(True)

    @pl.when(i >= n_prompt_tiles)
    def _():
        fn(False)


def _pair_specs(cols, npt, rows=ROW_TILE, off_p=0, off_s=0):
    return (pl.BlockSpec((rows, cols), lambda i: (off_p + jnp.minimum(i, npt - 1), 0)),
            pl.BlockSpec((rows, cols), lambda i: (off_s + jnp.maximum(i - npt, 0), 0)))


def _mod_specs(d, npt, tiles_per_seq, rows=ROW_TILE):
    return (pl.BlockSpec((1, 6, d), lambda i: (jnp.minimum(i, npt - 1) // tiles_per_seq, 0, 0)),
            pl.BlockSpec((rows // SUBLANES, 6, d), lambda i: (jnp.maximum(i - npt, 0), 0, 0)))


def _norm_mod(x, nw, m3, scale_idx, shift_idx):
    r, d = x.shape
    nb = m3.shape[0]
    var = jnp.mean(x * x, axis=-1, keepdims=True)
    y = x * lax.rsqrt(var + NORM_EPS) * nw
    y3 = y.reshape(nb, r // nb, d)
    u3 = y3 * (1.0 + m3[:, scale_idx:scale_idx + 1, :]) + m3[:, shift_idx:shift_idx + 1, :]
    return u3.reshape(r, d)


def _gate_residual(h, m3, gate_idx, out):
    r, d = h.shape
    nb = m3.shape[0]
    return (h.reshape(nb, r // nb, d) + m3[:, gate_idx:gate_idx + 1, :] * out.reshape(nb, r // nb, d)).reshape(r, d)


def _ada_kernel(c_ref, w_ref, b_ref, o_ref):
    o_ref[0] = _dot(c_ref[...], w_ref[0], precision=HIGHEST) + b_ref[0]


def _ada_mods(c_all, ada_w, ada_b):
    depth, d, n6 = ada_w.shape
    nb = c_all.shape[0]
    tn = 1024
    return pl.pallas_call(
        _ada_kernel,
        out_shape=jax.ShapeDtypeStruct((depth, nb, n6), F32),
        grid=(depth, n6 // tn),
        in_specs=[pl.BlockSpec((nb, d), lambda l, n: (0, 0)),
                  pl.BlockSpec((1, d, tn), lambda l, n: (l, 0, n)),
                  pl.BlockSpec((1, 1, tn), lambda l, n: (l, 0, n))],
        out_specs=pl.BlockSpec((1, nb, tn), lambda l, n: (l, 0, n)),
        compiler_params=_cparams(("arbitrary", "arbitrary")),
        name="ada_mods",
    )(c_all, ada_w, ada_b.reshape(depth, 1, n6))


def _even_pre_kernel(xp_ref, xs_ref, mp_ref, ms_ref, nw_ref, w_ref, hb_ref, invf_ref,
                     q_ref, kp_ref, ks_ref, vp_ref, vs_ref, hg_ref, *, npt, tiles_per_seq, past_len, layer_slot):
    i = pl.program_id(0)
    aw = MOBA_HEADS * MOBA_HEAD_DIM
    hw = HG_HEADS * HG_DK

    def run(is_prompt):
        x = (xp_ref if is_prompt else xs_ref)[...]
        m3 = (mp_ref if is_prompt else ms_ref)[...]
        r = x.shape[0]
        nb = m3.shape[0]
        tl = r // nb
        ub = _norm_mod(x, nw_ref[...], m3, 1, 0).astype(BF16)

        pos0 = (i % tiles_per_seq) * r if is_prompt else past_len
        pos = (pos0 + lax.broadcasted_iota(I32, (nb, tl, LANES), 1)).astype(F32).reshape(r, LANES)
        ang = pos * invf_ref[...]
        reps = aw // LANES
        cos = jnp.concatenate([jnp.cos(ang)] * reps, axis=1)
        sin = jnp.concatenate([jnp.sin(ang)] * reps, axis=1)
        lane = lax.broadcasted_iota(I32, (r, aw), 1)
        half = MOBA_HEAD_DIM // 2
        first = (lane % MOBA_HEAD_DIM) < half
        sin = jnp.where(first, -sin, sin)

        def rope(t):
            rot = jnp.where(first, pltpu.roll(t, aw - half, 1), pltpu.roll(t, half, 1))
            return t * cos + rot * sin

        def seg(a, b):
            return _dot(ub, w_ref[:, a:b])

        q_ref[...] = rope(seg(0, aw))
        k = rope(seg(aw, 2 * aw))
        v = seg(2 * aw, 3 * aw)
        if is_prompt:
            kp_ref[0] = k.T
            vp_ref[0] = v.T
        else:
            ks_ref[...] = k
            vs_ref[...] = v
        o = 3 * aw
        qb = seg(o, o + hw)
        fb = seg(o + hw, o + 2 * hw)
        hb = hb_ref[...]
        e = jnp.exp(hb - jnp.max(hb, axis=0, keepdims=True))
        sm = e / jnp.sum(e, axis=0, keepdims=True)
        lb = jnp.sum(sm[0:layer_slot + 1, :], axis=0, keepdims=True)
        hg_ref[:, 0:hw] = qb * _sigmoid(qb)
        hg_ref[:, hw:2 * hw] = lb + (1.0 - lb) * _sigmoid(fb)
        hg_ref[:, 2 * hw:3 * hw] = seg(o + 2 * hw, o + 3 * hw)
        hg_ref[:, 3 * hw:4 * hw] = seg(o + 3 * hw, o + 4 * hw)

    _by_group(i, npt, run)


def _even_pre(xp, xs, mp, ms, nw, w_in, hb, invf, *, seq, past_len, layer_slot):
    tp, d = xp.shape
    ts = xs.shape[0]
    npt, nst = tp // ROW_TILE, ts // ROW_TILE
    tiles_per_seq = seq // ROW_TILE
    aw = MOBA_HEADS * MOBA_HEAD_DIM
    hw = HG_HEADS * HG_DK
    t = tp + ts
    const = lambda shape: pl.BlockSpec(shape, lambda i: (0,) * len(shape))
    kern = functools.partial(_even_pre_kernel, npt=npt, tiles_per_seq=tiles_per_seq, past_len=past_len,
                             layer_slot=layer_slot)
    _, ks_spec = _pair_specs(aw, npt)
    kp_spec = pl.BlockSpec((1, aw, ROW_TILE), lambda i: (jnp.minimum(i, npt - 1) // tiles_per_seq, 0,
                                                          jnp.minimum(i, npt - 1) % tiles_per_seq))
    return pl.pallas_call(
        kern,
        out_shape=(jax.ShapeDtypeStruct((t, aw), F32),
                   jax.ShapeDtypeStruct((tp // seq, aw, seq), F32), jax.ShapeDtypeStruct((ts, aw), F32),
                   jax.ShapeDtypeStruct((tp // seq, aw, seq), F32), jax.ShapeDtypeStruct((ts, aw), F32),
                   jax.ShapeDtypeStruct((t, 4 * hw), F32)),
        grid=(npt + nst,),
        in_specs=[*_pair_specs(d, npt), *_mod_specs(d, npt, tiles_per_seq),
                  const((1, d)), const(w_in.shape), const(hb.shape), const((1, LANES))],
        out_specs=(pl.BlockSpec((ROW_TILE, aw), lambda i: (i, 0)), kp_spec, ks_spec, kp_spec, ks_spec,
                   pl.BlockSpec((ROW_TILE, 4 * hw), lambda i: (i, 0))),
        compiler_params=_cparams(("arbitrary",)),
        name="even_pre",
    )(xp, xs, mp, ms, nw, w_in, hb, invf)


def _top_blocks(g, valid):
    lane = lax.broadcasted_iota(I32, g.shape, 1)
    jl = lane % SUBLANES
    g = jnp.where(valid, g, -jnp.inf)
    cnt = jnp.zeros(g.shape, I32)
    for d in range(1, SUBLANES):
        wrapped = (jl + d) >= SUBLANES
        pg = jnp.where(wrapped, pltpu.roll(g, SUBLANES - d, 1), pltpu.roll(g, LANES - d, 1))
        beats = (pg > g) | ((pg == g) & wrapped)
        cnt = cnt + beats.astype(I32)
    return valid & (cnt < MOBA_TOPK)


def _head_expand(rows8, n_heads, width):
    x = jnp.concatenate([rows8] * n_heads, axis=0)
    r = lax.broadcasted_iota(I32, x.shape, 0) // SUBLANES
    l = lax.broadcasted_iota(I32, x.shape, 1) // (width // n_heads)
    return jnp.where(r == l, x, 0.0)


def _block_indicator(seq):
    key_blk = np.arange(seq)[None, :] // MOBA_BLOCK
    row = np.arange(LANES)[:, None]
    ind = (row < MOBA_HEADS * SUBLANES) & ((row % SUBLANES) == key_blk)
    return jnp.asarray(ind.astype(np.float32), dtype=BF16)


def _attn_prompt_kernel(q_ref, k_ref, v_ref, ind_ref, o_ref, km_ref, *, n_blk):
    i = pl.program_id(1)
    blk = MOBA_BLOCK
    aw = MOBA_HEADS * MOBA_HEAD_DIM
    scale = MOBA_HEAD_DIM ** -0.5

    @pl.when(i == 0)
    def _():
        r = lax.broadcasted_iota(I32, (aw, LANES), 0) // MOBA_HEAD_DIM
        c = lax.broadcasted_iota(I32, (aw, LANES), 1)
        km = jnp.zeros((aw, LANES), F32)
        for j in range(n_blk):
            mean_j = jnp.mean(k_ref[0, :, j * blk:(j + 1) * blk], axis=1, keepdims=True)
            km = jnp.where(((c % SUBLANES) == j) & ((c // SUBLANES) == r), mean_j, km)
        km_ref[...] = km

    q = q_ref[...]
    gate = _dot(q, km_ref[...], precision=HIGHEST)
    lane = lax.broadcasted_iota(I32, gate.shape, 1)
    jl = lane % SUBLANES
    valid = (jl < i) & (lane < MOBA_HEADS * SUBLANES)
    keep = _top_blocks(gate, valid) | (jl == i)
    bias = jnp.where(keep, 0.0, NEG)

    row = lax.broadcasted_iota(I32, (blk, blk), 0)
    col = lax.broadcasted_iota(I32, (blk, blk), 1)
    tril = row >= col

    def tile(c):
        n_keys = (c + 1) * blk
        for hp in range(aw // LANES):
            cols = slice(hp * LANES, (hp + 1) * LANES)
            rhs = jnp.concatenate([k_ref[0, cols, 0:n_keys].astype(BF16), ind_ref[:, 0:n_keys]], axis=0)
            vv = v_ref[0, cols, 0:n_keys].astype(BF16)
            qp = q[:, cols] * scale
            outs = []
            for s in (0, 1):
                h = 2 * hp + s
                qm = jnp.where((lane // MOBA_HEAD_DIM) == s, qp, 0.0).astype(BF16)
                hb = jnp.where((lane // SUBLANES) == h, bias, 0.0).astype(BF16)
                sc = _dot(jnp.concatenate([qm, hb], axis=1), rhs)
                own = jnp.where(tril, sc[:, c * blk:], NEG)
                sc = own if c == 0 else jnp.concatenate([sc[:, :c * blk], own], axis=1)
                p = jnp.exp(sc - jnp.max(sc, axis=1, keepdims=True))
                outs.append(_dot_nt(p.astype(BF16), vv) / jnp.sum(p, axis=1, keepdims=True))
            o_ref[:, cols] = jnp.where((lane // MOBA_HEAD_DIM) == 0, outs[0], outs[1]).astype(BF16)

    for c in range(n_blk):
        pl.when(i == c)(functools.partial(tile, c))


def _attn_prompt(q_all, k_p, v_p, *, batch, seq):
    aw = MOBA_HEADS * MOBA_HEAD_DIM
    n_blk = seq // MOBA_BLOCK
    assert n_blk <= SUBLANES
    ind = _block_indicator(seq)
    kern = functools.partial(_attn_prompt_kernel, n_blk=n_blk)
    return pl.pallas_call(
        kern,
        out_shape=jax.ShapeDtypeStruct((batch * seq, aw), BF16),
        grid=(batch, n_blk),
        in_specs=[pl.BlockSpec((MOBA_BLOCK, aw), lambda b, i: (b * n_blk + i, 0)),
                  pl.BlockSpec((1, aw, seq), lambda b, i: (b, 0, 0)),
                  pl.BlockSpec((1, aw, seq), lambda b, i: (b, 0, 0)),
                  pl.BlockSpec((LANES, seq), lambda b, i: (0, 0))],
        out_specs=pl.BlockSpec((MOBA_BLOCK, aw), lambda b, i: (b * n_blk + i, 0)),
        scratch_shapes=[pltpu.VMEM((aw, LANES), F32)],
        compiler_params=_cparams(("arbitrary", "arbitrary")),
        name="moba_prompt",
    )(q_all, k_p, v_p, ind)


def _attn_sample_kernel(pt_ref, q_ref, kn_ref, vn_ref, ck_ref, cv_ref, o_ref, kbuf, vbuf, sem,
                        *, n_pages, page, n_blk, layer):
    b = pl.program_id(0)
    nb = pl.num_programs(0)
    blk = MOBA_BLOCK
    hd = MOBA_HEAD_DIM
    dec = q_ref.shape[0]
    scale = hd ** -0.5

    def page_copies(seq_idx, slot):
        cps = []
        for p in range(n_pages):
            pg = pt_ref[seq_idx * n_pages + p]
            cps.append(pltpu.make_async_copy(ck_ref.at[layer, pg], kbuf.at[slot, p], sem.at[0, slot]))
            cps.append(pltpu.make_async_copy(cv_ref.at[layer, pg], vbuf.at[slot, p], sem.at[1, slot]))
        return cps

    @pl.when(b == 0)
    def _():
        for c in page_copies(0, 0):
            c.start()

    @pl.when(b + 1 < nb)
    def _():
        for c in page_copies(b + 1, (b + 1) % 2):
            c.start()

    slot = b % 2
    for c in page_copies(b, slot):
        c.wait()

    nh = MOBA_HEADS
    aw = nh * hd
    past = n_blk * blk
    kt = jnp.concatenate([kbuf[slot, p].reshape(aw, page) for p in range(n_pages)], axis=1)
    vt = jnp.concatenate([vbuf[slot, p].reshape(aw, page) for p in range(n_pages)], axis=1)
    qexp = _head_expand(q_ref[...], nh, aw)

    colj = lax.broadcasted_iota(I32, (aw, LANES), 1)
    km = jnp.zeros((aw, LANES), F32)
    for j in range(n_blk):
        km = jnp.where(colj == j, jnp.mean(kt[:, j * blk:(j + 1) * blk], axis=1, keepdims=True), km)
    gate = _dot(qexp, km, precision=HIGHEST)
    lane = lax.broadcasted_iota(I32, gate.shape, 1)
    sel = _top_blocks(gate, lane < n_blk).astype(F32)

    qs = (qexp * scale).astype(BF16)
    sc = _dot(qs, kt.astype(BF16))
    key_blk = lax.broadcasted_iota(I32, sc.shape, 1) // blk
    mask = jnp.zeros(sc.shape, F32)
    for j in range(n_blk):
        mask = jnp.where(key_blk == j, sel[:, j:j + 1], mask)
    sc = jnp.where(mask > 0.5, sc, NEG)

    pad = jnp.zeros((LANES - dec, aw), F32)
    kn = jnp.concatenate([kn_ref[...], pad], axis=0).astype(BF16)
    vn = jnp.concatenate([vn_ref[...], pad], axis=0).astype(BF16)
    qi = lax.broadcasted_iota(I32, gate.shape, 0) % dec
    sn = jnp.where(lane <= qi, _dot_nt(qs, kn), NEG)

    m = jnp.maximum(jnp.max(sc, axis=1, keepdims=True), jnp.max(sn, axis=1, keepdims=True))
    p = jnp.exp(sc - m)
    pn = jnp.exp(sn - m)
    l = jnp.sum(p, axis=1, keepdims=True) + jnp.sum(pn, axis=1, keepdims=True)
    o = (_dot_nt(p.astype(BF16), vt.astype(BF16)) + _dot(pn.astype(BF16), vn)) / l
    r = lax.broadcasted_iota(I32, o.shape, 0) // dec
    c = lax.broadcasted_iota(I32, o.shape, 1) // hd
    o = jnp.where(r == c, o, 0.0)
    acc = o[0:dec]
    for h in range(1, nh):
        acc = acc + o[h * dec:(h + 1) * dec]
    o_ref[...] = acc


def _attn_sample(page_table, q_all, k_s, v_s, cache_k, cache_v, *, layer, dec_batch, dec_seq, q_row0):
    _, n_phys, page, n_heads, hd = cache_k.shape
    aw = MOBA_HEADS * MOBA_HEAD_DIM
    n_pages = page_table.shape[1]
    past_len = n_pages * page
    assert past_len % MOBA_BLOCK == 0 and dec_seq == SUBLANES and past_len // MOBA_BLOCK == SUBLANES
    assert n_heads == MOBA_HEADS and hd == MOBA_HEAD_DIM
    kern = functools.partial(_attn_sample_kernel, n_pages=n_pages, page=page, n_blk=past_len // MOBA_BLOCK,
                             layer=layer)
    q_blk0 = q_row0 // dec_seq
    grid_spec = pltpu.PrefetchScalarGridSpec(
        num_scalar_prefetch=1,
        grid=(dec_batch,),
        in_specs=[pl.BlockSpec((dec_seq, aw), lambda b, pt: (q_blk0 + b, 0)),
                  pl.BlockSpec((dec_seq, aw), lambda b, pt: (b, 0)),
                  pl.BlockSpec((dec_seq, aw), lambda b, pt: (b, 0)),
                  pl.BlockSpec(memory_space=pl.ANY),
                  pl.BlockSpec(memory_space=pl.ANY)],
        out_specs=pl.BlockSpec((dec_seq, aw), lambda b, pt: (b, 0)),
        scratch_shapes=[pltpu.VMEM((2, n_pages, n_heads, hd, page), F32),
                        pltpu.VMEM((2, n_pages, n_heads, hd, page), F32), pltpu.SemaphoreType.DMA((2, 2))],
    )
    to_pos_minor = lambda c: jnp.transpose(c, (0, 1, 3, 4, 2))
    return pl.pallas_call(
        kern,
        out_shape=jax.ShapeDtypeStruct((dec_batch * dec_seq, aw), F32),
        grid_spec=grid_spec,
        compiler_params=_cparams(("arbitrary",)),
        name="moba_sample",
    )(page_table.reshape(-1), q_all, k_s, v_s, to_pos_minor(cache_k), to_pos_minor(cache_v))


def _hgrn_levels():
    n = HG_CHUNK
    return int(np.log2(n))


def _hgrn_cumsum_matrix():
    return jnp.asarray(np.tril(np.ones((HG_CHUNK, HG_CHUNK), np.float32)), dtype=BF16)


def _separator_rows(b, level):
    c, dk = b.shape
    m = 2 ** level
    if 2 * m >= SUBLANES:
        b3 = b.reshape(c // (2 * m), 2 * m, dk)
        return jnp.broadcast_to(b3[:, m - 1:m, :], b3.shape).reshape(c, dk)
    b3 = b.reshape(c // SUBLANES, SUBLANES, dk)
    sub = lax.broadcasted_iota(I32, b3.shape, 1)
    if m == 1:
        out = jnp.where(sub % 2 == 1, pltpu.roll(b3, 1, 1), b3)
    else:
        out = jnp.where(sub < 4, b3[:, 1:2, :], b3[:, 5:6, :])
    return out.reshape(c, dk)


def _hgrn_post(o, g, nw):
    o = o * lax.rsqrt(jnp.mean(o * o, axis=-1, keepdims=True) + NORM_EPS) * nw
    return o * (g * _sigmoid(g))


def _hgrn_prompt_kernel(q_ref, f_ref, i_ref, g_ref, ms_ref, nw_ref, o_ref, s_ref):
    c = HG_CHUNK
    dk = HG_DK
    n_chunks = q_ref.shape[0] // c
    levels = _hgrn_levels()
    row = lax.broadcasted_iota(I32, (c, c), 0)
    col = lax.broadcasted_iota(I32, (c, c), 1)
    rowk = lax.broadcasted_iota(I32, (c, dk), 0)
    upper = [((rowk // (2 ** l)) % 2) == 1 for l in range(levels)]
    same = [(row // (2 ** (l + 1))) == (col // (2 ** (l + 1))) for l in range(levels)]
    diag = row == col
    nw = nw_ref[...]

    heads = q_ref.shape[1] // dk

    def chunk(ci, sts):
        r0 = pl.multiple_of(ci * c, c)
        return tuple(one_head(r0, slice(hh * dk, (hh + 1) * dk), sts[hh]) for hh in range(heads))

    def one_head(r0, cols, st):
        q = q_ref[pl.ds(r0, c), cols]
        f = f_ref[pl.ds(r0, c), cols]
        v = i_ref[pl.ds(r0, c), cols]
        g = g_ref[pl.ds(r0, c), cols]
        lf = jnp.log(f)
        k = 1.0 - f
        hi = lf.astype(BF16)
        r1 = lf - hi.astype(F32)
        mid = r1.astype(BF16)
        lo = (r1 - mid.astype(F32)).astype(BF16)
        bb = _dot(ms_ref[...], jnp.concatenate([hi, mid, lo], axis=1))
        b = bb[:, 0:dk] + bb[:, dk:2 * dk] + bb[:, 2 * dk:3 * dk]
        a = jnp.where(diag, jnp.sum(q * k, axis=1, keepdims=True), 0.0)
        for l in range(levels):
            bs = _separator_rows(b, l)
            e = jnp.exp(jnp.where(upper[l], b - bs, bs - b))
            qp = jnp.where(upper[l], q * e, 0.0).astype(BF16)
            kp = jnp.where(upper[l], 0.0, k * e).astype(BF16)
            a = a + jnp.where(same[l], _dot_nt(qp, kp), 0.0)
        vb = v.astype(BF16)
        o = _dot_nt((q * jnp.exp(b)).astype(BF16), st.astype(BF16)) + _dot(a.astype(BF16), vb)
        blast = b[c - 1:c, :]
        kl = (k * jnp.exp(blast - b)).astype(BF16)
        st_new = st * jnp.exp(blast) + _dot(v.T.astype(BF16), kl)
        o_ref[pl.ds(r0, c), cols] = _hgrn_post(o, g, nw).astype(BF16)
        return st_new

    sts = lax.fori_loop(0, n_chunks, chunk, tuple(jnp.zeros((dk, dk), F32) for _ in range(heads)))
    for hh in range(heads):
        s_ref[hh] = sts[hh].T


def _hgrn_prompt(hg, nw, *, batch, seq):
    hw = HG_HEADS * HG_DK
    mstack = _hgrn_cumsum_matrix()
    hps = HG_HEADS
    groups = HG_HEADS // hps
    spec = lambda off: pl.BlockSpec((seq, hps * HG_DK), lambda b, h: (b, off + h))
    return pl.pallas_call(
        _hgrn_prompt_kernel,
        out_shape=(jax.ShapeDtypeStruct((batch * seq, hw), BF16),
                   jax.ShapeDtypeStruct((batch * HG_HEADS, HG_DK, HG_DK), F32)),
        grid=(batch, groups),
        in_specs=[spec(0), spec(groups), spec(2 * groups), spec(3 * groups),
                  pl.BlockSpec(mstack.shape, lambda b, h: (0, 0)),
                  pl.BlockSpec((1, HG_DK), lambda b, h: (0, 0))],
        out_specs=(pl.BlockSpec((seq, hps * HG_DK), lambda b, h: (b, h)),
                   pl.BlockSpec((hps, HG_DK, HG_DK), lambda b, h: (b * groups + h, 0, 0))),
        compiler_params=_cparams(("arbitrary", "arbitrary")),
        name="hgrn_prompt",
    )(hg, hg, hg, hg, mstack, nw)


def _hgrn_sample_kernel(hg_ref, s0_ref, nw_ref, o_ref, s_ref, *, dec, seqs):
    dk = HG_DK
    hw = HG_HEADS * HG_DK
    nw = nw_ref[...]
    trow = lax.broadcasted_iota(I32, (dec, dk), 0)
    srow = lax.broadcasted_iota(I32, (dk, dk), 0)
    zrows = jnp.zeros((dec, dk), F32)
    pad16 = lambda a: jnp.concatenate([a, zrows], axis=0).astype(BF16)
    dot_tn = lambda a, b: lax.dot_general(a, b, (((0,), (0,)), ((), ())), preferred_element_type=F32)

    def one_seq(si, carry):
        r0 = pl.multiple_of(si * dec, dec)
        for h in range(HG_HEADS):
            cols = slice(h * dk, (h + 1) * dk)
            q = hg_ref[pl.ds(r0, dec), h * dk:(h + 1) * dk]
            f = hg_ref[pl.ds(r0, dec), hw + h * dk:hw + (h + 1) * dk]
            v = hg_ref[pl.ds(r0, dec), 2 * hw + h * dk:2 * hw + (h + 1) * dk]
            g = hg_ref[pl.ds(r0, dec), 3 * hw + h * dk:3 * hw + (h + 1) * dk]
            k = 1.0 - f
            b = jnp.log(f)
            for dlt in (1, 2, 4):
                b = b + jnp.where(trow >= dlt, pltpu.roll(b, dlt, 0), 0.0)
            s0 = s0_ref[si * HG_HEADS + h]
            o = _dot(pad16(q * jnp.exp(b)), s0.astype(BF16))[0:dec]
            for s in range(dec):
                live = trow >= s
                e = jnp.exp(jnp.where(live, b - b[s:s + 1, :], 0.0))
                a_s = jnp.sum(jnp.where(live, q * e * k[s:s + 1, :], 0.0), axis=1, keepdims=True)
                o = o + a_s * v[s:s + 1, :]
            o_ref[pl.ds(r0, dec), cols] = _hgrn_post(o, g, nw)
            blast = b[dec - 1:dec, :]
            upd = dot_tn(pad16(k * jnp.exp(blast - b)), pad16(v))
            ecol = jnp.where(srow == 0, jnp.exp(blast), 0.0).T[:, 0:1]
            s_ref[si * HG_HEADS + h] = ecol * s0 + upd
        return carry

    lax.fori_loop(0, seqs, one_seq, 0)


def _hgrn_sample(hg, s0, nw, *, dec_batch, dec_seq, row0):
    hw = HG_HEADS * HG_DK
    seqs = 8
    rows = seqs * dec_seq
    kern = functools.partial(_hgrn_sample_kernel, dec=dec_seq, seqs=seqs)
    st_spec = pl.BlockSpec((seqs * HG_HEADS, HG_DK, HG_DK), lambda i: (i, 0, 0))
    return pl.pallas_call(
        kern,
        out_shape=(jax.ShapeDtypeStruct((dec_batch * dec_seq, hw), F32),
                   jax.ShapeDtypeStruct((dec_batch * HG_HEADS, HG_DK, HG_DK), F32)),
        grid=(dec_batch // seqs,),
        in_specs=[pl.BlockSpec((rows, 4 * hw), lambda i: (row0 // rows + i, 0)), st_spec,
                  pl.BlockSpec((1, HG_DK), lambda i: (0, 0))],
        out_specs=(pl.BlockSpec((rows, hw), lambda i: (i, 0)), st_spec),
        compiler_params=_cparams(("arbitrary",)),
        name="hgrn_sample",
    )(hg, s0, nw)


def _post_mixer_kernel(*refs, n_a, npt):
    hp_ref, hs_ref, mp_ref, ms_ref = refs[0:4]
    a_refs = refs[4:4 + 2 * n_a]
    w_refs = refs[4 + 2 * n_a:4 + 3 * n_a]
    nffn_ref, rw_ref, rb_ref, striu_ref = refs[4 + 3 * n_a:8 + 3 * n_a]
    hmid_ref, u_ref, rinfo_ref, counts_ref, carry = refs[8 + 3 * n_a:]
    i = pl.program_id(0)

    @pl.when(i == 0)
    def _():
        carry[...] = jnp.zeros_like(carry)

    def run(is_prompt):
        h = (hp_ref if is_prompt else hs_ref)[...]
        m3 = (mp_ref if is_prompt else ms_ref)[...]
        out = None
        for a in range(n_a):
            av = a_refs[2 * a + (0 if is_prompt else 1)][...].astype(BF16)
            t = _dot(av, w_refs[a][...])
            out = t if out is None else out + t
        hm = _gate_residual(h, m3, 2, out)
        hmid_ref[...] = hm
        u = _norm_mod(hm, nffn_ref[...], m3, 4, 3)
        u_ref[...] = u.reshape(u_ref.shape)

        u_hi = u.astype(BF16)
        u_lo = (u - u_hi.astype(F32)).astype(BF16)
        hh = _dot(u_hi, rw_ref[...])
        logits = hh[:, 0:LANES] + hh[:, LANES:2 * LANES] + _dot(u_lo, rw_ref[:, 0:LANES]) + rb_ref[...]
        lt = logits.T
        col = lambda j: lt[j:j + 1, :]
        gl = [col(j) for j in range(N_GROUPS)]
        gmax = functools.reduce(jnp.maximum, gl)
        gsum = functools.reduce(lambda x, y: x + y, [jnp.exp(x - gmax) for x in gl])
        gval = 1.0 / gsum
        gidx = jnp.where(gl[0] == gmax, 0, jnp.where(gl[1] == gmax, 1, jnp.where(gl[2] == gmax, 2, 3)))
        el = []
        for k in range(EXPERTS_PER_GROUP):
            c = [col(N_GROUPS + g * EXPERTS_PER_GROUP + k) for g in range(N_GROUPS)]
            el.append(jnp.where(gidx == 0, c[0], jnp.where(gidx == 1, c[1], jnp.where(gidx == 2, c[2], c[3]))))
        emax = functools.reduce(jnp.maximum, el)
        pe = [jnp.exp(x - emax) for x in el]
        esum = functools.reduce(lambda x, y: x + y, pe)
        pk = [x / esum for x in pe]
        v1 = functools.reduce(jnp.maximum, pk)
        i1 = jnp.where(pk[0] == v1, 0, jnp.where(pk[1] == v1, 1, jnp.where(pk[2] == v1, 2, 3)))
        pk2 = [jnp.where(i1 == k, -1.0, pk[k]) for k in range(EXPERTS_PER_GROUP)]
        v2 = functools.reduce(jnp.maximum, pk2)
        i2 = jnp.where(pk2[0] == v2, 0, jnp.where(pk2[1] == v2, 1, jnp.where(pk2[2] == v2, 2, 3)))
        den = v1 + v2
        w1 = gval * v1 / den
        w2 = gval * v2 / den
        lo = jnp.minimum(i1, i2)
        hi = jnp.maximum(i1, i2)
        pair = jnp.where(lo == 0, hi - 1, jnp.where(lo == 1, hi + 1, 5))
        bucket = gidx * 6 + pair
        w_lo = jnp.where(i1 < i2, w1, w2)
        w_hi = jnp.where(i1 < i2, w2, w1)

        r = h.shape[0]
        sub = lax.broadcasted_iota(I32, (LANES, r), 0)
        onehot = sub == bucket
        before = _dot(onehot.astype(BF16), striu_ref[...]) + carry[:, 0:1]
        rank = jnp.sum(jnp.where(onehot, before, 0.0), axis=0, keepdims=True)
        carry[...] = carry[...] + jnp.sum(onehot.astype(F32), axis=1, keepdims=True)
        info = jnp.where(sub == 0, bucket.astype(F32),
                         jnp.where(sub == 1, w_lo, jnp.where(sub == 2, w_hi, jnp.where(sub == 3, rank, 0.0))))
        rinfo_ref[...] = info.T
        counts_ref[...] = carry[...]

    _by_group(i, npt, run)


def _post_mixer(h_pair, h_off_s, mp, ms, a_pairs, ws, nffn, rw, rb, *, seq):
    hp, hs = h_pair
    d = hp.shape[1]
    tp = a_pairs[0][0].shape[0]
    ts = a_pairs[0][1].shape[0]
    npt, nst = tp // ROW_TILE, ts // ROW_TILE
    t = tp + ts
    n_a = len(a_pairs)
    tiles_per_seq = seq // ROW_TILE
    const = lambda shape: pl.BlockSpec(shape, lambda i: (0,) * len(shape))
    stril = jnp.asarray(np.triu(np.ones((ROW_TILE, ROW_TILE), np.float32), 1), dtype=BF16)
    in_specs = [*_pair_specs(d, npt, off_s=h_off_s), *_mod_specs(d, npt, tiles_per_seq)]
    args = [hp, hs, mp, ms]
    for ap, as_ in a_pairs:
        in_specs += list(_pair_specs(ap.shape[1], npt))
        args += [ap, as_]
    for w in ws:
        in_specs.append(const(w.shape))
        args.append(w)
    in_specs += [const((1, d)), const(rw.shape), const((1, LANES)), const(stril.shape)]
    args += [nffn, rw, rb, stril]
    tile = lambda c: pl.BlockSpec((ROW_TILE, c), lambda i: (i, 0))
    kern = functools.partial(_post_mixer_kernel, n_a=n_a, npt=npt)
    return pl.pallas_call(
        kern,
        out_shape=(jax.ShapeDtypeStruct((t, d), F32), jax.ShapeDtypeStruct((t, d // LANES, LANES), F32),
                   jax.ShapeDtypeStruct((t, LANES), F32), jax.ShapeDtypeStruct((LANES, LANES), F32)),
        grid=(npt + nst,),
        in_specs=in_specs,
        out_specs=(tile(d), pl.BlockSpec((ROW_TILE, d // LANES, LANES), lambda i: (i, 0, 0)), tile(LANES),
                   const((LANES, LANES))),
        scratch_shapes=[pltpu.VMEM((LANES, LANES), F32)],
        compiler_params=_cparams(("arbitrary",)),
        name="post_mixer",
    )(*args)


def _for_rows(rows, fn):
    def body(r, c):
        fn(r)
        return c

    lax.fori_loop(0, rows, body, 0, unroll=8)


def _slot_of(bucket_ref, rank_ref, starts_ref, t):
    return starts_ref[bucket_ref[t]] + rank_ref[t]


def _scatter_kernel(bucket_ref, rank_ref, starts_ref, src_ref, init_ref, o_ref, sem):
    del init_ref
    rows = src_ref.shape[0]
    base = pl.program_id(0) * rows
    copy = lambda r: pltpu.make_async_copy(
        src_ref.at[r], o_ref.at[_slot_of(bucket_ref, rank_ref, starts_ref, base + r)], sem)
    _for_rows(rows, lambda r: copy(r).start())
    _for_rows(rows, lambda r: copy(r).wait())


def _scatter_tokens(slots, src, n_slots):
    t = src.shape[0]
    rows = 4 * ROW_TILE if t % (4 * ROW_TILE) == 0 else ROW_TILE
    grid_spec = pltpu.PrefetchScalarGridSpec(
        num_scalar_prefetch=3,
        grid=(t // rows,),
        in_specs=[pl.BlockSpec((rows,) + src.shape[1:], lambda i, *_: (i, 0, 0)), pl.BlockSpec(memory_space=pl.ANY)],
        out_specs=pl.BlockSpec(memory_space=pl.ANY),
        scratch_shapes=[pltpu.SemaphoreType.DMA(())],
    )
    return pl.pallas_call(
        _scatter_kernel,
        out_shape=jax.ShapeDtypeStruct((n_slots,) + src.shape[1:], src.dtype),
        grid_spec=grid_spec,
        input_output_aliases={4: 0},
        compiler_params=_cparams(("arbitrary",)),
        name="moe_scatter",
    )(*slots, src, jnp.zeros((n_slots,) + src.shape[1:], src.dtype))


def _moe_kernel(ea_ref, eb_ref, valid_ref, fresh_ref, slot_ref, na_ref, nb_ref, more_ref, x_ref,
                w1_ref, w3_ref, w2_ref, o_ref, wf1, wf3, wf2, wbuf1, wbuf3, wbuf2, sem, *, layer):
    i = pl.program_id(0)
    nk = x_ref.shape[1]

    def weight_copies(e_a, e_b, slot):
        cps = []
        for s, e in ((0, e_a), (1, e_b)):
            for src, dst in ((w1_ref, wf1), (w3_ref, wf3), (w2_ref, wf2)):
                cps.append(pltpu.make_async_copy(src.at[layer, e], dst.at[slot, s], sem.at[slot]))
        return cps

    @pl.when(i == 0)
    def _():
        for c in weight_copies(ea_ref[0], eb_ref[0], slot_ref[0]):
            c.start()

    @pl.when(fresh_ref[i] == 1)
    def _():
        slot = slot_ref[i]
        for c in weight_copies(ea_ref[i], eb_ref[i], slot):
            c.wait()

        @pl.when(more_ref[i] == 1)
        def _():
            for c in weight_copies(na_ref[i], nb_ref[i], 1 - slot):
                c.start()

        for s in (0, 1):
            wbuf1[s] = wf1[slot, s].astype(BF16)
            wbuf3[s] = wf3[slot, s].astype(BF16)
            wbuf2[s] = wf2[slot, s].astype(BF16)

    @pl.when(valid_ref[i] == 1)
    def _():
        xb = x_ref[...].reshape(x_ref.shape[0], nk * LANES).astype(BF16)
        for s in (0, 1):
            h1 = _dot(xb, wbuf1[s])
            h3 = _dot(xb, wbuf3[s])
            y = _dot(((h1 * _sigmoid(h1)) * h3).astype(BF16), wbuf2[s])
            o_ref[:, s * nk:(s + 1) * nk, :] = y.reshape(y.shape[0], nk, LANES)

    @pl.when(valid_ref[i] == 0)
    def _():
        o_ref[...] = jnp.zeros_like(o_ref)


def _moe_experts(plan, xs, w1, w3, w2, layer):
    n_slots, nk, _ = xs.shape
    d = nk * LANES
    ff = w1.shape[-1]
    rows = lambda k: pl.BlockSpec((MOE_TILE, k, LANES), lambda i, *_: (i, 0, 0))
    hbm = pl.BlockSpec(memory_space=pl.ANY)
    grid_spec = pltpu.PrefetchScalarGridSpec(
        num_scalar_prefetch=len(plan),
        grid=(n_slots // MOE_TILE,),
        in_specs=[rows(nk), hbm, hbm, hbm],
        out_specs=rows(2 * nk),
        scratch_shapes=[pltpu.VMEM((2, 2, d, ff), F32), pltpu.VMEM((2, 2, d, ff), F32), pltpu.VMEM((2, 2, ff, d), F32),
                        pltpu.VMEM((2, d, ff), BF16), pltpu.VMEM((2, d, ff), BF16), pltpu.VMEM((2, ff, d), BF16),
                        pltpu.SemaphoreType.DMA((2,))],
    )
    return pl.pallas_call(
        functools.partial(_moe_kernel, layer=layer),
        out_shape=jax.ShapeDtypeStruct((n_slots, 2 * nk, LANES), F32),
        grid_spec=grid_spec,
        compiler_params=_cparams(("arbitrary",)),
        name="moe_experts",
    )(*plan, xs, w1, w3, w2)


def _moe_plan(rinfo, counts, n_slots):
    n_tiles = n_slots // MOE_TILE
    cnt = counts[:N_BUCKETS, 0].astype(I32)
    padded = ((cnt + MOE_TILE - 1) // MOE_TILE) * MOE_TILE
    ends = jnp.cumsum(padded)
    starts = ends - padded
    slots = (rinfo[:, 0].astype(I32), rinfo[:, 3].astype(I32), starts)
    n_valid = ends[-1] // MOE_TILE
    tiles = jnp.arange(n_tiles, dtype=I32)
    tb = jnp.sum((ends[None, :] <= (tiles * MOE_TILE)[:, None]).astype(I32), axis=1)
    tb = jnp.minimum(tb, N_BUCKETS - 1)
    valid = tiles < n_valid
    tb = jnp.where(valid, tb, tb[jnp.maximum(n_valid - 1, 0)])
    lo = jnp.asarray(PAIR_LO, I32)
    hi = jnp.asarray(PAIR_HI, I32)
    ea = (tb // 6) * EXPERTS_PER_GROUP + lo[tb % 6]
    eb = (tb // 6) * EXPERTS_PER_GROUP + hi[tb % 6]
    fresh = jnp.concatenate([jnp.ones((1,), I32), (tb[1:] != tb[:-1]).astype(I32)])
    run_slot = (jnp.cumsum(fresh) - 1) % 2
    later_start = (tiles[None, :] > tiles[:, None]) & (fresh[None, :] == 1)
    nxt = jnp.min(jnp.where(later_start, tiles[None, :], n_tiles), axis=1)
    more = (nxt < n_tiles).astype(I32)
    nxt = jnp.minimum(nxt, n_tiles - 1)
    plan = (ea, eb, valid.astype(I32), fresh, run_slot.astype(I32), ea[nxt], eb[nxt], more)
    return slots, plan


def _moe(u_tiles, rinfo, counts, w1, w3, w2, layer):
    t = u_tiles.shape[0]
    n_slots = t + N_BUCKETS * MOE_TILE
    slots, plan = _moe_plan(rinfo, counts, n_slots)
    xs = _scatter_tokens(slots, u_tiles, n_slots)
    ys = _moe_experts(plan, xs, w1, w3, w2, layer)
    return slots, ys


def _combine_kernel(bucket_ref, rank_ref, starts_ref, ys_ref, hm_ref, rinfo_ref, mp_ref, ms_ref, nf_ref, *rest,
                    npt, final):
    if final:
        yp_ref, ysm_ref, gbuf, sem = rest
    else:
        h_ref, gbuf, sem = rest
    i = pl.program_id(0)
    n = pl.num_programs(0)
    rows = gbuf.shape[1]
    nk = gbuf.shape[2] // 2

    def copy(tile, r):
        slot = tile % 2
        src = ys_ref.at[_slot_of(bucket_ref, rank_ref, starts_ref, tile * rows + r)]
        return pltpu.make_async_copy(src, gbuf.at[slot, r], sem.at[slot])

    @pl.when(i == 0)
    def _():
        _for_rows(rows, lambda r: copy(0, r).start())

    @pl.when(i + 1 < n)
    def _():
        _for_rows(rows, lambda r: copy(i + 1, r).start())

    _for_rows(rows, lambda r: copy(i, r).wait())
    slot = i % 2
    ya = gbuf[slot, :, 0:nk, :].reshape(rows, nk * LANES)
    yb = gbuf[slot, :, nk:2 * nk, :].reshape(rows, nk * LANES)
    rinfo = rinfo_ref[...]
    moe = rinfo[:, 1:2] * ya + rinfo[:, 2:3] * yb

    def run(is_prompt):
        m3 = (mp_ref if is_prompt else ms_ref)[...]
        h = _gate_residual(hm_ref[...], m3, 5, moe)
        if final:
            y = h * lax.rsqrt(jnp.mean(h * h, axis=-1, keepdims=True) + NORM_EPS) * nf_ref[...]
            (yp_ref if is_prompt else ysm_ref)[...] = y
        else:
            h_ref[...] = h

    _by_group(i, npt, run)


def _combine(slots, ys, hmid, rinfo, mp, ms, nf, *, tp, seq, final):
    t, d = hmid.shape
    npt = tp // ROW_TILE
    tiles_per_seq = seq // ROW_TILE
    mpm = lambda f: (lambda i, *_: f(i))
    mp_spec, ms_spec = _mod_specs(d, npt, tiles_per_seq)
    yp_spec, ysm_spec = _pair_specs(d, npt)
    wrap = lambda s: pl.BlockSpec(s.block_shape, mpm(s.index_map))
    if final:
        out_shape = (jax.ShapeDtypeStruct((tp, d), F32), jax.ShapeDtypeStruct((t - tp, d), F32))
        out_specs = (wrap(yp_spec), wrap(ysm_spec))
    else:
        out_shape = jax.ShapeDtypeStruct((t, d), F32)
        out_specs = pl.BlockSpec((ROW_TILE, d), lambda i, *_: (i, 0))
    grid_spec = pltpu.PrefetchScalarGridSpec(
        num_scalar_prefetch=3,
        grid=(t // ROW_TILE,),
        in_specs=[pl.BlockSpec(memory_space=pl.ANY),
                  pl.BlockSpec((ROW_TILE, d), lambda i, *_: (i, 0)),
                  pl.BlockSpec((ROW_TILE, LANES), lambda i, *_: (i, 0)),
                  wrap(mp_spec), wrap(ms_spec),
                  pl.BlockSpec((1, d), lambda i, *_: (0, 0))],
        out_specs=out_specs,
        scratch_shapes=[pltpu.VMEM((2, ROW_TILE) + ys.shape[1:], F32), pltpu.SemaphoreType.DMA((2,))],
    )
    kern = functools.partial(_combine_kernel, npt=npt, final=final)
    return pl.pallas_call(
        kern, out_shape=out_shape, grid_spec=grid_spec,
        compiler_params=_cparams(("arbitrary",)),
        name="moe_combine_final" if final else "moe_combine",
    )(*slots, ys, hmid, rinfo, mp, ms, nf)


def _odd_pre_kernel(hp_ref, hs_ref, mp_ref, ms_ref, nw_ref, w_ref, gate_ref, rec_ref, *, npt):
    i = pl.program_id(0)
    width = gate_ref.shape[1]

    def run(is_prompt):
        h = (hp_ref if is_prompt else hs_ref)[...]
        m3 = (mp_ref if is_prompt else ms_ref)[...]
        ub = _norm_mod(h, nw_ref[...], m3, 1, 0).astype(BF16)
        x = _dot(ub, w_ref[:, 0:width])
        gate_ref[...] = 0.5 * x * (1.0 + jnp.tanh(np.sqrt(2.0 / np.pi) * (x + 0.044715 * (x * x * x))))
        rec_ref[...] = _dot(ub, w_ref[:, width:2 * width])

    _by_group(i, npt, run)


def _odd_pre(h, mp, ms, nw, w_in, *, tp, seq):
    t, d = h.shape
    npt = tp // ROW_TILE
    width = w_in.shape[1] // 2
    const = lambda shape: pl.BlockSpec(shape, lambda i: (0,) * len(shape))
    tile = lambda c: pl.BlockSpec((ROW_TILE, c), lambda i: (i, 0))
    kern = functools.partial(_odd_pre_kernel, npt=npt)
    return pl.pallas_call(
        kern,
        out_shape=(jax.ShapeDtypeStruct((t, width), F32), jax.ShapeDtypeStruct((t, width), F32)),
        grid=(t // ROW_TILE,),
        in_specs=[*_pair_specs(d, npt, off_s=npt), *_mod_specs(d, npt, seq // ROW_TILE),
                  const((1, d)), const(w_in.shape)],
        out_specs=(tile(width), tile(width)),
        compiler_params=_cparams(("arbitrary",)),
        name="odd_pre",
    )(h, h, mp, ms, nw, w_in)


def _lru_kernel(rec_ref, gate_ref, c0_ref, h0_ref, cw_ref, cb_ref, wa_ref, ba_ref, wx_ref, bx_ref, lam_ref,
                yg_ref, cs_ref, hl_ref, hist, hc, a_s, h_s, *, nb, tl):
    i = pl.program_id(1)
    r, width = rec_ref.shape
    taps = CONV_WIDTH
    blk = width // LRU_HEADS

    @pl.when(i == 0)
    def _():
        hist[:, SUBLANES - (taps - 1):SUBLANES, :] = c0_ref[...]
        hc[...] = h0_ref[...]

    rec3 = rec_ref[...].reshape(nb, tl, width)
    hist[:, SUBLANES:SUBLANES + tl, :] = rec3
    cw = cw_ref[...]
    conv = cb_ref[...] + rec3 * cw[taps - 1:taps, :]
    for back in range(1, taps):
        conv = conv + hist[:, SUBLANES - back:SUBLANES - back + tl, :] * cw[taps - 1 - back:taps - back, :]
    tail = hist[:, SUBLANES + tl - (taps - 1):SUBLANES + tl, :]
    cs_ref[...] = tail
    hist[:, SUBLANES - (taps - 1):SUBLANES, :] = tail

    cf = conv.reshape(r, width)
    cb16 = cf.astype(BF16)
    gr = jnp.concatenate([_dot(cb16[:, h * blk:(h + 1) * blk], wa_ref[h]) for h in range(LRU_HEADS)], axis=1)
    gi = jnp.concatenate([_dot(cb16[:, h * blk:(h + 1) * blk], wx_ref[h]) for h in range(LRU_HEADS)], axis=1)
    rg = _sigmoid(gr + ba_ref[...])
    ig = _sigmoid(gi + bx_ref[...])
    z = -lam_ref[...]
    softplus = jnp.maximum(z, 0.0) + jnp.log1p(jnp.exp(-jnp.abs(z)))
    log_a = -LRU_C * rg * softplus
    a = jnp.exp(log_a)
    gx = jnp.sqrt(1.0 - a * a) * (ig * cf)

    groups = r // SUBLANES
    sub = lax.broadcasted_iota(I32, (groups, SUBLANES, width), 1)
    aa, hh = a.reshape(groups, SUBLANES, width), gx.reshape(groups, SUBLANES, width)
    for dlt in (1, 2, 4):
        ok = sub >= dlt
        a_sh = pltpu.roll(aa, dlt, 1)
        h_sh = pltpu.roll(hh, dlt, 1)
        hh = jnp.where(ok, hh + aa * h_sh, hh)
        aa = jnp.where(ok, aa * a_sh, aa)
    aa, hh = aa.reshape(r, width), hh.reshape(r, width)

    if tl == SUBLANES:
        y3 = hh.reshape(nb, tl, width) + aa.reshape(nb, tl, width) * hc[...]
        hl_ref[...] = y3[:, tl - 1:tl, :]
        y = y3.reshape(r, width)
    else:
        a_s[...] = aa
        h_s[...] = hh

        def group(j, carry):
            r0 = pl.multiple_of(j * SUBLANES, SUBLANES)
            yj = h_s[pl.ds(r0, SUBLANES), :] + a_s[pl.ds(r0, SUBLANES), :] * carry
            h_s[pl.ds(r0, SUBLANES), :] = yj
            return yj[SUBLANES - 1:SUBLANES, :]

        last = lax.fori_loop(0, r // SUBLANES, group, hc[0])
        hc[0] = last
        hl_ref[0] = last
        y = h_s[...]
    yg_ref[...] = (y * gate_ref[...]).astype(BF16)


def _lru(rec, gate, conv0, h0, cw, cb, wa, ba, wx, bx, lam, *, batch, seq, row0):
    width = rec.shape[1]
    taps = CONV_WIDTH
    if seq == SUBLANES:
        nb, tl = ROW_TILE // SUBLANES, SUBLANES
    else:
        nb, tl = 1, ROW_TILE
    r = nb * tl
    n_l = seq // tl
    blk0 = row0 // r
    const = lambda shape: pl.BlockSpec(shape, lambda b, i: (0,) * len(shape))
    tile_in = pl.BlockSpec((r, width), lambda b, i: (blk0 + b * n_l + i, 0))
    kern = functools.partial(_lru_kernel, nb=nb, tl=tl)
    return pl.pallas_call(
        kern,
        out_shape=(jax.ShapeDtypeStruct((batch * seq, width), BF16),
                   jax.ShapeDtypeStruct((batch, taps - 1, width), F32),
                   jax.ShapeDtypeStruct((batch, 1, width), F32)),
        grid=(batch // nb, n_l),
        in_specs=[tile_in, tile_in,
                  pl.BlockSpec((nb, taps - 1, width), lambda b, i: (b, 0, 0)),
                  pl.BlockSpec((nb, 1, width), lambda b, i: (b, 0, 0)),
                  const((taps, width)), const((1, width)), const(wa.shape), const((1, width)),
                  const(wx.shape), const((1, width)), const((1, width))],
        out_specs=(pl.BlockSpec((r, width), lambda b, i: (b * n_l + i, 0)),
                   pl.BlockSpec((nb, taps - 1, width), lambda b, i: (b, 0, 0)),
                   pl.BlockSpec((nb, 1, width), lambda b, i: (b, 0, 0))),
        scratch_shapes=[pltpu.VMEM((nb, tl + SUBLANES, width), F32), pltpu.VMEM((nb, 1, width), F32),
                        pltpu.VMEM((r, width), F32), pltpu.VMEM((r, width), F32)],
        compiler_params=_cparams(("arbitrary", "arbitrary")),
        name="rg_lru_sample" if seq == SUBLANES else "rg_lru_prompt",
    )(rec, gate, conv0, h0, cw, cb, wa, ba, wx, bx, lam)


def kernel(x_prompt, x_sample, c_prompt, c_sample, cache_k, cache_v, state_hgrn, state_conv, state_lru, page_table,
           ada_w, ada_b, norm_mix, norm_ffn, norm_final, w_in_even, w_out_even, hg_lower_bounds, hg_norm_w, w_in_odd,
           conv_w, conv_b, lru_wa, lru_ba, lru_wx, lru_bx, lru_lambda, w_out_odd, router_group_w, router_group_b,
           router_expert_w, router_expert_b, moe_w1, moe_w3, moe_w2):
    batch, seq, d = x_prompt.shape
    dec_batch, dec_seq, _ = x_sample.shape
    depth = ada_w.shape[0]
    assert depth == 2 and seq % ROW_TILE == 0 and seq % MOBA_BLOCK == 0
    tp, ts = batch * seq, dec_batch * dec_seq
    n_pages, page = page_table.shape[1], cache_k.shape[2]
    past_len = n_pages * page
    aw = MOBA_HEADS * MOBA_HEAD_DIM
    hw = HG_HEADS * HG_DK

    xp = x_prompt.reshape(tp, d)
    xs = x_sample.reshape(ts, d)
    mods = _ada_mods(jnp.concatenate([c_prompt, c_sample], axis=0), ada_w, ada_b)
    mods = mods.reshape(depth, batch + dec_batch, 6, d)
    mods_p, mods_s = mods[:, :batch], mods[:, batch:]

    half = MOBA_HEAD_DIM // 2
    inv_freq = jnp.power(ROPE_THETA, -jnp.arange(half, dtype=F32) / half)
    invf = jnp.tile(inv_freq, LANES // half).reshape(1, LANES)

    def router_mats(l):
        rw = jnp.zeros((d, LANES), F32)
        rw = rw.at[:, 0:N_GROUPS].set(router_group_w[l])
        rw = rw.at[:, N_GROUPS:N_GROUPS + N_GROUPS * EXPERTS_PER_GROUP].set(router_expert_w[l])
        rb = jnp.zeros((1, LANES), F32)
        rb = rb.at[0, 0:N_GROUPS].set(router_group_b[l])
        rb = rb.at[0, N_GROUPS:N_GROUPS + N_GROUPS * EXPERTS_PER_GROUP].set(router_expert_b[l])
        rw_hi = rw.astype(BF16)
        rw_lo = (rw - rw_hi.astype(F32)).astype(BF16)
        return jnp.concatenate([rw_hi, rw_lo], axis=1), rb

    q_all, k_p, k_s, v_p, v_s, hg = _even_pre(
        xp, xs, mods_p[0], mods_s[0], norm_mix[0].reshape(1, d), w_in_even[0].astype(BF16), hg_lower_bounds, invf,
        seq=seq, past_len=past_len, layer_slot=0)
    oa_p = _attn_prompt(q_all, k_p, v_p, batch=batch, seq=seq)
    oa_s = _attn_sample(page_table, q_all, k_s, v_s, cache_k, cache_v,
                        layer=0, dec_batch=dec_batch, dec_seq=dec_seq, q_row0=tp)
    hnw = hg_norm_w[0].reshape(1, HG_DK)
    ob_p, hgrn_p = _hgrn_prompt(hg, hnw, batch=batch, seq=seq)
    ob_s, hgrn_s = _hgrn_sample(hg, state_hgrn[0].reshape(dec_batch * HG_HEADS, HG_DK, HG_DK), hnw,
                                dec_batch=dec_batch, dec_seq=dec_seq, row0=tp)
    w_out = w_out_even[0].astype(BF16)
    rw, rb = router_mats(0)
    hmid, u, rinfo, counts = _post_mixer((xp, xs), 0, mods_p[0], mods_s[0], [(oa_p, oa_s), (ob_p, ob_s)],
                                         [w_out[:aw], w_out[aw:]], norm_ffn[0].reshape(1, d), rw, rb, seq=seq)
    dest, ys = _moe(u, rinfo, counts, moe_w1, moe_w3, moe_w2, 0)
    h1 = _combine(dest, ys, hmid, rinfo, mods_p[0], mods_s[0], norm_final.reshape(1, d), tp=tp, seq=seq,
                  final=False)

    gate, rec = _odd_pre(h1, mods_p[1], mods_s[1], norm_mix[1].reshape(1, d), w_in_odd[0].astype(BF16), tp=tp, seq=seq)
    width = rec.shape[1]
    lru_args = (conv_w[0], conv_b[0].reshape(1, width), lru_wa[0].astype(BF16), lru_ba[0].reshape(1, width),
                lru_wx[0].astype(BF16), lru_bx[0].reshape(1, width), lru_lambda[0].reshape(1, width))
    yg_p, conv_p, lru_p = _lru(rec, gate, jnp.zeros((batch, CONV_WIDTH - 1, width), F32),
                               jnp.zeros((batch, 1, width), F32), *lru_args, batch=batch, seq=seq, row0=0)
    yg_s, conv_s, lru_s = _lru(rec, gate, state_conv[0], state_lru[0].reshape(dec_batch, 1, width), *lru_args,
                               batch=dec_batch, seq=dec_seq, row0=tp)
    rw, rb = router_mats(1)
    hmid, u, rinfo, counts = _post_mixer((h1, h1), tp // ROW_TILE, mods_p[1], mods_s[1], [(yg_p, yg_s)],
                                         [w_out_odd[0].astype(BF16)], norm_ffn[1].reshape(1, d), rw, rb, seq=seq)
    dest, ys = _moe(u, rinfo, counts, moe_w1, moe_w3, moe_w2, 1)
    y_p, y_s = _combine(dest, ys, hmid, rinfo, mods_p[1], mods_s[1], norm_final.reshape(1, d), tp=tp, seq=seq,
                        final=True)

    pos_major = lambda a: jnp.transpose(a.reshape(batch, MOBA_HEADS, MOBA_HEAD_DIM, seq), (0, 3, 1, 2))[None]
    return (y_p.reshape(batch, seq, d), y_s.reshape(dec_batch, dec_seq, d),
            pos_major(k_p), pos_major(v_p),
            k_s.reshape(1, dec_batch, dec_seq, MOBA_HEADS, MOBA_HEAD_DIM),
            v_s.reshape(1, dec_batch, dec_seq, MOBA_HEADS, MOBA_HEAD_DIM),
            hgrn_p.reshape(1, batch, HG_HEADS, HG_DK, HG_DK), hgrn_s.reshape(1, dec_batch, HG_HEADS, HG_DK, HG_DK),
            conv_p.reshape(1, batch, CONV_WIDTH - 1, width), conv_s.reshape(1, dec_batch, CONV_WIDTH - 1, width),
            lru_p.reshape(1, batch, width), lru_s.reshape(1, dec_batch, width))
```

```python
import functools

import numpy as np
import jax
import jax.numpy as jnp
from jax import lax
from jax.experimental import pallas as pl
from jax.experimental.pallas import tpu as pltpu

F32 = jnp.float32
BF16 = jnp.bfloat16
I32 = jnp.int32
HIGHEST = lax.Precision.HIGHEST

MOBA_HEADS = 8
MOBA_HEAD_DIM = 64
MOBA_BLOCK = 256
MOBA_TOPK = 3
ROPE_THETA = 10000.0
HG_HEADS = 4
HG_DK = 128
LRU_HEADS = 4
CONV_WIDTH = 4
LRU_C = 8.0
N_GROUPS = 4
EXPERTS_PER_GROUP = 4
NORM_EPS = 1e-6

LANES = 128
SUBLANES = 8
VMEM_LIMIT = 56 * 1024 * 1024

ROW_TILE = 256
MOE_TILE = 256
HG_CHUNK = 128
N_BUCKETS = N_GROUPS * 6
PAIR_LO = (0, 0, 0, 1, 1, 2)
PAIR_HI = (1, 2, 3, 2, 3, 3)
NEG = -1e30


def _cparams(sem, vmem=VMEM_LIMIT):
    return pltpu.CompilerParams(dimension_semantics=sem, vmem_limit_bytes=vmem)


def _dot(a, b, **kw):
    return jnp.dot(a, b, preferred_element_type=F32, **kw)


def _dot_nt(a, b, **kw):
    return lax.dot_general(a, b, (((1,), (1,)), ((), ())), preferred_element_type=F32, **kw)


def _sigmoid(x):
    return jax.nn.sigmoid(x)


def _by_group(i, n_prompt_tiles, fn):
    @pl.when(i < n_prompt_tiles)
    def _():
        fn(True)

    @pl.when(i >= n_prompt_tiles)
    def _():
        fn(False)


def _pair_specs(cols, npt, rows=ROW_TILE, off_p=0, off_s=0):
    return (pl.BlockSpec((rows, cols), lambda i: (off_p + jnp.minimum(i, npt - 1), 0)),
            pl.BlockSpec((rows, cols), lambda i: (off_s + jnp.maximum(i - npt, 0), 0)))


def _mod_specs(d, npt, tiles_per_seq, rows=ROW_TILE):
    return (pl.BlockSpec((1, 6, d), lambda i: (jnp.minimum(i, npt - 1) // tiles_per_seq, 0, 0)),
            pl.BlockSpec((rows // SUBLANES, 6, d), lambda i: (jnp.maximum(i - npt, 0), 0, 0)))


def _norm_mod(x, nw, m3, scale_idx, shift_idx):
    r, d = x.shape
    nb = m3.shape[0]
    var = jnp.mean(x * x, axis=-1, keepdims=True)
    y = x * lax.rsqrt(var + NORM_EPS) * nw
    y3 = y.reshape(nb, r // nb, d)
    u3 = y3 * (1.0 + m3[:, scale_idx:scale_idx + 1, :]) + m3[:, shift_idx:shift_idx + 1, :]
    return u3.reshape(r, d)


def _gate_residual(h, m3, gate_idx, out):
    r, d = h.shape
    nb = m3.shape[0]
    return (h.reshape(nb, r // nb, d) + m3[:, gate_idx:gate_idx + 1, :] * out.reshape(nb, r // nb, d)).reshape(r, d)


def _ada_kernel(c_ref, w_ref, b_ref, o_ref):
    o_ref[0] = _dot(c_ref[...], w_ref[0], precision=HIGHEST) + b_ref[0]


def _ada_mods(c_all, ada_w, ada_b):
    depth, d, n6 = ada_w.shape
    nb = c_all.shape[0]
    tn = 1024
    return pl.pallas_call(
        _ada_kernel,
        out_shape=jax.ShapeDtypeStruct((depth, nb, n6), F32),
        grid=(depth, n6 // tn),
        in_specs=[pl.BlockSpec((nb, d), lambda l, n: (0, 0)),
                  pl.BlockSpec((1, d, tn), lambda l, n: (l, 0, n)),
                  pl.BlockSpec((1, 1, tn), lambda l, n: (l, 0, n))],
        out_specs=pl.BlockSpec((1, nb, tn), lambda l, n: (l, 0, n)),
        compiler_params=_cparams(("arbitrary", "arbitrary")),
        name="ada_mods",
    )(c_all, ada_w, ada_b.reshape(depth, 1, n6))


def _even_pre_kernel(xp_ref, xs_ref, mp_ref, ms_ref, nw_ref, w_ref, hb_ref, invf_ref,
                     q_ref, kp_ref, ks_ref, vp_ref, vs_ref, hg_ref, *, npt, tiles_per_seq, past_len, layer_slot):
    i = pl.program_id(0)
    aw = MOBA_HEADS * MOBA_HEAD_DIM
    hw = HG_HEADS * HG_DK

    def run(is_prompt):
        x = (xp_ref if is_prompt else xs_ref)[...]
        m3 = (mp_ref if is_prompt else ms_ref)[...]
        r = x.shape[0]
        nb = m3.shape[0]
        tl = r // nb
        ub = _norm_mod(x, nw_ref[...], m3, 1, 0).astype(BF16)

        pos0 = (i % tiles_per_seq) * r if is_prompt else past_len
        pos = (pos0 + lax.broadcasted_iota(I32, (nb, tl, LANES), 1)).astype(F32).reshape(r, LANES)
        ang = pos * invf_ref[...]
        reps = aw // LANES
        cos = jnp.concatenate([jnp.cos(ang)] * reps, axis=1)
        sin = jnp.concatenate([jnp.sin(ang)] * reps, axis=1)
        lane = lax.broadcasted_iota(I32, (r, aw), 1)
        half = MOBA_HEAD_DIM // 2
        first = (lane % MOBA_HEAD_DIM) < half
        sin = jnp.where(first, -sin, sin)

        def rope(t):
            rot = jnp.where(first, pltpu.roll(t, aw - half, 1), pltpu.roll(t, half, 1))
            return t * cos + rot * sin

        def seg(a, b):
            return _dot(ub, w_ref[:, a:b])

        q_ref[...] = rope(seg(0, aw))
        k = rope(seg(aw, 2 * aw))
        v = seg(2 * aw, 3 * aw)
        if is_prompt:
            kp_ref[0] = k.T
            vp_ref[0] = v.T
        else:
            ks_ref[...] = k
            vs_ref[...] = v
        o = 3 * aw
        qb = seg(o, o + hw)
        fb = seg(o + hw, o + 2 * hw)
        hb = hb_ref[...]
        e = jnp.exp(hb - jnp.max(hb, axis=0, keepdims=True))
        sm = e / jnp.sum(e, axis=0, keepdims=True)
        lb = jnp.sum(sm[0:layer_slot + 1, :], axis=0, keepdims=True)
        hg_ref[:, 0:hw] = qb * _sigmoid(qb)
        hg_ref[:, hw:2 * hw] = lb + (1.0 - lb) * _sigmoid(fb)
        hg_ref[:, 2 * hw:3 * hw] = seg(o + 2 * hw, o + 3 * hw)
        hg_ref[:, 3 * hw:4 * hw] = seg(o + 3 * hw, o + 4 * hw)

    _by_group(i, npt, run)


def _even_pre(xp, xs, mp, ms, nw, w_in, hb, invf, *, seq, past_len, layer_slot):
    tp, d = xp.shape
    ts = xs.shape[0]
    npt, nst = tp // ROW_TILE, ts // ROW_TILE
    tiles_per_seq = seq // ROW_TILE
    aw = MOBA_HEADS * MOBA_HEAD_DIM
    hw = HG_HEADS * HG_DK
    t = tp + ts
    const = lambda shape: pl.BlockSpec(shape, lambda i: (0,) * len(shape))
    kern = functools.partial(_even_pre_kernel, npt=npt, tiles_per_seq=tiles_per_seq, past_len=past_len,
                             layer_slot=layer_slot)
    _, ks_spec = _pair_specs(aw, npt)
    kp_spec = pl.BlockSpec((1, aw, ROW_TILE), lambda i: (jnp.minimum(i, npt - 1) // tiles_per_seq, 0,
                                                          jnp.minimum(i, npt - 1) % tiles_per_seq))
    return pl.pallas_call(
        kern,
        out_shape=(jax.ShapeDtypeStruct((t, aw), F32),
                   jax.ShapeDtypeStruct((tp // seq, aw, seq), F32), jax.ShapeDtypeStruct((ts, aw), F32),
                   jax.ShapeDtypeStruct((tp // seq, aw, seq), F32), jax.ShapeDtypeStruct((ts, aw), F32),
                   jax.ShapeDtypeStruct((t, 4 * hw), F32)),
        grid=(npt + nst,),
        in_specs=[*_pair_specs(d, npt), *_mod_specs(d, npt, tiles_per_seq),
                  const((1, d)), const(w_in.shape), const(hb.shape), const((1, LANES))],
        out_specs=(pl.BlockSpec((ROW_TILE, aw), lambda i: (i, 0)), kp_spec, ks_spec, kp_spec, ks_spec,
                   pl.BlockSpec((ROW_TILE, 4 * hw), lambda i: (i, 0))),
        compiler_params=_cparams(("arbitrary",)),
        name="even_pre",
    )(xp, xs, mp, ms, nw, w_in, hb, invf)


def _top_blocks(g, valid):
    lane = lax.broadcasted_iota(I32, g.shape, 1)
    jl = lane % SUBLANES
    g = jnp.where(valid, g, -jnp.inf)
    cnt = jnp.zeros(g.shape, I32)
    for d in range(1, SUBLANES):
        wrapped = (jl + d) >= SUBLANES
        pg = jnp.where(wrapped, pltpu.roll(g, SUBLANES - d, 1), pltpu.roll(g, LANES - d, 1))
        beats = (pg > g) | ((pg == g) & wrapped)
        cnt = cnt + beats.astype(I32)
    return valid & (cnt < MOBA_TOPK)


def _head_expand(rows8, n_heads, width):
    x = jnp.concatenate([rows8] * n_heads, axis=0)
    r = lax.broadcasted_iota(I32, x.shape, 0) // SUBLANES
    l = lax.broadcasted_iota(I32, x.shape, 1) // (width // n_heads)
    return jnp.where(r == l, x, 0.0)


def _block_indicator(seq):
    key_blk = np.arange(seq)[None, :] // MOBA_BLOCK
    row = np.arange(LANES)[:, None]
    ind = (row < MOBA_HEADS * SUBLANES) & ((row % SUBLANES) == key_blk)
    return jnp.asarray(ind.astype(np.float32), dtype=BF16)


def _attn_prompt_kernel(q_ref, k_ref, v_ref, ind_ref, o_ref, km_ref, *, n_blk):
    i = pl.program_id(1)
    blk = MOBA_BLOCK
    aw = MOBA_HEADS * MOBA_HEAD_DIM
    scale = MOBA_HEAD_DIM ** -0.5

    @pl.when(i == 0)
    def _():
        r = lax.broadcasted_iota(I32, (aw, LANES), 0) // MOBA_HEAD_DIM
        c = lax.broadcasted_iota(I32, (aw, LANES), 1)
        km = jnp.zeros((aw, LANES), F32)
        for j in range(n_blk):
            mean_j = jnp.mean(k_ref[0, :, j * blk:(j + 1) * blk], axis=1, keepdims=True)
            km = jnp.where(((c % SUBLANES) == j) & ((c // SUBLANES) == r), mean_j, km)
        km_ref[...] = km

    q = q_ref[...]
    gate = _dot(q, km_ref[...], precision=HIGHEST)
    lane = lax.broadcasted_iota(I32, gate.shape, 1)
    jl = lane % SUBLANES
    valid = (jl < i) & (lane < MOBA_HEADS * SUBLANES)
    keep = _top_blocks(gate, valid) | (jl == i)
    bias = jnp.where(keep, 0.0, NEG)

    row = lax.broadcasted_iota(I32, (blk, blk), 0)
    col = lax.broadcasted_iota(I32, (blk, blk), 1)
    tril = row >= col

    def tile(c):
        n_keys = (c + 1) * blk
        for hp in range(aw // LANES):
            cols = slice(hp * LANES, (hp + 1) * LANES)
            rhs = jnp.concatenate([k_ref[0, cols, 0:n_keys].astype(BF16), ind_ref[:, 0:n_keys]], axis=0)
            vv = v_ref[0, cols, 0:n_keys].astype(BF16)
            qp = q[:, cols] * scale
            outs = []
            for s in (0, 1):
                h = 2 * hp + s
                qm = jnp.where((lane // MOBA_HEAD_DIM) == s, qp, 0.0).astype(BF16)
                hb = jnp.where((lane // SUBLANES) == h, bias, 0.0).astype(BF16)
                sc = _dot(jnp.concatenate([qm, hb], axis=1), rhs)
                own = jnp.where(tril, sc[:, c * blk:], NEG)
                sc = own if c == 0 else jnp.concatenate([sc[:, :c * blk], own], axis=1)
                p = jnp.exp(sc - jnp.max(sc, axis=1, keepdims=True))
                outs.append(_dot_nt(p.astype(BF16), vv) / jnp.sum(p, axis=1, keepdims=True))
            o_ref[:, cols] = jnp.where((lane // MOBA_HEAD_DIM) == 0, outs[0], outs[1]).astype(BF16)

    for c in range(n_blk):
        pl.when(i == c)(functools.partial(tile, c))


def _attn_prompt(q_all, k_p, v_p, *, batch, seq):
    aw = MOBA_HEADS * MOBA_HEAD_DIM
    n_blk = seq // MOBA_BLOCK
    assert n_blk <= SUBLANES
    ind = _block_indicator(seq)
    kern = functools.partial(_attn_prompt_kernel, n_blk=n_blk)
    return pl.pallas_call(
        kern,
        out_shape=jax.ShapeDtypeStruct((batch * seq, aw), BF16),
        grid=(batch, n_blk),
        in_specs=[pl.BlockSpec((MOBA_BLOCK, aw), lambda b, i: (b * n_blk + i, 0)),
                  pl.BlockSpec((1, aw, seq), lambda b, i: (b, 0, 0)),
                  pl.BlockSpec((1, aw, seq), lambda b, i: (b, 0, 0)),
                  pl.BlockSpec((LANES, seq), lambda b, i: (0, 0))],
        out_specs=pl.BlockSpec((MOBA_BLOCK, aw), lambda b, i: (b * n_blk + i, 0)),
        scratch_shapes=[pltpu.VMEM((aw, LANES), F32)],
        compiler_params=_cparams(("arbitrary", "arbitrary")),
        name="moba_prompt",
    )(q_all, k_p, v_p, ind)


def _attn_sample_kernel(pt_ref, q_ref, kn_ref, vn_ref, ck_ref, cv_ref, o_ref, kbuf, vbuf, sem,
                        *, n_pages, page, n_blk, layer):
    b = pl.program_id(0)
    nb = pl.num_programs(0)
    blk = MOBA_BLOCK
    hd = MOBA_HEAD_DIM
    dec = q_ref.shape[0]
    scale = hd ** -0.5

    def page_copies(seq_idx, slot):
        cps = []
        for p in range(n_pages):
            pg = pt_ref[seq_idx * n_pages + p]
            cps.append(pltpu.make_async_copy(ck_ref.at[layer, pg], kbuf.at[slot, p], sem.at[0, slot]))
            cps.append(pltpu.make_async_copy(cv_ref.at[layer, pg], vbuf.at[slot, p], sem.at[1, slot]))
        return cps

    @pl.when(b == 0)
    def _():
        for c in page_copies(0, 0):
            c.start()

    @pl.when(b + 1 < nb)
    def _():
        for c in page_copies(b + 1, (b + 1) % 2):
            c.start()

    slot = b % 2
    for c in page_copies(b, slot):
        c.wait()

    nh = MOBA_HEADS
    aw = nh * hd
    past = n_blk * blk
    kt = jnp.concatenate([kbuf[slot, p].reshape(aw, page) for p in range(n_pages)], axis=1)
    vt = jnp.concatenate([vbuf[slot, p].reshape(aw, page) for p in range(n_pages)], axis=1)
    qexp = _head_expand(q_ref[...], nh, aw)

    colj = lax.broadcasted_iota(I32, (aw, LANES), 1)
    km = jnp.zeros((aw, LANES), F32)
    for j in range(n_blk):
        km = jnp.where(colj == j, jnp.mean(kt[:, j * blk:(j + 1) * blk], axis=1, keepdims=True), km)
    gate = _dot(qexp, km, precision=HIGHEST)
    lane = lax.broadcasted_iota(I32, gate.shape, 1)
    sel = _top_blocks(gate, lane < n_blk).astype(F32)

    qs = (qexp * scale).astype(BF16)
    sc = _dot(qs, kt.astype(BF16))
    key_blk = lax.broadcasted_iota(I32, sc.shape, 1) // blk
    mask = jnp.zeros(sc.shape, F32)
    for j in range(n_blk):
        mask = jnp.where(key_blk == j, sel[:, j:j + 1], mask)
    sc = jnp.where(mask > 0.5, sc, NEG)

    pad = jnp.zeros((LANES - dec, aw), F32)
    kn = jnp.concatenate([kn_ref[...], pad], axis=0).astype(BF16)
    vn = jnp.concatenate([vn_ref[...], pad], axis=0).astype(BF16)
    qi = lax.broadcasted_iota(I32, gate.shape, 0) % dec
    sn = jnp.where(lane <= qi, _dot_nt(qs, kn), NEG)

    m = jnp.maximum(jnp.max(sc, axis=1, keepdims=True), jnp.max(sn, axis=1, keepdims=True))
    p = jnp.exp(sc - m)
    pn = jnp.exp(sn - m)
    l = jnp.sum(p, axis=1, keepdims=True) + jnp.sum(pn, axis=1, keepdims=True)
    o = (_dot_nt(p.astype(BF16), vt.astype(BF16)) + _dot(pn.astype(BF16), vn)) / l
    r = lax.broadcasted_iota(I32, o.shape, 0) // dec
    c = lax.broadcasted_iota(I32, o.shape, 1) // hd
    o = jnp.where(r == c, o, 0.0)
    acc = o[0:dec]
    for h in range(1, nh):
        acc = acc + o[h * dec:(h + 1) * dec]
    o_ref[...] = acc


def _attn_sample(page_table, q_all, k_s, v_s, cache_k, cache_v, *, layer, dec_batch, dec_seq, q_row0):
    _, n_phys, page, n_heads, hd = cache_k.shape
    aw = MOBA_HEADS * MOBA_HEAD_DIM
    n_pages = page_table.shape[1]
    past_len = n_pages * page
    assert past_len % MOBA_BLOCK == 0 and dec_seq == SUBLANES and past_len // MOBA_BLOCK == SUBLANES
    assert n_heads == MOBA_HEADS and hd == MOBA_HEAD_DIM
    kern = functools.partial(_attn_sample_kernel, n_pages=n_pages, page=page, n_blk=past_len // MOBA_BLOCK,
                             layer=layer)
    q_blk0 = q_row0 // dec_seq
    grid_spec = pltpu.PrefetchScalarGridSpec(
        num_scalar_prefetch=1,
        grid=(dec_batch,),
        in_specs=[pl.BlockSpec((dec_seq, aw), lambda b, pt: (q_blk0 + b, 0)),
                  pl.BlockSpec((dec_seq, aw), lambda b, pt: (b, 0)),
                  pl.BlockSpec((dec_seq, aw), lambda b, pt: (b, 0)),
                  pl.BlockSpec(memory_space=pl.ANY),
                  pl.BlockSpec(memory_space=pl.ANY)],
        out_specs=pl.BlockSpec((dec_seq, aw), lambda b, pt: (b, 0)),
        scratch_shapes=[pltpu.VMEM((2, n_pages, n_heads, hd, page), F32),
                        pltpu.VMEM((2, n_pages, n_heads, hd, page), F32), pltpu.SemaphoreType.DMA((2, 2))],
    )
    to_pos_minor = lambda c: jnp.transpose(c, (0, 1, 3, 4, 2))
    return pl.pallas_call(
        kern,
        out_shape=jax.ShapeDtypeStruct((dec_batch * dec_seq, aw), F32),
        grid_spec=grid_spec,
        compiler_params=_cparams(("arbitrary",)),
        name="moba_sample",
    )(page_table.reshape(-1), q_all, k_s, v_s, to_pos_minor(cache_k), to_pos_minor(cache_v))


def _hgrn_levels():
    n = HG_CHUNK
    return int(np.log2(n))


def _hgrn_cumsum_matrix():
    return jnp.asarray(np.tril(np.ones((HG_CHUNK, HG_CHUNK), np.float32)), dtype=BF16)


def _separator_rows(b, level):
    c, dk = b.shape
    m = 2 ** level
    if 2 * m >= SUBLANES:
        b3 = b.reshape(c // (2 * m), 2 * m, dk)
        return jnp.broadcast_to(b3[:, m - 1:m, :], b3.shape).reshape(c, dk)
    b3 = b.reshape(c // SUBLANES, SUBLANES, dk)
    sub = lax.broadcasted_iota(I32, b3.shape, 1)
    if m == 1:
        out = jnp.where(sub % 2 == 1, pltpu.roll(b3, 1, 1), b3)
    else:
        out = jnp.where(sub < 4, b3[:, 1:2, :], b3[:, 5:6, :])
    return out.reshape(c, dk)


def _hgrn_post(o, g, nw):
    o = o * lax.rsqrt(jnp.mean(o * o, axis=-1, keepdims=True) + NORM_EPS) * nw
    return o * (g * _sigmoid(g))


def _hgrn_prompt_kernel(q_ref, f_ref, i_ref, g_ref, ms_ref, nw_ref, o_ref, s_ref):
    c = HG_CHUNK
    dk = HG_DK
    n_chunks = q_ref.shape[0] // c
    levels = _hgrn_levels()
    row = lax.broadcasted_iota(I32, (c, c), 0)
    col = lax.broadcasted_iota(I32, (c, c), 1)
    rowk = lax.broadcasted_iota(I32, (c, dk), 0)
    upper = [((rowk // (2 ** l)) % 2) == 1 for l in range(levels)]
    same = [(row // (2 ** (l + 1))) == (col // (2 ** (l + 1))) for l in range(levels)]
    diag = row == col
    nw = nw_ref[...]

    heads = q_ref.shape[1] // dk

    def chunk(ci, sts):
        r0 = pl.multiple_of(ci * c, c)
        return tuple(one_head(r0, slice(hh * dk, (hh + 1) * dk), sts[hh]) for hh in range(heads))

    def one_head(r0, cols, st):
        q = q_ref[pl.ds(r0, c), cols]
        f = f_ref[pl.ds(r0, c), cols]
        v = i_ref[pl.ds(r0, c), cols]
        g = g_ref[pl.ds(r0, c), cols]
        lf = jnp.log(f)
        k = 1.0 - f
        hi = lf.astype(BF16)
        r1 = lf - hi.astype(F32)
        mid = r1.astype(BF16)
        lo = (r1 - mid.astype(F32)).astype(BF16)
        bb = _dot(ms_ref[...], jnp.concatenate([hi, mid, lo], axis=1))
        b = bb[:, 0:dk] + bb[:, dk:2 * dk] + bb[:, 2 * dk:3 * dk]
        a = jnp.where(diag, jnp.sum(q * k, axis=1, keepdims=True), 0.0)
        for l in range(levels):
            bs = _separator_rows(b, l)
            e = jnp.exp(jnp.where(upper[l], b - bs, bs - b))
            qp = jnp.where(upper[l], q * e, 0.0).astype(BF16)
            kp = jnp.where(upper[l], 0.0, k * e).astype(BF16)
            a = a + jnp.where(same[l], _dot_nt(qp, kp), 0.0)
        vb = v.astype(BF16)
        o = _dot_nt((q * jnp.exp(b)).astype(BF16), st.astype(BF16)) + _dot(a.astype(BF16), vb)
        blast = b[c - 1:c, :]
        kl = (k * jnp.exp(blast - b)).astype(BF16)
        st_new = st * jnp.exp(blast) + _dot(v.T.astype(BF16), kl)
        o_ref[pl.ds(r0, c), cols] = _hgrn_post(o, g, nw).astype(BF16)
        return st_new

    sts = lax.fori_loop(0, n_chunks, chunk, tuple(jnp.zeros((dk, dk), F32) for _ in range(heads)))
    for hh in range(heads):
        s_ref[hh] = sts[hh].T


def _hgrn_prompt(hg, nw, *, batch, seq):
    hw = HG_HEADS * HG_DK
    mstack = _hgrn_cumsum_matrix()
    hps = HG_HEADS
    groups = HG_HEADS // hps
    spec = lambda off: pl.BlockSpec((seq, hps * HG_DK), lambda b, h: (b, off + h))
    return pl.pallas_call(
        _hgrn_prompt_kernel,
        out_shape=(jax.ShapeDtypeStruct((batch * seq, hw), BF16),
                   jax.ShapeDtypeStruct((batch * HG_HEADS, HG_DK, HG_DK), F32)),
        grid=(batch, groups),
        in_specs=[spec(0), spec(groups), spec(2 * groups), spec(3 * groups),
                  pl.BlockSpec(mstack.shape, lambda b, h: (0, 0)),
                  pl.BlockSpec((1, HG_DK), lambda b, h: (0, 0))],
        out_specs=(pl.BlockSpec((seq, hps * HG_DK), lambda b, h: (b, h)),
                   pl.BlockSpec((hps, HG_DK, HG_DK), lambda b, h: (b * groups + h, 0, 0))),
        compiler_params=_cparams(("arbitrary", "arbitrary")),
        name="hgrn_prompt",
    )(hg, hg, hg, hg, mstack, nw)


def _hgrn_sample_kernel(hg_ref, s0_ref, nw_ref, o_ref, s_ref, *, dec, seqs):
    dk = HG_DK
    hw = HG_HEADS * HG_DK
    nw = nw_ref[...]
    trow = lax.broadcasted_iota(I32, (dec, dk), 0)
    srow = lax.broadcasted_iota(I32, (dk, dk), 0)
    zrows = jnp.zeros((dec, dk), F32)
    pad16 = lambda a: jnp.concatenate([a, zrows], axis=0).astype(BF16)
    dot_tn = lambda a, b: lax.dot_general(a, b, (((0,), (0,)), ((), ())), preferred_element_type=F32)

    def one_seq(si, carry):
        r0 = pl.multiple_of(si * dec, dec)
        for h in range(HG_HEADS):
            cols = slice(h * dk, (h + 1) * dk)
            q = hg_ref[pl.ds(r0, dec), h * dk:(h + 1) * dk]
            f = hg_ref[pl.ds(r0, dec), hw + h * dk:hw + (h + 1) * dk]
            v = hg_ref[pl.ds(r0, dec), 2 * hw + h * dk:2 * hw + (h + 1) * dk]
            g = hg_ref[pl.ds(r0, dec), 3 * hw + h * dk:3 * hw + (h + 1) * dk]
            k = 1.0 - f
            b = jnp.log(f)
            for dlt in (1, 2, 4):
                b = b + jnp.where(trow >= dlt, pltpu.roll(b, dlt, 0), 0.0)
            s0 = s0_ref[si * HG_HEADS + h]
            o = _dot(pad16(q * jnp.exp(b)), s0.astype(BF16))[0:dec]
            for s in range(dec):
                live = trow >= s
                e = jnp.exp(jnp.where(live, b - b[s:s + 1, :], 0.0))
                a_s = jnp.sum(jnp.where(live, q * e * k[s:s + 1, :], 0.0), axis=1, keepdims=True)
                o = o + a_s * v[s:s + 1, :]
            o_ref[pl.ds(r0, dec), cols] = _hgrn_post(o, g, nw)
            blast = b[dec - 1:dec, :]
            upd = dot_tn(pad16(k * jnp.exp(blast - b)), pad16(v))
            ecol = jnp.where(srow == 0, jnp.exp(blast), 0.0).T[:, 0:1]
            s_ref[si * HG_HEADS + h] = ecol * s0 + upd
        return carry

    lax.fori_loop(0, seqs, one_seq, 0)


def _hgrn_sample(hg, s0, nw, *, dec_batch, dec_seq, row0):
    hw = HG_HEADS * HG_DK
    seqs = 8
    rows = seqs * dec_seq
    kern = functools.partial(_hgrn_sample_kernel, dec=dec_seq, seqs=seqs)
    st_spec = pl.BlockSpec((seqs * HG_HEADS, HG_DK, HG_DK), lambda i: (i, 0, 0))
    return pl.pallas_call(
        kern,
        out_shape=(jax.ShapeDtypeStruct((dec_batch * dec_seq, hw), F32),
                   jax.ShapeDtypeStruct((dec_batch * HG_HEADS, HG_DK, HG_DK), F32)),
        grid=(dec_batch // seqs,),
        in_specs=[pl.BlockSpec((rows, 4 * hw), lambda i: (row0 // rows + i, 0)), st_spec,
                  pl.BlockSpec((1, HG_DK), lambda i: (0, 0))],
        out_specs=(pl.BlockSpec((rows, hw), lambda i: (i, 0)), st_spec),
        compiler_params=_cparams(("arbitrary",)),
        name="hgrn_sample",
    )(hg, s0, nw)


def _post_mixer_kernel(*refs, n_a, npt):
    hp_ref, hs_ref, mp_ref, ms_ref = refs[0:4]
    a_refs = refs[4:4 + 2 * n_a]
    w_refs = refs[4 + 2 * n_a:4 + 3 * n_a]
    nffn_ref, rw_ref, rb_ref, striu_ref = refs[4 + 3 * n_a:8 + 3 * n_a]
    hmid_ref, u_ref, rinfo_ref, counts_ref, carry = refs[8 + 3 * n_a:]
    i = pl.program_id(0)

    @pl.when(i == 0)
    def _():
        carry[...] = jnp.zeros_like(carry)

    def run(is_prompt):
        h = (hp_ref if is_prompt else hs_ref)[...]
        m3 = (mp_ref if is_prompt else ms_ref)[...]
        out = None
        for a in range(n_a):
            av = a_refs[2 * a + (0 if is_prompt else 1)][...].astype(BF16)
            t = _dot(av, w_refs[a][...])
            out = t if out is None else out + t
        hm = _gate_residual(h, m3, 2, out)
        hmid_ref[...] = hm
        u = _norm_mod(hm, nffn_ref[...], m3, 4, 3)
        u_ref[...] = u.reshape(u_ref.shape)

        u_hi = u.astype(BF16)
        u_lo = (u - u_hi.astype(F32)).astype(BF16)
        hh = _dot(u_hi, rw_ref[...])
        logits = hh[:, 0:LANES] + hh[:, LANES:2 * LANES] + _dot(u_lo, rw_ref[:, 0:LANES]) + rb_ref[...]
        lt = logits.T
        col = lambda j: lt[j:j + 1, :]
        gl = [col(j) for j in range(N_GROUPS)]
        gmax = functools.reduce(jnp.maximum, gl)
        gsum = functools.reduce(lambda x, y: x + y, [jnp.exp(x - gmax) for x in gl])
        gval = 1.0 / gsum
        gidx = jnp.where(gl[0] == gmax, 0, jnp.where(gl[1] == gmax, 1, jnp.where(gl[2] == gmax, 2, 3)))
        el = []
        for k in range(EXPERTS_PER_GROUP):
            c = [col(N_GROUPS + g * EXPERTS_PER_GROUP + k) for g in range(N_GROUPS)]
            el.append(jnp.where(gidx == 0, c[0], jnp.where(gidx == 1, c[1], jnp.where(gidx == 2, c[2], c[3]))))
        emax = functools.reduce(jnp.maximum, el)
        pe = [jnp.exp(x - emax) for x in el]
        esum = functools.reduce(lambda x, y: x + y, pe)
        pk = [x / esum for x in pe]
        v1 = functools.reduce(jnp.maximum, pk)
        i1 = jnp.where(pk[0] == v1, 0, jnp.where(pk[1] == v1, 1, jnp.where(pk[2] == v1, 2, 3)))
        pk2 = [jnp.where(i1 == k, -1.0, pk[k]) for k in range(EXPERTS_PER_GROUP)]
        v2 = functools.reduce(jnp.maximum, pk2)
        i2 = jnp.where(pk2[0] == v2, 0, jnp.where(pk2[1] == v2, 1, jnp.where(pk2[2] == v2, 2, 3)))
        den = v1 + v2
        w1 = gval * v1 / den
        w2 = gval * v2 / den
        lo = jnp.minimum(i1, i2)
        hi = jnp.maximum(i1, i2)
        pair = jnp.where(lo == 0, hi - 1, jnp.where(lo == 1, hi + 1, 5))
        bucket = gidx * 6 + pair
        w_lo = jnp.where(i1 < i2, w1, w2)
        w_hi = jnp.where(i1 < i2, w2, w1)

        r = h.shape[0]
        sub = lax.broadcasted_iota(I32, (LANES, r), 0)
        onehot = sub == bucket
        before = _dot(onehot.astype(BF16), striu_ref[...]) + carry[:, 0:1]
        rank = jnp.sum(jnp.where(onehot, before, 0.0), axis=0, keepdims=True)
        carry[...] = carry[...] + jnp.sum(onehot.astype(F32), axis=1, keepdims=True)
        info = jnp.where(sub == 0, bucket.astype(F32),
                         jnp.where(sub == 1, w_lo, jnp.where(sub == 2, w_hi, jnp.where(sub == 3, rank, 0.0))))
        rinfo_ref[...] = info.T
        counts_ref[...] = carry[...]

    _by_group(i, npt, run)


def _post_mixer(h_pair, h_off_s, mp, ms, a_pairs, ws, nffn, rw, rb, *, seq):
    hp, hs = h_pair
    d = hp.shape[1]
    tp = a_pairs[0][0].shape[0]
    ts = a_pairs[0][1].shape[0]
    npt, nst = tp // ROW_TILE, ts // ROW_TILE
    t = tp + ts
    n_a = len(a_pairs)
    tiles_per_seq = seq // ROW_TILE
    const = lambda shape: pl.BlockSpec(shape, lambda i: (0,) * len(shape))
    stril = jnp.asarray(np.triu(np.ones((ROW_TILE, ROW_TILE), np.float32), 1), dtype=BF16)
    in_specs = [*_pair_specs(d, npt, off_s=h_off_s), *_mod_specs(d, npt, tiles_per_seq)]
    args = [hp, hs, mp, ms]
    for ap, as_ in a_pairs:
        in_specs += list(_pair_specs(ap.shape[1], npt))
        args += [ap, as_]
    for w in ws:
        in_specs.append(const(w.shape))
        args.append(w)
    in_specs += [const((1, d)), const(rw.shape), const((1, LANES)), const(stril.shape)]
    args += [nffn, rw, rb, stril]
    tile = lambda c: pl.BlockSpec((ROW_TILE, c), lambda i: (i, 0))
    kern = functools.partial(_post_mixer_kernel, n_a=n_a, npt=npt)
    return pl.pallas_call(
        kern,
        out_shape=(jax.ShapeDtypeStruct((t, d), F32), jax.ShapeDtypeStruct((t, d // LANES, LANES), F32),
                   jax.ShapeDtypeStruct((t, LANES), F32), jax.ShapeDtypeStruct((LANES, LANES), F32)),
        grid=(npt + nst,),
        in_specs=in_specs,
        out_specs=(tile(d), pl.BlockSpec((ROW_TILE, d // LANES, LANES), lambda i: (i, 0, 0)), tile(LANES),
                   const((LANES, LANES))),
        scratch_shapes=[pltpu.VMEM((LANES, LANES), F32)],
        compiler_params=_cparams(("arbitrary",)),
        name="post_mixer",
    )(*args)


ROW_GROUP = 8


def _for_rows(rows, fn):
    def body(g, c):
        for k in range(ROW_GROUP):
            fn(g * ROW_GROUP + k, k % 2)
        return c

    lax.fori_loop(0, rows // ROW_GROUP, body, 0)


def _slot_of(bucket_ref, rank_ref, starts_ref, t):
    return starts_ref[bucket_ref[t]] + rank_ref[t]


def _scatter_kernel(bucket_ref, rank_ref, starts_ref, src_ref, init_ref, o_ref, sem):
    del init_ref
    rows = src_ref.shape[0]
    base = pl.program_id(0) * rows
    copy = lambda r: pltpu.make_async_copy(
        src_ref.at[r], o_ref.at[_slot_of(bucket_ref, rank_ref, starts_ref, base + r)], sem)
    _for_rows(rows, lambda r, pr: copy(r).start(priority=pr))
    _for_rows(rows, lambda r, pr: copy(r).wait())


def _scatter_tokens(slots, src, n_slots):
    t = src.shape[0]
    rows = 4 * ROW_TILE if t % (4 * ROW_TILE) == 0 else ROW_TILE
    grid_spec = pltpu.PrefetchScalarGridSpec(
        num_scalar_prefetch=3,
        grid=(t // rows,),
        in_specs=[pl.BlockSpec((rows,) + src.shape[1:], lambda i, *_: (i, 0, 0)), pl.BlockSpec(memory_space=pl.ANY)],
        out_specs=pl.BlockSpec(memory_space=pl.ANY),
        scratch_shapes=[pltpu.SemaphoreType.DMA(())],
    )
    return pl.pallas_call(
        _scatter_kernel,
        out_shape=jax.ShapeDtypeStruct((n_slots,) + src.shape[1:], src.dtype),
        grid_spec=grid_spec,
        input_output_aliases={4: 0},
        compiler_params=_cparams(("arbitrary",)),
        name="moe_scatter",
    )(*slots, src, jnp.zeros((n_slots,) + src.shape[1:], src.dtype))


def _moe_kernel(ea_ref, eb_ref, valid_ref, fresh_ref, slot_ref, na_ref, nb_ref, more_ref, x_ref,
                w1_ref, w3_ref, w2_ref, o_ref, wf1, wf3, wf2, wbuf1, wbuf3, wbuf2, sem, *, layer):
    i = pl.program_id(0)
    nk = x_ref.shape[1]

    def weight_copies(e_a, e_b, slot):
        cps = []
        for s, e in ((0, e_a), (1, e_b)):
            for src, dst in ((w1_ref, wf1), (w3_ref, wf3), (w2_ref, wf2)):
                cps.append(pltpu.make_async_copy(src.at[layer, e], dst.at[slot, s], sem.at[slot]))
        return cps

    @pl.when(i == 0)
    def _():
        for c in weight_copies(ea_ref[0], eb_ref[0], slot_ref[0]):
            c.start()

    @pl.when(fresh_ref[i] == 1)
    def _():
        slot = slot_ref[i]
        for c in weight_copies(ea_ref[i], eb_ref[i], slot):
            c.wait()

        @pl.when(more_ref[i] == 1)
        def _():
            for c in weight_copies(na_ref[i], nb_ref[i], 1 - slot):
                c.start()

        for s in (0, 1):
            wbuf1[s] = wf1[slot, s].astype(BF16)
            wbuf3[s] = wf3[slot, s].astype(BF16)
            wbuf2[s] = wf2[slot, s].astype(BF16)

    @pl.when(valid_ref[i] == 1)
    def _():
        xb = x_ref[...].reshape(x_ref.shape[0], nk * LANES).astype(BF16)
        for s in (0, 1):
            h1 = _dot(xb, wbuf1[s])
            h3 = _dot(xb, wbuf3[s])
            y = _dot(((h1 * _sigmoid(h1)) * h3).astype(BF16), wbuf2[s])
            o_ref[:, s * nk:(s + 1) * nk, :] = y.reshape(y.shape[0], nk, LANES)

    @pl.when(valid_ref[i] == 0)
    def _():
        o_ref[...] = jnp.zeros_like(o_ref)


def _moe_experts(plan, xs, w1, w3, w2, layer):
    n_slots, nk, _ = xs.shape
    d = nk * LANES
    ff = w1.shape[-1]
    rows = lambda k: pl.BlockSpec((MOE_TILE, k, LANES), lambda i, *_: (i, 0, 0))
    hbm = pl.BlockSpec(memory_space=pl.ANY)
    grid_spec = pltpu.PrefetchScalarGridSpec(
        num_scalar_prefetch=len(plan),
        grid=(n_slots // MOE_TILE,),
        in_specs=[rows(nk), hbm, hbm, hbm],
        out_specs=rows(2 * nk),
        scratch_shapes=[pltpu.VMEM((2, 2, d, ff), F32), pltpu.VMEM((2, 2, d, ff), F32), pltpu.VMEM((2, 2, ff, d), F32),
                        pltpu.VMEM((2, d, ff), BF16), pltpu.VMEM((2, d, ff), BF16), pltpu.VMEM((2, ff, d), BF16),
                        pltpu.SemaphoreType.DMA((2,))],
    )
    return pl.pallas_call(
        functools.partial(_moe_kernel, layer=layer),
        out_shape=jax.ShapeDtypeStruct((n_slots, 2 * nk, LANES), F32),
        grid_spec=grid_spec,
        compiler_params=_cparams(("arbitrary",)),
        name="moe_experts",
    )(*plan, xs, w1, w3, w2)


def _moe_plan(rinfo, counts, n_slots):
    n_tiles = n_slots // MOE_TILE
    cnt = counts[:N_BUCKETS, 0].astype(I32)
    padded = ((cnt + MOE_TILE - 1) // MOE_TILE) * MOE_TILE
    ends = jnp.cumsum(padded)
    starts = ends - padded
    slots = (rinfo[:, 0].astype(I32), rinfo[:, 3].astype(I32), starts)
    n_valid = ends[-1] // MOE_TILE
    tiles = jnp.arange(n_tiles, dtype=I32)
    tb = jnp.sum((ends[None, :] <= (tiles * MOE_TILE)[:, None]).astype(I32), axis=1)
    tb = jnp.minimum(tb, N_BUCKETS - 1)
    valid = tiles < n_valid
    tb = jnp.where(valid, tb, tb[jnp.maximum(n_valid - 1, 0)])
    lo = jnp.asarray(PAIR_LO, I32)
    hi = jnp.asarray(PAIR_HI, I32)
    ea = (tb // 6) * EXPERTS_PER_GROUP + lo[tb % 6]
    eb = (tb // 6) * EXPERTS_PER_GROUP + hi[tb % 6]
    fresh = jnp.concatenate([jnp.ones((1,), I32), (tb[1:] != tb[:-1]).astype(I32)])
    run_slot = (jnp.cumsum(fresh) - 1) % 2
    later_start = (tiles[None, :] > tiles[:, None]) & (fresh[None, :] == 1)
    nxt = jnp.min(jnp.where(later_start, tiles[None, :], n_tiles), axis=1)
    more = (nxt < n_tiles).astype(I32)
    nxt = jnp.minimum(nxt, n_tiles - 1)
    plan = (ea, eb, valid.astype(I32), fresh, run_slot.astype(I32), ea[nxt], eb[nxt], more)
    return slots, plan


def _moe(u_tiles, rinfo, counts, w1, w3, w2, layer):
    t = u_tiles.shape[0]
    n_slots = t + N_BUCKETS * MOE_TILE
    slots, plan = _moe_plan(rinfo, counts, n_slots)
    xs = _scatter_tokens(slots, u_tiles, n_slots)
    ys = _moe_experts(plan, xs, w1, w3, w2, layer)
    return slots, ys


def _combine_kernel(bucket_ref, rank_ref, starts_ref, ys_ref, hm_ref, rinfo_ref, mp_ref, ms_ref, nf_ref, *rest,
                    npt, final):
    if final:
        yp_ref, ysm_ref, gbuf, sem = rest
    else:
        h_ref, gbuf, sem = rest
    i = pl.program_id(0)
    n = pl.num_programs(0)
    rows = gbuf.shape[1]
    nk = gbuf.shape[2] // 2

    def copy(tile, r):
        slot = tile % 2
        src = ys_ref.at[_slot_of(bucket_ref, rank_ref, starts_ref, tile * rows + r)]
        return pltpu.make_async_copy(src, gbuf.at[slot, r], sem.at[slot])

    @pl.when(i == 0)
    def _():
        _for_rows(rows, lambda r, pr: copy(0, r).start(priority=pr))

    @pl.when(i + 1 < n)
    def _():
        _for_rows(rows, lambda r, pr: copy(i + 1, r).start(priority=pr))

    _for_rows(rows, lambda r, pr: copy(i, r).wait())
    slot = i % 2
    ya = gbuf[slot, :, 0:nk, :].reshape(rows, nk * LANES)
    yb = gbuf[slot, :, nk:2 * nk, :].reshape(rows, nk * LANES)
    rinfo = rinfo_ref[...]
    moe = rinfo[:, 1:2] * ya + rinfo[:, 2:3] * yb

    def run(is_prompt):
        m3 = (mp_ref if is_prompt else ms_ref)[...]
        h = _gate_residual(hm_ref[...], m3, 5, moe)
        if final:
            y = h * lax.rsqrt(jnp.mean(h * h, axis=-1, keepdims=True) + NORM_EPS) * nf_ref[...]
            (yp_ref if is_prompt else ysm_ref)[...] = y
        else:
            h_ref[...] = h

    _by_group(i, npt, run)


def _combine(slots, ys, hmid, rinfo, mp, ms, nf, *, tp, seq, final):
    t, d = hmid.shape
    npt = tp // ROW_TILE
    tiles_per_seq = seq // ROW_TILE
    mpm = lambda f: (lambda i, *_: f(i))
    mp_spec, ms_spec = _mod_specs(d, npt, tiles_per_seq)
    yp_spec, ysm_spec = _pair_specs(d, npt)
    wrap = lambda s: pl.BlockSpec(s.block_shape, mpm(s.index_map))
    if final:
        out_shape = (jax.ShapeDtypeStruct((tp, d), F32), jax.ShapeDtypeStruct((t - tp, d), F32))
        out_specs = (wrap(yp_spec), wrap(ysm_spec))
    else:
        out_shape = jax.ShapeDtypeStruct((t, d), F32)
        out_specs = pl.BlockSpec((ROW_TILE, d), lambda i, *_: (i, 0))
    grid_spec = pltpu.PrefetchScalarGridSpec(
        num_scalar_prefetch=3,
        grid=(t // ROW_TILE,),
        in_specs=[pl.BlockSpec(memory_space=pl.ANY),
                  pl.BlockSpec((ROW_TILE, d), lambda i, *_: (i, 0)),
                  pl.BlockSpec((ROW_TILE, LANES), lambda i, *_: (i, 0)),
                  wrap(mp_spec), wrap(ms_spec),
                  pl.BlockSpec((1, d), lambda i, *_: (0, 0))],
        out_specs=out_specs,
        scratch_shapes=[pltpu.VMEM((2, ROW_TILE) + ys.shape[1:], F32), pltpu.SemaphoreType.DMA((2,))],
    )
    kern = functools.partial(_combine_kernel, npt=npt, final=final)
    return pl.pallas_call(
        kern, out_shape=out_shape, grid_spec=grid_spec,
        compiler_params=_cparams(("arbitrary",)),
        name="moe_combine_final" if final else "moe_combine",
    )(*slots, ys, hmid, rinfo, mp, ms, nf)


def _odd_pre_kernel(hp_ref, hs_ref, mp_ref, ms_ref, nw_ref, w_ref, gate_ref, rec_ref, *, npt):
    i = pl.program_id(0)
    width = gate_ref.shape[1]

    def run(is_prompt):
        h = (hp_ref if is_prompt else hs_ref)[...]
        m3 = (mp_ref if is_prompt else ms_ref)[...]
        ub = _norm_mod(h, nw_ref[...], m3, 1, 0).astype(BF16)
        x = _dot(ub, w_ref[:, 0:width])
        gate_ref[...] = 0.5 * x * (1.0 + jnp.tanh(np.sqrt(2.0 / np.pi) * (x + 0.044715 * (x * x * x))))
        rec_ref[...] = _dot(ub, w_ref[:, width:2 * width])

    _by_group(i, npt, run)


def _odd_pre(h, mp, ms, nw, w_in, *, tp, seq):
    t, d = h.shape
    npt = tp // ROW_TILE
    width = w_in.shape[1] // 2
    const = lambda shape: pl.BlockSpec(shape, lambda i: (0,) * len(shape))
    tile = lambda c: pl.BlockSpec((ROW_TILE, c), lambda i: (i, 0))
    kern = functools.partial(_odd_pre_kernel, npt=npt)
    return pl.pallas_call(
        kern,
        out_shape=(jax.ShapeDtypeStruct((t, width), F32), jax.ShapeDtypeStruct((t, width), F32)),
        grid=(t // ROW_TILE,),
        in_specs=[*_pair_specs(d, npt, off_s=npt), *_mod_specs(d, npt, seq // ROW_TILE),
                  const((1, d)), const(w_in.shape)],
        out_specs=(tile(width), tile(width)),
        compiler_params=_cparams(("arbitrary",)),
        name="odd_pre",
    )(h, h, mp, ms, nw, w_in)


def _lru_kernel(rec_ref, gate_ref, c0_ref, h0_ref, cw_ref, cb_ref, wa_ref, ba_ref, wx_ref, bx_ref, lam_ref,
                yg_ref, cs_ref, hl_ref, hist, hc, a_s, h_s, *, nb, tl):
    i = pl.program_id(1)
    r, width = rec_ref.shape
    taps = CONV_WIDTH
    blk = width // LRU_HEADS

    @pl.when(i == 0)
    def _():
        hist[:, SUBLANES - (taps - 1):SUBLANES, :] = c0_ref[...]
        hc[...] = h0_ref[...]

    rec3 = rec_ref[...].reshape(nb, tl, width)
    hist[:, SUBLANES:SUBLANES + tl, :] = rec3
    cw = cw_ref[...]
    conv = cb_ref[...] + rec3 * cw[taps - 1:taps, :]
    for back in range(1, taps):
        conv = conv + hist[:, SUBLANES - back:SUBLANES - back + tl, :] * cw[taps - 1 - back:taps - back, :]
    tail = hist[:, SUBLANES + tl - (taps - 1):SUBLANES + tl, :]
    cs_ref[...] = tail
    hist[:, SUBLANES - (taps - 1):SUBLANES, :] = tail

    cf = conv.reshape(r, width)
    cb16 = cf.astype(BF16)
    gr = jnp.concatenate([_dot(cb16[:, h * blk:(h + 1) * blk], wa_ref[h]) for h in range(LRU_HEADS)], axis=1)
    gi = jnp.concatenate([_dot(cb16[:, h * blk:(h + 1) * blk], wx_ref[h]) for h in range(LRU_HEADS)], axis=1)
    rg = _sigmoid(gr + ba_ref[...])
    ig = _sigmoid(gi + bx_ref[...])
    z = -lam_ref[...]
    softplus = jnp.maximum(z, 0.0) + jnp.log1p(jnp.exp(-jnp.abs(z)))
    log_a = -LRU_C * rg * softplus
    a = jnp.exp(log_a)
    gx = jnp.sqrt(1.0 - a * a) * (ig * cf)

    groups = r // SUBLANES
    sub = lax.broadcasted_iota(I32, (groups, SUBLANES, width), 1)
    aa, hh = a.reshape(groups, SUBLANES, width), gx.reshape(groups, SUBLANES, width)
    for dlt in (1, 2, 4):
        ok = sub >= dlt
        a_sh = pltpu.roll(aa, dlt, 1)
        h_sh = pltpu.roll(hh, dlt, 1)
        hh = jnp.where(ok, hh + aa * h_sh, hh)
        aa = jnp.where(ok, aa * a_sh, aa)
    aa, hh = aa.reshape(r, width), hh.reshape(r, width)

    if tl == SUBLANES:
        y3 = hh.reshape(nb, tl, width) + aa.reshape(nb, tl, width) * hc[...]
        hl_ref[...] = y3[:, tl - 1:tl, :]
        y = y3.reshape(r, width)
    else:
        a_s[...] = aa
        h_s[...] = hh

        def group(j, carry):
            r0 = pl.multiple_of(j * SUBLANES, SUBLANES)
            yj = h_s[pl.ds(r0, SUBLANES), :] + a_s[pl.ds(r0, SUBLANES), :] * carry
            h_s[pl.ds(r0, SUBLANES), :] = yj
            return yj[SUBLANES - 1:SUBLANES, :]

        last = lax.fori_loop(0, r // SUBLANES, group, hc[0])
        hc[0] = last
        hl_ref[0] = last
        y = h_s[...]
    yg_ref[...] = (y * gate_ref[...]).astype(BF16)


def _lru(rec, gate, conv0, h0, cw, cb, wa, ba, wx, bx, lam, *, batch, seq, row0):
    width = rec.shape[1]
    taps = CONV_WIDTH
    if seq == SUBLANES:
        nb, tl = ROW_TILE // SUBLANES, SUBLANES
    else:
        nb, tl = 1, ROW_TILE
    r = nb * tl
    n_l = seq // tl
    blk0 = row0 // r
    const = lambda shape: pl.BlockSpec(shape, lambda b, i: (0,) * len(shape))
    tile_in = pl.BlockSpec((r, width), lambda b, i: (blk0 + b * n_l + i, 0))
    kern = functools.partial(_lru_kernel, nb=nb, tl=tl)
    return pl.pallas_call(
        kern,
        out_shape=(jax.ShapeDtypeStruct((batch * seq, width), BF16),
                   jax.ShapeDtypeStruct((batch, taps - 1, width), F32),
                   jax.ShapeDtypeStruct((batch, 1, width), F32)),
        grid=(batch // nb, n_l),
        in_specs=[tile_in, tile_in,
                  pl.BlockSpec((nb, taps - 1, width), lambda b, i: (b, 0, 0)),
                  pl.BlockSpec((nb, 1, width), lambda b, i: (b, 0, 0)),
                  const((taps, width)), const((1, width)), const(wa.shape), const((1, width)),
                  const(wx.shape), const((1, width)), const((1, width))],
        out_specs=(pl.BlockSpec((r, width), lambda b, i: (b * n_l + i, 0)),
                   pl.BlockSpec((nb, taps - 1, width), lambda b, i: (b, 0, 0)),
                   pl.BlockSpec((nb, 1, width), lambda b, i: (b, 0, 0))),
        scratch_shapes=[pltpu.VMEM((nb, tl + SUBLANES, width), F32), pltpu.VMEM((nb, 1, width), F32),
                        pltpu.VMEM((r, width), F32), pltpu.VMEM((r, width), F32)],
        compiler_params=_cparams(("arbitrary", "arbitrary")),
        name="rg_lru_sample" if seq == SUBLANES else "rg_lru_prompt",
    )(rec, gate, conv0, h0, cw, cb, wa, ba, wx, bx, lam)


def kernel(x_prompt, x_sample, c_prompt, c_sample, cache_k, cache_v, state_hgrn, state_conv, state_lru, page_table,
           ada_w, ada_b, norm_mix, norm_ffn, norm_final, w_in_even, w_out_even, hg_lower_bounds, hg_norm_w, w_in_odd,
           conv_w, conv_b, lru_wa, lru_ba, lru_wx, lru_bx, lru_lambda, w_out_odd, router_group_w, router_group_b,
           router_expert_w, router_expert_b, moe_w1, moe_w3, moe_w2):
    batch, seq, d = x_prompt.shape
    dec_batch, dec_seq, _ = x_sample.shape
    depth = ada_w.shape[0]
    assert depth == 2 and seq % ROW_TILE == 0 and seq % MOBA_BLOCK == 0
    tp, ts = batch * seq, dec_batch * dec_seq
    n_pages, page = page_table.shape[1], cache_k.shape[2]
    past_len = n_pages * page
    aw = MOBA_HEADS * MOBA_HEAD_DIM
    hw = HG_HEADS * HG_DK

    xp = x_prompt.reshape(tp, d)
    xs = x_sample.reshape(ts, d)
    mods = _ada_mods(jnp.concatenate([c_prompt, c_sample], axis=0), ada_w, ada_b)
    mods = mods.reshape(depth, batch + dec_batch, 6, d)
    mods_p, mods_s = mods[:, :batch], mods[:, batch:]

    half = MOBA_HEAD_DIM // 2
    inv_freq = jnp.power(ROPE_THETA, -jnp.arange(half, dtype=F32) / half)
    invf = jnp.tile(inv_freq, LANES // half).reshape(1, LANES)

    def router_mats(l):
        rw = jnp.zeros((d, LANES), F32)
        rw = rw.at[:, 0:N_GROUPS].set(router_group_w[l])
        rw = rw.at[:, N_GROUPS:N_GROUPS + N_GROUPS * EXPERTS_PER_GROUP].set(router_expert_w[l])
        rb = jnp.zeros((1, LANES), F32)
        rb = rb.at[0, 0:N_GROUPS].set(router_group_b[l])
        rb = rb.at[0, N_GROUPS:N_GROUPS + N_GROUPS * EXPERTS_PER_GROUP].set(router_expert_b[l])
        rw_hi = rw.astype(BF16)
        rw_lo = (rw - rw_hi.astype(F32)).astype(BF16)
        return jnp.concatenate([rw_hi, rw_lo], axis=1), rb

    q_all, k_p, k_s, v_p, v_s, hg = _even_pre(
        xp, xs, mods_p[0], mods_s[0], norm_mix[0].reshape(1, d), w_in_even[0].astype(BF16), hg_lower_bounds, invf,
        seq=seq, past_len=past_len, layer_slot=0)
    oa_p = _attn_prompt(q_all, k_p, v_p, batch=batch, seq=seq)
    oa_s = _attn_sample(page_table, q_all, k_s, v_s, cache_k, cache_v,
                        layer=0, dec_batch=dec_batch, dec_seq=dec_seq, q_row0=tp)
    hnw = hg_norm_w[0].reshape(1, HG_DK)
    ob_p, hgrn_p = _hgrn_prompt(hg, hnw, batch=batch, seq=seq)
    ob_s, hgrn_s = _hgrn_sample(hg, state_hgrn[0].reshape(dec_batch * HG_HEADS, HG_DK, HG_DK), hnw,
                                dec_batch=dec_batch, dec_seq=dec_seq, row0=tp)
    w_out = w_out_even[0].astype(BF16)
    rw, rb = router_mats(0)
    hmid, u, rinfo, counts = _post_mixer((xp, xs), 0, mods_p[0], mods_s[0], [(oa_p, oa_s), (ob_p, ob_s)],
                                         [w_out[:aw], w_out[aw:]], norm_ffn[0].reshape(1, d), rw, rb, seq=seq)
    dest, ys = _moe(u, rinfo, counts, moe_w1, moe_w3, moe_w2, 0)
    h1 = _combine(dest, ys, hmid, rinfo, mods_p[0], mods_s[0], norm_final.reshape(1, d), tp=tp, seq=seq,
                  final=False)

    gate, rec = _odd_pre(h1, mods_p[1], mods_s[1], norm_mix[1].reshape(1, d), w_in_odd[0].astype(BF16), tp=tp, seq=seq)
    width = rec.shape[1]
    lru_args = (conv_w[0], conv_b[0].reshape(1, width), lru_wa[0].astype(BF16), lru_ba[0].reshape(1, width),
                lru_wx[0].astype(BF16), lru_bx[0].reshape(1, width), lru_lambda[0].reshape(1, width))
    yg_p, conv_p, lru_p = _lru(rec, gate, jnp.zeros((batch, CONV_WIDTH - 1, width), F32),
                               jnp.zeros((batch, 1, width), F32), *lru_args, batch=batch, seq=seq, row0=0)
    yg_s, conv_s, lru_s = _lru(rec, gate, state_conv[0], state_lru[0].reshape(dec_batch, 1, width), *lru_args,
                               batch=dec_batch, seq=dec_seq, row0=tp)
    rw, rb = router_mats(1)
    hmid, u, rinfo, counts = _post_mixer((h1, h1), tp // ROW_TILE, mods_p[1], mods_s[1], [(yg_p, yg_s)],
                                         [w_out_odd[0].astype(BF16)], norm_ffn[1].reshape(1, d), rw, rb, seq=seq)
    dest, ys = _moe(u, rinfo, counts, moe_w1, moe_w3, moe_w2, 1)
    y_p, y_s = _combine(dest, ys, hmid, rinfo, mods_p[1], mods_s[1], norm_final.reshape(1, d), tp=tp, seq=seq,
                        final=True)

    pos_major = lambda a: jnp.transpose(a.reshape(batch, MOBA_HEADS, MOBA_HEAD_DIM, seq), (0, 3, 1, 2))[None]
    return (y_p.reshape(batch, seq, d), y_s.reshape(dec_batch, dec_seq, d),
            pos_major(k_p), pos_major(v_p),
            k_s.reshape(1, dec_batch, dec_seq, MOBA_HEADS, MOBA_HEAD_DIM),
            v_s.reshape(1, dec_batch, dec_seq, MOBA_HEADS, MOBA_HEAD_DIM),
            hgrn_p.reshape(1, batch, HG_HEADS, HG_DK, HG_DK), hgrn_s.reshape(1, dec_batch, HG_HEADS, HG_DK, HG_DK),
            conv_p.reshape(1, batch, CONV_WIDTH - 1, width), conv_s.reshape(1, dec_batch, CONV_WIDTH - 1, width),
            lru_p.reshape(1, batch, width), lru_s.reshape(1, dec_batch, width))
```
